```python
import math
import jax, jax.numpy as jnp
from jax import lax
import numpy as np

D_MODEL = 2048
BATCH = 4
SEQ = 2048
DEPTH = 1
DEC_BATCH = 8
DEC_SEQ = 8
PAST_LEN = 16384
PAGE_SIZE = 128

SB_HEADS = 8
SB_HEAD_DIM = 128
SB_WIDTH = SB_HEADS * SB_HEAD_DIM
SB_BIAS_HI = -1.0
SB_BIAS_LO = -10.0
SSD_HEADS = 8
SSD_HEAD_DIM = 64
SSD_WIDTH = SSD_HEADS * SSD_HEAD_DIM
SSD_GROUPS = 2
SSD_STATE = 128
CONV_WIDTH = 4
SSD_CHUNK = 128
XBC_WIDTH = SSD_WIDTH + 2 * SSD_GROUPS * SSD_STATE
MEM_TOKENS = 256
MEM_HEADS = 4
MEM_HEAD_DIM = 128
MEM_WIDTH = MEM_HEADS * MEM_HEAD_DIM
MIX_WIDTH = SB_WIDTH + SSD_WIDTH + MEM_WIDTH
Q_BLOCK = 128
EPS = 1e-6
IN_WIDTHS = (SB_WIDTH, SB_WIDTH, SB_WIDTH, SB_WIDTH, SSD_WIDTH, XBC_WIDTH, SSD_HEADS, MEM_WIDTH, MEM_WIDTH)
IN_WIDTH = 4 * SB_WIDTH + SSD_WIDTH + XBC_WIDTH + SSD_HEADS + 2 * MEM_WIDTH

kernel_name = "hymba_stickbreak_ssd_memory_step"


def rmsnorm(x, w):
    x32 = x.astype(jnp.float32)
    y = x32 * lax.rsqrt(jnp.mean(x32 * x32, axis=-1, keepdims=True) + EPS)
    return (y * w.astype(jnp.float32)).astype(x.dtype)


def split_projection(proj):
    offsets, acc = [], 0
    for w in IN_WIDTHS[:-1]:
        acc += w
        offsets.append(acc)
    return jnp.split(proj, offsets, axis=-1)


def memory_kv(mem, mem_norm_w, w_mem_kv, mem_k_norm):
    b, m, _ = mem.shape
    k, v = jnp.split(rmsnorm(mem, mem_norm_w) @ w_mem_kv, 2, axis=-1)
    k = rmsnorm(k.reshape(b, m, MEM_HEADS, MEM_HEAD_DIM), mem_k_norm)
    return k, v.reshape(b, m, MEM_HEADS, MEM_HEAD_DIM)


def stick_breaking_attend(q, k, v, sb_bias, q_pos, k_pos):
    z = jnp.einsum('bqhd,bkhd->bhqk', q, k).astype(jnp.float32) * (SB_HEAD_DIM ** -0.5)
    z = z + sb_bias.astype(jnp.float32)[None, :, None, None]
    mask = (k_pos[None, :] < q_pos[:, None])[None, None]
    log_fail = jnp.where(mask, jax.nn.log_sigmoid(-z), 0.0)
    log_between = lax.cumsum(log_fail, axis=3, reverse=True) - log_fail
    w = jnp.where(mask, jnp.exp(jax.nn.log_sigmoid(z) + log_between), 0.0)
    return jnp.einsum('bhqk,bkhd->bqhd', w.astype(v.dtype), v)


def stick_breaking_prompt(q, k, v, sb_bias):
    b, s, h, d = q.shape
    nb = s // Q_BLOCK
    q_blocks = jnp.moveaxis(q.reshape(b, nb, Q_BLOCK, h, d), 1, 0)
    pos = jnp.arange(s, dtype=jnp.int32)
    out = lax.map(lambda blk: stick_breaking_attend(blk[0], k, v, sb_bias, blk[1], pos),
                  (q_blocks, pos.reshape(nb, Q_BLOCK)))
    return jnp.moveaxis(out, 0, 1).reshape(b, s, h, d)


def causal_conv(xbc, prefix, conv_w, conv_b):
    length = xbc.shape[1]
    xp = jnp.concatenate([prefix.astype(xbc.dtype), xbc], axis=1)
    out = conv_b + sum(xp[:, j:j + length] * conv_w[j] for j in range(CONV_WIDTH))
    return jax.nn.silu(out), xp[:, -(CONV_WIDTH - 1):]


def ssd_scan(x, dt, a, b_in, c_in, h0, d_skip):
    f32 = jnp.float32
    bsz, length = x.shape[:2]
    chunk = min(SSD_CHUNK, length)
    n_chunks = -(-length // chunk)
    pad = n_chunks * chunk - length
    r = SSD_HEADS // SSD_GROUPS

    def prep(arr):
        arr = arr.astype(f32)
        if pad:
            arr = jnp.pad(arr, [(0, 0), (0, pad)] + [(0, 0)] * (arr.ndim - 2))
        return arr.reshape((bsz, n_chunks, chunk) + arr.shape[2:])

    xc = prep(x).reshape(bsz, n_chunks, chunk, SSD_GROUPS, r, SSD_HEAD_DIM)
    dtc = prep(dt).reshape(bsz, n_chunks, chunk, SSD_GROUPS, r)
    bc, cc = prep(b_in), prep(c_in)
    cs = jnp.cumsum(dtc * a.reshape(SSD_GROUPS, r), axis=2)
    xdt = xc * dtc[..., None]
    causal = jnp.tril(jnp.ones((chunk, chunk), bool))[:, :, None, None]
    seg = cs[:, :, :, None] - cs[:, :, None, :]
    decay = jnp.exp(jnp.where(causal, seg, -jnp.inf))
    cb = jnp.einsum('bclgn,bcsgn->bclsg', cc, bc)
    y_diag = jnp.einsum('bclsg,bclsgr,bcsgrp->bclgrp', cb, decay, xdt)
    to_end = jnp.exp(cs[:, :, -1:] - cs)
    states = jnp.einsum('bclgn,bclgr,bclgrp->bcgrpn', bc, to_end, xdt)
    chunk_decay = jnp.exp(cs[:, :, -1])

    def step(h, inp):
        st, dec = inp
        return h * dec[..., None, None] + st, h

    h_init = h0.astype(f32).reshape(bsz, SSD_GROUPS, r, SSD_HEAD_DIM, SSD_STATE)
    h_final, h_prev = lax.scan(step, h_init, (jnp.moveaxis(states, 1, 0), jnp.moveaxis(chunk_decay, 1, 0)))
    h_prev = jnp.moveaxis(h_prev, 0, 1)
    y_off = jnp.einsum('bclgn,bcgrpn,bclgr->bclgrp', cc, h_prev, jnp.exp(cs))
    y = y_diag + y_off + xc * d_skip.astype(f32).reshape(SSD_GROUPS, r)[:, :, None]
    y = y.reshape(bsz, n_chunks * chunk, SSD_HEADS, SSD_HEAD_DIM)[:, :length]
    return y, h_final.reshape(bsz, SSD_HEADS, SSD_HEAD_DIM, SSD_STATE)


def mixer_layer(x, past_k, past_v, conv_prefix, ssm_init, mem_k, mem_v, norm_w, w_in,
                sb_q_norm, sb_k_norm, sb_bias, conv_w, conv_b, dt_bias, a_log, d_skip, ssd_norm_w,
                mem_q_norm, w_out):
    b, t, _ = x.shape
    q, k, v, g_sb, z, xbc, dt_raw, q_mem, g_mem = split_projection(rmsnorm(x, norm_w) @ w_in)
    q = rmsnorm(q.reshape(b, t, SB_HEADS, SB_HEAD_DIM), sb_q_norm)
    k = rmsnorm(k.reshape(b, t, SB_HEADS, SB_HEAD_DIM), sb_k_norm)
    v = v.reshape(b, t, SB_HEADS, SB_HEAD_DIM)
    if past_k is None:
        sb = stick_breaking_prompt(q, k, v, sb_bias)
    else:
        past = past_k.shape[1]
        k_all = jnp.concatenate([past_k.astype(k.dtype), k], axis=1)
        v_all = jnp.concatenate([past_v.astype(v.dtype), v], axis=1)
        sb = stick_breaking_attend(q, k_all, v_all, sb_bias, past + jnp.arange(t, dtype=jnp.int32),
                                   jnp.arange(past + t, dtype=jnp.int32))
    sb = sb.reshape(b, t, SB_WIDTH) * jax.nn.silu(g_sb)
    xbc_act, conv_new = causal_conv(xbc, conv_prefix, conv_w, conv_b)
    xs, bm, cm = jnp.split(xbc_act, [SSD_WIDTH, SSD_WIDTH + SSD_GROUPS * SSD_STATE], axis=-1)
    dt = jax.nn.softplus(dt_raw.astype(jnp.float32) + dt_bias.astype(jnp.float32))
    a = -jnp.exp(a_log.astype(jnp.float32))
    y, ssm_new = ssd_scan(xs.reshape(b, t, SSD_HEADS, SSD_HEAD_DIM), dt, a,
                          bm.reshape(b, t, SSD_GROUPS, SSD_STATE), cm.reshape(b, t, SSD_GROUPS, SSD_STATE),
                          ssm_init, d_skip)
    ssd = rmsnorm(y.reshape(b, t, SSD_WIDTH) * jax.nn.silu(z.astype(jnp.float32)), ssd_norm_w).astype(x.dtype)
    qm = rmsnorm(q_mem.reshape(b, t, MEM_HEADS, MEM_HEAD_DIM), mem_q_norm)
    s = jnp.einsum('bqhd,bmhd->bhqm', qm, mem_k).astype(jnp.float32) * (MEM_HEAD_DIM ** -0.5)
    p = jax.nn.softmax(s, axis=-1)
    mo = jnp.einsum('bhqm,bmhd->bqhd', p.astype(mem_v.dtype), mem_v).reshape(b, t, MEM_WIDTH) * jax.nn.silu(g_mem)
    mix = jnp.concatenate([sb, ssd, mo.astype(x.dtype)], axis=-1)
    return x + mix @ w_out, k, v, conv_new, ssm_new


def setup_inputs(seed: int = 0) -> dict:
    key = jax.random.key(seed)
    ks = jax.random.split(key, 32)
    f32 = jnp.float32
    n_pages = PAST_LEN // PAGE_SIZE
    n_pool = (5 * DEC_BATCH * n_pages + 3) // 4

    def nrm(k, shape, scale=1.0):
        return scale * jax.random.normal(k, shape, f32)

    def gain(k, shape):
        return 1.0 + 0.02 * jax.random.normal(k, shape, f32)

    dt0 = jnp.exp(jax.random.uniform(ks[20], (DEPTH, SSD_HEADS), f32, math.log(1e-3), math.log(1e-1)))
    page_table = jax.random.permutation(ks[9], n_pool)[:DEC_BATCH * n_pages]
    sb_bias = jnp.linspace(SB_BIAS_HI, SB_BIAS_LO, SB_HEADS, dtype=f32)[None] + nrm(ks[25], (DEPTH, SB_HEADS), 0.1)
    return {
        'x_prompt': nrm(ks[0], (BATCH, SEQ, D_MODEL)),
        'x_sample': nrm(ks[1], (DEC_BATCH, DEC_SEQ, D_MODEL)),
        'cache_sb_k': nrm(ks[2], (DEPTH, n_pool, PAGE_SIZE, SB_HEADS, SB_HEAD_DIM)),
        'cache_sb_v': nrm(ks[3], (DEPTH, n_pool, PAGE_SIZE, SB_HEADS, SB_HEAD_DIM)),
        'state_ssm': nrm(ks[4], (DEPTH, DEC_BATCH, SSD_HEADS, SSD_HEAD_DIM, SSD_STATE), 0.1),
        'state_conv': nrm(ks[5], (DEPTH, DEC_BATCH, CONV_WIDTH - 1, XBC_WIDTH)),
        'cache_mem_k': nrm(ks[6], (DEPTH, DEC_BATCH, MEM_TOKENS, MEM_HEADS, MEM_HEAD_DIM)),
        'cache_mem_v': nrm(ks[7], (DEPTH, DEC_BATCH, MEM_TOKENS, MEM_HEADS, MEM_HEAD_DIM)),
        'page_table': page_table.reshape(DEC_BATCH, n_pages).astype(jnp.int32),
        'mem_prompt': nrm(ks[8], (BATCH, MEM_TOKENS, D_MODEL)),
        'norm_w': gain(ks[10], (DEPTH, D_MODEL)),
        'w_in': nrm(ks[11], (DEPTH, D_MODEL, IN_WIDTH), D_MODEL ** -0.5),
        'sb_q_norm': gain(ks[12], (DEPTH, SB_HEAD_DIM)),
        'sb_k_norm': gain(ks[13], (DEPTH, SB_HEAD_DIM)),
        'sb_bias': sb_bias,
        'conv_w': nrm(ks[14], (DEPTH, CONV_WIDTH, XBC_WIDTH), CONV_WIDTH ** -0.5),
        'conv_b': nrm(ks[15], (DEPTH, XBC_WIDTH), 0.01),
        'dt_bias': dt0 + jnp.log(-jnp.expm1(-dt0)),
        'a_log': jnp.log(jax.random.uniform(ks[16], (DEPTH, SSD_HEADS), f32, 1.0, 16.0)),
        'd_skip': 1.0 + 0.1 * jax.random.normal(ks[17], (DEPTH, SSD_HEADS), f32),
        'ssd_norm_w': gain(ks[18], (DEPTH, SSD_WIDTH)),
        'mem_norm_w': gain(ks[19], (DEPTH, D_MODEL)),
        'w_mem_kv': nrm(ks[21], (DEPTH, D_MODEL, 2 * MEM_WIDTH), D_MODEL ** -0.5),
        'mem_q_norm': gain(ks[22], (DEPTH, MEM_HEAD_DIM)),
        'mem_k_norm': gain(ks[23], (DEPTH, MEM_HEAD_DIM)),
        'w_out': nrm(ks[24], (DEPTH, MIX_WIDTH, D_MODEL), MIX_WIDTH ** -0.5),
    }


def reference(x_prompt, x_sample, cache_sb_k, cache_sb_v, state_ssm, state_conv, cache_mem_k,
              cache_mem_v, page_table, mem_prompt, norm_w, w_in, sb_q_norm, sb_k_norm, sb_bias,
              conv_w, conv_b, dt_bias, a_log, d_skip, ssd_norm_w, mem_norm_w, w_mem_kv, mem_q_norm,
              mem_k_norm, w_out):
    n_seq = page_table.shape[0]
    yp, ys = x_prompt, x_sample
    pk, pv, pssm, pconv, pmk, pmv = [], [], [], [], [], []
    sk, sv, sssm, sconv = [], [], [], []
    for layer in range(DEPTH):
        mk, mv = memory_kv(mem_prompt, mem_norm_w[layer], w_mem_kv[layer], mem_k_norm[layer])
        zero_conv = jnp.zeros((yp.shape[0], CONV_WIDTH - 1, XBC_WIDTH), yp.dtype)
        zero_ssm = jnp.zeros((yp.shape[0], SSD_HEADS, SSD_HEAD_DIM, SSD_STATE), jnp.float32)
        yp, k_new, v_new, conv_new, ssm_new = mixer_layer(
            yp, None, None, zero_conv, zero_ssm, mk, mv, norm_w[layer], w_in[layer],
            sb_q_norm[layer], sb_k_norm[layer], sb_bias[layer], conv_w[layer], conv_b[layer],
            dt_bias[layer], a_log[layer], d_skip[layer], ssd_norm_w[layer], mem_q_norm[layer],
            w_out[layer])
        pk.append(k_new); pv.append(v_new); pssm.append(ssm_new); pconv.append(conv_new)
        pmk.append(mk); pmv.append(mv)
        past_k = cache_sb_k[layer][page_table].reshape(n_seq, -1, SB_HEADS, SB_HEAD_DIM)
        past_v = cache_sb_v[layer][page_table].reshape(n_seq, -1, SB_HEADS, SB_HEAD_DIM)
        ys, k_new, v_new, conv_new, ssm_new = mixer_layer(
            ys, past_k, past_v, state_conv[layer], state_ssm[layer], cache_mem_k[layer],
            cache_mem_v[layer], norm_w[layer], w_in[layer], sb_q_norm[layer], sb_k_norm[layer],
            sb_bias[layer], conv_w[layer], conv_b[layer], dt_bias[layer], a_log[layer],
            d_skip[layer], ssd_norm_w[layer], mem_q_norm[layer], w_out[layer])
        sk.append(k_new); sv.append(v_new); sssm.append(ssm_new); sconv.append(conv_new)
    return (yp, ys, jnp.stack(pk), jnp.stack(pv), jnp.stack(pssm), jnp.stack(pconv),
            jnp.stack(pmk), jnp.stack(pmv), jnp.stack(sk), jnp.stack(sv), jnp.stack(sssm),
            jnp.stack(sconv))
```

```python
import functools

import jax
import jax.numpy as jnp
from jax import lax
from jax.experimental import pallas as pl
from jax.experimental.pallas import tpu as pltpu

F32 = jnp.float32
BF16 = jnp.bfloat16

D_MODEL = 2048
SB_HEADS = 8
HEAD_DIM = 128
SB_WIDTH = SB_HEADS * HEAD_DIM
SSD_HEADS = 8
SSD_HEAD_DIM = 64
SSD_WIDTH = SSD_HEADS * SSD_HEAD_DIM
SSD_GROUPS = 2
SSD_STATE = 128
CONV_WIDTH = 4
XBC_WIDTH = SSD_WIDTH + 2 * SSD_GROUPS * SSD_STATE
MEM_TOKENS = 256
MEM_HEADS = 4
MEM_WIDTH = MEM_HEADS * HEAD_DIM
PAGE_SIZE = 128
EPS = 1e-6
ATTN_SCALE = HEAD_DIM ** -0.5

SSD_CHUNK = 128
PROJ_TN = 1024
SUBLANES = 8
VMEM_LIMIT = 56 * 1024 * 1024

_NT = (((1,), (1,)), ((), ()))


def _dot(a, b):
    return jnp.dot(a, b, preferred_element_type=F32)


def _dot_nt(a, b):
    return lax.dot_general(a, b, _NT, preferred_element_type=F32)


def _split2(x):
    hi = x.astype(BF16)
    lo = (x - hi.astype(F32)).astype(BF16)
    return hi, lo


def _split3(x):
    hi = x.astype(BF16)
    r = x - hi.astype(F32)
    mid = r.astype(BF16)
    lo = (r - mid.astype(F32)).astype(BF16)
    return hi, mid, lo


def _dot_exact_lhs(x, m):
    hi, mid, lo = _split3(x)
    return _dot(hi, m) + _dot(mid, m) + _dot(lo, m)


def _dot_exact_rhs(m, x):
    hi, mid, lo = _split3(x)
    return _dot(m, hi) + _dot(m, mid) + _dot(m, lo)


def _silu(x):
    return x * (1.0 / (1.0 + jnp.exp(-x)))


def _log_fail_and_hit(z):
    soft = jnp.log(1.0 + jnp.exp(-jnp.abs(z)))
    return jnp.minimum(-z, 0.0) - soft, jnp.minimum(z, 0.0) - soft


def _proj_kernel(plan, n_out, tm, x_ref, nw_ref, w_ref, hn_ref, *rest):
    outs = rest[:n_out]
    h_ref = rest[n_out]
    n = pl.program_id(1)

    @pl.when(n == 0)
    def _():
        rc = min(tm, 64)

        def body(r, carry):
            rows = pl.ds(pl.multiple_of(r * rc, rc), rc)
            xv = x_ref[rows, :]
            ms = jnp.mean(xv * xv, axis=-1, keepdims=True)
            h_ref[rows, :] = (xv * lax.rsqrt(ms + EPS) * nw_ref[...]).astype(BF16)
            return carry

        lax.fori_loop(0, tm // rc, body, 0)

    y = _dot(h_ref[...], w_ref[...])

    for step, segs in enumerate(plan):
        @pl.when(n == step)
        def _(segs=segs):
            for col0, width, hn_row, o32, o16 in segs:
                for c in range(0, width, HEAD_DIM):
                    yc = y[:, col0 + c:col0 + c + HEAD_DIM]
                    if hn_row is not None:
                        ms = jnp.mean(yc * yc, axis=-1, keepdims=True)
                        yc = yc * lax.rsqrt(ms + EPS) * hn_ref[hn_row:hn_row + 1, :]
                    if o32 is not None:
                        outs[o32][:, c:c + HEAD_DIM] = yc
                    if o16 is not None:
                        outs[o16][:, c:c + HEAD_DIM] = yc.astype(BF16)


def _proj(x, norm_w, w_cat, head_norms, plan, out_defs, tm):
    t, d = x.shape
    n_steps = len(plan)
    assert w_cat.shape == (d, n_steps * PROJ_TN) and t % tm == 0
    kern = functools.partial(_proj_kernel, plan, len(out_defs), tm)
    return pl.pallas_call(
        kern,
        grid=(t // tm, n_steps),
        in_specs=[
            pl.BlockSpec((tm, d), lambda m, n: (m, 0)),
            pl.BlockSpec((1, d), lambda m, n: (0, 0)),
            pl.BlockSpec((d, PROJ_TN), lambda m, n: (0, n)),
            pl.BlockSpec((SUBLANES, HEAD_DIM), lambda m, n: (0, 0)),
        ],
        out_specs=[pl.BlockSpec((tm, w), lambda m, n: (m, 0)) for w, _ in out_defs],
        out_shape=[jax.ShapeDtypeStruct((t, w), dt) for w, dt in out_defs],
        scratch_shapes=[pltpu.VMEM((tm, d), BF16)],
        compiler_params=pltpu.CompilerParams(
            dimension_semantics=("parallel", "arbitrary"), vmem_limit_bytes=VMEM_LIMIT),
        name="norm_proj",
    )(x, norm_w.reshape(1, d), w_cat, head_norms)


def _sb_block(q, ks, vs, u, bias, carry, diag):
    z = _dot_nt(q, ks) * ATTN_SCALE + bias
    lf, lh = _log_fail_and_hit(z)
    if diag is not None:
        lf = jnp.where(diag, lf, 0.0)
    hi, lo = _split2(lf)
    between = _dot(hi, u) + _dot(lo, u)
    logw = lh + between
    if carry is not None:
        logw = logw + carry
    w = jnp.exp(logw)
    if diag is not None:
        w = jnp.where(diag, w, 0.0)
    return _dot(w.astype(BF16), vs), jnp.sum(lf, axis=-1, keepdims=True)


def _sb_prompt_kernel(tq, bias_ref, q_ref, k_ref, v_ref, g_ref, u_ref, o_ref, acc_ref, c_ref):
    h = pl.program_id(1)
    i = pl.program_id(2)
    bias = bias_ref[h]
    q = q_ref[...]
    u = u_ref[...]

    row = lax.broadcasted_iota(jnp.int32, (tq, tq), 0)
    col = lax.broadcasted_iota(jnp.int32, (tq, tq), 1)
    start = pl.multiple_of(i * tq, tq)
    pv, rs = _sb_block(q, k_ref[pl.ds(start, tq), :], v_ref[pl.ds(start, tq), :], u, bias,
                       None, col < row)
    acc_ref[...] = pv
    c_ref[...] = rs

    def body(jj, carry):
        st = pl.multiple_of((i - 1 - jj) * tq, tq)
        pv, rs = _sb_block(q, k_ref[pl.ds(st, tq), :], v_ref[pl.ds(st, tq), :], u, bias,
                           c_ref[...], None)
        acc_ref[...] += pv
        c_ref[...] += rs
        return carry

    lax.fori_loop(0, i, body, 0)
    o_ref[...] = (acc_ref[...] * _silu(g_ref[...])).astype(o_ref.dtype)


def _sb_prompt(q, k, v, g, sb_bias, batch, seq, tq):
    q3, k3, v3, g3 = (a.reshape(batch, seq, SB_WIDTH) for a in (q, k, v, g))
    u = (jnp.arange(tq)[:, None] > jnp.arange(tq)[None, :]).astype(BF16)
    kern = functools.partial(_sb_prompt_kernel, tq)
    out = pl.pallas_call(
        kern,
        grid=(batch, SB_HEADS, seq // tq),
        in_specs=[
            pl.BlockSpec(memory_space=pltpu.SMEM),
            pl.BlockSpec((None, tq, HEAD_DIM), lambda b, h, i: (b, i, h)),
            pl.BlockSpec((None, seq, HEAD_DIM), lambda b, h, i: (b, 0, h)),
            pl.BlockSpec((None, seq, HEAD_DIM), lambda b, h, i: (b, 0, h)),
            pl.BlockSpec((None, tq, HEAD_DIM), lambda b, h, i: (b, i, h)),
            pl.BlockSpec((tq, tq), lambda b, h, i: (0, 0)),
        ],
        out_specs=pl.BlockSpec((None, tq, HEAD_DIM), lambda b, h, i: (b, i, h)),
        out_shape=jax.ShapeDtypeStruct((batch, seq, SB_WIDTH), BF16),
        scratch_shapes=[pltpu.VMEM((tq, HEAD_DIM), F32), pltpu.VMEM((tq, 1), F32)],
        compiler_params=pltpu.CompilerParams(
            dimension_semantics=("parallel", "parallel", "arbitrary"),
            vmem_limit_bytes=VMEM_LIMIT),
        name="sb_prompt",
    )(sb_bias, q3, k3, v3, g3, u)
    return out.reshape(batch * seq, SB_WIDTH)


def _sb_sample_kernel(t_new, n_pages, pt_ref, q_ref, kn_ref, vn_ref, kp_ref, vp_ref, g_ref,
                      bias_ref, u_ref, o_ref, acc_ref, c_ref):
    s = pl.program_id(1)
    rows = SB_HEADS * t_new
    u = u_ref[...]

    def process(kb, vb, first):
        q = q_ref[...]
        sc = jnp.concatenate(
            [_dot_nt(q[:, HEAD_DIM * h:HEAD_DIM * (h + 1)].astype(BF16),
                     kb[:, HEAD_DIM * h:HEAD_DIM * (h + 1)]) for h in range(SB_HEADS)], axis=0)
        z = sc * ATTN_SCALE + bias_ref[...]
        lf, lh = _log_fail_and_hit(z)
        if first:
            r = lax.broadcasted_iota(jnp.int32, (rows, PAGE_SIZE), 0)
            cidx = lax.broadcasted_iota(jnp.int32, (rows, PAGE_SIZE), 1)
            mask = cidx < lax.rem(r, t_new)
            lf = jnp.where(mask, lf, 0.0)
        hi, lo = _split2(lf)
        logw = lh + _dot(hi, u) + _dot(lo, u)
        if not first:
            logw = logw + c_ref[...]
        w = jnp.exp(logw)
        if first:
            w = jnp.where(mask, w, 0.0)
        pv = jnp.concatenate(
            [_dot(w[t_new * h:t_new * (h + 1), :].astype(BF16),
                  vb[:, HEAD_DIM * h:HEAD_DIM * (h + 1)]) for h in range(SB_HEADS)], axis=0)
        rs = jnp.sum(lf, axis=-1, keepdims=True)
        if first:
            acc_ref[...] = pv
            c_ref[...] = rs
        else:
            acc_ref[...] += pv
            c_ref[...] += rs

    @pl.when(s == 0)
    def _():
        process(kn_ref[...], vn_ref[...], True)

    @pl.when(s > 0)
    def _():
        kb = jnp.concatenate([kp_ref[:, h, :].astype(BF16) for h in range(SB_HEADS)], axis=1)
        vb = jnp.concatenate([vp_ref[:, h, :].astype(BF16) for h in range(SB_HEADS)], axis=1)
        process(kb, vb, False)

    @pl.when(s == n_pages)
    def _():
        g = g_ref[...]
        for h in range(SB_HEADS):
            cols = slice(HEAD_DIM * h, HEAD_DIM * (h + 1))
            o_ref[:, cols] = acc_ref[t_new * h:t_new * (h + 1), :] * _silu(g[:, cols])


def _sb_sample(q, k_new, v_new, g, cache_k, cache_v, layer, page_table, sb_bias, t_new):
    n_seq, n_pages = page_table.shape
    rows = SB_HEADS * t_new

    def pad_new(a):
        a = a.reshape(n_seq, t_new, SB_WIDTH)
        a = jnp.pad(a, ((0, 0), (0, PAGE_SIZE - t_new), (0, 0)))
        return a.reshape(n_seq * PAGE_SIZE, SB_WIDTH).astype(BF16)

    bias_rows = jnp.broadcast_to(jnp.repeat(sb_bias, t_new)[:, None], (rows, PAGE_SIZE))
    u = (jnp.arange(PAGE_SIZE)[:, None] > jnp.arange(PAGE_SIZE)[None, :]).astype(BF16)

    def page_map(b, s, pt):
        return (layer, pt[b * n_pages + (n_pages - 1) - jnp.maximum(s - 1, 0)], 0, 0, 0)

    page_spec = pl.BlockSpec((None, None, PAGE_SIZE, SB_HEADS, HEAD_DIM), page_map)

    kern = functools.partial(_sb_sample_kernel, t_new, n_pages)
    grid_spec = pltpu.PrefetchScalarGridSpec(
        num_scalar_prefetch=1,
        grid=(n_seq, n_pages + 1),
        in_specs=[
            pl.BlockSpec((t_new, SB_WIDTH), lambda b, s, pt: (b, 0)),
            pl.BlockSpec((PAGE_SIZE, SB_WIDTH), lambda b, s, pt: (b, 0)),
            pl.BlockSpec((PAGE_SIZE, SB_WIDTH), lambda b, s, pt: (b, 0)),
            page_spec,
            page_spec,
            pl.BlockSpec((t_new, SB_WIDTH), lambda b, s, pt: (b, 0)),
            pl.BlockSpec((rows, PAGE_SIZE), lambda b, s, pt: (0, 0)),
            pl.BlockSpec((PAGE_SIZE, PAGE_SIZE), lambda b, s, pt: (0, 0)),
        ],
        out_specs=pl.BlockSpec((t_new, SB_WIDTH), lambda b, s, pt: (b, 0)),
        scratch_shapes=[pltpu.VMEM((rows, HEAD_DIM), F32), pltpu.VMEM((rows, 1), F32)],
    )
    return pl.pallas_call(
        kern,
        grid_spec=grid_spec,
        out_shape=jax.ShapeDtypeStruct((n_seq * t_new, SB_WIDTH), F32),
        compiler_params=pltpu.CompilerParams(
            dimension_semantics=("parallel", "arbitrary"), vmem_limit_bytes=VMEM_LIMIT),
        name="sb_sample",
    )(page_table.reshape(-1), q, pad_new(k_new), pad_new(v_new), cache_k, cache_v, g,
      bias_rows, u)


def _ssd_kernel(length, xbc_ref, z_ref, dt_ref, pre_ref, h0_ref, cw_ref, cb_ref, dtb_ref,
                alog_ref, dsk_ref, nw_ref, ltri_ref, e_ref, out_ref, cnew_ref, snew_ref,
                ext_ref, st_ref):
    L = SSD_CHUNK
    P = SSD_HEAD_DIM
    c = pl.program_id(1)

    @pl.when(c == 0)
    def _():
        ext_ref[0:SUBLANES, :] = pre_ref[...]
        st_ref[...] = h0_ref[...]

    ext_ref[SUBLANES:SUBLANES + L, :] = xbc_ref[...]
    cw = cw_ref[...]
    conv = cb_ref[...]
    for j in range(CONV_WIDTH):
        off = SUBLANES - (CONV_WIDTH - 1) + j
        conv = conv + ext_ref[off:off + L, :] * cw[j:j + 1, :]
    act = _silu(conv)
    tail = ext_ref[length:length + SUBLANES, :]
    cnew_ref[...] = tail
    ext_ref[0:SUBLANES, :] = tail

    xs = act[:, :SSD_WIDTH]
    bm = act[:, SSD_WIDTH:SSD_WIDTH + SSD_GROUPS * SSD_STATE]
    cm = act[:, SSD_WIDTH + SSD_GROUPS * SSD_STATE:]

    x_dt = dt_ref[...] + dtb_ref[...]
    dt = jnp.maximum(x_dt, 0.0) + jnp.log1p(jnp.exp(-jnp.abs(x_dt)))
    if length < L:
        valid = lax.broadcasted_iota(jnp.int32, dt.shape, 0) < length
        dt = jnp.where(valid, dt, 0.0)
    da = dt * (-jnp.exp(alog_ref[...]))
    cs = _dot_exact_rhs(ltri_ref[...], da)
    cs_t = cs.T
    e = e_ref[...]
    dt_x = _dot_exact_lhs(dt, e)
    cs_x = _dot_exact_lhs(cs, e)
    xdt = xs * dt_x
    ecs = jnp.exp(cs_x)
    xw_t = (xdt * jnp.exp(cs_x[L - 1:L, :] - cs_x)).T
    xdt16 = xdt.astype(BF16)

    row = lax.broadcasted_iota(jnp.int32, (L, L), 0)
    col = lax.broadcasted_iota(jnp.int32, (L, L), 1)
    causal = col <= row
    heads_per_group = SSD_HEADS // SSD_GROUPS
    gw = heads_per_group * P
    y_diag, y_off = [], []
    for g in range(SSD_GROUPS):
        bg = bm[:, SSD_STATE * g:SSD_STATE * (g + 1)].astype(BF16)
        cg = cm[:, SSD_STATE * g:SSD_STATE * (g + 1)].astype(BF16)
        cb = _dot_nt(cg, bg)
        prev = st_ref[gw * g:gw * (g + 1), :]
        y_off.append(_dot_nt(cg, prev.astype(BF16)))
        new = _dot(xw_t[gw * g:gw * (g + 1), :].astype(BF16), bg)
        for r in range(heads_per_group):
            h = heads_per_group * g + r
            seg = cs[:, h:h + 1] - cs_t[h:h + 1, :]
            decay = jnp.exp(jnp.where(causal, seg, -jnp.inf))
            y_diag.append(_dot((cb * decay).astype(BF16), xdt16[:, P * h:P * (h + 1)]))
            chunk_decay = jnp.exp(cs[L - 1:L, h:h + 1])
            st_ref[P * h:P * (h + 1), :] = (prev[P * r:P * (r + 1), :] * chunk_decay
                                            + new[P * r:P * (r + 1), :])
    snew_ref[...] = st_ref[...]
    y = (jnp.concatenate(y_diag, axis=1) + jnp.concatenate(y_off, axis=1) * ecs
         + xs * dsk_ref[...])
    gated = y * _silu(z_ref[...])
    ms = jnp.mean(gated * gated, axis=-1, keepdims=True)
    out_ref[...] = (gated * lax.rsqrt(ms + EPS) * nw_ref[...]).astype(out_ref.dtype)


def _ssd(xbc, zdt, prefix, h0, conv_w, conv_b, dt_bias, a_log, d_skip, ssd_norm_w,
         batch, n_chunks, length, out_dtype):
    L = SSD_CHUNK
    pre = jnp.pad(prefix, ((0, 0), (SUBLANES - (CONV_WIDTH - 1), 0), (0, 0)))
    pad_h = lambda a: jnp.pad(a, (0, HEAD_DIM - SSD_HEADS)).reshape(1, HEAD_DIM)
    ltri = (jnp.arange(L)[:, None] >= jnp.arange(L)[None, :]).astype(BF16)
    expand = (jnp.arange(HEAD_DIM)[:, None] == jnp.arange(SSD_WIDTH)[None, :] // SSD_HEAD_DIM
              ).astype(BF16)
    dsk = jnp.repeat(d_skip, SSD_HEAD_DIM).reshape(1, SSD_WIDTH)
    z_blk = SSD_WIDTH // HEAD_DIM
    const = lambda b, c: (0, 0)
    kern = functools.partial(_ssd_kernel, length)
    return pl.pallas_call(
        kern,
        grid=(batch, n_chunks),
        in_specs=[
            pl.BlockSpec((L, XBC_WIDTH), lambda b, c: (b * n_chunks + c, 0)),
            pl.BlockSpec((L, SSD_WIDTH), lambda b, c: (b * n_chunks + c, 0)),
            pl.BlockSpec((L, HEAD_DIM), lambda b, c: (b * n_chunks + c, z_blk)),
            pl.BlockSpec((None, SUBLANES, XBC_WIDTH), lambda b, c: (b, 0, 0)),
            pl.BlockSpec((None, SSD_WIDTH, SSD_STATE), lambda b, c: (b, 0, 0)),
            pl.BlockSpec((CONV_WIDTH, XBC_WIDTH), const),
            pl.BlockSpec((1, XBC_WIDTH), const),
            pl.BlockSpec((1, HEAD_DIM), const),
            pl.BlockSpec((1, HEAD_DIM), const),
            pl.BlockSpec((1, SSD_WIDTH), const),
            pl.BlockSpec((1, SSD_WIDTH), const),
            pl.BlockSpec((L, L), const),
            pl.BlockSpec((HEAD_DIM, SSD_WIDTH), const),
        ],
        out_specs=[
            pl.BlockSpec((L, SSD_WIDTH), lambda b, c: (b * n_chunks + c, 0)),
            pl.BlockSpec((None, SUBLANES, XBC_WIDTH), lambda b, c: (b, 0, 0)),
            pl.BlockSpec((None, SSD_WIDTH, SSD_STATE), lambda b, c: (b, 0, 0)),
        ],
        out_shape=[
            jax.ShapeDtypeStruct((batch * n_chunks * L, SSD_WIDTH), out_dtype),
            jax.ShapeDtypeStruct((batch, SUBLANES, XBC_WIDTH), F32),
            jax.ShapeDtypeStruct((batch, SSD_WIDTH, SSD_STATE), F32),
        ],
        scratch_shapes=[pltpu.VMEM((SUBLANES + L, XBC_WIDTH), F32),
                        pltpu.VMEM((SSD_WIDTH, SSD_STATE), F32)],
        compiler_params=pltpu.CompilerParams(
            dimension_semantics=("parallel", "arbitrary"), vmem_limit_bytes=VMEM_LIMIT),
        name="ssd_scan",
    )(xbc, zdt, zdt, pre, h0.reshape(batch, SSD_WIDTH, SSD_STATE), conv_w,
      conv_b.reshape(1, XBC_WIDTH), pad_h(dt_bias), pad_h(a_log), dsk,
      ssd_norm_w.reshape(1, SSD_WIDTH), ltri, expand)


def _mem_attn_kernel(q_ref, g_ref, k_ref, v_ref, o_ref):
    for h in range(MEM_HEADS):
        cols = slice(HEAD_DIM * h, HEAD_DIM * (h + 1))
        s = _dot_nt(q_ref[:, cols].astype(BF16), k_ref[:, cols].astype(BF16)) * ATTN_SCALE
        p = jnp.exp(s - jnp.max(s, axis=-1, keepdims=True))
        den = jnp.sum(p, axis=-1, keepdims=True)
        o = _dot(p.astype(BF16), v_ref[:, cols].astype(BF16)) / den
        o_ref[:, cols] = (o * _silu(g_ref[:, cols])).astype(o_ref.dtype)


def _mem_attn(q, g, mem_k, mem_v, batch, t, tq, out_dtype):
    nq = t // tq
    return pl.pallas_call(
        _mem_attn_kernel,
        grid=(batch, nq),
        in_specs=[
            pl.BlockSpec((tq, MEM_WIDTH), lambda b, i: (b * nq + i, 0)),
            pl.BlockSpec((tq, MEM_WIDTH), lambda b, i: (b * nq + i, 0)),
            pl.BlockSpec((None, MEM_TOKENS, MEM_WIDTH), lambda b, i: (b, 0, 0)),
            pl.BlockSpec((None, MEM_TOKENS, MEM_WIDTH), lambda b, i: (b, 0, 0)),
        ],
        out_specs=pl.BlockSpec((tq, MEM_WIDTH), lambda b, i: (b * nq + i, 0)),
        out_shape=jax.ShapeDtypeStruct((batch * t, MEM_WIDTH), out_dtype),
        compiler_params=pltpu.CompilerParams(
            dimension_semantics=("parallel", "parallel"), vmem_limit_bytes=VMEM_LIMIT),
        name="mem_attn",
    )(q, g, mem_k, mem_v)


def _out_proj_kernel(x_ref, sb_ref, ssd_ref, mo_ref, w_ref, o_ref):
    mix = jnp.concatenate([sb_ref[...].astype(BF16), ssd_ref[...].astype(BF16),
                           mo_ref[...].astype(BF16)], axis=-1)
    o_ref[...] = x_ref[...] + _dot(mix, w_ref[...])


def _out_proj(x, sb, ssd, mo, w_out, tm, tn):
    t, d = x.shape
    return pl.pallas_call(
        _out_proj_kernel,
        grid=(t // tm, d // tn),
        in_specs=[
            pl.BlockSpec((tm, tn), lambda m, n: (m, n)),
            pl.BlockSpec((tm, SB_WIDTH), lambda m, n: (m, 0)),
            pl.BlockSpec((tm, SSD_WIDTH), lambda m, n: (m, 0)),
            pl.BlockSpec((tm, MEM_WIDTH), lambda m, n: (m, 0)),
            pl.BlockSpec((w_out.shape[0], tn), lambda m, n: (0, n)),
        ],
        out_specs=pl.BlockSpec((tm, tn), lambda m, n: (m, n)),
        out_shape=jax.ShapeDtypeStruct((t, d), F32),
        compiler_params=pltpu.CompilerParams(
            dimension_semantics=("parallel", "arbitrary"), vmem_limit_bytes=VMEM_LIMIT),
        name="out_proj",
    )(x, sb, ssd, mo, w_out)


def _in_proj_plan(act_dtype):
    lowp = act_dtype == BF16
    outs, plan = [], []

    def add(width, dtype):
        outs.append((width, dtype))
        return len(outs) - 1

    q = add(SB_WIDTH, act_dtype)
    plan.append(((0, SB_WIDTH, 0, None if lowp else q, q if lowp else None),))
    k32 = add(SB_WIDTH, F32)
    k16 = add(SB_WIDTH, BF16) if lowp else None
    plan.append(((0, SB_WIDTH, 1, k32, k16),))
    v32 = add(SB_WIDTH, F32)
    v16 = add(SB_WIDTH, BF16) if lowp else None
    plan.append(((0, SB_WIDTH, None, v32, v16),))
    g = add(SB_WIDTH, F32)
    plan.append(((0, SB_WIDTH, None, g, None),))
    xbc = add(XBC_WIDTH, F32)
    plan.append(((0, XBC_WIDTH, None, xbc, None),))
    zdt = add(PROJ_TN, F32)
    plan.append(((0, PROJ_TN, None, zdt, None),))
    mq = add(MEM_WIDTH, act_dtype)
    mg = add(MEM_WIDTH, F32)
    plan.append(((0, MEM_WIDTH, 2, None if lowp else mq, mq if lowp else None),
                 (MEM_WIDTH, MEM_WIDTH, None, mg, None)))
    names = dict(q=q, k32=k32, k16=k16, v32=v32, v16=v16, g=g, xbc=xbc, zdt=zdt, mq=mq, mg=mg)
    return tuple(plan), outs, names


def _rearranged_w_in(w):
    o_z = 4 * SB_WIDTH
    o_xbc = o_z + SSD_WIDTH
    o_dt = o_xbc + XBC_WIDTH
    o_mem = o_dt + SSD_HEADS
    pad = jnp.zeros((w.shape[0], PROJ_TN - SSD_WIDTH - SSD_HEADS), w.dtype)
    return jnp.concatenate([w[:, :o_z], w[:, o_xbc:o_dt], w[:, o_z:o_xbc], w[:, o_dt:o_mem], pad,
                            w[:, o_mem:]], axis=1).astype(BF16)


def kernel(x_prompt, x_sample, cache_sb_k, cache_sb_v, state_ssm, state_conv, cache_mem_k,
           cache_mem_v, page_table, mem_prompt, norm_w, w_in, sb_q_norm, sb_k_norm, sb_bias,
           conv_w, conv_b, dt_bias, a_log, d_skip, ssd_norm_w, mem_norm_w, w_mem_kv, mem_q_norm,
           mem_k_norm, w_out):
    depth = w_in.shape[0]
    assert depth == 1
    layer = 0
    bp, sp, d = x_prompt.shape
    bs, ts, _ = x_sample.shape
    n_pool = cache_sb_k.shape[1]
    L = SSD_CHUNK

    w_cat = _rearranged_w_in(w_in[layer])
    w_o = w_out[layer].astype(BF16)
    w_kv = w_mem_kv[layer].astype(BF16)
    head_norms = jnp.concatenate(
        [sb_q_norm[layer][None], sb_k_norm[layer][None], mem_q_norm[layer][None],
         mem_k_norm[layer][None], jnp.zeros((SUBLANES - 4, HEAD_DIM), F32)], axis=0)
    ssd_params = (conv_w[layer], conv_b[layer], dt_bias[layer], a_log[layer], d_skip[layer],
                  ssd_norm_w[layer])

    xp = x_prompt.reshape(bp * sp, d)
    mem_plan = (((0, MEM_WIDTH, 3, 0, None), (MEM_WIDTH, MEM_WIDTH, None, 1, None)),)
    mk, mv = _proj(mem_prompt.reshape(bp * MEM_TOKENS, d), mem_norm_w[layer], w_kv, head_norms,
                   mem_plan, [(MEM_WIDTH, F32), (MEM_WIDTH, F32)], tm=512)
    plan, outs, nm = _in_proj_plan(BF16)
    pr = _proj(xp, norm_w[layer], w_cat, head_norms, plan, outs, tm=512)
    sb = _sb_prompt(pr[nm['q']], pr[nm['k16']], pr[nm['v16']], pr[nm['g']], sb_bias[layer],
                    bp, sp, tq=256)
    ssd, conv_p, ssm_p = _ssd(
        pr[nm['xbc']], pr[nm['zdt']], jnp.zeros((bp, CONV_WIDTH - 1, XBC_WIDTH), F32),
        jnp.zeros((bp, SSD_HEADS, SSD_HEAD_DIM, SSD_STATE), F32), *ssd_params,
        batch=bp, n_chunks=sp // L, length=L, out_dtype=BF16)
    mo = _mem_attn(pr[nm['mq']], pr[nm['mg']], mk.reshape(bp, MEM_TOKENS, MEM_WIDTH),
                   mv.reshape(bp, MEM_TOKENS, MEM_WIDTH), bp, sp, tq=512, out_dtype=BF16)
    yp = _out_proj(xp, sb, ssd, mo, w_o, tm=512, tn=1024)

    xs = x_sample.reshape(bs * ts, d)
    plan_s, outs_s, ns = _in_proj_plan(F32)
    ps = _proj(xs, norm_w[layer], w_cat, head_norms, plan_s, outs_s, tm=bs * ts)
    sb_s = _sb_sample(ps[ns['q']], ps[ns['k32']], ps[ns['v32']], ps[ns['g']],
                      cache_sb_k, cache_sb_v, layer, page_table, sb_bias[layer], ts)

    def pad_chunk(a):
        a = a.reshape(bs, ts, a.shape[-1])
        return jnp.pad(a, ((0, 0), (0, L - ts), (0, 0))).reshape(bs * L, a.shape[-1])

    ssd_s, conv_s, ssm_s = _ssd(
        pad_chunk(ps[ns['xbc']]), pad_chunk(ps[ns['zdt']]), state_conv[layer], state_ssm[layer],
        *ssd_params, batch=bs, n_chunks=1, length=ts, out_dtype=F32)
    ssd_s = ssd_s.reshape(bs, L, SSD_WIDTH)[:, :ts].reshape(bs * ts, SSD_WIDTH)
    mo_s = _mem_attn(ps[ns['mq']], ps[ns['mg']],
                     cache_mem_k[layer].reshape(bs, MEM_TOKENS, MEM_WIDTH),
                     cache_mem_v[layer].reshape(bs, MEM_TOKENS, MEM_WIDTH), bs, ts, tq=ts,
                     out_dtype=F32)
    ys = _out_proj(xs, sb_s, ssd_s, mo_s, w_o, tm=bs * ts, tn=1024)

    tail = slice(SUBLANES - (CONV_WIDTH - 1), SUBLANES)
    return (
        yp.reshape(bp, sp, d),
        ys.reshape(bs, ts, d),
        pr[nm['k32']].reshape(1, bp, sp, SB_HEADS, HEAD_DIM),
        pr[nm['v32']].reshape(1, bp, sp, SB_HEADS, HEAD_DIM),
        ssm_p.reshape(1, bp, SSD_HEADS, SSD_HEAD_DIM, SSD_STATE),
        conv_p[:, tail][None],
        mk.reshape(1, bp, MEM_TOKENS, MEM_HEADS, HEAD_DIM),
        mv.reshape(1, bp, MEM_TOKENS, MEM_HEADS, HEAD_DIM),
        ps[ns['k32']].reshape(1, bs, ts, SB_HEADS, HEAD_DIM),
        ps[ns['v32']].reshape(1, bs, ts, SB_HEADS, HEAD_DIM),
        ssm_s.reshape(1, bs, SSD_HEADS, SSD_HEAD_DIM, SSD_STATE),
        conv_s[:, tail][None],
    )
```

```python
import functools

import jax
import jax.numpy as jnp
from jax import lax
from jax.experimental import pallas as pl
from jax.experimental.pallas import tpu as pltpu

F32 = jnp.float32
BF16 = jnp.bfloat16

D_MODEL = 2048
SB_HEADS = 8
HEAD_DIM = 128
SB_WIDTH = SB_HEADS * HEAD_DIM
SSD_HEADS = 8
SSD_HEAD_DIM = 64
SSD_WIDTH = SSD_HEADS * SSD_HEAD_DIM
SSD_GROUPS = 2
SSD_STATE = 128
CONV_WIDTH = 4
XBC_WIDTH = SSD_WIDTH + 2 * SSD_GROUPS * SSD_STATE
MEM_TOKENS = 256
MEM_HEADS = 4
MEM_WIDTH = MEM_HEADS * HEAD_DIM
PAGE_SIZE = 128
EPS = 1e-6
ATTN_SCALE = HEAD_DIM ** -0.5

SSD_CHUNK = 128
PROJ_TN = 1024
SAMPLE_PAGES_PER_STEP = 8
SUBLANES = 8
VMEM_LIMIT = 56 * 1024 * 1024

_NT = (((1,), (1,)), ((), ()))


def _dot(a, b):
    return jnp.dot(a, b, preferred_element_type=F32)


def _dot_nt(a, b):
    return lax.dot_general(a, b, _NT, preferred_element_type=F32)


def _split2(x):
    hi = x.astype(BF16)
    lo = (x - hi.astype(F32)).astype(BF16)
    return hi, lo


def _split3(x):
    hi = x.astype(BF16)
    r = x - hi.astype(F32)
    mid = r.astype(BF16)
    lo = (r - mid.astype(F32)).astype(BF16)
    return hi, mid, lo


def _dot_exact_lhs(x, m):
    hi, mid, lo = _split3(x)
    return _dot(hi, m) + _dot(mid, m) + _dot(lo, m)


def _dot_exact_rhs(m, x):
    hi, mid, lo = _split3(x)
    return _dot(m, hi) + _dot(m, mid) + _dot(m, lo)


def _silu(x):
    return x * (1.0 / (1.0 + jnp.exp(-x)))


def _log_fail_and_hit(z):
    soft = jnp.log(1.0 + jnp.exp(-jnp.abs(z)))
    log_fail = jnp.minimum(-z, 0.0) - soft
    return log_fail, log_fail + z


def _proj_kernel(plan, n_out, tm, x_ref, nw_ref, w_ref, hn_ref, *rest):
    outs = rest[:n_out]
    h_ref = rest[n_out]
    n = pl.program_id(1)

    @pl.when(n == 0)
    def _():
        rc = min(tm, 64)

        def body(r, carry):
            rows = pl.ds(pl.multiple_of(r * rc, rc), rc)
            xv = x_ref[rows, :]
            ms = jnp.mean(xv * xv, axis=-1, keepdims=True)
            h_ref[rows, :] = (xv * lax.rsqrt(ms + EPS) * nw_ref[...]).astype(BF16)
            return carry

        lax.fori_loop(0, tm // rc, body, 0)

    y = _dot(h_ref[...], w_ref[...])

    for step, segs in enumerate(plan):
        @pl.when(n == step)
        def _(segs=segs):
            for col0, width, hn_row, o32, o16 in segs:
                for c in range(0, width, HEAD_DIM):
                    yc = y[:, col0 + c:col0 + c + HEAD_DIM]
                    if hn_row is not None:
                        ms = jnp.mean(yc * yc, axis=-1, keepdims=True)
                        yc = yc * lax.rsqrt(ms + EPS) * hn_ref[hn_row:hn_row + 1, :]
                    if o32 is not None:
                        outs[o32][:, c:c + HEAD_DIM] = yc
                    if o16 is not None:
                        outs[o16][:, c:c + HEAD_DIM] = yc.astype(BF16)


def _proj(x, norm_w, w_cat, head_norms, plan, out_defs, tm):
    t, d = x.shape
    n_steps = len(plan)
    assert w_cat.shape == (d, n_steps * PROJ_TN) and t % tm == 0
    kern = functools.partial(_proj_kernel, plan, len(out_defs), tm)
    return pl.pallas_call(
        kern,
        grid=(t // tm, n_steps),
        in_specs=[
            pl.BlockSpec((tm, d), lambda m, n: (m, 0)),
            pl.BlockSpec((1, d), lambda m, n: (0, 0)),
            pl.BlockSpec((d, PROJ_TN), lambda m, n: (0, n)),
            pl.BlockSpec((SUBLANES, HEAD_DIM), lambda m, n: (0, 0)),
        ],
        out_specs=[pl.BlockSpec((tm, w), lambda m, n: (m, 0)) for w, _ in out_defs],
        out_shape=[jax.ShapeDtypeStruct((t, w), dt) for w, dt in out_defs],
        scratch_shapes=[pltpu.VMEM((tm, d), BF16)],
        compiler_params=pltpu.CompilerParams(
            dimension_semantics=("parallel", "arbitrary"), vmem_limit_bytes=VMEM_LIMIT),
        name="norm_proj",
    )(x, norm_w.reshape(1, d), w_cat, head_norms)


def _sb_scores(q, ks, uu, bias, diag):
    z = _dot_nt(q, ks) * ATTN_SCALE + bias
    lf, lh = _log_fail_and_hit(z)
    if diag is not None:
        lf = jnp.where(diag, lf, 0.0)
    hi, lo = _split2(lf)
    between = _dot(jnp.concatenate([hi, lo], axis=1), uu)
    return lf, lh, between


def _sb_prompt_kernel(tq, n_heads, bias_ref, q_ref, k_ref, v_ref, g_ref, uu_ref, o_ref, acc_ref,
                      c_ref):
    hg = pl.program_id(1)
    i = pl.program_id(2)
    uu = uu_ref[...]
    cols = [slice(HEAD_DIM * h, HEAD_DIM * (h + 1)) for h in range(n_heads)]
    biases = [bias_ref[hg * n_heads + h] for h in range(n_heads)]
    qs = [q_ref[:, cols[h]] for h in range(n_heads)]

    def block(start, diag, first):
        kb = k_ref[pl.ds(start, tq), :]
        vb = v_ref[pl.ds(start, tq), :]
        parts = [_sb_scores(qs[h], kb[:, cols[h]], uu, biases[h], diag) for h in range(n_heads)]
        for h in range(n_heads):
            lf, lh, between = parts[h]
            logw = lh + between
            if not first:
                logw = logw + c_ref[h]
            w = jnp.exp(logw)
            if diag is not None:
                w = jnp.where(diag, w, 0.0)
            pv = _dot(w.astype(BF16), vb[:, cols[h]])
            total = between[:, 0:1] + lf[:, 0:1]
            if first:
                acc_ref[:, cols[h]] = pv
                c_ref[h] = total
            else:
                acc_ref[:, cols[h]] += pv
                c_ref[h] += total

    row = lax.broadcasted_iota(jnp.int32, (tq, tq), 0)
    col = lax.broadcasted_iota(jnp.int32, (tq, tq), 1)
    block(pl.multiple_of(i * tq, tq), col < row, True)

    def body(jj, carry):
        block(pl.multiple_of((i - 1 - jj) * tq, tq), None, False)
        return carry

    lax.fori_loop(0, i, body, 0)
    o_ref[...] = (acc_ref[...] * _silu(g_ref[...])).astype(o_ref.dtype)


def _sb_prompt(q, k, v, g, sb_bias, batch, seq, tq, n_heads):
    q3, k3, v3, g3 = (a.reshape(batch, seq, SB_WIDTH) for a in (q, k, v, g))
    u = (jnp.arange(tq)[:, None] > jnp.arange(tq)[None, :]).astype(BF16)
    uu = jnp.concatenate([u, u], axis=0)
    width = n_heads * HEAD_DIM
    kern = functools.partial(_sb_prompt_kernel, tq, n_heads)
    out = pl.pallas_call(
        kern,
        grid=(batch, SB_HEADS // n_heads, seq // tq),
        in_specs=[
            pl.BlockSpec(memory_space=pltpu.SMEM),
            pl.BlockSpec((None, tq, width), lambda b, h, i: (b, i, h)),
            pl.BlockSpec((None, seq, width), lambda b, h, i: (b, 0, h)),
            pl.BlockSpec((None, seq, width), lambda b, h, i: (b, 0, h)),
            pl.BlockSpec((None, tq, width), lambda b, h, i: (b, i, h)),
            pl.BlockSpec((2 * tq, tq), lambda b, h, i: (0, 0)),
        ],
        out_specs=pl.BlockSpec((None, tq, width), lambda b, h, i: (b, i, h)),
        out_shape=jax.ShapeDtypeStruct((batch, seq, SB_WIDTH), BF16),
        scratch_shapes=[pltpu.VMEM((tq, width), F32), pltpu.VMEM((n_heads, tq, 1), F32)],
        compiler_params=pltpu.CompilerParams(
            dimension_semantics=("parallel", "parallel", "arbitrary"),
            vmem_limit_bytes=VMEM_LIMIT),
        name="sb_prompt",
    )(sb_bias, q3, k3, v3, g3, uu)
    return out.reshape(batch * seq, SB_WIDTH)


def _sb_sample_kernel(t_new, n_group, pt_ref, q_ref, kn_ref, vn_ref, *rest):
    kps, vps = rest[:n_group], rest[n_group:2 * n_group]
    g_ref, bias_ref, uo_ref, o_ref, acc_ref, c_ref = rest[2 * n_group:]
    s = pl.program_id(1)
    lanes = PAGE_SIZE * SB_HEADS
    n_blk = lanes // HEAD_DIM
    lane = lax.broadcasted_iota(jnp.int32, (t_new, lanes), 1)
    lane_head = lane & (SB_HEADS - 1)
    q = q_ref[...]
    q_all = jnp.concatenate([q[:, HEAD_DIM * h:HEAD_DIM * (h + 1)] for h in range(SB_HEADS)],
                            axis=0).astype(BF16)

    def scores(k_ref, mask):
        s_all = _dot_nt(q_all, k_ref[...].astype(BF16))
        sc = s_all[0:t_new, :]
        for h in range(1, SB_HEADS):
            sc = jnp.where(lane_head == h, s_all[t_new * h:t_new * (h + 1), :], sc)
        z = sc * ATTN_SCALE + bias_ref[...]
        lf, lh = _log_fail_and_hit(z)
        if mask is not None:
            lf = jnp.where(mask, lf, 0.0)
        blocks = jnp.concatenate([lf[:, HEAD_DIM * j:HEAD_DIM * (j + 1)] for j in range(n_blk)],
                                 axis=0)
        hi, lo = _split2(blocks)
        return lh, _dot(hi, uo_ref[...]) + _dot(lo, uo_ref[...])

    def weights(lh, res, mask, run):
        ws = [None] * n_blk
        for j in reversed(range(n_blk)):
            rows = slice(t_new * j, t_new * (j + 1))
            logw = lh[:, HEAD_DIM * j:HEAD_DIM * (j + 1)] + res[rows, :HEAD_DIM]
            if run is not None:
                logw = logw + run
            ws[j] = jnp.exp(logw)
            tot = res[rows, HEAD_DIM:]
            run = tot if run is None else run + tot
        w = jnp.concatenate(ws, axis=1)
        if mask is not None:
            w = jnp.where(mask, w, 0.0)
        w_all = jnp.concatenate([jnp.where(lane_head == h, w, 0.0) for h in range(SB_HEADS)],
                                axis=0).astype(BF16)
        return w_all, run

    @pl.when(s == 0)
    def _():
        mask = (lane >> 3) < lax.broadcasted_iota(jnp.int32, (t_new, lanes), 0)
        lh, res = scores(kn_ref, mask)
        w_all, run = weights(lh, res, mask, None)
        c_ref[...] = run
        acc_ref[...] = _dot(w_all, vn_ref[...].astype(BF16))

    @pl.when(s > 0)
    def _():
        parts = [scores(kps[j], None) for j in range(n_group)]
        run = c_ref[...]
        acc = acc_ref[...]
        for j in range(n_group):
            w_all, run = weights(*parts[j], None, run)
            acc = acc + _dot(w_all, vps[j][...].astype(BF16))
        c_ref[...] = run
        acc_ref[...] = acc

    @pl.when(s == pl.num_programs(1) - 1)
    def _():
        g = g_ref[...]
        for h in range(SB_HEADS):
            cols = slice(HEAD_DIM * h, HEAD_DIM * (h + 1))
            o_ref[:, cols] = acc_ref[t_new * h:t_new * (h + 1), :] * _silu(g[:, cols])


def _sb_sample(q, k_new, v_new, g, cache_k, cache_v, layer, page_table, sb_bias, t_new):
    n_seq, n_pages = page_table.shape
    n_pool = cache_k.shape[1]
    n_group = SAMPLE_PAGES_PER_STEP
    assert n_pages % n_group == 0
    rows = SB_HEADS * t_new
    page_rows = PAGE_SIZE * SB_HEADS

    def as_page(a):
        a = a.reshape(n_seq, t_new, SB_HEADS, HEAD_DIM)
        a = jnp.pad(a, ((0, 0), (0, PAGE_SIZE - t_new), (0, 0), (0, 0)))
        return a.reshape(n_seq * page_rows, HEAD_DIM)

    cache_k = cache_k.reshape(-1, page_rows, HEAD_DIM)
    cache_v = cache_v.reshape(-1, page_rows, HEAD_DIM)

    lane_head = jnp.arange(page_rows) % SB_HEADS
    bias_lanes = jnp.broadcast_to(sb_bias[lane_head][None, :], (t_new, page_rows))
    r = jnp.arange(HEAD_DIM)
    same_head = (r[:, None] % SB_HEADS) == (r[None, :] % SB_HEADS)
    later = (r[:, None] // SB_HEADS) > (r[None, :] // SB_HEADS)
    uo = jnp.concatenate([same_head & later, same_head], axis=1).astype(BF16)

    def page_spec(j):
        def index_map(b, s, pt):
            page = (n_pages - 1) - (jnp.maximum(s - 1, 0) * n_group + j)
            return (layer * n_pool + pt[b * n_pages + page], 0, 0)
        return pl.BlockSpec((None, page_rows, HEAD_DIM), index_map)

    pages = [page_spec(j) for j in range(n_group)]
    kern = functools.partial(_sb_sample_kernel, t_new, n_group)
    grid_spec = pltpu.PrefetchScalarGridSpec(
        num_scalar_prefetch=1,
        grid=(n_seq, n_pages // n_group + 1),
        in_specs=[
            pl.BlockSpec((t_new, SB_WIDTH), lambda b, s, pt: (b, 0)),
            pl.BlockSpec((page_rows, HEAD_DIM), lambda b, s, pt: (b, 0)),
            pl.BlockSpec((page_rows, HEAD_DIM), lambda b, s, pt: (b, 0)),
            *pages,
            *pages,
            pl.BlockSpec((t_new, SB_WIDTH), lambda b, s, pt: (b, 0)),
            pl.BlockSpec((t_new, page_rows), lambda b, s, pt: (0, 0)),
            pl.BlockSpec((HEAD_DIM, 2 * HEAD_DIM), lambda b, s, pt: (0, 0)),
        ],
        out_specs=pl.BlockSpec((t_new, SB_WIDTH), lambda b, s, pt: (b, 0)),
        scratch_shapes=[pltpu.VMEM((rows, HEAD_DIM), F32), pltpu.VMEM((t_new, HEAD_DIM), F32)],
    )
    return pl.pallas_call(
        kern,
        grid_spec=grid_spec,
        out_shape=jax.ShapeDtypeStruct((n_seq * t_new, SB_WIDTH), F32),
        compiler_params=pltpu.CompilerParams(
            dimension_semantics=("parallel", "arbitrary"), vmem_limit_bytes=VMEM_LIMIT),
        name="sb_sample",
    )(page_table.reshape(-1), q, as_page(k_new), as_page(v_new), *([cache_k] * n_group),
      *([cache_v] * n_group), g, bias_lanes, uo)


def _ssd_kernel(length, xbc_ref, z_ref, dt_ref, pre_ref, h0_ref, cw_ref, cb_ref, dtb_ref,
                alog_ref, dsk_ref, nw_ref, ltri_ref, e_ref, out_ref, cnew_ref, snew_ref,
                ext_ref, st_ref):
    L = SSD_CHUNK
    P = SSD_HEAD_DIM
    c = pl.program_id(1)

    @pl.when(c == 0)
    def _():
        ext_ref[0:SUBLANES, :] = pre_ref[...]
        st_ref[...] = h0_ref[...]

    ext_ref[SUBLANES:SUBLANES + L, :] = xbc_ref[...]
    cw = cw_ref[...]
    conv = cb_ref[...]
    for j in range(CONV_WIDTH):
        off = SUBLANES - (CONV_WIDTH - 1) + j
        conv = conv + ext_ref[off:off + L, :] * cw[j:j + 1, :]
    act = _silu(conv)
    tail = ext_ref[length:length + SUBLANES, :]
    cnew_ref[...] = tail
    ext_ref[0:SUBLANES, :] = tail

    xs = act[:, :SSD_WIDTH]
    bm = act[:, SSD_WIDTH:SSD_WIDTH + SSD_GROUPS * SSD_STATE]
    cm = act[:, SSD_WIDTH + SSD_GROUPS * SSD_STATE:]

    x_dt = dt_ref[...] + dtb_ref[...]
    dt = jnp.maximum(x_dt, 0.0) + jnp.log1p(jnp.exp(-jnp.abs(x_dt)))
    if length < L:
        valid = lax.broadcasted_iota(jnp.int32, dt.shape, 0) < length
        dt = jnp.where(valid, dt, 0.0)
    da = dt * (-jnp.exp(alog_ref[...]))
    cs = _dot_exact_rhs(ltri_ref[...], da)
    cs_t = cs.T
    e = e_ref[...]
    dt_x = _dot_exact_lhs(dt, e)
    cs_x = _dot_exact_lhs(cs, e)
    xdt = xs * dt_x
    ecs = jnp.exp(cs_x)
    xw_t = (xdt * jnp.exp(cs_x[L - 1:L, :] - cs_x)).T
    xdt16 = xdt.astype(BF16)

    row = lax.broadcasted_iota(jnp.int32, (L, L), 0)
    col = lax.broadcasted_iota(jnp.int32, (L, L), 1)
    causal = col <= row
    heads_per_group = SSD_HEADS // SSD_GROUPS
    gw = heads_per_group * P
    y_diag, y_off = [], []
    for g in range(SSD_GROUPS):
        bg = bm[:, SSD_STATE * g:SSD_STATE * (g + 1)].astype(BF16)
        cg = cm[:, SSD_STATE * g:SSD_STATE * (g + 1)].astype(BF16)
        cb = _dot_nt(cg, bg)
        prev = st_ref[gw * g:gw * (g + 1), :]
        y_off.append(_dot_nt(cg, prev.astype(BF16)))
        new = _dot(xw_t[gw * g:gw * (g + 1), :].astype(BF16), bg)
        for r in range(heads_per_group):
            h = heads_per_group * g + r
            seg = cs[:, h:h + 1] - cs_t[h:h + 1, :]
            decay = jnp.exp(jnp.where(causal, seg, -jnp.inf))
            y_diag.append(_dot((cb * decay).astype(BF16), xdt16[:, P * h:P * (h + 1)]))
            chunk_decay = jnp.exp(cs[L - 1:L, h:h + 1])
            st_ref[P * h:P * (h + 1), :] = (prev[P * r:P * (r + 1), :] * chunk_decay
                                            + new[P * r:P * (r + 1), :])
    snew_ref[...] = st_ref[...]
    y = (jnp.concatenate(y_diag, axis=1) + jnp.concatenate(y_off, axis=1) * ecs
         + xs * dsk_ref[...])
    gated = y * _silu(z_ref[...])
    ms = jnp.mean(gated * gated, axis=-1, keepdims=True)
    out_ref[...] = (gated * lax.rsqrt(ms + EPS) * nw_ref[...]).astype(out_ref.dtype)


def _ssd(xbc, zdt, prefix, h0, conv_w, conv_b, dt_bias, a_log, d_skip, ssd_norm_w,
         batch, n_chunks, length, out_dtype):
    L = SSD_CHUNK
    pre = jnp.pad(prefix, ((0, 0), (SUBLANES - (CONV_WIDTH - 1), 0), (0, 0)))
    pad_h = lambda a: jnp.pad(a, (0, HEAD_DIM - SSD_HEADS)).reshape(1, HEAD_DIM)
    ltri = (jnp.arange(L)[:, None] >= jnp.arange(L)[None, :]).astype(BF16)
    expand = (jnp.arange(HEAD_DIM)[:, None] == jnp.arange(SSD_WIDTH)[None, :] // SSD_HEAD_DIM
              ).astype(BF16)
    dsk = jnp.repeat(d_skip, SSD_HEAD_DIM).reshape(1, SSD_WIDTH)
    z_blk = SSD_WIDTH // HEAD_DIM
    const = lambda b, c: (0, 0)
    kern = functools.partial(_ssd_kernel, length)
    return pl.pallas_call(
        kern,
        grid=(batch, n_chunks),
        in_specs=[
            pl.BlockSpec((L, XBC_WIDTH), lambda b, c: (b * n_chunks + c, 0)),
            pl.BlockSpec((L, SSD_WIDTH), lambda b, c: (b * n_chunks + c, 0)),
            pl.BlockSpec((L, HEAD_DIM), lambda b, c: (b * n_chunks + c, z_blk)),
            pl.BlockSpec((None, SUBLANES, XBC_WIDTH), lambda b, c: (b, 0, 0)),
            pl.BlockSpec((None, SSD_WIDTH, SSD_STATE), lambda b, c: (b, 0, 0)),
            pl.BlockSpec((CONV_WIDTH, XBC_WIDTH), const),
            pl.BlockSpec((1, XBC_WIDTH), const),
            pl.BlockSpec((1, HEAD_DIM), const),
            pl.BlockSpec((1, HEAD_DIM), const),
            pl.BlockSpec((1, SSD_WIDTH), const),
            pl.BlockSpec((1, SSD_WIDTH), const),
            pl.BlockSpec((L, L), const),
            pl.BlockSpec((HEAD_DIM, SSD_WIDTH), const),
        ],
        out_specs=[
            pl.BlockSpec((L, SSD_WIDTH), lambda b, c: (b * n_chunks + c, 0)),
            pl.BlockSpec((None, SUBLANES, XBC_WIDTH), lambda b, c: (b, 0, 0)),
            pl.BlockSpec((None, SSD_WIDTH, SSD_STATE), lambda b, c: (b, 0, 0)),
        ],
        out_shape=[
            jax.ShapeDtypeStruct((batch * n_chunks * L, SSD_WIDTH), out_dtype),
            jax.ShapeDtypeStruct((batch, SUBLANES, XBC_WIDTH), F32),
            jax.ShapeDtypeStruct((batch, SSD_WIDTH, SSD_STATE), F32),
        ],
        scratch_shapes=[pltpu.VMEM((SUBLANES + L, XBC_WIDTH), F32),
                        pltpu.VMEM((SSD_WIDTH, SSD_STATE), F32)],
        compiler_params=pltpu.CompilerParams(
            dimension_semantics=("parallel", "arbitrary"), vmem_limit_bytes=VMEM_LIMIT),
        name="ssd_scan",
    )(xbc, zdt, zdt, pre, h0.reshape(batch, SSD_WIDTH, SSD_STATE), conv_w,
      conv_b.reshape(1, XBC_WIDTH), pad_h(dt_bias), pad_h(a_log), dsk,
      ssd_norm_w.reshape(1, SSD_WIDTH), ltri, expand)


def _mem_attn_kernel(q_ref, g_ref, k_ref, v_ref, o_ref):
    for h in range(MEM_HEADS):
        cols = slice(HEAD_DIM * h, HEAD_DIM * (h + 1))
        s = _dot_nt(q_ref[:, cols].astype(BF16), k_ref[:, cols].astype(BF16)) * ATTN_SCALE
        p = jnp.exp(s - jnp.max(s, axis=-1, keepdims=True))
        den = jnp.sum(p, axis=-1, keepdims=True)
        o = _dot(p.astype(BF16), v_ref[:, cols].astype(BF16)) / den
        o_ref[:, cols] = (o * _silu(g_ref[:, cols])).astype(o_ref.dtype)


def _mem_attn(q, g, mem_k, mem_v, batch, t, tq, out_dtype):
    nq = t // tq
    return pl.pallas_call(
        _mem_attn_kernel,
        grid=(batch, nq),
        in_specs=[
            pl.BlockSpec((tq, MEM_WIDTH), lambda b, i: (b * nq + i, 0)),
            pl.BlockSpec((tq, MEM_WIDTH), lambda b, i: (b * nq + i, 0)),
            pl.BlockSpec((None, MEM_TOKENS, MEM_WIDTH), lambda b, i: (b, 0, 0)),
            pl.BlockSpec((None, MEM_TOKENS, MEM_WIDTH), lambda b, i: (b, 0, 0)),
        ],
        out_specs=pl.BlockSpec((tq, MEM_WIDTH), lambda b, i: (b * nq + i, 0)),
        out_shape=jax.ShapeDtypeStruct((batch * t, MEM_WIDTH), out_dtype),
        compiler_params=pltpu.CompilerParams(
            dimension_semantics=("parallel", "parallel"), vmem_limit_bytes=VMEM_LIMIT),
        name="mem_attn",
    )(q, g, mem_k, mem_v)


def _out_proj_kernel(x_ref, sb_ref, ssd_ref, mo_ref, w_ref, o_ref):
    mix = jnp.concatenate([sb_ref[...].astype(BF16), ssd_ref[...].astype(BF16),
                           mo_ref[...].astype(BF16)], axis=-1)
    o_ref[...] = x_ref[...] + _dot(mix, w_ref[...])


def _out_proj(x, sb, ssd, mo, w_out, tm, tn):
    t, d = x.shape
    return pl.pallas_call(
        _out_proj_kernel,
        grid=(t // tm, d // tn),
        in_specs=[
            pl.BlockSpec((tm, tn), lambda m, n: (m, n)),
            pl.BlockSpec((tm, SB_WIDTH), lambda m, n: (m, 0)),
            pl.BlockSpec((tm, SSD_WIDTH), lambda m, n: (m, 0)),
            pl.BlockSpec((tm, MEM_WIDTH), lambda m, n: (m, 0)),
            pl.BlockSpec((w_out.shape[0], tn), lambda m, n: (0, n)),
        ],
        out_specs=pl.BlockSpec((tm, tn), lambda m, n: (m, n)),
        out_shape=jax.ShapeDtypeStruct((t, d), F32),
        compiler_params=pltpu.CompilerParams(
            dimension_semantics=("parallel", "arbitrary"), vmem_limit_bytes=VMEM_LIMIT),
        name="out_proj",
    )(x, sb, ssd, mo, w_out)


def _in_proj_plan(act_dtype):
    lowp = act_dtype == BF16
    outs, plan = [], []

    def add(width, dtype):
        outs.append((width, dtype))
        return len(outs) - 1

    q = add(SB_WIDTH, act_dtype)
    plan.append(((0, SB_WIDTH, 0, None if lowp else q, q if lowp else None),))
    k32 = add(SB_WIDTH, F32)
    k16 = add(SB_WIDTH, BF16) if lowp else None
    plan.append(((0, SB_WIDTH, 1, k32, k16),))
    v32 = add(SB_WIDTH, F32)
    v16 = add(SB_WIDTH, BF16) if lowp else None
    plan.append(((0, SB_WIDTH, None, v32, v16),))
    g = add(SB_WIDTH, F32)
    plan.append(((0, SB_WIDTH, None, g, None),))
    xbc = add(XBC_WIDTH, F32)
    plan.append(((0, XBC_WIDTH, None, xbc, None),))
    zdt = add(PROJ_TN, F32)
    plan.append(((0, PROJ_TN, None, zdt, None),))
    mq = add(MEM_WIDTH, act_dtype)
    mg = add(MEM_WIDTH, F32)
    plan.append(((0, MEM_WIDTH, 2, None if lowp else mq, mq if lowp else None),
                 (MEM_WIDTH, MEM_WIDTH, None, mg, None)))
    names = dict(q=q, k32=k32, k16=k16, v32=v32, v16=v16, g=g, xbc=xbc, zdt=zdt, mq=mq, mg=mg)
    return tuple(plan), outs, names


def _rearranged_w_in(w):
    o_z = 4 * SB_WIDTH
    o_xbc = o_z + SSD_WIDTH
    o_dt = o_xbc + XBC_WIDTH
    o_mem = o_dt + SSD_HEADS
    pad = jnp.zeros((w.shape[0], PROJ_TN - SSD_WIDTH - SSD_HEADS), w.dtype)
    return jnp.concatenate([w[:, :o_z], w[:, o_xbc:o_dt], w[:, o_z:o_xbc], w[:, o_dt:o_mem], pad,
                            w[:, o_mem:]], axis=1).astype(BF16)


def kernel(x_prompt, x_sample, cache_sb_k, cache_sb_v, state_ssm, state_conv, cache_mem_k,
           cache_mem_v, page_table, mem_prompt, norm_w, w_in, sb_q_norm, sb_k_norm, sb_bias,
           conv_w, conv_b, dt_bias, a_log, d_skip, ssd_norm_w, mem_norm_w, w_mem_kv, mem_q_norm,
           mem_k_norm, w_out):
    depth = w_in.shape[0]
    assert depth == 1
    layer = 0
    bp, sp, d = x_prompt.shape
    bs, ts, _ = x_sample.shape
    n_pool = cache_sb_k.shape[1]
    L = SSD_CHUNK

    w_cat = _rearranged_w_in(w_in[layer])
    w_o = w_out[layer].astype(BF16)
    w_kv = w_mem_kv[layer].astype(BF16)
    head_norms = jnp.concatenate(
        [sb_q_norm[layer][None], sb_k_norm[layer][None], mem_q_norm[layer][None],
         mem_k_norm[layer][None], jnp.zeros((SUBLANES - 4, HEAD_DIM), F32)], axis=0)
    ssd_params = (conv_w[layer], conv_b[layer], dt_bias[layer], a_log[layer], d_skip[layer],
                  ssd_norm_w[layer])

    xp = x_prompt.reshape(bp * sp, d)
    mem_plan = (((0, MEM_WIDTH, 3, 0, None), (MEM_WIDTH, MEM_WIDTH, None, 1, None)),)
    mk, mv = _proj(mem_prompt.reshape(bp * MEM_TOKENS, d), mem_norm_w[layer], w_kv, head_norms,
                   mem_plan, [(MEM_WIDTH, F32), (MEM_WIDTH, F32)], tm=512)
    plan, outs, nm = _in_proj_plan(BF16)
    pr = _proj(xp, norm_w[layer], w_cat, head_norms, plan, outs, tm=512)
    sb = _sb_prompt(pr[nm['q']], pr[nm['k16']], pr[nm['v16']], pr[nm['g']], sb_bias[layer],
                    bp, sp, tq=256, n_heads=4)
    ssd, conv_p, ssm_p = _ssd(
        pr[nm['xbc']], pr[nm['zdt']], jnp.zeros((bp, CONV_WIDTH - 1, XBC_WIDTH), F32),
        jnp.zeros((bp, SSD_HEADS, SSD_HEAD_DIM, SSD_STATE), F32), *ssd_params,
        batch=bp, n_chunks=sp // L, length=L, out_dtype=BF16)
    mo = _mem_attn(pr[nm['mq']], pr[nm['mg']], mk.reshape(bp, MEM_TOKENS, MEM_WIDTH),
                   mv.reshape(bp, MEM_TOKENS, MEM_WIDTH), bp, sp, tq=512, out_dtype=BF16)
    yp = _out_proj(xp, sb, ssd, mo, w_o, tm=512, tn=1024)

    xs = x_sample.reshape(bs * ts, d)
    plan_s, outs_s, ns = _in_proj_plan(F32)
    ps = _proj(xs, norm_w[layer], w_cat, head_norms, plan_s, outs_s, tm=bs * ts)
    sb_s = _sb_sample(ps[ns['q']], ps[ns['k32']], ps[ns['v32']], ps[ns['g']],
                      cache_sb_k, cache_sb_v, layer, page_table, sb_bias[layer], ts)

    def pad_chunk(a):
        a = a.reshape(bs, ts, a.shape[-1])
        return jnp.pad(a, ((0, 0), (0, L - ts), (0, 0))).reshape(bs * L, a.shape[-1])

    ssd_s, conv_s, ssm_s = _ssd(
        pad_chunk(ps[ns['xbc']]), pad_chunk(ps[ns['zdt']]), state_conv[layer], state_ssm[layer],
        *ssd_params, batch=bs, n_chunks=1, length=ts, out_dtype=F32)
    ssd_s = ssd_s.reshape(bs, L, SSD_WIDTH)[:, :ts].reshape(bs * ts, SSD_WIDTH)
    mo_s = _mem_attn(ps[ns['mq']], ps[ns['mg']],
                     cache_mem_k[layer].reshape(bs, MEM_TOKENS, MEM_WIDTH),
                     cache_mem_v[layer].reshape(bs, MEM_TOKENS, MEM_WIDTH), bs, ts, tq=ts,
                     out_dtype=F32)
    ys = _out_proj(xs, sb_s, ssd_s, mo_s, w_o, tm=bs * ts, tn=1024)

    tail = slice(SUBLANES - (CONV_WIDTH - 1), SUBLANES)
    return (
        yp.reshape(bp, sp, d),
        ys.reshape(bs, ts, d),
        pr[nm['k32']].reshape(1, bp, sp, SB_HEADS, HEAD_DIM),
        pr[nm['v32']].reshape(1, bp, sp, SB_HEADS, HEAD_DIM),
        ssm_p.reshape(1, bp, SSD_HEADS, SSD_HEAD_DIM, SSD_STATE),
        conv_p[:, tail][None],
        mk.reshape(1, bp, MEM_TOKENS, MEM_HEADS, HEAD_DIM),
        mv.reshape(1, bp, MEM_TOKENS, MEM_HEADS, HEAD_DIM),
        ps[ns['k32']].reshape(1, bs, ts, SB_HEADS, HEAD_DIM),
        ps[ns['v32']].reshape(1, bs, ts, SB_HEADS, HEAD_DIM),
        ssm_s.reshape(1, bs, SSD_HEADS, SSD_HEAD_DIM, SSD_STATE),
        conv_s[:, tail][None],
    )
```

```python
import functools
import math

import jax
import jax.numpy as jnp
from jax import lax
from jax.experimental import pallas as pl
from jax.experimental.pallas import tpu as pltpu

F32 = jnp.float32
BF16 = jnp.bfloat16

D_MODEL = 2048
SB_HEADS = 8
HEAD_DIM = 128
SB_WIDTH = SB_HEADS * HEAD_DIM
SSD_HEADS = 8
SSD_HEAD_DIM = 64
SSD_WIDTH = SSD_HEADS * SSD_HEAD_DIM
SSD_GROUPS = 2
SSD_STATE = 128
CONV_WIDTH = 4
XBC_WIDTH = SSD_WIDTH + 2 * SSD_GROUPS * SSD_STATE
MEM_TOKENS = 256
MEM_HEADS = 4
MEM_WIDTH = MEM_HEADS * HEAD_DIM
PAGE_SIZE = 128
EPS = 1e-6
ATTN_SCALE = HEAD_DIM ** -0.5
LOG2E = math.log2(math.e)

SSD_CHUNK = 128
PROJ_TN = 1024
PROJ_CHUNK = 256
SAMPLE_PAGES_PER_STEP = 8
SUBLANES = 8
VMEM_LIMIT = 56 * 1024 * 1024

_NT = (((1,), (1,)), ((), ()))


def _dot(a, b):
    return jnp.dot(a, b, preferred_element_type=F32)


def _dot_nt(a, b):
    return lax.dot_general(a, b, _NT, preferred_element_type=F32)


def _split2(x):
    hi = x.astype(BF16)
    lo = (x - hi.astype(F32)).astype(BF16)
    return hi, lo


def _split3(x):
    hi = x.astype(BF16)
    r = x - hi.astype(F32)
    mid = r.astype(BF16)
    lo = (r - mid.astype(F32)).astype(BF16)
    return hi, mid, lo


def _dot_exact_lhs(x, m):
    hi, mid, lo = _split3(x)
    return _dot(hi, m) + _dot(mid, m) + _dot(lo, m)


def _dot_exact_rhs(m, x):
    hi, mid, lo = _split3(x)
    return _dot(m, hi) + _dot(m, mid) + _dot(m, lo)


def _silu(x):
    return x * (1.0 / (1.0 + jnp.exp(-x)))


def _log2_fail_and_hit(z2):
    nz = -z2
    soft = jnp.log(1.0 + jnp.exp2(jnp.minimum(z2, nz))) * LOG2E
    log_fail = jnp.minimum(nz, 0.0) - soft
    return log_fail, log_fail + z2


def _proj_kernel(plan, n_out, tm, x_ref, nw_ref, w_ref, hn_ref, *rest):
    outs = rest[:n_out]
    h_ref = rest[n_out]
    n = pl.program_id(1)

    @pl.when(n == 0)
    def _():
        rc = min(tm, 64)

        def body(r, carry):
            rows = pl.ds(pl.multiple_of(r * rc, rc), rc)
            xv = x_ref[rows, :]
            ms = jnp.mean(xv * xv, axis=-1, keepdims=True)
            h_ref[rows, :] = (xv * lax.rsqrt(ms + EPS) * nw_ref[...]).astype(BF16)
            return carry

        lax.fori_loop(0, tm // rc, body, 0)

    for step, segs in enumerate(plan):
        @pl.when(n == step)
        def _(segs=segs):
            for col0, width, hn_row, o32, o16 in segs:
                for c0 in range(0, width, PROJ_CHUNK):
                    y = _dot(h_ref[...], w_ref[:, col0 + c0:col0 + c0 + PROJ_CHUNK])
                    for c in range(0, PROJ_CHUNK, HEAD_DIM):
                        yc = y[:, c:c + HEAD_DIM]
                        if hn_row is not None:
                            ms = jnp.mean(yc * yc, axis=-1, keepdims=True)
                            yc = yc * lax.rsqrt(ms + EPS) * hn_ref[hn_row:hn_row + 1, :]
                        cols = slice(c0 + c, c0 + c + HEAD_DIM)
                        if o32 is not None:
                            outs[o32][:, cols] = yc
                        if o16 is not None:
                            outs[o16][:, cols] = yc.astype(BF16)


def _proj(x, norm_w, w_cat, head_norms, plan, out_defs, tm):
    t, d = x.shape
    n_steps = len(plan)
    assert w_cat.shape == (d, n_steps * PROJ_TN) and t % tm == 0
    kern = functools.partial(_proj_kernel, plan, len(out_defs), tm)
    return pl.pallas_call(
        kern,
        grid=(t // tm, n_steps),
        in_specs=[
            pl.BlockSpec((tm, d), lambda m, n: (m, 0)),
            pl.BlockSpec((1, d), lambda m, n: (0, 0)),
            pl.BlockSpec((d, PROJ_TN), lambda m, n: (0, n)),
            pl.BlockSpec((SUBLANES, HEAD_DIM), lambda m, n: (0, 0)),
        ],
        out_specs=[pl.BlockSpec((tm, w), lambda m, n: (m, 0)) for w, _ in out_defs],
        out_shape=[jax.ShapeDtypeStruct((t, w), dt) for w, dt in out_defs],
        scratch_shapes=[pltpu.VMEM((tm, d), BF16)],
        compiler_params=pltpu.CompilerParams(
            dimension_semantics=("parallel", "arbitrary"), vmem_limit_bytes=VMEM_LIMIT),
        name="norm_proj",
    )(x, norm_w.reshape(1, d), w_cat, head_norms)


def _sb_scores(q, ks, uu, bias2, diag):
    lf, lh = _log2_fail_and_hit(_dot_nt(q, ks) * (ATTN_SCALE * LOG2E) + bias2)
    if diag is not None:
        lf = jnp.where(diag, lf, 0.0)
    hi, lo = _split2(lf)
    between = _dot(jnp.concatenate([hi, lo], axis=1), uu)
    return lf, lh, between


def _sb_prompt_kernel(tq, n_heads, bias_ref, q_ref, k_ref, v_ref, g_ref, uu_ref, o_ref, acc_ref,
                      c_ref):
    hg = pl.program_id(1)
    i = pl.program_id(2)
    uu = uu_ref[...]
    cols = [slice(HEAD_DIM * h, HEAD_DIM * (h + 1)) for h in range(n_heads)]
    biases = [bias_ref[hg * n_heads + h] * LOG2E for h in range(n_heads)]
    qs = [q_ref[:, cols[h]] for h in range(n_heads)]

    def block(start, diag, first):
        kb = k_ref[pl.ds(start, tq), :]
        vb = v_ref[pl.ds(start, tq), :]
        parts = [_sb_scores(qs[h], kb[:, cols[h]], uu, biases[h], diag) for h in range(n_heads)]
        for h in range(n_heads):
            lf, lh, between = parts[h]
            logw = lh + between
            if not first:
                logw = logw + c_ref[h]
            w = jnp.exp2(logw)
            if diag is not None:
                w = jnp.where(diag, w, 0.0)
            pv = _dot(w.astype(BF16), vb[:, cols[h]])
            total = between[:, 0:1] + lf[:, 0:1]
            if first:
                acc_ref[:, cols[h]] = pv
                c_ref[h] = total
            else:
                acc_ref[:, cols[h]] += pv
                c_ref[h] += total

    row = lax.broadcasted_iota(jnp.int32, (tq, tq), 0)
    col = lax.broadcasted_iota(jnp.int32, (tq, tq), 1)
    block(pl.multiple_of(i * tq, tq), col < row, True)

    def body(jj, carry):
        block(pl.multiple_of((i - 1 - jj) * tq, tq), None, False)
        return carry

    lax.fori_loop(0, i, body, 0)
    o_ref[...] = (acc_ref[...] * _silu(g_ref[...])).astype(o_ref.dtype)


def _sb_prompt(q, k, v, g, sb_bias, batch, seq, tq, n_heads):
    q3, k3, v3, g3 = (a.reshape(batch, seq, SB_WIDTH) for a in (q, k, v, g))
    u = (jnp.arange(tq)[:, None] > jnp.arange(tq)[None, :]).astype(BF16)
    uu = jnp.concatenate([u, u], axis=0)
    width = n_heads * HEAD_DIM
    kern = functools.partial(_sb_prompt_kernel, tq, n_heads)
    out = pl.pallas_call(
        kern,
        grid=(batch, SB_HEADS // n_heads, seq // tq),
        in_specs=[
            pl.BlockSpec(memory_space=pltpu.SMEM),
            pl.BlockSpec((None, tq, width), lambda b, h, i: (b, i, h)),
            pl.BlockSpec((None, seq, width), lambda b, h, i: (b, 0, h)),
            pl.BlockSpec((None, seq, width), lambda b, h, i: (b, 0, h)),
            pl.BlockSpec((None, tq, width), lambda b, h, i: (b, i, h)),
            pl.BlockSpec((2 * tq, tq), lambda b, h, i: (0, 0)),
        ],
        out_specs=pl.BlockSpec((None, tq, width), lambda b, h, i: (b, i, h)),
        out_shape=jax.ShapeDtypeStruct((batch, seq, SB_WIDTH), BF16),
        scratch_shapes=[pltpu.VMEM((tq, width), F32), pltpu.VMEM((n_heads, tq, 1), F32)],
        compiler_params=pltpu.CompilerParams(
            dimension_semantics=("parallel", "parallel", "arbitrary"),
            vmem_limit_bytes=VMEM_LIMIT),
        name="sb_prompt",
    )(sb_bias, q3, k3, v3, g3, uu)
    return out.reshape(batch * seq, SB_WIDTH)


def _sb_sample_kernel(t_new, n_group, pt_ref, q_ref, kn_ref, vn_ref, *rest):
    kps, vps = rest[:n_group], rest[n_group:2 * n_group]
    g_ref, bias_ref, uo_ref, o_ref, acc_ref, c_ref = rest[2 * n_group:]
    s = pl.program_id(1)
    lanes = PAGE_SIZE * SB_HEADS
    n_blk = lanes // HEAD_DIM
    lane = lax.broadcasted_iota(jnp.int32, (t_new, lanes), 1)
    lane_head = lane & (SB_HEADS - 1)
    q = q_ref[...]
    q_all = jnp.concatenate([q[:, HEAD_DIM * h:HEAD_DIM * (h + 1)] for h in range(SB_HEADS)],
                            axis=0).astype(BF16)

    def scores(k_ref, mask):
        s_all = _dot_nt(q_all, k_ref[...].astype(BF16))
        sc = s_all[0:t_new, :]
        for h in range(1, SB_HEADS):
            sc = jnp.where(lane_head == h, s_all[t_new * h:t_new * (h + 1), :], sc)
        lf, lh = _log2_fail_and_hit(sc * (ATTN_SCALE * LOG2E) + bias_ref[...])
        if mask is not None:
            lf = jnp.where(mask, lf, 0.0)
        blocks = jnp.concatenate([lf[:, HEAD_DIM * j:HEAD_DIM * (j + 1)] for j in range(n_blk)],
                                 axis=0)
        hi, lo = _split2(blocks)
        return lh, _dot(hi, uo_ref[...]) + _dot(lo, uo_ref[...])

    def weights(lh, res, mask, run):
        ws = [None] * n_blk
        for j in reversed(range(n_blk)):
            rows = slice(t_new * j, t_new * (j + 1))
            logw = lh[:, HEAD_DIM * j:HEAD_DIM * (j + 1)] + res[rows, :HEAD_DIM]
            if run is not None:
                logw = logw + run
            ws[j] = jnp.exp2(logw)
            tot = res[rows, HEAD_DIM:]
            run = tot if run is None else run + tot
        w = jnp.concatenate(ws, axis=1)
        if mask is not None:
            w = jnp.where(mask, w, 0.0)
        w_all = jnp.concatenate([jnp.where(lane_head == h, w, 0.0) for h in range(SB_HEADS)],
                                axis=0).astype(BF16)
        return w_all, run

    @pl.when(s == 0)
    def _():
        mask = (lane >> 3) < lax.broadcasted_iota(jnp.int32, (t_new, lanes), 0)
        lh, res = scores(kn_ref, mask)
        w_all, run = weights(lh, res, mask, None)
        c_ref[...] = run
        acc_ref[...] = _dot(w_all, vn_ref[...].astype(BF16))

    @pl.when(s > 0)
    def _():
        parts = [scores(kps[j], None) for j in range(n_group)]
        run = c_ref[...]
        acc = acc_ref[...]
        for j in range(n_group):
            w_all, run = weights(*parts[j], None, run)
            acc = acc + _dot(w_all, vps[j][...].astype(BF16))
        c_ref[...] = run
        acc_ref[...] = acc

    @pl.when(s == pl.num_programs(1) - 1)
    def _():
        g = g_ref[...]
        for h in range(SB_HEADS):
            cols = slice(HEAD_DIM * h, HEAD_DIM * (h + 1))
            o_ref[:, cols] = acc_ref[t_new * h:t_new * (h + 1), :] * _silu(g[:, cols])


def _sb_sample(q, k_new, v_new, g, cache_k, cache_v, layer, page_table, sb_bias, t_new):
    n_seq, n_pages = page_table.shape
    n_pool = cache_k.shape[1]
    n_group = SAMPLE_PAGES_PER_STEP
    assert n_pages % n_group == 0
    rows = SB_HEADS * t_new
    page_rows = PAGE_SIZE * SB_HEADS

    def as_page(a):
        a = a.reshape(n_seq, t_new, SB_HEADS, HEAD_DIM)
        a = jnp.pad(a, ((0, 0), (0, PAGE_SIZE - t_new), (0, 0), (0, 0)))
        return a.reshape(n_seq * page_rows, HEAD_DIM)

    cache_k = cache_k.reshape(-1, page_rows, HEAD_DIM)
    cache_v = cache_v.reshape(-1, page_rows, HEAD_DIM)

    lane_head = jnp.arange(page_rows) % SB_HEADS
    bias_lanes = jnp.broadcast_to((sb_bias * LOG2E)[lane_head][None, :], (t_new, page_rows))
    r = jnp.arange(HEAD_DIM)
    same_head = (r[:, None] % SB_HEADS) == (r[None, :] % SB_HEADS)
    later = (r[:, None] // SB_HEADS) > (r[None, :] // SB_HEADS)
    uo = jnp.concatenate([same_head & later, same_head], axis=1).astype(BF16)

    def page_spec(j):
        def index_map(b, s, pt):
            page = (n_pages - 1) - (jnp.maximum(s - 1, 0) * n_group + j)
            return (layer * n_pool + pt[b * n_pages + page], 0, 0)
        return pl.BlockSpec((None, page_rows, HEAD_DIM), index_map)

    pages = [page_spec(j) for j in range(n_group)]
    kern = functools.partial(_sb_sample_kernel, t_new, n_group)
    grid_spec = pltpu.PrefetchScalarGridSpec(
        num_scalar_prefetch=1,
        grid=(n_seq, n_pages // n_group + 1),
        in_specs=[
            pl.BlockSpec((t_new, SB_WIDTH), lambda b, s, pt: (b, 0)),
            pl.BlockSpec((page_rows, HEAD_DIM), lambda b, s, pt: (b, 0)),
            pl.BlockSpec((page_rows, HEAD_DIM), lambda b, s, pt: (b, 0)),
            *pages,
            *pages,
            pl.BlockSpec((t_new, SB_WIDTH), lambda b, s, pt: (b, 0)),
            pl.BlockSpec((t_new, page_rows), lambda b, s, pt: (0, 0)),
            pl.BlockSpec((HEAD_DIM, 2 * HEAD_DIM), lambda b, s, pt: (0, 0)),
        ],
        out_specs=pl.BlockSpec((t_new, SB_WIDTH), lambda b, s, pt: (b, 0)),
        scratch_shapes=[pltpu.VMEM((rows, HEAD_DIM), F32), pltpu.VMEM((t_new, HEAD_DIM), F32)],
    )
    return pl.pallas_call(
        kern,
        grid_spec=grid_spec,
        out_shape=jax.ShapeDtypeStruct((n_seq * t_new, SB_WIDTH), F32),
        compiler_params=pltpu.CompilerParams(
            dimension_semantics=("parallel", "arbitrary"), vmem_limit_bytes=VMEM_LIMIT),
        name="sb_sample",
    )(page_table.reshape(-1), q, as_page(k_new), as_page(v_new), *([cache_k] * n_group),
      *([cache_v] * n_group), g, bias_lanes, uo)


def _ssd_kernel(length, xbc_ref, z_ref, dt_ref, pre_ref, h0_ref, cw_ref, cb_ref, dtb_ref,
                alog_ref, dsk_ref, nw_ref, ltri_ref, e_ref, out_ref, cnew_ref, snew_ref,
                ext_ref, st_ref):
    L = SSD_CHUNK
    P = SSD_HEAD_DIM
    c = pl.program_id(1)

    @pl.when(c == 0)
    def _():
        ext_ref[0:SUBLANES, :] = pre_ref[...]
        st_ref[...] = h0_ref[...]

    ext_ref[SUBLANES:SUBLANES + L, :] = xbc_ref[...]
    cw = cw_ref[...]
    conv = cb_ref[...]
    for j in range(CONV_WIDTH):
        off = SUBLANES - (CONV_WIDTH - 1) + j
        conv = conv + ext_ref[off:off + L, :] * cw[j:j + 1, :]
    act = _silu(conv)
    tail = ext_ref[length:length + SUBLANES, :]
    cnew_ref[...] = tail
    ext_ref[0:SUBLANES, :] = tail

    xs = act[:, :SSD_WIDTH]
    bm = act[:, SSD_WIDTH:SSD_WIDTH + SSD_GROUPS * SSD_STATE]
    cm = act[:, SSD_WIDTH + SSD_GROUPS * SSD_STATE:]

    x_dt = dt_ref[...] + dtb_ref[...]
    dt = jnp.maximum(x_dt, 0.0) + jnp.log1p(jnp.exp(-jnp.abs(x_dt)))
    if length < L:
        valid = lax.broadcasted_iota(jnp.int32, dt.shape, 0) < length
        dt = jnp.where(valid, dt, 0.0)
    da = dt * (-jnp.exp(alog_ref[...]))
    cs = _dot_exact_rhs(ltri_ref[...], da)
    cs_t = cs.T
    e = e_ref[...]
    dt_x = _dot_exact_lhs(dt, e)
    cs_x = _dot_exact_lhs(cs, e)
    xdt = xs * dt_x
    ecs = jnp.exp(cs_x)
    xw_t = (xdt * jnp.exp(cs_x[L - 1:L, :] - cs_x)).T
    xdt16 = xdt.astype(BF16)

    row = lax.broadcasted_iota(jnp.int32, (L, L), 0)
    col = lax.broadcasted_iota(jnp.int32, (L, L), 1)
    causal = col <= row
    heads_per_group = SSD_HEADS // SSD_GROUPS
    gw = heads_per_group * P
    y_diag, y_off = [], []
    for g in range(SSD_GROUPS):
        bg = bm[:, SSD_STATE * g:SSD_STATE * (g + 1)].astype(BF16)
        cg = cm[:, SSD_STATE * g:SSD_STATE * (g + 1)].astype(BF16)
        cb = _dot_nt(cg, bg)
        prev = st_ref[gw * g:gw * (g + 1), :]
        y_off.append(_dot_nt(cg, prev.astype(BF16)))
        new = _dot(xw_t[gw * g:gw * (g + 1), :].astype(BF16), bg)
        for r in range(heads_per_group):
            h = heads_per_group * g + r
            seg = cs[:, h:h + 1] - cs_t[h:h + 1, :]
            decay = jnp.exp(jnp.where(causal, seg, -jnp.inf))
            y_diag.append(_dot((cb * decay).astype(BF16), xdt16[:, P * h:P * (h + 1)]))
            chunk_decay = jnp.exp(cs[L - 1:L, h:h + 1])
            st_ref[P * h:P * (h + 1), :] = (prev[P * r:P * (r + 1), :] * chunk_decay
                                            + new[P * r:P * (r + 1), :])
    snew_ref[...] = st_ref[...]
    y = (jnp.concatenate(y_diag, axis=1) + jnp.concatenate(y_off, axis=1) * ecs
         + xs * dsk_ref[...])
    gated = y * _silu(z_ref[...])
    ms = jnp.mean(gated * gated, axis=-1, keepdims=True)
    out_ref[...] = (gated * lax.rsqrt(ms + EPS) * nw_ref[...]).astype(out_ref.dtype)


def _ssd(xbc, zdt, prefix, h0, conv_w, conv_b, dt_bias, a_log, d_skip, ssd_norm_w,
         batch, n_chunks, length, out_dtype):
    L = SSD_CHUNK
    pre = jnp.pad(prefix, ((0, 0), (SUBLANES - (CONV_WIDTH - 1), 0), (0, 0)))
    pad_h = lambda a: jnp.pad(a, (0, HEAD_DIM - SSD_HEADS)).reshape(1, HEAD_DIM)
    ltri = (jnp.arange(L)[:, None] >= jnp.arange(L)[None, :]).astype(BF16)
    expand = (jnp.arange(HEAD_DIM)[:, None] == jnp.arange(SSD_WIDTH)[None, :] // SSD_HEAD_DIM
              ).astype(BF16)
    dsk = jnp.repeat(d_skip, SSD_HEAD_DIM).reshape(1, SSD_WIDTH)
    z_blk = SSD_WIDTH // HEAD_DIM
    const = lambda b, c: (0, 0)
    kern = functools.partial(_ssd_kernel, length)
    return pl.pallas_call(
        kern,
        grid=(batch, n_chunks),
        in_specs=[
            pl.BlockSpec((L, XBC_WIDTH), lambda b, c: (b * n_chunks + c, 0)),
            pl.BlockSpec((L, SSD_WIDTH), lambda b, c: (b * n_chunks + c, 0)),
            pl.BlockSpec((L, HEAD_DIM), lambda b, c: (b * n_chunks + c, z_blk)),
            pl.BlockSpec((None, SUBLANES, XBC_WIDTH), lambda b, c: (b, 0, 0)),
            pl.BlockSpec((None, SSD_WIDTH, SSD_STATE), lambda b, c: (b, 0, 0)),
            pl.BlockSpec((CONV_WIDTH, XBC_WIDTH), const),
            pl.BlockSpec((1, XBC_WIDTH), const),
            pl.BlockSpec((1, HEAD_DIM), const),
            pl.BlockSpec((1, HEAD_DIM), const),
            pl.BlockSpec((1, SSD_WIDTH), const),
            pl.BlockSpec((1, SSD_WIDTH), const),
            pl.BlockSpec((L, L), const),
            pl.BlockSpec((HEAD_DIM, SSD_WIDTH), const),
        ],
        out_specs=[
            pl.BlockSpec((L, SSD_WIDTH), lambda b, c: (b * n_chunks + c, 0)),
            pl.BlockSpec((None, SUBLANES, XBC_WIDTH), lambda b, c: (b, 0, 0)),
            pl.BlockSpec((None, SSD_WIDTH, SSD_STATE), lambda b, c: (b, 0, 0)),
        ],
        out_shape=[
            jax.ShapeDtypeStruct((batch * n_chunks * L, SSD_WIDTH), out_dtype),
            jax.ShapeDtypeStruct((batch, SUBLANES, XBC_WIDTH), F32),
            jax.ShapeDtypeStruct((batch, SSD_WIDTH, SSD_STATE), F32),
        ],
        scratch_shapes=[pltpu.VMEM((SUBLANES + L, XBC_WIDTH), F32),
                        pltpu.VMEM((SSD_WIDTH, SSD_STATE), F32)],
        compiler_params=pltpu.CompilerParams(
            dimension_semantics=("parallel", "arbitrary"), vmem_limit_bytes=VMEM_LIMIT),
        name="ssd_scan",
    )(xbc, zdt, zdt, pre, h0.reshape(batch, SSD_WIDTH, SSD_STATE), conv_w,
      conv_b.reshape(1, XBC_WIDTH), pad_h(dt_bias), pad_h(a_log), dsk,
      ssd_norm_w.reshape(1, SSD_WIDTH), ltri, expand)


def _mem_attn_kernel(q_ref, g_ref, k_ref, v_ref, o_ref):
    for h in range(MEM_HEADS):
        cols = slice(HEAD_DIM * h, HEAD_DIM * (h + 1))
        s = _dot_nt(q_ref[:, cols].astype(BF16), k_ref[:, cols].astype(BF16)) * ATTN_SCALE
        p = jnp.exp(s - jnp.max(s, axis=-1, keepdims=True))
        den = jnp.sum(p, axis=-1, keepdims=True)
        o = _dot(p.astype(BF16), v_ref[:, cols].astype(BF16)) / den
        o_ref[:, cols] = (o * _silu(g_ref[:, cols])).astype(o_ref.dtype)


def _mem_attn(q, g, mem_k, mem_v, batch, t, tq, out_dtype):
    nq = t // tq
    return pl.pallas_call(
        _mem_attn_kernel,
        grid=(batch, nq),
        in_specs=[
            pl.BlockSpec((tq, MEM_WIDTH), lambda b, i: (b * nq + i, 0)),
            pl.BlockSpec((tq, MEM_WIDTH), lambda b, i: (b * nq + i, 0)),
            pl.BlockSpec((None, MEM_TOKENS, MEM_WIDTH), lambda b, i: (b, 0, 0)),
            pl.BlockSpec((None, MEM_TOKENS, MEM_WIDTH), lambda b, i: (b, 0, 0)),
        ],
        out_specs=pl.BlockSpec((tq, MEM_WIDTH), lambda b, i: (b * nq + i, 0)),
        out_shape=jax.ShapeDtypeStruct((batch * t, MEM_WIDTH), out_dtype),
        compiler_params=pltpu.CompilerParams(
            dimension_semantics=("parallel", "parallel"), vmem_limit_bytes=VMEM_LIMIT),
        name="mem_attn",
    )(q, g, mem_k, mem_v)


def _out_proj_kernel(x_ref, sb_ref, ssd_ref, mo_ref, w_ref, o_ref):
    mix = jnp.concatenate([sb_ref[...].astype(BF16), ssd_ref[...].astype(BF16),
                           mo_ref[...].astype(BF16)], axis=-1)
    for c in range(0, o_ref.shape[1], PROJ_CHUNK):
        cols = slice(c, c + PROJ_CHUNK)
        o_ref[:, cols] = x_ref[:, cols] + _dot(mix, w_ref[:, cols])


def _out_proj(x, sb, ssd, mo, w_out, tm, tn):
    t, d = x.shape
    return pl.pallas_call(
        _out_proj_kernel,
        grid=(t // tm, d // tn),
        in_specs=[
            pl.BlockSpec((tm, tn), lambda m, n: (m, n)),
            pl.BlockSpec((tm, SB_WIDTH), lambda m, n: (m, 0)),
            pl.BlockSpec((tm, SSD_WIDTH), lambda m, n: (m, 0)),
            pl.BlockSpec((tm, MEM_WIDTH), lambda m, n: (m, 0)),
            pl.BlockSpec((w_out.shape[0], tn), lambda m, n: (0, n)),
        ],
        out_specs=pl.BlockSpec((tm, tn), lambda m, n: (m, n)),
        out_shape=jax.ShapeDtypeStruct((t, d), F32),
        compiler_params=pltpu.CompilerParams(
            dimension_semantics=("parallel", "arbitrary"), vmem_limit_bytes=VMEM_LIMIT),
        name="out_proj",
    )(x, sb, ssd, mo, w_out)


def _in_proj_plan(act_dtype):
    lowp = act_dtype == BF16
    outs, plan = [], []

    def add(width, dtype):
        outs.append((width, dtype))
        return len(outs) - 1

    q = add(SB_WIDTH, act_dtype)
    plan.append(((0, SB_WIDTH, 0, None if lowp else q, q if lowp else None),))
    k32 = add(SB_WIDTH, F32)
    k16 = add(SB_WIDTH, BF16) if lowp else None
    plan.append(((0, SB_WIDTH, 1, k32, k16),))
    v32 = add(SB_WIDTH, F32)
    v16 = add(SB_WIDTH, BF16) if lowp else None
    plan.append(((0, SB_WIDTH, None, v32, v16),))
    g = add(SB_WIDTH, F32)
    plan.append(((0, SB_WIDTH, None, g, None),))
    xbc = add(XBC_WIDTH, F32)
    plan.append(((0, XBC_WIDTH, None, xbc, None),))
    zdt = add(PROJ_TN, F32)
    plan.append(((0, PROJ_TN, None, zdt, None),))
    mq = add(MEM_WIDTH, act_dtype)
    mg = add(MEM_WIDTH, F32)
    plan.append(((0, MEM_WIDTH, 2, None if lowp else mq, mq if lowp else None),
                 (MEM_WIDTH, MEM_WIDTH, None, mg, None)))
    names = dict(q=q, k32=k32, k16=k16, v32=v32, v16=v16, g=g, xbc=xbc, zdt=zdt, mq=mq, mg=mg)
    return tuple(plan), outs, names


def _rearranged_w_in(w):
    o_z = 4 * SB_WIDTH
    o_xbc = o_z + SSD_WIDTH
    o_dt = o_xbc + XBC_WIDTH
    o_mem = o_dt + SSD_HEADS
    pad = jnp.zeros((w.shape[0], PROJ_TN - SSD_WIDTH - SSD_HEADS), BF16)
    parts = [w[:, :o_z], w[:, o_xbc:o_dt], w[:, o_z:o_xbc], w[:, o_dt:o_mem]]
    return jnp.concatenate([p.astype(BF16) for p in parts] + [pad, w[:, o_mem:].astype(BF16)],
                           axis=1)


def kernel(x_prompt, x_sample, cache_sb_k, cache_sb_v, state_ssm, state_conv, cache_mem_k,
           cache_mem_v, page_table, mem_prompt, norm_w, w_in, sb_q_norm, sb_k_norm, sb_bias,
           conv_w, conv_b, dt_bias, a_log, d_skip, ssd_norm_w, mem_norm_w, w_mem_kv, mem_q_norm,
           mem_k_norm, w_out):
    depth = w_in.shape[0]
    assert depth == 1
    layer = 0
    bp, sp, d = x_prompt.shape
    bs, ts, _ = x_sample.shape
    n_pool = cache_sb_k.shape[1]
    L = SSD_CHUNK

    w_cat = _rearranged_w_in(w_in[layer])
    w_o = w_out[layer].astype(BF16)
    w_kv = w_mem_kv[layer].astype(BF16)
    head_norms = jnp.concatenate(
        [sb_q_norm[layer][None], sb_k_norm[layer][None], mem_q_norm[layer][None],
         mem_k_norm[layer][None], jnp.zeros((SUBLANES - 4, HEAD_DIM), F32)], axis=0)
    ssd_params = (conv_w[layer], conv_b[layer], dt_bias[layer], a_log[layer], d_skip[layer],
                  ssd_norm_w[layer])

    xp = x_prompt.reshape(bp * sp, d)
    mem_plan = (((0, MEM_WIDTH, 3, 0, None), (MEM_WIDTH, MEM_WIDTH, None, 1, None)),)
    mk, mv = _proj(mem_prompt.reshape(bp * MEM_TOKENS, d), mem_norm_w[layer], w_kv, head_norms,
                   mem_plan, [(MEM_WIDTH, F32), (MEM_WIDTH, F32)], tm=512)
    plan, outs, nm = _in_proj_plan(BF16)
    pr = _proj(xp, norm_w[layer], w_cat, head_norms, plan, outs, tm=512)
    sb = _sb_prompt(pr[nm['q']], pr[nm['k16']], pr[nm['v16']], pr[nm['g']], sb_bias[layer],
                    bp, sp, tq=256, n_heads=4)
    ssd, conv_p, ssm_p = _ssd(
        pr[nm['xbc']], pr[nm['zdt']], jnp.zeros((bp, CONV_WIDTH - 1, XBC_WIDTH), F32),
        jnp.zeros((bp, SSD_HEADS, SSD_HEAD_DIM, SSD_STATE), F32), *ssd_params,
        batch=bp, n_chunks=sp // L, length=L, out_dtype=BF16)
    mo = _mem_attn(pr[nm['mq']], pr[nm['mg']], mk.reshape(bp, MEM_TOKENS, MEM_WIDTH),
                   mv.reshape(bp, MEM_TOKENS, MEM_WIDTH), bp, sp, tq=512, out_dtype=BF16)
    yp = _out_proj(xp, sb, ssd, mo, w_o, tm=512, tn=d)

    xs = x_sample.reshape(bs * ts, d)
    plan_s, outs_s, ns = _in_proj_plan(F32)
    ps = _proj(xs, norm_w[layer], w_cat, head_norms, plan_s, outs_s, tm=bs * ts)
    sb_s = _sb_sample(ps[ns['q']], ps[ns['k32']], ps[ns['v32']], ps[ns['g']],
                      cache_sb_k, cache_sb_v, layer, page_table, sb_bias[layer], ts)

    def pad_chunk(a):
        a = a.reshape(bs, ts, a.shape[-1])
        return jnp.pad(a, ((0, 0), (0, L - ts), (0, 0))).reshape(bs * L, a.shape[-1])

    ssd_s, conv_s, ssm_s = _ssd(
        pad_chunk(ps[ns['xbc']]), pad_chunk(ps[ns['zdt']]), state_conv[layer], state_ssm[layer],
        *ssd_params, batch=bs, n_chunks=1, length=ts, out_dtype=F32)
    ssd_s = ssd_s.reshape(bs, L, SSD_WIDTH)[:, :ts].reshape(bs * ts, SSD_WIDTH)
    mo_s = _mem_attn(ps[ns['mq']], ps[ns['mg']],
                     cache_mem_k[layer].reshape(bs, MEM_TOKENS, MEM_WIDTH),
                     cache_mem_v[layer].reshape(bs, MEM_TOKENS, MEM_WIDTH), bs, ts, tq=ts,
                     out_dtype=F32)
    ys = _out_proj(xs, sb_s, ssd_s, mo_s, w_o, tm=bs * ts, tn=d)

    tail = slice(SUBLANES - (CONV_WIDTH - 1), SUBLANES)
    return (
        yp.reshape(bp, sp, d),
        ys.reshape(bs, ts, d),
        pr[nm['k32']].reshape(1, bp, sp, SB_HEADS, HEAD_DIM),
        pr[nm['v32']].reshape(1, bp, sp, SB_HEADS, HEAD_DIM),
        ssm_p.reshape(1, bp, SSD_HEADS, SSD_HEAD_DIM, SSD_STATE),
        conv_p[:, tail][None],
        mk.reshape(1, bp, MEM_TOKENS, MEM_HEADS, HEAD_DIM),
        mv.reshape(1, bp, MEM_TOKENS, MEM_HEADS, HEAD_DIM),
        ps[ns['k32']].reshape(1, bs, ts, SB_HEADS, HEAD_DIM),
        ps[ns['v32']].reshape(1, bs, ts, SB_HEADS, HEAD_DIM),
        ssm_s.reshape(1, bs, SSD_HEADS, SSD_HEAD_DIM, SSD_STATE),
        conv_s[:, tail][None],
    )
```

```python
import functools
import math

import jax
import jax.numpy as jnp
from jax import lax
from jax.experimental import pallas as pl
from jax.experimental.pallas import tpu as pltpu

F32 = jnp.float32
BF16 = jnp.bfloat16

D_MODEL = 2048
SB_HEADS = 8
HEAD_DIM = 128
SB_WIDTH = SB_HEADS * HEAD_DIM
SSD_HEADS = 8
SSD_HEAD_DIM = 64
SSD_WIDTH = SSD_HEADS * SSD_HEAD_DIM
SSD_GROUPS = 2
SSD_STATE = 128
CONV_WIDTH = 4
XBC_WIDTH = SSD_WIDTH + 2 * SSD_GROUPS * SSD_STATE
MEM_TOKENS = 256
MEM_HEADS = 4
MEM_WIDTH = MEM_HEADS * HEAD_DIM
PAGE_SIZE = 128
EPS = 1e-6
ATTN_SCALE = HEAD_DIM ** -0.5
LOG2E = math.log2(math.e)

SSD_CHUNK = 128
PROJ_TN = 1024
PROJ_CHUNK = 256
SAMPLE_PAGES_PER_STEP = 8
SSD_SEQS_PER_STEP = 4
SUBLANES = 8
VMEM_LIMIT = 56 * 1024 * 1024

_NT = (((1,), (1,)), ((), ()))


def _dot(a, b):
    return jnp.dot(a, b, preferred_element_type=F32)


def _dot_nt(a, b):
    return lax.dot_general(a, b, _NT, preferred_element_type=F32)


def _split2(x):
    hi = x.astype(BF16)
    lo = (x - hi.astype(F32)).astype(BF16)
    return hi, lo


def _split3(x):
    hi = x.astype(BF16)
    r = x - hi.astype(F32)
    mid = r.astype(BF16)
    lo = (r - mid.astype(F32)).astype(BF16)
    return hi, mid, lo


def _dot_exact_lhs(x, m):
    hi, mid, lo = _split3(x)
    return _dot(hi, m) + _dot(mid, m) + _dot(lo, m)


def _dot_exact_rhs(m, x):
    hi, mid, lo = _split3(x)
    return _dot(m, hi) + _dot(m, mid) + _dot(m, lo)


def _silu(x):
    return x * (1.0 / (1.0 + jnp.exp(-x)))


def _log2_fail_and_hit(z2):
    nz = -z2
    soft = jnp.log(1.0 + jnp.exp2(jnp.minimum(z2, nz))) * LOG2E
    log_fail = jnp.minimum(nz, 0.0) - soft
    return log_fail, log_fail + z2


def _proj_kernel(plan, n_out, tm, x_ref, nw_ref, w_ref, hn_ref, *rest):
    outs = rest[:n_out]
    h_ref = rest[n_out]
    n = pl.program_id(1)

    @pl.when(n == 0)
    def _():
        rc = min(tm, 64)

        def body(r, carry):
            rows = pl.ds(pl.multiple_of(r * rc, rc), rc)
            xv = x_ref[rows, :]
            ms = jnp.mean(xv * xv, axis=-1, keepdims=True)
            h_ref[rows, :] = (xv * lax.rsqrt(ms + EPS) * nw_ref[...]).astype(BF16)
            return carry

        lax.fori_loop(0, tm // rc, body, 0)

    for step, segs in enumerate(plan):
        @pl.when(n == step)
        def _(segs=segs):
            for col0, width, hn_row, o32, o16 in segs:
                for c0 in range(0, width, PROJ_CHUNK):
                    y = _dot(h_ref[...], w_ref[:, col0 + c0:col0 + c0 + PROJ_CHUNK])
                    for c in range(0, PROJ_CHUNK, HEAD_DIM):
                        yc = y[:, c:c + HEAD_DIM]
                        if hn_row is not None:
                            ms = jnp.mean(yc * yc, axis=-1, keepdims=True)
                            yc = yc * lax.rsqrt(ms + EPS) * hn_ref[hn_row:hn_row + 1, :]
                        cols = slice(c0 + c, c0 + c + HEAD_DIM)
                        if o32 is not None:
                            outs[o32][:, cols] = yc
                        if o16 is not None:
                            outs[o16][:, cols] = yc.astype(BF16)


def _proj(x, norm_w, w_cat, head_norms, plan, out_defs, tm):
    t, d = x.shape
    n_steps = len(plan)
    assert w_cat.shape == (d, n_steps * PROJ_TN) and t % tm == 0
    kern = functools.partial(_proj_kernel, plan, len(out_defs), tm)
    return pl.pallas_call(
        kern,
        grid=(t // tm, n_steps),
        in_specs=[
            pl.BlockSpec((tm, d), lambda m, n: (m, 0)),
            pl.BlockSpec((1, d), lambda m, n: (0, 0)),
            pl.BlockSpec((d, PROJ_TN), lambda m, n: (0, n)),
            pl.BlockSpec((SUBLANES, HEAD_DIM), lambda m, n: (0, 0)),
        ],
        out_specs=[pl.BlockSpec((tm, w), lambda m, n: (m, 0)) for w, _ in out_defs],
        out_shape=[jax.ShapeDtypeStruct((t, w), dt) for w, dt in out_defs],
        scratch_shapes=[pltpu.VMEM((tm, d), BF16)],
        compiler_params=pltpu.CompilerParams(
            dimension_semantics=("parallel", "arbitrary"), vmem_limit_bytes=VMEM_LIMIT),
        name="norm_proj",
    )(x, norm_w.reshape(1, d), w_cat, head_norms)


def _sb_scores(q, ks, uu, bias2, diag):
    lf, lh = _log2_fail_and_hit(_dot_nt(q, ks) * (ATTN_SCALE * LOG2E) + bias2)
    if diag is not None:
        lf = jnp.where(diag, lf, 0.0)
    hi, lo = _split2(lf)
    between = _dot(jnp.concatenate([hi, lo], axis=1), uu)
    return lf, lh, between


def _sb_prompt_kernel(tq, n_heads, bias_ref, q_ref, k_ref, v_ref, g_ref, uu_ref, o_ref, acc_ref,
                      c_ref):
    hg = pl.program_id(1)
    i = pl.program_id(2)
    uu = uu_ref[...]
    cols = [slice(HEAD_DIM * h, HEAD_DIM * (h + 1)) for h in range(n_heads)]
    biases = [bias_ref[hg * n_heads + h] * LOG2E for h in range(n_heads)]
    qs = [q_ref[:, cols[h]] for h in range(n_heads)]

    def block(start, diag, first):
        kb = k_ref[pl.ds(start, tq), :]
        vb = v_ref[pl.ds(start, tq), :]
        parts = [_sb_scores(qs[h], kb[:, cols[h]], uu, biases[h], diag) for h in range(n_heads)]
        for h in range(n_heads):
            lf, lh, between = parts[h]
            logw = lh + between
            if not first:
                logw = logw + c_ref[h]
            w = jnp.exp2(logw)
            if diag is not None:
                w = jnp.where(diag, w, 0.0)
            pv = _dot(w.astype(BF16), vb[:, cols[h]])
            total = between[:, 0:1] + lf[:, 0:1]
            if first:
                acc_ref[:, cols[h]] = pv
                c_ref[h] = total
            else:
                acc_ref[:, cols[h]] += pv
                c_ref[h] += total

    row = lax.broadcasted_iota(jnp.int32, (tq, tq), 0)
    col = lax.broadcasted_iota(jnp.int32, (tq, tq), 1)
    block(pl.multiple_of(i * tq, tq), col < row, True)

    def body(jj, carry):
        block(pl.multiple_of((i - 1 - jj) * tq, tq), None, False)
        return carry

    lax.fori_loop(0, i, body, 0)
    o_ref[...] = (acc_ref[...] * _silu(g_ref[...])).astype(o_ref.dtype)


def _sb_prompt(q, k, v, g, sb_bias, batch, seq, tq, n_heads):
    q3, k3, v3, g3 = (a.reshape(batch, seq, SB_WIDTH) for a in (q, k, v, g))
    u = (jnp.arange(tq)[:, None] > jnp.arange(tq)[None, :]).astype(BF16)
    uu = jnp.concatenate([u, u], axis=0)
    width = n_heads * HEAD_DIM
    kern = functools.partial(_sb_prompt_kernel, tq, n_heads)
    out = pl.pallas_call(
        kern,
        grid=(batch, SB_HEADS // n_heads, seq // tq),
        in_specs=[
            pl.BlockSpec(memory_space=pltpu.SMEM),
            pl.BlockSpec((None, tq, width), lambda b, h, i: (b, i, h)),
            pl.BlockSpec((None, seq, width), lambda b, h, i: (b, 0, h)),
            pl.BlockSpec((None, seq, width), lambda b, h, i: (b, 0, h)),
            pl.BlockSpec((None, tq, width), lambda b, h, i: (b, i, h)),
            pl.BlockSpec((2 * tq, tq), lambda b, h, i: (0, 0)),
        ],
        out_specs=pl.BlockSpec((None, tq, width), lambda b, h, i: (b, i, h)),
        out_shape=jax.ShapeDtypeStruct((batch, seq, SB_WIDTH), BF16),
        scratch_shapes=[pltpu.VMEM((tq, width), F32), pltpu.VMEM((n_heads, tq, 1), F32)],
        compiler_params=pltpu.CompilerParams(
            dimension_semantics=("parallel", "parallel", "arbitrary"),
            vmem_limit_bytes=VMEM_LIMIT),
        name="sb_prompt",
    )(sb_bias, q3, k3, v3, g3, uu)
    return out.reshape(batch * seq, SB_WIDTH)


def _sb_sample_kernel(t_new, n_group, pt_ref, q_ref, kn_ref, vn_ref, *rest):
    kps, vps = rest[:n_group], rest[n_group:2 * n_group]
    g_ref, bias_ref, uo_ref, o_ref, acc_ref, c_ref = rest[2 * n_group:]
    s = pl.program_id(1)
    lanes = PAGE_SIZE * SB_HEADS
    n_blk = lanes // HEAD_DIM
    lane = lax.broadcasted_iota(jnp.int32, (t_new, lanes), 1)
    lane_head = lane & (SB_HEADS - 1)
    q = q_ref[...]
    q_all = jnp.concatenate([q[:, HEAD_DIM * h:HEAD_DIM * (h + 1)] for h in range(SB_HEADS)],
                            axis=0).astype(BF16)

    def scores(k_ref, mask):
        s_all = _dot_nt(q_all, k_ref[...].astype(BF16))
        sc = s_all[0:t_new, :]
        for h in range(1, SB_HEADS):
            sc = jnp.where(lane_head == h, s_all[t_new * h:t_new * (h + 1), :], sc)
        lf, lh = _log2_fail_and_hit(sc * (ATTN_SCALE * LOG2E) + bias_ref[...])
        if mask is not None:
            lf = jnp.where(mask, lf, 0.0)
        blocks = jnp.concatenate([lf[:, HEAD_DIM * j:HEAD_DIM * (j + 1)] for j in range(n_blk)],
                                 axis=0)
        hi, lo = _split2(blocks)
        return lh, _dot(hi, uo_ref[...]) + _dot(lo, uo_ref[...])

    def weights(lh, res, mask, run):
        ws = [None] * n_blk
        for j in reversed(range(n_blk)):
            rows = slice(t_new * j, t_new * (j + 1))
            logw = lh[:, HEAD_DIM * j:HEAD_DIM * (j + 1)] + res[rows, :HEAD_DIM]
            if run is not None:
                logw = logw + run
            ws[j] = jnp.exp2(logw)
            tot = res[rows, HEAD_DIM:]
            run = tot if run is None else run + tot
        w = jnp.concatenate(ws, axis=1)
        if mask is not None:
            w = jnp.where(mask, w, 0.0)
        w_all = jnp.concatenate([jnp.where(lane_head == h, w, 0.0) for h in range(SB_HEADS)],
                                axis=0).astype(BF16)
        return w_all, run

    @pl.when(s == 0)
    def _():
        mask = (lane >> 3) < lax.broadcasted_iota(jnp.int32, (t_new, lanes), 0)
        lh, res = scores(kn_ref, mask)
        w_all, run = weights(lh, res, mask, None)
        c_ref[...] = run
        acc_ref[...] = _dot(w_all, vn_ref[...].astype(BF16))

    @pl.when(s > 0)
    def _():
        parts = [scores(kps[j], None) for j in range(n_group)]
        run = c_ref[...]
        acc = acc_ref[...]
        for j in range(n_group):
            w_all, run = weights(*parts[j], None, run)
            acc = acc + _dot(w_all, vps[j][...].astype(BF16))
        c_ref[...] = run
        acc_ref[...] = acc

    @pl.when(s == pl.num_programs(1) - 1)
    def _():
        g = g_ref[...]
        for h in range(SB_HEADS):
            cols = slice(HEAD_DIM * h, HEAD_DIM * (h + 1))
            o_ref[:, cols] = acc_ref[t_new * h:t_new * (h + 1), :] * _silu(g[:, cols])


def _sb_sample(q, k_new, v_new, g, cache_k, cache_v, layer, page_table, sb_bias, t_new):
    n_seq, n_pages = page_table.shape
    n_pool = cache_k.shape[1]
    n_group = SAMPLE_PAGES_PER_STEP
    assert n_pages % n_group == 0
    rows = SB_HEADS * t_new
    page_rows = PAGE_SIZE * SB_HEADS

    def as_page(a):
        a = a.reshape(n_seq, t_new, SB_HEADS, HEAD_DIM)
        a = jnp.pad(a, ((0, 0), (0, PAGE_SIZE - t_new), (0, 0), (0, 0)))
        return a.reshape(n_seq * page_rows, HEAD_DIM)

    cache_k = cache_k.reshape(-1, page_rows, HEAD_DIM)
    cache_v = cache_v.reshape(-1, page_rows, HEAD_DIM)

    lane_head = jnp.arange(page_rows) % SB_HEADS
    bias_lanes = jnp.broadcast_to((sb_bias * LOG2E)[lane_head][None, :], (t_new, page_rows))
    r = jnp.arange(HEAD_DIM)
    same_head = (r[:, None] % SB_HEADS) == (r[None, :] % SB_HEADS)
    later = (r[:, None] // SB_HEADS) > (r[None, :] // SB_HEADS)
    uo = jnp.concatenate([same_head & later, same_head], axis=1).astype(BF16)

    def page_spec(j):
        def index_map(b, s, pt):
            page = (n_pages - 1) - (jnp.maximum(s - 1, 0) * n_group + j)
            return (layer * n_pool + pt[b * n_pages + page], 0, 0)
        return pl.BlockSpec((None, page_rows, HEAD_DIM), index_map)

    pages = [page_spec(j) for j in range(n_group)]
    kern = functools.partial(_sb_sample_kernel, t_new, n_group)
    grid_spec = pltpu.PrefetchScalarGridSpec(
        num_scalar_prefetch=1,
        grid=(n_seq, n_pages // n_group + 1),
        in_specs=[
            pl.BlockSpec((t_new, SB_WIDTH), lambda b, s, pt: (b, 0)),
            pl.BlockSpec((page_rows, HEAD_DIM), lambda b, s, pt: (b, 0)),
            pl.BlockSpec((page_rows, HEAD_DIM), lambda b, s, pt: (b, 0)),
            *pages,
            *pages,
            pl.BlockSpec((t_new, SB_WIDTH), lambda b, s, pt: (b, 0)),
            pl.BlockSpec((t_new, page_rows), lambda b, s, pt: (0, 0)),
            pl.BlockSpec((HEAD_DIM, 2 * HEAD_DIM), lambda b, s, pt: (0, 0)),
        ],
        out_specs=pl.BlockSpec((t_new, SB_WIDTH), lambda b, s, pt: (b, 0)),
        scratch_shapes=[pltpu.VMEM((rows, HEAD_DIM), F32), pltpu.VMEM((t_new, HEAD_DIM), F32)],
    )
    return pl.pallas_call(
        kern,
        grid_spec=grid_spec,
        out_shape=jax.ShapeDtypeStruct((n_seq * t_new, SB_WIDTH), F32),
        compiler_params=pltpu.CompilerParams(
            dimension_semantics=("parallel", "arbitrary"), vmem_limit_bytes=VMEM_LIMIT),
        name="sb_sample",
    )(page_table.reshape(-1), q, as_page(k_new), as_page(v_new), *([cache_k] * n_group),
      *([cache_v] * n_group), g, bias_lanes, uo)


def _ssd_kernel(length, n_par, *refs):
    per_seq_in, shared, per_seq_out = refs[:5], refs[5:13], refs[13:]
    pre_ref, h0_ref = per_seq_in[3:5]
    ext_ref, st_ref = per_seq_out[3:5]

    @pl.when(pl.program_id(1) == 0)
    def _():
        ext_ref[:, 0:SUBLANES, :] = pre_ref[...]
        st_ref[...] = h0_ref[...]

    for s in range(n_par):
        _ssd_chunk(length, *[r.at[s] for r in per_seq_in[:3]], *shared,
                   *[r.at[s] for r in per_seq_out])


def _pad_rows(a, rows):
    if a.shape[0] == rows:
        return a
    return jnp.concatenate([a, jnp.zeros((rows - a.shape[0], a.shape[1]), a.dtype)], axis=0)


def _ssd_chunk(length, xbc_ref, z_ref, dt_ref, cw_ref, cb_ref, dtb_ref,
               alog_ref, dsk_ref, nw_ref, ltri_ref, e_ref, out_ref, cnew_ref, snew_ref,
               ext_ref, st_ref):
    L = SSD_CHUNK
    P = SSD_HEAD_DIM

    ext_ref[SUBLANES:SUBLANES + L, :] = _pad_rows(xbc_ref[...], L)
    cw = cw_ref[...]
    conv = cb_ref[...]
    for j in range(CONV_WIDTH):
        off = SUBLANES - (CONV_WIDTH - 1) + j
        conv = conv + ext_ref[off:off + L, :] * cw[j:j + 1, :]
    act = _silu(conv)
    tail = ext_ref[length:length + SUBLANES, :]
    cnew_ref[...] = tail
    ext_ref[0:SUBLANES, :] = tail

    xs = act[:, :SSD_WIDTH]
    bm = act[:, SSD_WIDTH:SSD_WIDTH + SSD_GROUPS * SSD_STATE]
    cm = act[:, SSD_WIDTH + SSD_GROUPS * SSD_STATE:]

    x_dt = _pad_rows(dt_ref[...], L) + dtb_ref[...]
    dt = jnp.maximum(x_dt, 0.0) + jnp.log1p(jnp.exp(-jnp.abs(x_dt)))
    if length < L:
        valid = lax.broadcasted_iota(jnp.int32, dt.shape, 0) < length
        dt = jnp.where(valid, dt, 0.0)
    da = dt * (-jnp.exp(alog_ref[...]))
    cs = _dot_exact_rhs(ltri_ref[...], da)
    cs_t = cs.T
    e = e_ref[...]
    dt_x = _dot_exact_lhs(dt, e)
    cs_x = _dot_exact_lhs(cs, e)
    xdt = xs * dt_x
    ecs = jnp.exp(cs_x)
    xw_t = (xdt * jnp.exp(cs_x[L - 1:L, :] - cs_x)).T
    xdt16 = xdt.astype(BF16)

    row = lax.broadcasted_iota(jnp.int32, (L, L), 0)
    col = lax.broadcasted_iota(jnp.int32, (L, L), 1)
    causal = col <= row
    heads_per_group = SSD_HEADS // SSD_GROUPS
    gw = heads_per_group * P
    y_diag, y_off = [], []
    for g in range(SSD_GROUPS):
        bg = bm[:, SSD_STATE * g:SSD_STATE * (g + 1)].astype(BF16)
        cg = cm[:, SSD_STATE * g:SSD_STATE * (g + 1)].astype(BF16)
        cb = _dot_nt(cg, bg)
        prev = st_ref[gw * g:gw * (g + 1), :]
        y_off.append(_dot_nt(cg, prev.astype(BF16)))
        new = _dot(xw_t[gw * g:gw * (g + 1), :].astype(BF16), bg)
        for r in range(heads_per_group):
            h = heads_per_group * g + r
            seg = cs[:, h:h + 1] - cs_t[h:h + 1, :]
            decay = jnp.exp(jnp.where(causal, seg, -jnp.inf))
            y_diag.append(_dot((cb * decay).astype(BF16), xdt16[:, P * h:P * (h + 1)]))
            chunk_decay = jnp.exp(cs[L - 1:L, h:h + 1])
            st_ref[P * h:P * (h + 1), :] = (prev[P * r:P * (r + 1), :] * chunk_decay
                                            + new[P * r:P * (r + 1), :])
    snew_ref[...] = st_ref[...]
    y = (jnp.concatenate(y_diag, axis=1) + jnp.concatenate(y_off, axis=1) * ecs
         + xs * dsk_ref[...])
    gated = y[:length] * _silu(z_ref[...])
    ms = jnp.mean(gated * gated, axis=-1, keepdims=True)
    out_ref[...] = (gated * lax.rsqrt(ms + EPS) * nw_ref[...]).astype(out_ref.dtype)


def _ssd(xbc, zdt, prefix, h0, conv_w, conv_b, dt_bias, a_log, d_skip, ssd_norm_w,
         batch, n_chunks, length, out_dtype):
    L = SSD_CHUNK
    pre = jnp.pad(prefix, ((0, 0), (SUBLANES - (CONV_WIDTH - 1), 0), (0, 0)))
    pad_h = lambda a: jnp.pad(a, (0, HEAD_DIM - SSD_HEADS)).reshape(1, HEAD_DIM)
    ltri = (jnp.arange(L)[:, None] >= jnp.arange(L)[None, :]).astype(BF16)
    expand = (jnp.arange(HEAD_DIM)[:, None] == jnp.arange(SSD_WIDTH)[None, :] // SSD_HEAD_DIM
              ).astype(BF16)
    dsk = jnp.repeat(d_skip, SSD_HEAD_DIM).reshape(1, SSD_WIDTH)
    z_blk = SSD_WIDTH // HEAD_DIM
    n_par = SSD_SEQS_PER_STEP
    assert batch % n_par == 0 and length <= L and (length == L or n_chunks == 1)
    rows = n_chunks * length
    xbc3 = xbc.reshape(batch, rows, XBC_WIDTH)
    zdt3 = zdt.reshape(batch, rows, PROJ_TN)
    const = lambda b, c: (0, 0)
    seq = lambda b, c: (b, 0, 0)
    kern = functools.partial(_ssd_kernel, length, n_par)
    out, conv_new, ssm_new = pl.pallas_call(
        kern,
        grid=(batch // n_par, n_chunks),
        in_specs=[
            pl.BlockSpec((n_par, length, XBC_WIDTH), lambda b, c: (b, c, 0)),
            pl.BlockSpec((n_par, length, SSD_WIDTH), lambda b, c: (b, c, 0)),
            pl.BlockSpec((n_par, length, HEAD_DIM), lambda b, c: (b, c, z_blk)),
            pl.BlockSpec((n_par, SUBLANES, XBC_WIDTH), seq),
            pl.BlockSpec((n_par, SSD_WIDTH, SSD_STATE), seq),
            pl.BlockSpec((CONV_WIDTH, XBC_WIDTH), const),
            pl.BlockSpec((1, XBC_WIDTH), const),
            pl.BlockSpec((1, HEAD_DIM), const),
            pl.BlockSpec((1, HEAD_DIM), const),
            pl.BlockSpec((1, SSD_WIDTH), const),
            pl.BlockSpec((1, SSD_WIDTH), const),
            pl.BlockSpec((L, L), const),
            pl.BlockSpec((HEAD_DIM, SSD_WIDTH), const),
        ],
        out_specs=[
            pl.BlockSpec((n_par, length, SSD_WIDTH), lambda b, c: (b, c, 0)),
            pl.BlockSpec((n_par, SUBLANES, XBC_WIDTH), seq),
            pl.BlockSpec((n_par, SSD_WIDTH, SSD_STATE), seq),
        ],
        out_shape=[
            jax.ShapeDtypeStruct((batch, rows, SSD_WIDTH), out_dtype),
            jax.ShapeDtypeStruct((batch, SUBLANES, XBC_WIDTH), F32),
            jax.ShapeDtypeStruct((batch, SSD_WIDTH, SSD_STATE), F32),
        ],
        scratch_shapes=[pltpu.VMEM((n_par, SUBLANES + L, XBC_WIDTH), F32),
                        pltpu.VMEM((n_par, SSD_WIDTH, SSD_STATE), F32)],
        compiler_params=pltpu.CompilerParams(
            dimension_semantics=("parallel", "arbitrary"), vmem_limit_bytes=VMEM_LIMIT),
        name="ssd_scan",
    )(xbc3, zdt3, zdt3, pre, h0.reshape(batch, SSD_WIDTH, SSD_STATE), conv_w,
      conv_b.reshape(1, XBC_WIDTH), pad_h(dt_bias), pad_h(a_log), dsk,
      ssd_norm_w.reshape(1, SSD_WIDTH), ltri, expand)
    return out.reshape(batch * rows, SSD_WIDTH), conv_new, ssm_new


def _mem_attn_kernel(q_ref, g_ref, k_ref, v_ref, o_ref):
    for h in range(MEM_HEADS):
        cols = slice(HEAD_DIM * h, HEAD_DIM * (h + 1))
        s = _dot_nt(q_ref[:, cols].astype(BF16), k_ref[:, cols].astype(BF16)) * ATTN_SCALE
        p = jnp.exp(s - jnp.max(s, axis=-1, keepdims=True))
        den = jnp.sum(p, axis=-1, keepdims=True)
        o = _dot(p.astype(BF16), v_ref[:, cols].astype(BF16)) / den
        o_ref[:, cols] = (o * _silu(g_ref[:, cols])).astype(o_ref.dtype)


def _mem_attn(q, g, mem_k, mem_v, batch, t, tq, out_dtype):
    nq = t // tq
    return pl.pallas_call(
        _mem_attn_kernel,
        grid=(batch, nq),
        in_specs=[
            pl.BlockSpec((tq, MEM_WIDTH), lambda b, i: (b * nq + i, 0)),
            pl.BlockSpec((tq, MEM_WIDTH), lambda b, i: (b * nq + i, 0)),
            pl.BlockSpec((None, MEM_TOKENS, MEM_WIDTH), lambda b, i: (b, 0, 0)),
            pl.BlockSpec((None, MEM_TOKENS, MEM_WIDTH), lambda b, i: (b, 0, 0)),
        ],
        out_specs=pl.BlockSpec((tq, MEM_WIDTH), lambda b, i: (b * nq + i, 0)),
        out_shape=jax.ShapeDtypeStruct((batch * t, MEM_WIDTH), out_dtype),
        compiler_params=pltpu.CompilerParams(
            dimension_semantics=("parallel", "parallel"), vmem_limit_bytes=VMEM_LIMIT),
        name="mem_attn",
    )(q, g, mem_k, mem_v)


def _out_proj_kernel(x_ref, sb_ref, ssd_ref, mo_ref, w_ref, o_ref):
    mix = jnp.concatenate([sb_ref[...].astype(BF16), ssd_ref[...].astype(BF16),
                           mo_ref[...].astype(BF16)], axis=-1)
    for c in range(0, o_ref.shape[1], PROJ_CHUNK):
        cols = slice(c, c + PROJ_CHUNK)
        o_ref[:, cols] = x_ref[:, cols] + _dot(mix, w_ref[:, cols])


def _out_proj(x, sb, ssd, mo, w_out, tm, tn):
    t, d = x.shape
    return pl.pallas_call(
        _out_proj_kernel,
        grid=(t // tm, d // tn),
        in_specs=[
            pl.BlockSpec((tm, tn), lambda m, n: (m, n)),
            pl.BlockSpec((tm, SB_WIDTH), lambda m, n: (m, 0)),
            pl.BlockSpec((tm, SSD_WIDTH), lambda m, n: (m, 0)),
            pl.BlockSpec((tm, MEM_WIDTH), lambda m, n: (m, 0)),
            pl.BlockSpec((w_out.shape[0], tn), lambda m, n: (0, n)),
        ],
        out_specs=pl.BlockSpec((tm, tn), lambda m, n: (m, n)),
        out_shape=jax.ShapeDtypeStruct((t, d), F32),
        compiler_params=pltpu.CompilerParams(
            dimension_semantics=("parallel", "arbitrary"), vmem_limit_bytes=VMEM_LIMIT),
        name="out_proj",
    )(x, sb, ssd, mo, w_out)


def _in_proj_plan(act_dtype):
    lowp = act_dtype == BF16
    outs, plan = [], []

    def add(width, dtype):
        outs.append((width, dtype))
        return len(outs) - 1

    q = add(SB_WIDTH, act_dtype)
    plan.append(((0, SB_WIDTH, 0, None if lowp else q, q if lowp else None),))
    k32 = add(SB_WIDTH, F32)
    k16 = add(SB_WIDTH, BF16) if lowp else None
    plan.append(((0, SB_WIDTH, 1, k32, k16),))
    v32 = add(SB_WIDTH, F32)
    v16 = add(SB_WIDTH, BF16) if lowp else None
    plan.append(((0, SB_WIDTH, None, v32, v16),))
    g = add(SB_WIDTH, F32)
    plan.append(((0, SB_WIDTH, None, g, None),))
    xbc = add(XBC_WIDTH, F32)
    plan.append(((0, XBC_WIDTH, None, xbc, None),))
    zdt = add(PROJ_TN, F32)
    plan.append(((0, PROJ_TN, None, zdt, None),))
    mq = add(MEM_WIDTH, act_dtype)
    mg = add(MEM_WIDTH, F32)
    plan.append(((0, MEM_WIDTH, 2, None if lowp else mq, mq if lowp else None),
                 (MEM_WIDTH, MEM_WIDTH, None, mg, None)))
    names = dict(q=q, k32=k32, k16=k16, v32=v32, v16=v16, g=g, xbc=xbc, zdt=zdt, mq=mq, mg=mg)
    return tuple(plan), outs, names


_O_Z = 4 * SB_WIDTH
_O_XBC = _O_Z + SSD_WIDTH
_O_DT = _O_XBC + XBC_WIDTH
_O_MEM = _O_DT + SSD_HEADS
_IN_WIDTH = _O_MEM + 2 * MEM_WIDTH
_CAT_WIDTH = 7 * PROJ_TN
W_PREP_ROWS = 256


def _w_prep_kernel(w_ref, o_ref):
    def put(dst, src, width):
        o_ref[:, dst:dst + width] = w_ref[:, src:src + width].astype(BF16)

    put(0, 0, _O_Z)
    put(_O_Z, _O_XBC, XBC_WIDTH)
    put(_O_Z + XBC_WIDTH, _O_Z, SSD_WIDTH)
    dt0 = _O_Z + XBC_WIDTH + SSD_WIDTH
    tail = w_ref[:, _O_DT:_IN_WIDTH]
    lane = lax.broadcasted_iota(jnp.int32, (tail.shape[0], HEAD_DIM), 1)
    o_ref[:, dt0:dt0 + HEAD_DIM] = jnp.where(lane < SSD_HEADS, tail[:, :HEAD_DIM], 0.0
                                             ).astype(BF16)
    o_ref[:, dt0 + HEAD_DIM:6 * PROJ_TN] = jnp.zeros(
        (tail.shape[0], 6 * PROJ_TN - dt0 - HEAD_DIM), BF16)
    o_ref[:, 6 * PROJ_TN:] = tail[:, SSD_HEADS:].astype(BF16)


def _rearranged_w_in(w):
    d = w.shape[0]
    assert w.shape[1] == _IN_WIDTH and d % W_PREP_ROWS == 0
    return pl.pallas_call(
        _w_prep_kernel,
        grid=(d // W_PREP_ROWS,),
        in_specs=[pl.BlockSpec((W_PREP_ROWS, _IN_WIDTH), lambda r: (r, 0))],
        out_specs=pl.BlockSpec((W_PREP_ROWS, _CAT_WIDTH), lambda r: (r, 0)),
        out_shape=jax.ShapeDtypeStruct((d, _CAT_WIDTH), BF16),
        compiler_params=pltpu.CompilerParams(
            dimension_semantics=("parallel",), vmem_limit_bytes=VMEM_LIMIT),
        name="w_prep",
    )(w)


def kernel(x_prompt, x_sample, cache_sb_k, cache_sb_v, state_ssm, state_conv, cache_mem_k,
           cache_mem_v, page_table, mem_prompt, norm_w, w_in, sb_q_norm, sb_k_norm, sb_bias,
           conv_w, conv_b, dt_bias, a_log, d_skip, ssd_norm_w, mem_norm_w, w_mem_kv, mem_q_norm,
           mem_k_norm, w_out):
    depth = w_in.shape[0]
    assert depth == 1
    layer = 0
    bp, sp, d = x_prompt.shape
    bs, ts, _ = x_sample.shape
    n_pool = cache_sb_k.shape[1]
    L = SSD_CHUNK

    w_cat = _rearranged_w_in(w_in[layer])
    w_o = w_out[layer].astype(BF16)
    w_kv = w_mem_kv[layer].astype(BF16)
    head_norms = jnp.concatenate(
        [sb_q_norm[layer][None], sb_k_norm[layer][None], mem_q_norm[layer][None],
         mem_k_norm[layer][None], jnp.zeros((SUBLANES - 4, HEAD_DIM), F32)], axis=0)
    ssd_params = (conv_w[layer], conv_b[layer], dt_bias[layer], a_log[layer], d_skip[layer],
                  ssd_norm_w[layer])

    xp = x_prompt.reshape(bp * sp, d)
    mem_plan = (((0, MEM_WIDTH, 3, 0, None), (MEM_WIDTH, MEM_WIDTH, None, 1, None)),)
    mk, mv = _proj(mem_prompt.reshape(bp * MEM_TOKENS, d), mem_norm_w[layer], w_kv, head_norms,
                   mem_plan, [(MEM_WIDTH, F32), (MEM_WIDTH, F32)], tm=512)
    plan, outs, nm = _in_proj_plan(BF16)
    pr = _proj(xp, norm_w[layer], w_cat, head_norms, plan, outs, tm=512)
    sb = _sb_prompt(pr[nm['q']], pr[nm['k16']], pr[nm['v16']], pr[nm['g']], sb_bias[layer],
                    bp, sp, tq=256, n_heads=8)
    ssd, conv_p, ssm_p = _ssd(
        pr[nm['xbc']], pr[nm['zdt']], jnp.zeros((bp, CONV_WIDTH - 1, XBC_WIDTH), F32),
        jnp.zeros((bp, SSD_HEADS, SSD_HEAD_DIM, SSD_STATE), F32), *ssd_params,
        batch=bp, n_chunks=sp // L, length=L, out_dtype=BF16)
    mo = _mem_attn(pr[nm['mq']], pr[nm['mg']], mk.reshape(bp, MEM_TOKENS, MEM_WIDTH),
                   mv.reshape(bp, MEM_TOKENS, MEM_WIDTH), bp, sp, tq=512, out_dtype=BF16)
    yp = _out_proj(xp, sb, ssd, mo, w_o, tm=512, tn=d)

    xs = x_sample.reshape(bs * ts, d)
    plan_s, outs_s, ns = _in_proj_plan(F32)
    ps = _proj(xs, norm_w[layer], w_cat, head_norms, plan_s, outs_s, tm=bs * ts)
    sb_s = _sb_sample(ps[ns['q']], ps[ns['k32']], ps[ns['v32']], ps[ns['g']],
                      cache_sb_k, cache_sb_v, layer, page_table, sb_bias[layer], ts)

    ssd_s, conv_s, ssm_s = _ssd(
        ps[ns['xbc']], ps[ns['zdt']], state_conv[layer], state_ssm[layer],
        *ssd_params, batch=bs, n_chunks=1, length=ts, out_dtype=F32)
    mo_s = _mem_attn(ps[ns['mq']], ps[ns['mg']],
                     cache_mem_k[layer].reshape(bs, MEM_TOKENS, MEM_WIDTH),
                     cache_mem_v[layer].reshape(bs, MEM_TOKENS, MEM_WIDTH), bs, ts, tq=ts,
                     out_dtype=F32)
    ys = _out_proj(xs, sb_s, ssd_s, mo_s, w_o, tm=bs * ts, tn=d)

    tail = slice(SUBLANES - (CONV_WIDTH - 1), SUBLANES)
    return (
        yp.reshape(bp, sp, d),
        ys.reshape(bs, ts, d),
        pr[nm['k32']].reshape(1, bp, sp, SB_HEADS, HEAD_DIM),
        pr[nm['v32']].reshape(1, bp, sp, SB_HEADS, HEAD_DIM),
        ssm_p.reshape(1, bp, SSD_HEADS, SSD_HEAD_DIM, SSD_STATE),
        conv_p[:, tail][None],
        mk.reshape(1, bp, MEM_TOKENS, MEM_HEADS, HEAD_DIM),
        mv.reshape(1, bp, MEM_TOKENS, MEM_HEADS, HEAD_DIM),
        ps[ns['k32']].reshape(1, bs, ts, SB_HEADS, HEAD_DIM),
        ps[ns['v32']].reshape(1, bs, ts, SB_HEADS, HEAD_DIM),
        ssm_s.reshape(1, bs, SSD_HEADS, SSD_HEAD_DIM, SSD_STATE),
        conv_s[:, tail][None],
    )
```

```python
import functools
import math

import jax
import jax.numpy as jnp
from jax import lax
from jax.experimental import pallas as pl
from jax.experimental.pallas import tpu as pltpu

F32 = jnp.float32
BF16 = jnp.bfloat16

D_MODEL = 2048
SB_HEADS = 8
HEAD_DIM = 128
SB_WIDTH = SB_HEADS * HEAD_DIM
SSD_HEADS = 8
SSD_HEAD_DIM = 64
SSD_WIDTH = SSD_HEADS * SSD_HEAD_DIM
SSD_GROUPS = 2
SSD_STATE = 128
CONV_WIDTH = 4
XBC_WIDTH = SSD_WIDTH + 2 * SSD_GROUPS * SSD_STATE
MEM_TOKENS = 256
MEM_HEADS = 4
MEM_WIDTH = MEM_HEADS * HEAD_DIM
PAGE_SIZE = 128
EPS = 1e-6
ATTN_SCALE = HEAD_DIM ** -0.5
LOG2E = math.log2(math.e)

SSD_CHUNK = 128
PROJ_TN = 1024
PROJ_CHUNK = 256
SAMPLE_PAGES_PER_STEP = 8
SSD_SEQS_PER_STEP = 4
SUBLANES = 8
VMEM_LIMIT = 56 * 1024 * 1024

_NT = (((1,), (1,)), ((), ()))


def _dot(a, b):
    return jnp.dot(a, b, preferred_element_type=F32)


def _dot_nt(a, b):
    return lax.dot_general(a, b, _NT, preferred_element_type=F32)


def _split2(x):
    hi = x.astype(BF16)
    lo = (x - hi.astype(F32)).astype(BF16)
    return hi, lo


def _split3(x):
    hi = x.astype(BF16)
    r = x - hi.astype(F32)
    mid = r.astype(BF16)
    lo = (r - mid.astype(F32)).astype(BF16)
    return hi, mid, lo


def _dot_exact_lhs(x, m):
    hi, mid, lo = _split3(x)
    return _dot(hi, m) + _dot(mid, m) + _dot(lo, m)


def _dot_exact_rhs(m, x):
    hi, mid, lo = _split3(x)
    return _dot(m, hi) + _dot(m, mid) + _dot(m, lo)


def _silu(x):
    return x * (1.0 / (1.0 + jnp.exp(-x)))


def _log2_fail_and_hit(z2):
    nz = -z2
    soft = jnp.log(1.0 + jnp.exp2(jnp.minimum(z2, nz))) * LOG2E
    log_fail = jnp.minimum(nz, 0.0) - soft
    return log_fail, log_fail + z2


def _proj_kernel(plan, n_out, tm, x_ref, nw_ref, w_ref, hn_ref, *rest):
    outs = rest[:n_out]
    h_ref = rest[n_out]
    n = pl.program_id(1)

    @pl.when(n == 0)
    def _():
        rc = min(tm, 64)

        def body(r, carry):
            rows = pl.ds(pl.multiple_of(r * rc, rc), rc)
            xv = x_ref[rows, :]
            ms = jnp.mean(xv * xv, axis=-1, keepdims=True)
            h_ref[rows, :] = (xv * lax.rsqrt(ms + EPS) * nw_ref[...]).astype(BF16)
            return carry

        lax.fori_loop(0, tm // rc, body, 0)

    for step, segs in enumerate(plan):
        @pl.when(n == step)
        def _(segs=segs):
            for col0, width, hn_row, o32, o16 in segs:
                for c0 in range(0, width, PROJ_CHUNK):
                    y = _dot_nt(h_ref[...], w_ref[col0 + c0:col0 + c0 + PROJ_CHUNK, :])
                    for c in range(0, PROJ_CHUNK, HEAD_DIM):
                        yc = y[:, c:c + HEAD_DIM]
                        if hn_row is not None:
                            ms = jnp.mean(yc * yc, axis=-1, keepdims=True)
                            yc = yc * lax.rsqrt(ms + EPS) * hn_ref[hn_row:hn_row + 1, :]
                        cols = slice(c0 + c, c0 + c + HEAD_DIM)
                        if o32 is not None:
                            outs[o32][:, cols] = yc
                        if o16 is not None:
                            outs[o16][:, cols] = yc.astype(BF16)


def _proj(x, norm_w, w_t, head_norms, plan, out_defs, tm):
    t, d = x.shape
    n_steps = len(plan)
    assert w_t.shape == (n_steps * PROJ_TN, d) and t % tm == 0
    kern = functools.partial(_proj_kernel, plan, len(out_defs), tm)
    return pl.pallas_call(
        kern,
        grid=(t // tm, n_steps),
        in_specs=[
            pl.BlockSpec((tm, d), lambda m, n: (m, 0)),
            pl.BlockSpec((1, d), lambda m, n: (0, 0)),
            pl.BlockSpec((PROJ_TN, d), lambda m, n: (n, 0)),
            pl.BlockSpec((SUBLANES, HEAD_DIM), lambda m, n: (0, 0)),
        ],
        out_specs=[pl.BlockSpec((tm, w), lambda m, n: (m, 0)) for w, _ in out_defs],
        out_shape=[jax.ShapeDtypeStruct((t, w), dt) for w, dt in out_defs],
        scratch_shapes=[pltpu.VMEM((tm, d), BF16)],
        compiler_params=pltpu.CompilerParams(
            dimension_semantics=("parallel", "arbitrary"), vmem_limit_bytes=VMEM_LIMIT),
        name="norm_proj",
    )(x, norm_w.reshape(1, d), w_t, head_norms)


def _sb_scores(q, ks, uu, bias2, diag):
    lf, lh = _log2_fail_and_hit(_dot_nt(q, ks) * (ATTN_SCALE * LOG2E) + bias2)
    if diag is not None:
        lf = jnp.where(diag, lf, 0.0)
    hi, lo = _split2(lf)
    between = _dot(jnp.concatenate([hi, lo], axis=1), uu)
    return lf, lh, between


def _sb_prompt_kernel(tq, n_heads, bias_ref, q_ref, k_ref, v_ref, g_ref, uu_ref, o_ref, acc_ref,
                      c_ref):
    hg = pl.program_id(1)
    i = pl.program_id(2)
    uu = uu_ref[...]
    cols = [slice(HEAD_DIM * h, HEAD_DIM * (h + 1)) for h in range(n_heads)]
    biases = [bias_ref[hg * n_heads + h] * LOG2E for h in range(n_heads)]
    qs = [q_ref[:, cols[h]] for h in range(n_heads)]

    def block(start, diag, first):
        kb = k_ref[pl.ds(start, tq), :]
        vb = v_ref[pl.ds(start, tq), :]
        parts = [_sb_scores(qs[h], kb[:, cols[h]], uu, biases[h], diag) for h in range(n_heads)]
        for h in range(n_heads):
            lf, lh, between = parts[h]
            logw = lh + between
            if not first:
                logw = logw + c_ref[h]
            w = jnp.exp2(logw)
            if diag is not None:
                w = jnp.where(diag, w, 0.0)
            pv = _dot(w.astype(BF16), vb[:, cols[h]])
            total = between[:, 0:1] + lf[:, 0:1]
            if first:
                acc_ref[:, cols[h]] = pv
                c_ref[h] = total
            else:
                acc_ref[:, cols[h]] += pv
                c_ref[h] += total

    row = lax.broadcasted_iota(jnp.int32, (tq, tq), 0)
    col = lax.broadcasted_iota(jnp.int32, (tq, tq), 1)
    block(pl.multiple_of(i * tq, tq), col < row, True)

    def body(jj, carry):
        block(pl.multiple_of((i - 1 - jj) * tq, tq), None, False)
        return carry

    lax.fori_loop(0, i, body, 0)
    o_ref[...] = (acc_ref[...] * _silu(g_ref[...])).astype(o_ref.dtype)


def _sb_prompt(q, k, v, g, sb_bias, batch, seq, tq, n_heads):
    q3, k3, v3, g3 = (a.reshape(batch, seq, SB_WIDTH) for a in (q, k, v, g))
    u = (jnp.arange(tq)[:, None] > jnp.arange(tq)[None, :]).astype(BF16)
    uu = jnp.concatenate([u, u], axis=0)
    width = n_heads * HEAD_DIM
    kern = functools.partial(_sb_prompt_kernel, tq, n_heads)
    out = pl.pallas_call(
        kern,
        grid=(batch, SB_HEADS // n_heads, seq // tq),
        in_specs=[
            pl.BlockSpec(memory_space=pltpu.SMEM),
            pl.BlockSpec((None, tq, width), lambda b, h, i: (b, i, h)),
            pl.BlockSpec((None, seq, width), lambda b, h, i: (b, 0, h)),
            pl.BlockSpec((None, seq, width), lambda b, h, i: (b, 0, h)),
            pl.BlockSpec((None, tq, width), lambda b, h, i: (b, i, h)),
            pl.BlockSpec((2 * tq, tq), lambda b, h, i: (0, 0)),
        ],
        out_specs=pl.BlockSpec((None, tq, width), lambda b, h, i: (b, i, h)),
        out_shape=jax.ShapeDtypeStruct((batch, seq, SB_WIDTH), BF16),
        scratch_shapes=[pltpu.VMEM((tq, width), F32), pltpu.VMEM((n_heads, tq, 1), F32)],
        compiler_params=pltpu.CompilerParams(
            dimension_semantics=("parallel", "parallel", "arbitrary"),
            vmem_limit_bytes=VMEM_LIMIT),
        name="sb_prompt",
    )(sb_bias, q3, k3, v3, g3, uu)
    return out.reshape(batch * seq, SB_WIDTH)


def _sb_sample_kernel(t_new, n_group, pt_ref, q_ref, kn_ref, vn_ref, *rest):
    kps, vps = rest[:n_group], rest[n_group:2 * n_group]
    g_ref, bias_ref, uo_ref, o_ref, acc_ref, c_ref = rest[2 * n_group:]
    s = pl.program_id(1)
    lanes = PAGE_SIZE * SB_HEADS
    n_blk = lanes // HEAD_DIM
    lane = lax.broadcasted_iota(jnp.int32, (t_new, lanes), 1)
    lane_head = lane & (SB_HEADS - 1)
    q = q_ref[...]
    q_all = jnp.concatenate([q[:, HEAD_DIM * h:HEAD_DIM * (h + 1)] for h in range(SB_HEADS)],
                            axis=0).astype(BF16)

    def scores(k_ref, mask):
        s_all = _dot_nt(q_all, k_ref[...].astype(BF16))
        sc = s_all[0:t_new, :]
        for h in range(1, SB_HEADS):
            sc = jnp.where(lane_head == h, s_all[t_new * h:t_new * (h + 1), :], sc)
        lf, lh = _log2_fail_and_hit(sc * (ATTN_SCALE * LOG2E) + bias_ref[...])
        if mask is not None:
            lf = jnp.where(mask, lf, 0.0)
        blocks = jnp.concatenate([lf[:, HEAD_DIM * j:HEAD_DIM * (j + 1)] for j in range(n_blk)],
                                 axis=0)
        hi, lo = _split2(blocks)
        return lh, _dot(hi, uo_ref[...]) + _dot(lo, uo_ref[...])

    def weights(lh, res, mask, run):
        ws = [None] * n_blk
        for j in reversed(range(n_blk)):
            rows = slice(t_new * j, t_new * (j + 1))
            logw = lh[:, HEAD_DIM * j:HEAD_DIM * (j + 1)] + res[rows, :HEAD_DIM]
            if run is not None:
                logw = logw + run
            ws[j] = jnp.exp2(logw)
            tot = res[rows, HEAD_DIM:]
            run = tot if run is None else run + tot
        w = jnp.concatenate(ws, axis=1)
        if mask is not None:
            w = jnp.where(mask, w, 0.0)
        w_all = jnp.concatenate([jnp.where(lane_head == h, w, 0.0) for h in range(SB_HEADS)],
                                axis=0).astype(BF16)
        return w_all, run

    @pl.when(s == 0)
    def _():
        mask = (lane >> 3) < lax.broadcasted_iota(jnp.int32, (t_new, lanes), 0)
        lh, res = scores(kn_ref, mask)
        w_all, run = weights(lh, res, mask, None)
        c_ref[...] = run
        acc_ref[...] = _dot(w_all, vn_ref[...].astype(BF16))

    @pl.when(s > 0)
    def _():
        parts = [scores(kps[j], None) for j in range(n_group)]
        run = c_ref[...]
        acc = acc_ref[...]
        for j in range(n_group):
            w_all, run = weights(*parts[j], None, run)
            acc = acc + _dot(w_all, vps[j][...].astype(BF16))
        c_ref[...] = run
        acc_ref[...] = acc

    @pl.when(s == pl.num_programs(1) - 1)
    def _():
        g = g_ref[...]
        for h in range(SB_HEADS):
            cols = slice(HEAD_DIM * h, HEAD_DIM * (h + 1))
            o_ref[:, cols] = acc_ref[t_new * h:t_new * (h + 1), :] * _silu(g[:, cols])


def _sb_sample(q, k_new, v_new, g, cache_k, cache_v, layer, page_table, sb_bias, t_new):
    n_seq, n_pages = page_table.shape
    n_pool = cache_k.shape[1]
    n_group = SAMPLE_PAGES_PER_STEP
    assert n_pages % n_group == 0
    rows = SB_HEADS * t_new
    page_rows = PAGE_SIZE * SB_HEADS

    def as_page(a):
        a = a.reshape(n_seq, t_new, SB_HEADS, HEAD_DIM)
        a = jnp.pad(a, ((0, 0), (0, PAGE_SIZE - t_new), (0, 0), (0, 0)))
        return a.reshape(n_seq * page_rows, HEAD_DIM)

    cache_k = cache_k.reshape(-1, page_rows, HEAD_DIM)
    cache_v = cache_v.reshape(-1, page_rows, HEAD_DIM)

    lane_head = jnp.arange(page_rows) % SB_HEADS
    bias_lanes = jnp.broadcast_to((sb_bias * LOG2E)[lane_head][None, :], (t_new, page_rows))
    r = jnp.arange(HEAD_DIM)
    same_head = (r[:, None] % SB_HEADS) == (r[None, :] % SB_HEADS)
    later = (r[:, None] // SB_HEADS) > (r[None, :] // SB_HEADS)
    uo = jnp.concatenate([same_head & later, same_head], axis=1).astype(BF16)

    def page_spec(j):
        def index_map(b, s, pt):
            page = (n_pages - 1) - (jnp.maximum(s - 1, 0) * n_group + j)
            return (layer * n_pool + pt[b * n_pages + page], 0, 0)
        return pl.BlockSpec((None, page_rows, HEAD_DIM), index_map)

    pages = [page_spec(j) for j in range(n_group)]
    kern = functools.partial(_sb_sample_kernel, t_new, n_group)
    grid_spec = pltpu.PrefetchScalarGridSpec(
        num_scalar_prefetch=1,
        grid=(n_seq, n_pages // n_group + 1),
        in_specs=[
            pl.BlockSpec((t_new, SB_WIDTH), lambda b, s, pt: (b, 0)),
            pl.BlockSpec((page_rows, HEAD_DIM), lambda b, s, pt: (b, 0)),
            pl.BlockSpec((page_rows, HEAD_DIM), lambda b, s, pt: (b, 0)),
            *pages,
            *pages,
            pl.BlockSpec((t_new, SB_WIDTH), lambda b, s, pt: (b, 0)),
            pl.BlockSpec((t_new, page_rows), lambda b, s, pt: (0, 0)),
            pl.BlockSpec((HEAD_DIM, 2 * HEAD_DIM), lambda b, s, pt: (0, 0)),
        ],
        out_specs=pl.BlockSpec((t_new, SB_WIDTH), lambda b, s, pt: (b, 0)),
        scratch_shapes=[pltpu.VMEM((rows, HEAD_DIM), F32), pltpu.VMEM((t_new, HEAD_DIM), F32)],
    )
    return pl.pallas_call(
        kern,
        grid_spec=grid_spec,
        out_shape=jax.ShapeDtypeStruct((n_seq * t_new, SB_WIDTH), F32),
        compiler_params=pltpu.CompilerParams(
            dimension_semantics=("parallel", "arbitrary"), vmem_limit_bytes=VMEM_LIMIT),
        name="sb_sample",
    )(page_table.reshape(-1), q, as_page(k_new), as_page(v_new), *([cache_k] * n_group),
      *([cache_v] * n_group), g, bias_lanes, uo)


def _ssd_kernel(length, n_par, *refs):
    per_seq_in, shared, per_seq_out = refs[:5], refs[5:13], refs[13:]
    pre_ref, h0_ref = per_seq_in[3:5]
    ext_ref, st_ref = per_seq_out[3:5]

    @pl.when(pl.program_id(1) == 0)
    def _():
        ext_ref[:, 0:SUBLANES, :] = pre_ref[...]
        st_ref[...] = h0_ref[...]

    for s in range(n_par):
        _ssd_chunk(length, *[r.at[s] for r in per_seq_in[:3]], *shared,
                   *[r.at[s] for r in per_seq_out])


def _pad_rows(a, rows):
    if a.shape[0] == rows:
        return a
    return jnp.concatenate([a, jnp.zeros((rows - a.shape[0], a.shape[1]), a.dtype)], axis=0)


def _ssd_chunk(length, xbc_ref, z_ref, dt_ref, cw_ref, cb_ref, dtb_ref,
               alog_ref, dsk_ref, nw_ref, ltri_ref, e_ref, out_ref, cnew_ref, snew_ref,
               ext_ref, st_ref):
    L = SSD_CHUNK
    P = SSD_HEAD_DIM

    ext_ref[SUBLANES:SUBLANES + L, :] = _pad_rows(xbc_ref[...], L)
    cw = cw_ref[...]
    conv = cb_ref[...]
    for j in range(CONV_WIDTH):
        off = SUBLANES - (CONV_WIDTH - 1) + j
        conv = conv + ext_ref[off:off + L, :] * cw[j:j + 1, :]
    act = _silu(conv)
    tail = ext_ref[length:length + SUBLANES, :]
    cnew_ref[...] = tail
    ext_ref[0:SUBLANES, :] = tail

    xs = act[:, :SSD_WIDTH]
    bm = act[:, SSD_WIDTH:SSD_WIDTH + SSD_GROUPS * SSD_STATE]
    cm = act[:, SSD_WIDTH + SSD_GROUPS * SSD_STATE:]

    x_dt = _pad_rows(dt_ref[...], L) + dtb_ref[...]
    dt = jnp.maximum(x_dt, 0.0) + jnp.log1p(jnp.exp(-jnp.abs(x_dt)))
    if length < L:
        valid = lax.broadcasted_iota(jnp.int32, dt.shape, 0) < length
        dt = jnp.where(valid, dt, 0.0)
    da = dt * (-jnp.exp(alog_ref[...]))
    cs = _dot_exact_rhs(ltri_ref[...], da)
    cs_t = cs.T
    e = e_ref[...]
    dt_x = _dot_exact_lhs(dt, e)
    cs_x = _dot_exact_lhs(cs, e)
    xdt = xs * dt_x
    ecs = jnp.exp(cs_x)
    xw_t = (xdt * jnp.exp(cs_x[L - 1:L, :] - cs_x)).T
    xdt16 = xdt.astype(BF16)

    row = lax.broadcasted_iota(jnp.int32, (L, L), 0)
    col = lax.broadcasted_iota(jnp.int32, (L, L), 1)
    causal = col <= row
    heads_per_group = SSD_HEADS // SSD_GROUPS
    gw = heads_per_group * P
    y_diag, y_off = [], []
    for g in range(SSD_GROUPS):
        bg = bm[:, SSD_STATE * g:SSD_STATE * (g + 1)].astype(BF16)
        cg = cm[:, SSD_STATE * g:SSD_STATE * (g + 1)].astype(BF16)
        cb = _dot_nt(cg, bg)
        prev = st_ref[gw * g:gw * (g + 1), :]
        y_off.append(_dot_nt(cg, prev.astype(BF16)))
        new = _dot(xw_t[gw * g:gw * (g + 1), :].astype(BF16), bg)
        for r in range(heads_per_group):
            h = heads_per_group * g + r
            seg = cs[:, h:h + 1] - cs_t[h:h + 1, :]
            decay = jnp.exp(jnp.where(causal, seg, -jnp.inf))
            y_diag.append(_dot((cb * decay).astype(BF16), xdt16[:, P * h:P * (h + 1)]))
            chunk_decay = jnp.exp(cs[L - 1:L, h:h + 1])
            st_ref[P * h:P * (h + 1), :] = (prev[P * r:P * (r + 1), :] * chunk_decay
                                            + new[P * r:P * (r + 1), :])
    snew_ref[...] = st_ref[...]
    y = (jnp.concatenate(y_diag, axis=1) + jnp.concatenate(y_off, axis=1) * ecs
         + xs * dsk_ref[...])
    gated = y[:length] * _silu(z_ref[...])
    ms = jnp.mean(gated * gated, axis=-1, keepdims=True)
    out_ref[...] = (gated * lax.rsqrt(ms + EPS) * nw_ref[...]).astype(out_ref.dtype)


def _ssd(xbc, zdt, prefix, h0, conv_w, conv_b, dt_bias, a_log, d_skip, ssd_norm_w,
         batch, n_chunks, length, out_dtype):
    L = SSD_CHUNK
    pre = jnp.pad(prefix, ((0, 0), (SUBLANES - (CONV_WIDTH - 1), 0), (0, 0)))
    pad_h = lambda a: jnp.pad(a, (0, HEAD_DIM - SSD_HEADS)).reshape(1, HEAD_DIM)
    ltri = (jnp.arange(L)[:, None] >= jnp.arange(L)[None, :]).astype(BF16)
    expand = (jnp.arange(HEAD_DIM)[:, None] == jnp.arange(SSD_WIDTH)[None, :] // SSD_HEAD_DIM
              ).astype(BF16)
    dsk = jnp.repeat(d_skip, SSD_HEAD_DIM).reshape(1, SSD_WIDTH)
    z_blk = SSD_WIDTH // HEAD_DIM
    n_par = SSD_SEQS_PER_STEP
    assert batch % n_par == 0 and length <= L and (length == L or n_chunks == 1)
    rows = n_chunks * length
    xbc3 = xbc.reshape(batch, rows, XBC_WIDTH)
    zdt3 = zdt.reshape(batch, rows, PROJ_TN)
    const = lambda b, c: (0, 0)
    seq = lambda b, c: (b, 0, 0)
    kern = functools.partial(_ssd_kernel, length, n_par)
    out, conv_new, ssm_new = pl.pallas_call(
        kern,
        grid=(batch // n_par, n_chunks),
        in_specs=[
            pl.BlockSpec((n_par, length, XBC_WIDTH), lambda b, c: (b, c, 0)),
            pl.BlockSpec((n_par, length, SSD_WIDTH), lambda b, c: (b, c, 0)),
            pl.BlockSpec((n_par, length, HEAD_DIM), lambda b, c: (b, c, z_blk)),
            pl.BlockSpec((n_par, SUBLANES, XBC_WIDTH), seq),
            pl.BlockSpec((n_par, SSD_WIDTH, SSD_STATE), seq),
            pl.BlockSpec((CONV_WIDTH, XBC_WIDTH), const),
            pl.BlockSpec((1, XBC_WIDTH), const),
            pl.BlockSpec((1, HEAD_DIM), const),
            pl.BlockSpec((1, HEAD_DIM), const),
            pl.BlockSpec((1, SSD_WIDTH), const),
            pl.BlockSpec((1, SSD_WIDTH), const),
            pl.BlockSpec((L, L), const),
            pl.BlockSpec((HEAD_DIM, SSD_WIDTH), const),
        ],
        out_specs=[
            pl.BlockSpec((n_par, length, SSD_WIDTH), lambda b, c: (b, c, 0)),
            pl.BlockSpec((n_par, SUBLANES, XBC_WIDTH), seq),
            pl.BlockSpec((n_par, SSD_WIDTH, SSD_STATE), seq),
        ],
        out_shape=[
            jax.ShapeDtypeStruct((batch, rows, SSD_WIDTH), out_dtype),
            jax.ShapeDtypeStruct((batch, SUBLANES, XBC_WIDTH), F32),
            jax.ShapeDtypeStruct((batch, SSD_WIDTH, SSD_STATE), F32),
        ],
        scratch_shapes=[pltpu.VMEM((n_par, SUBLANES + L, XBC_WIDTH), F32),
                        pltpu.VMEM((n_par, SSD_WIDTH, SSD_STATE), F32)],
        compiler_params=pltpu.CompilerParams(
            dimension_semantics=("parallel", "arbitrary"), vmem_limit_bytes=VMEM_LIMIT),
        name="ssd_scan",
    )(xbc3, zdt3, zdt3, pre, h0.reshape(batch, SSD_WIDTH, SSD_STATE), conv_w,
      conv_b.reshape(1, XBC_WIDTH), pad_h(dt_bias), pad_h(a_log), dsk,
      ssd_norm_w.reshape(1, SSD_WIDTH), ltri, expand)
    return out.reshape(batch * rows, SSD_WIDTH), conv_new, ssm_new


def _mem_attn_kernel(q_ref, g_ref, k_ref, v_ref, o_ref):
    for h in range(MEM_HEADS):
        cols = slice(HEAD_DIM * h, HEAD_DIM * (h + 1))
        s = _dot_nt(q_ref[:, cols].astype(BF16), k_ref[:, cols].astype(BF16)) * ATTN_SCALE
        p = jnp.exp(s - jnp.max(s, axis=-1, keepdims=True))
        den = jnp.sum(p, axis=-1, keepdims=True)
        o = _dot(p.astype(BF16), v_ref[:, cols].astype(BF16)) / den
        o_ref[:, cols] = (o * _silu(g_ref[:, cols])).astype(o_ref.dtype)


def _mem_attn(q, g, mem_k, mem_v, batch, t, tq, out_dtype):
    nq = t // tq
    return pl.pallas_call(
        _mem_attn_kernel,
        grid=(batch, nq),
        in_specs=[
            pl.BlockSpec((tq, MEM_WIDTH), lambda b, i: (b * nq + i, 0)),
            pl.BlockSpec((tq, MEM_WIDTH), lambda b, i: (b * nq + i, 0)),
            pl.BlockSpec((None, MEM_TOKENS, MEM_WIDTH), lambda b, i: (b, 0, 0)),
            pl.BlockSpec((None, MEM_TOKENS, MEM_WIDTH), lambda b, i: (b, 0, 0)),
        ],
        out_specs=pl.BlockSpec((tq, MEM_WIDTH), lambda b, i: (b * nq + i, 0)),
        out_shape=jax.ShapeDtypeStruct((batch * t, MEM_WIDTH), out_dtype),
        compiler_params=pltpu.CompilerParams(
            dimension_semantics=("parallel", "parallel"), vmem_limit_bytes=VMEM_LIMIT),
        name="mem_attn",
    )(q, g, mem_k, mem_v)


def _out_proj_kernel(x_ref, sb_ref, ssd_ref, mo_ref, w_ref, o_ref):
    mix = jnp.concatenate([sb_ref[...].astype(BF16), ssd_ref[...].astype(BF16),
                           mo_ref[...].astype(BF16)], axis=-1)
    for c in range(0, o_ref.shape[1], PROJ_CHUNK):
        cols = slice(c, c + PROJ_CHUNK)
        o_ref[:, cols] = x_ref[:, cols] + _dot(mix, w_ref[:, cols])


def _out_proj(x, sb, ssd, mo, w_out, tm, tn):
    t, d = x.shape
    return pl.pallas_call(
        _out_proj_kernel,
        grid=(t // tm, d // tn),
        in_specs=[
            pl.BlockSpec((tm, tn), lambda m, n: (m, n)),
            pl.BlockSpec((tm, SB_WIDTH), lambda m, n: (m, 0)),
            pl.BlockSpec((tm, SSD_WIDTH), lambda m, n: (m, 0)),
            pl.BlockSpec((tm, MEM_WIDTH), lambda m, n: (m, 0)),
            pl.BlockSpec((w_out.shape[0], tn), lambda m, n: (0, n)),
        ],
        out_specs=pl.BlockSpec((tm, tn), lambda m, n: (m, n)),
        out_shape=jax.ShapeDtypeStruct((t, d), F32),
        compiler_params=pltpu.CompilerParams(
            dimension_semantics=("parallel", "arbitrary"), vmem_limit_bytes=VMEM_LIMIT),
        name="out_proj",
    )(x, sb, ssd, mo, w_out)


def _in_proj_plan(act_dtype):
    lowp = act_dtype == BF16
    outs, plan = [], []

    def add(width, dtype):
        outs.append((width, dtype))
        return len(outs) - 1

    q = add(SB_WIDTH, act_dtype)
    plan.append(((0, SB_WIDTH, 0, None if lowp else q, q if lowp else None),))
    k32 = add(SB_WIDTH, F32)
    k16 = add(SB_WIDTH, BF16) if lowp else None
    plan.append(((0, SB_WIDTH, 1, k32, k16),))
    v32 = add(SB_WIDTH, F32)
    v16 = add(SB_WIDTH, BF16) if lowp else None
    plan.append(((0, SB_WIDTH, None, v32, v16),))
    g = add(SB_WIDTH, F32)
    plan.append(((0, SB_WIDTH, None, g, None),))
    xbc = add(XBC_WIDTH, F32)
    plan.append(((0, XBC_WIDTH, None, xbc, None),))
    zdt = add(PROJ_TN, F32)
    plan.append(((0, PROJ_TN, None, zdt, None),))
    mq = add(MEM_WIDTH, act_dtype)
    mg = add(MEM_WIDTH, F32)
    plan.append(((0, MEM_WIDTH, 2, None if lowp else mq, mq if lowp else None),
                 (MEM_WIDTH, MEM_WIDTH, None, mg, None)))
    names = dict(q=q, k32=k32, k16=k16, v32=v32, v16=v16, g=g, xbc=xbc, zdt=zdt, mq=mq, mg=mg)
    return tuple(plan), outs, names


_O_Z = 4 * SB_WIDTH
_O_XBC = _O_Z + SSD_WIDTH
_O_DT = _O_XBC + XBC_WIDTH
_O_MEM = _O_DT + SSD_HEADS
_IN_WIDTH = _O_MEM + 2 * MEM_WIDTH
_CAT_WIDTH = 7 * PROJ_TN
W_PREP_COLS = 256
BF16_ROWS = 16


def _w_prep_kernel(w_ref, o_ref):
    def put(dst, src, rows):
        o_ref[dst:dst + rows, :] = w_ref[src:src + rows, :].astype(BF16)

    cols = w_ref.shape[1]
    put(0, 0, _O_Z)
    put(_O_Z, _O_XBC, XBC_WIDTH)
    put(_O_Z + XBC_WIDTH, _O_Z, SSD_WIDTH)
    dt0 = _O_Z + XBC_WIDTH + SSD_WIDTH
    o_ref[dt0:dt0 + BF16_ROWS, :] = jnp.concatenate(
        [w_ref[_O_DT:_O_MEM, :], jnp.zeros((BF16_ROWS - SSD_HEADS, cols), F32)], axis=0
    ).astype(BF16)
    o_ref[dt0 + BF16_ROWS:6 * PROJ_TN, :] = jnp.zeros((6 * PROJ_TN - dt0 - BF16_ROWS, cols), BF16)
    put(6 * PROJ_TN, _O_MEM, 2 * MEM_WIDTH)


def _rearranged_w_in(w_t):
    d = w_t.shape[1]
    assert w_t.shape[0] == _IN_WIDTH and d % W_PREP_COLS == 0
    return pl.pallas_call(
        _w_prep_kernel,
        grid=(d // W_PREP_COLS,),
        in_specs=[pl.BlockSpec((_IN_WIDTH, W_PREP_COLS), lambda c: (0, c))],
        out_specs=pl.BlockSpec((_CAT_WIDTH, W_PREP_COLS), lambda c: (0, c)),
        out_shape=jax.ShapeDtypeStruct((_CAT_WIDTH, d), BF16),
        compiler_params=pltpu.CompilerParams(
            dimension_semantics=("parallel",), vmem_limit_bytes=VMEM_LIMIT),
        name="w_prep",
    )(w_t)


def kernel(x_prompt, x_sample, cache_sb_k, cache_sb_v, state_ssm, state_conv, cache_mem_k,
           cache_mem_v, page_table, mem_prompt, norm_w, w_in, sb_q_norm, sb_k_norm, sb_bias,
           conv_w, conv_b, dt_bias, a_log, d_skip, ssd_norm_w, mem_norm_w, w_mem_kv, mem_q_norm,
           mem_k_norm, w_out):
    depth = w_in.shape[0]
    assert depth == 1
    layer = 0
    bp, sp, d = x_prompt.shape
    bs, ts, _ = x_sample.shape
    n_pool = cache_sb_k.shape[1]
    L = SSD_CHUNK

    w_cat = _rearranged_w_in(w_in[layer].T)
    w_o = w_out[layer].astype(BF16)
    w_kv = w_mem_kv[layer].T.astype(BF16)
    head_norms = jnp.concatenate(
        [sb_q_norm[layer][None], sb_k_norm[layer][None], mem_q_norm[layer][None],
         mem_k_norm[layer][None], jnp.zeros((SUBLANES - 4, HEAD_DIM), F32)], axis=0)
    ssd_params = (conv_w[layer], conv_b[layer], dt_bias[layer], a_log[layer], d_skip[layer],
                  ssd_norm_w[layer])

    xp = x_prompt.reshape(bp * sp, d)
    mem_plan = (((0, MEM_WIDTH, 3, 0, None), (MEM_WIDTH, MEM_WIDTH, None, 1, None)),)
    mk, mv = _proj(mem_prompt.reshape(bp * MEM_TOKENS, d), mem_norm_w[layer], w_kv, head_norms,
                   mem_plan, [(MEM_WIDTH, F32), (MEM_WIDTH, F32)], tm=512)
    plan, outs, nm = _in_proj_plan(BF16)
    pr = _proj(xp, norm_w[layer], w_cat, head_norms, plan, outs, tm=512)
    sb = _sb_prompt(pr[nm['q']], pr[nm['k16']], pr[nm['v16']], pr[nm['g']], sb_bias[layer],
                    bp, sp, tq=256, n_heads=8)
    ssd, conv_p, ssm_p = _ssd(
        pr[nm['xbc']], pr[nm['zdt']], jnp.zeros((bp, CONV_WIDTH - 1, XBC_WIDTH), F32),
        jnp.zeros((bp, SSD_HEADS, SSD_HEAD_DIM, SSD_STATE), F32), *ssd_params,
        batch=bp, n_chunks=sp // L, length=L, out_dtype=BF16)
    mo = _mem_attn(pr[nm['mq']], pr[nm['mg']], mk.reshape(bp, MEM_TOKENS, MEM_WIDTH),
                   mv.reshape(bp, MEM_TOKENS, MEM_WIDTH), bp, sp, tq=512, out_dtype=BF16)
    yp = _out_proj(xp, sb, ssd, mo, w_o, tm=512, tn=d)

    xs = x_sample.reshape(bs * ts, d)
    plan_s, outs_s, ns = _in_proj_plan(F32)
    ps = _proj(xs, norm_w[layer], w_cat, head_norms, plan_s, outs_s, tm=bs * ts)
    sb_s = _sb_sample(ps[ns['q']], ps[ns['k32']], ps[ns['v32']], ps[ns['g']],
                      cache_sb_k, cache_sb_v, layer, page_table, sb_bias[layer], ts)

    ssd_s, conv_s, ssm_s = _ssd(
        ps[ns['xbc']], ps[ns['zdt']], state_conv[layer], state_ssm[layer],
        *ssd_params, batch=bs, n_chunks=1, length=ts, out_dtype=F32)
    mo_s = _mem_attn(ps[ns['mq']], ps[ns['mg']],
                     cache_mem_k[layer].reshape(bs, MEM_TOKENS, MEM_WIDTH),
                     cache_mem_v[layer].reshape(bs, MEM_TOKENS, MEM_WIDTH), bs, ts, tq=ts,
                     out_dtype=F32)
    ys = _out_proj(xs, sb_s, ssd_s, mo_s, w_o, tm=bs * ts, tn=d)

    tail = slice(SUBLANES - (CONV_WIDTH - 1), SUBLANES)
    return (
        yp.reshape(bp, sp, d),
        ys.reshape(bs, ts, d),
        pr[nm['k32']].reshape(1, bp, sp, SB_HEADS, HEAD_DIM),
        pr[nm['v32']].reshape(1, bp, sp, SB_HEADS, HEAD_DIM),
        ssm_p.reshape(1, bp, SSD_HEADS, SSD_HEAD_DIM, SSD_STATE),
        conv_p[:, tail][None],
        mk.reshape(1, bp, MEM_TOKENS, MEM_HEADS, HEAD_DIM),
        mv.reshape(1, bp, MEM_TOKENS, MEM_HEADS, HEAD_DIM),
        ps[ns['k32']].reshape(1, bs, ts, SB_HEADS, HEAD_DIM),
        ps[ns['v32']].reshape(1, bs, ts, SB_HEADS, HEAD_DIM),
        ssm_s.reshape(1, bs, SSD_HEADS, SSD_HEAD_DIM, SSD_STATE),
        conv_s[:, tail][None],
    )
```

```python
import functools
import math

import jax
import jax.numpy as jnp
from jax import lax
from jax.experimental import pallas as pl
from jax.experimental.pallas import tpu as pltpu

F32 = jnp.float32
BF16 = jnp.bfloat16

D_MODEL = 2048
SB_HEADS = 8
HEAD_DIM = 128
SB_WIDTH = SB_HEADS * HEAD_DIM
SSD_HEADS = 8
SSD_HEAD_DIM = 64
SSD_WIDTH = SSD_HEADS * SSD_HEAD_DIM
SSD_GROUPS = 2
SSD_STATE = 128
CONV_WIDTH = 4
XBC_WIDTH = SSD_WIDTH + 2 * SSD_GROUPS * SSD_STATE
MEM_TOKENS = 256
MEM_HEADS = 4
MEM_WIDTH = MEM_HEADS * HEAD_DIM
PAGE_SIZE = 128
EPS = 1e-6
ATTN_SCALE = HEAD_DIM ** -0.5
LOG2E = math.log2(math.e)

SSD_CHUNK = 128
PROJ_TN = 1024
PROJ_CHUNK = 256
SAMPLE_PAGES_PER_STEP = 16
SAMPLE_BUFS = 2
SSD_SEQS_PER_STEP = 4
SUBLANES = 8
VMEM_LIMIT = 56 * 1024 * 1024

_NT = (((1,), (1,)), ((), ()))


def _dot(a, b):
    return jnp.dot(a, b, preferred_element_type=F32)


def _dot_nt(a, b):
    return lax.dot_general(a, b, _NT, preferred_element_type=F32)


def _split2(x):
    hi = x.astype(BF16)
    lo = (x - hi.astype(F32)).astype(BF16)
    return hi, lo


def _split3(x):
    hi = x.astype(BF16)
    r = x - hi.astype(F32)
    mid = r.astype(BF16)
    lo = (r - mid.astype(F32)).astype(BF16)
    return hi, mid, lo


def _dot_exact_lhs(x, m):
    hi, mid, lo = _split3(x)
    return _dot(hi, m) + _dot(mid, m) + _dot(lo, m)


def _dot_exact_rhs(m, x):
    hi, mid, lo = _split3(x)
    return _dot(m, hi) + _dot(m, mid) + _dot(m, lo)


def _silu(x):
    return x * (1.0 / (1.0 + jnp.exp(-x)))


def _log2_fail_and_hit(z2):
    nz = -z2
    soft = jnp.log(1.0 + jnp.exp2(jnp.minimum(z2, nz))) * LOG2E
    log_fail = jnp.minimum(nz, 0.0) - soft
    return log_fail, log_fail + z2


def _proj_kernel(plan, n_out, tm, x_ref, nw_ref, w_ref, hn_ref, *rest):
    outs = rest[:n_out]
    h_ref = rest[n_out]
    n = pl.program_id(1)

    @pl.when(n == 0)
    def _():
        rc = min(tm, 64)

        def body(r, carry):
            rows = pl.ds(pl.multiple_of(r * rc, rc), rc)
            xv = x_ref[rows, :]
            ms = jnp.mean(xv * xv, axis=-1, keepdims=True)
            h_ref[rows, :] = (xv * lax.rsqrt(ms + EPS) * nw_ref[...]).astype(BF16)
            return carry

        lax.fori_loop(0, tm // rc, body, 0)

    for step, segs in enumerate(plan):
        @pl.when(n == step)
        def _(segs=segs):
            for col0, width, hn_row, o32, o16 in segs:
                for c0 in range(0, width, PROJ_CHUNK):
                    y = _dot_nt(h_ref[...], w_ref[col0 + c0:col0 + c0 + PROJ_CHUNK, :])
                    for c in range(0, PROJ_CHUNK, HEAD_DIM):
                        yc = y[:, c:c + HEAD_DIM]
                        if hn_row is not None:
                            ms = jnp.mean(yc * yc, axis=-1, keepdims=True)
                            yc = yc * lax.rsqrt(ms + EPS) * hn_ref[hn_row:hn_row + 1, :]
                        cols = slice(c0 + c, c0 + c + HEAD_DIM)
                        if o32 is not None:
                            outs[o32][:, cols] = yc
                        if o16 is not None:
                            outs[o16][:, cols] = yc.astype(BF16)


def _proj(x, norm_w, w_t, head_norms, plan, out_defs, tm):
    t, d = x.shape
    n_steps = len(plan)
    assert w_t.shape == (n_steps * PROJ_TN, d) and t % tm == 0
    kern = functools.partial(_proj_kernel, plan, len(out_defs), tm)
    return pl.pallas_call(
        kern,
        grid=(t // tm, n_steps),
        in_specs=[
            pl.BlockSpec((tm, d), lambda m, n: (m, 0)),
            pl.BlockSpec((1, d), lambda m, n: (0, 0)),
            pl.BlockSpec((PROJ_TN, d), lambda m, n: (n, 0)),
            pl.BlockSpec((SUBLANES, HEAD_DIM), lambda m, n: (0, 0)),
        ],
        out_specs=[pl.BlockSpec((tm, w), lambda m, n: (m, 0)) for w, _ in out_defs],
        out_shape=[jax.ShapeDtypeStruct((t, w), dt) for w, dt in out_defs],
        scratch_shapes=[pltpu.VMEM((tm, d), BF16)],
        compiler_params=pltpu.CompilerParams(
            dimension_semantics=("parallel", "arbitrary"), vmem_limit_bytes=VMEM_LIMIT),
        name="norm_proj",
    )(x, norm_w.reshape(1, d), w_t, head_norms)


def _sb_scores(q, ks, uu, bias2, diag):
    lf, lh = _log2_fail_and_hit(_dot_nt(q, ks) * (ATTN_SCALE * LOG2E) + bias2)
    if diag is not None:
        lf = jnp.where(diag, lf, 0.0)
    hi, lo = _split2(lf)
    between = _dot(jnp.concatenate([hi, lo], axis=1), uu)
    return lf, lh, between


def _sb_prompt_kernel(tq, n_heads, bias_ref, q_ref, k_ref, v_ref, g_ref, uu_ref, o_ref, acc_ref,
                      c_ref):
    hg = pl.program_id(1)
    i = pl.program_id(2)
    uu = uu_ref[...]
    cols = [slice(HEAD_DIM * h, HEAD_DIM * (h + 1)) for h in range(n_heads)]
    biases = [bias_ref[hg * n_heads + h] * LOG2E for h in range(n_heads)]
    qs = [q_ref[:, cols[h]] for h in range(n_heads)]

    def block(start, diag, first):
        kb = k_ref[pl.ds(start, tq), :]
        vb = v_ref[pl.ds(start, tq), :]
        parts = [_sb_scores(qs[h], kb[:, cols[h]], uu, biases[h], diag) for h in range(n_heads)]
        for h in range(n_heads):
            lf, lh, between = parts[h]
            logw = lh + between
            if not first:
                logw = logw + c_ref[h]
            w = jnp.exp2(logw)
            if diag is not None:
                w = jnp.where(diag, w, 0.0)
            pv = _dot(w.astype(BF16), vb[:, cols[h]])
            total = between[:, 0:1] + lf[:, 0:1]
            if first:
                acc_ref[:, cols[h]] = pv
                c_ref[h] = total
            else:
                acc_ref[:, cols[h]] += pv
                c_ref[h] += total

    row = lax.broadcasted_iota(jnp.int32, (tq, tq), 0)
    col = lax.broadcasted_iota(jnp.int32, (tq, tq), 1)
    block(pl.multiple_of(i * tq, tq), col < row, True)

    def body(jj, carry):
        block(pl.multiple_of((i - 1 - jj) * tq, tq), None, False)
        return carry

    lax.fori_loop(0, i, body, 0)
    o_ref[...] = (acc_ref[...] * _silu(g_ref[...])).astype(o_ref.dtype)


def _sb_prompt(q, k, v, g, sb_bias, batch, seq, tq, n_heads):
    q3, k3, v3, g3 = (a.reshape(batch, seq, SB_WIDTH) for a in (q, k, v, g))
    u = (jnp.arange(tq)[:, None] > jnp.arange(tq)[None, :]).astype(BF16)
    uu = jnp.concatenate([u, u], axis=0)
    width = n_heads * HEAD_DIM
    kern = functools.partial(_sb_prompt_kernel, tq, n_heads)
    out = pl.pallas_call(
        kern,
        grid=(batch, SB_HEADS // n_heads, seq // tq),
        in_specs=[
            pl.BlockSpec(memory_space=pltpu.SMEM),
            pl.BlockSpec((None, tq, width), lambda b, h, i: (b, i, h)),
            pl.BlockSpec((None, seq, width), lambda b, h, i: (b, 0, h)),
            pl.BlockSpec((None, seq, width), lambda b, h, i: (b, 0, h)),
            pl.BlockSpec((None, tq, width), lambda b, h, i: (b, i, h)),
            pl.BlockSpec((2 * tq, tq), lambda b, h, i: (0, 0)),
        ],
        out_specs=pl.BlockSpec((None, tq, width), lambda b, h, i: (b, i, h)),
        out_shape=jax.ShapeDtypeStruct((batch, seq, SB_WIDTH), BF16),
        scratch_shapes=[pltpu.VMEM((tq, width), F32), pltpu.VMEM((n_heads, tq, 1), F32)],
        compiler_params=pltpu.CompilerParams(
            dimension_semantics=("parallel", "parallel", "arbitrary"),
            vmem_limit_bytes=VMEM_LIMIT),
        name="sb_prompt",
    )(sb_bias, q3, k3, v3, g3, uu)
    return out.reshape(batch * seq, SB_WIDTH)


def _sb_sample_kernel(t_new, n_group, pt_ref, q_ref, kn_ref, vn_ref, *rest):
    kps, vps = rest[:n_group], rest[n_group:2 * n_group]
    g_ref, bias_ref, uo_ref, o_ref, acc_ref, c_ref = rest[2 * n_group:]
    s = pl.program_id(1)
    lanes = PAGE_SIZE * SB_HEADS
    n_blk = lanes // HEAD_DIM
    lane = lax.broadcasted_iota(jnp.int32, (t_new, lanes), 1)
    lane_head = lane & (SB_HEADS - 1)
    q = q_ref[...]
    q_all = jnp.concatenate([q[:, HEAD_DIM * h:HEAD_DIM * (h + 1)] for h in range(SB_HEADS)],
                            axis=0).astype(BF16)

    def scores(k_ref, mask):
        s_all = _dot_nt(q_all, k_ref[...].astype(BF16))
        sc = s_all[0:t_new, :]
        for h in range(1, SB_HEADS):
            sc = jnp.where(lane_head == h, s_all[t_new * h:t_new * (h + 1), :], sc)
        lf, lh = _log2_fail_and_hit(sc * (ATTN_SCALE * LOG2E) + bias_ref[...])
        if mask is not None:
            lf = jnp.where(mask, lf, 0.0)
        blocks = jnp.concatenate([lf[:, HEAD_DIM * j:HEAD_DIM * (j + 1)] for j in range(n_blk)],
                                 axis=0)
        hi, lo = _split2(blocks)
        return lh, _dot(hi, uo_ref[...]) + _dot(lo, uo_ref[...])

    def weights(lh, res, mask, run):
        ws = [None] * n_blk
        for j in reversed(range(n_blk)):
            rows = slice(t_new * j, t_new * (j + 1))
            logw = lh[:, HEAD_DIM * j:HEAD_DIM * (j + 1)] + res[rows, :HEAD_DIM]
            if run is not None:
                logw = logw + run
            ws[j] = jnp.exp2(logw)
            tot = res[rows, HEAD_DIM:]
            run = tot if run is None else run + tot
        w = jnp.concatenate(ws, axis=1)
        if mask is not None:
            w = jnp.where(mask, w, 0.0)
        w_all = jnp.concatenate([jnp.where(lane_head == h, w, 0.0) for h in range(SB_HEADS)],
                                axis=0).astype(BF16)
        return w_all, run

    @pl.when(s == 0)
    def _():
        mask = (lane >> 3) < lax.broadcasted_iota(jnp.int32, (t_new, lanes), 0)
        lh, res = scores(kn_ref, mask)
        w_all, run = weights(lh, res, mask, None)
        c_ref[...] = run
        acc_ref[...] = _dot(w_all, vn_ref[...].astype(BF16))

    @pl.when(s > 0)
    def _():
        parts = [scores(kps[j], None) for j in range(n_group)]
        run = c_ref[...]
        acc = acc_ref[...]
        for j in range(n_group):
            w_all, run = weights(*parts[j], None, run)
            acc = acc + _dot(w_all, vps[j][...].astype(BF16))
        c_ref[...] = run
        acc_ref[...] = acc

    @pl.when(s == pl.num_programs(1) - 1)
    def _():
        g = g_ref[...]
        for h in range(SB_HEADS):
            cols = slice(HEAD_DIM * h, HEAD_DIM * (h + 1))
            o_ref[:, cols] = acc_ref[t_new * h:t_new * (h + 1), :] * _silu(g[:, cols])


def _sb_sample(q, k_new, v_new, g, cache_k, cache_v, layer, page_table, sb_bias, t_new):
    n_seq, n_pages = page_table.shape
    n_pool = cache_k.shape[1]
    n_group = SAMPLE_PAGES_PER_STEP
    assert n_pages % n_group == 0
    rows = SB_HEADS * t_new
    page_rows = PAGE_SIZE * SB_HEADS

    def as_page(a):
        a = a.reshape(n_seq, t_new, SB_HEADS, HEAD_DIM)
        a = jnp.pad(a, ((0, 0), (0, PAGE_SIZE - t_new), (0, 0), (0, 0)))
        return a.reshape(n_seq * page_rows, HEAD_DIM)

    cache_k = cache_k.reshape(-1, page_rows, HEAD_DIM)
    cache_v = cache_v.reshape(-1, page_rows, HEAD_DIM)

    lane_head = jnp.arange(page_rows) % SB_HEADS
    bias_lanes = jnp.broadcast_to((sb_bias * LOG2E)[lane_head][None, :], (t_new, page_rows))
    r = jnp.arange(HEAD_DIM)
    same_head = (r[:, None] % SB_HEADS) == (r[None, :] % SB_HEADS)
    later = (r[:, None] // SB_HEADS) > (r[None, :] // SB_HEADS)
    uo = jnp.concatenate([same_head & later, same_head], axis=1).astype(BF16)

    def page_spec(j):
        def index_map(b, s, pt):
            page = (n_pages - 1) - (jnp.maximum(s - 1, 0) * n_group + j)
            return (layer * n_pool + pt[b * n_pages + page], 0, 0)
        return pl.BlockSpec((None, page_rows, HEAD_DIM), index_map)

    pages = [page_spec(j) for j in range(n_group)]
    kern = functools.partial(_sb_sample_kernel, t_new, n_group)
    grid_spec = pltpu.PrefetchScalarGridSpec(
        num_scalar_prefetch=1,
        grid=(n_seq, n_pages // n_group + 1),
        in_specs=[
            pl.BlockSpec((t_new, SB_WIDTH), lambda b, s, pt: (b, 0)),
            pl.BlockSpec((page_rows, HEAD_DIM), lambda b, s, pt: (b, 0)),
            pl.BlockSpec((page_rows, HEAD_DIM), lambda b, s, pt: (b, 0)),
            *pages,
            *pages,
            pl.BlockSpec((t_new, SB_WIDTH), lambda b, s, pt: (b, 0)),
            pl.BlockSpec((t_new, page_rows), lambda b, s, pt: (0, 0)),
            pl.BlockSpec((HEAD_DIM, 2 * HEAD_DIM), lambda b, s, pt: (0, 0)),
        ],
        out_specs=pl.BlockSpec((t_new, SB_WIDTH), lambda b, s, pt: (b, 0)),
        scratch_shapes=[pltpu.VMEM((rows, HEAD_DIM), F32), pltpu.VMEM((t_new, HEAD_DIM), F32)],
    )
    return pl.pallas_call(
        kern,
        grid_spec=grid_spec,
        out_shape=jax.ShapeDtypeStruct((n_seq * t_new, SB_WIDTH), F32),
        compiler_params=pltpu.CompilerParams(
            dimension_semantics=("parallel", "arbitrary"), vmem_limit_bytes=VMEM_LIMIT),
        name="sb_sample",
    )(page_table.reshape(-1), q, as_page(k_new), as_page(v_new), *([cache_k] * n_group),
      *([cache_v] * n_group), g, bias_lanes, uo)


def _sb_fused_kernel(tq, nq, ppc, n_pages, t_new, page_base,
                     pt_ref, bias_ref, qa_ref, qb_ref, k_ref, v_ref, ga_ref, gb_ref, uu_ref,
                     qs_ref, kn_ref, vn_ref, gs_ref, bl_ref, uo_ref, ck_hbm, cv_hbm,
                     op_ref, os_ref, acc_ref, c_ref, accs_ref, cs_ref, kbuf, vbuf, sem):
    step = pl.program_id(0)
    n_steps = pl.num_programs(0)
    pair = lax.rem(step, nq // 2)
    i1 = pair
    i2 = nq - 1 - pair
    chunks_per_seq = n_pages // ppc
    seq_steps = chunks_per_seq // nq
    part = lax.rem(step, seq_steps)
    n_chunks = n_steps * nq

    def chunk_copies(chunk):
        slot = lax.rem(chunk, SAMPLE_BUFS)
        seq = lax.div(chunk, chunks_per_seq)
        first_pos = (n_pages - 1) - lax.rem(chunk, chunks_per_seq) * ppc
        copies = []
        for j in range(ppc):
            page = page_base + pt_ref[seq * n_pages + first_pos - j]
            copies.append(pltpu.make_async_copy(ck_hbm.at[page], kbuf.at[slot, j], sem.at[slot, 0]))
            copies.append(pltpu.make_async_copy(cv_hbm.at[page], vbuf.at[slot, j], sem.at[slot, 1]))
        return copies

    def start_chunk(chunk):
        for cp in chunk_copies(chunk):
            cp.start()

    def wait_chunk(chunk):
        for cp in chunk_copies(chunk):
            cp.wait()

    def start_next(chunk):
        @pl.when(chunk + SAMPLE_BUFS < n_chunks)
        def _():
            start_chunk(chunk + SAMPLE_BUFS)

    @pl.when(step == 0)
    def _():
        for c in range(SAMPLE_BUFS):
            start_chunk(jnp.int32(c))

    lanes = PAGE_SIZE * SB_HEADS
    n_blk = lanes // HEAD_DIM
    lane = lax.broadcasted_iota(jnp.int32, (t_new, lanes), 1)
    lane_head = lane & (SB_HEADS - 1)
    qs = qs_ref[...]
    q_all = jnp.concatenate([qs[:, HEAD_DIM * h:HEAD_DIM * (h + 1)] for h in range(SB_HEADS)],
                            axis=0).astype(BF16)

    def scores(kpage, mask):
        s_all = _dot_nt(q_all, kpage.astype(BF16))
        sc = s_all[0:t_new, :]
        for h in range(1, SB_HEADS):
            sc = jnp.where(lane_head == h, s_all[t_new * h:t_new * (h + 1), :], sc)
        lf, lh = _log2_fail_and_hit(sc * (ATTN_SCALE * LOG2E) + bl_ref[...])
        if mask is not None:
            lf = jnp.where(mask, lf, 0.0)
        blocks = jnp.concatenate([lf[:, HEAD_DIM * j:HEAD_DIM * (j + 1)] for j in range(n_blk)],
                                 axis=0)
        hi, lo = _split2(blocks)
        return lh, _dot(hi, uo_ref[...]) + _dot(lo, uo_ref[...])

    def weights(lh, res, mask, run):
        ws = [None] * n_blk
        for j in reversed(range(n_blk)):
            rows = slice(t_new * j, t_new * (j + 1))
            logw = lh[:, HEAD_DIM * j:HEAD_DIM * (j + 1)] + res[rows, :HEAD_DIM]
            if run is not None:
                logw = logw + run
            ws[j] = jnp.exp2(logw)
            tot = res[rows, HEAD_DIM:]
            run = tot if run is None else run + tot
        w = jnp.concatenate(ws, axis=1)
        if mask is not None:
            w = jnp.where(mask, w, 0.0)
        w_all = jnp.concatenate([jnp.where(lane_head == h, w, 0.0) for h in range(SB_HEADS)],
                                axis=0).astype(BF16)
        return w_all, run

    def new_keys():
        mask = (lane >> 3) < lax.broadcasted_iota(jnp.int32, (t_new, lanes), 0)
        lh, res = scores(kn_ref[...], mask)
        w_all, run = weights(lh, res, mask, None)
        cs_ref[...] = run
        accs_ref[...] = _dot(w_all, vn_ref[...].astype(BF16))

    uu = uu_ref[...]
    cols = [slice(HEAD_DIM * h, HEAD_DIM * (h + 1)) for h in range(SB_HEADS)]
    biases = [bias_ref[h] * LOG2E for h in range(SB_HEADS)]

    def section(q_ref, start, diag, first, chunk):
        kb = k_ref[pl.ds(start, tq), :]
        vb = v_ref[pl.ds(start, tq), :]
        n_pages_here = 0 if chunk is None else ppc
        slot = None if chunk is None else lax.rem(chunk, SAMPLE_BUFS)
        p_parts, s_parts = [], []
        for idx in range(max(SB_HEADS, n_pages_here)):
            if idx < SB_HEADS:
                p_parts.append(_sb_scores(q_ref[:, cols[idx]], kb[:, cols[idx]], uu, biases[idx],
                                          diag))
            if idx < n_pages_here:
                s_parts.append(scores(kbuf[slot, idx], None))
        if n_pages_here:
            run = cs_ref[...]
            acc = accs_ref[...]
        for idx in range(max(SB_HEADS, n_pages_here)):
            if idx < SB_HEADS:
                lf, lh, between = p_parts[idx]
                logw = lh + between
                if not first:
                    logw = logw + c_ref[idx]
                w = jnp.exp2(logw)
                if diag is not None:
                    w = jnp.where(diag, w, 0.0)
                pv = _dot(w.astype(BF16), vb[:, cols[idx]])
                total = between[:, 0:1] + lf[:, 0:1]
                if first:
                    acc_ref[:, cols[idx]] = pv
                    c_ref[idx] = total
                else:
                    acc_ref[:, cols[idx]] += pv
                    c_ref[idx] += total
            if idx < n_pages_here:
                w_all, run = weights(*s_parts[idx], None, run)
                acc = acc + _dot(w_all, vbuf[slot, idx].astype(BF16))
        if n_pages_here:
            cs_ref[...] = run
            accs_ref[...] = acc

    def finish(i, g_ref):
        rows = pl.ds(pl.multiple_of(i * tq, tq), tq)
        op_ref[rows, :] = (acc_ref[...] * _silu(g_ref[...])).astype(op_ref.dtype)

    row = lax.broadcasted_iota(jnp.int32, (tq, tq), 0)
    col = lax.broadcasted_iota(jnp.int32, (tq, tq), 1)
    diag = col < row
    chunk0 = step * nq

    @pl.when(part == 0)
    def _():
        new_keys()

    section(qa_ref, pl.multiple_of(i1 * tq, tq), diag, True, None)

    def body_a(t, carry):
        chunk = chunk0 + t
        wait_chunk(chunk)
        section(qa_ref, pl.multiple_of((i1 - 1 - t) * tq, tq), None, False, chunk)
        start_next(chunk)
        return carry

    lax.fori_loop(0, i1, body_a, 0)
    finish(i1, ga_ref)

    chunk = chunk0 + i1
    wait_chunk(chunk)
    section(qb_ref, pl.multiple_of(i2 * tq, tq), diag, True, chunk)
    start_next(chunk)

    def body_b(t, carry):
        chunk = chunk0 + i1 + 1 + t
        wait_chunk(chunk)
        section(qb_ref, pl.multiple_of((i2 - 1 - t) * tq, tq), None, False, chunk)
        start_next(chunk)
        return carry

    lax.fori_loop(0, i2, body_b, 0)
    finish(i2, gb_ref)

    @pl.when(part == seq_steps - 1)
    def _():
        g = gs_ref[...]
        for h in range(SB_HEADS):
            os_ref[:, cols[h]] = accs_ref[t_new * h:t_new * (h + 1), :] * _silu(g[:, cols[h]])


def _sb_fused(q, k, v, g, batch, seq, tq, q_s, k_new, v_new, g_s, cache_k, cache_v, layer,
              page_table, sb_bias, t_new):
    n_seq, n_pages = page_table.shape
    n_pool = cache_k.shape[1]
    nq = seq // tq
    n_steps = batch * (nq // 2)
    assert nq % 2 == 0 and (n_seq * n_pages) % (n_steps * nq) == 0
    ppc = n_seq * n_pages // (n_steps * nq)
    assert n_pages % (ppc * nq) == 0
    seq_steps = n_pages // (ppc * nq)
    page_rows = PAGE_SIZE * SB_HEADS

    q3, k3, v3, g3 = (a.reshape(batch, seq, SB_WIDTH) for a in (q, k, v, g))
    u = (jnp.arange(tq)[:, None] > jnp.arange(tq)[None, :]).astype(BF16)
    uu = jnp.concatenate([u, u], axis=0)

    def as_page(a):
        a = a.reshape(n_seq, t_new, SB_HEADS, HEAD_DIM)
        a = jnp.pad(a, ((0, 0), (0, PAGE_SIZE - t_new), (0, 0), (0, 0)))
        return a.reshape(n_seq * page_rows, HEAD_DIM)

    cache_k = cache_k.reshape(-1, page_rows, HEAD_DIM)
    cache_v = cache_v.reshape(-1, page_rows, HEAD_DIM)
    lane_head = jnp.arange(page_rows) % SB_HEADS
    bias_lanes = jnp.broadcast_to((sb_bias * LOG2E)[lane_head][None, :], (t_new, page_rows))
    r = jnp.arange(HEAD_DIM)
    same_head = (r[:, None] % SB_HEADS) == (r[None, :] % SB_HEADS)
    later = (r[:, None] // SB_HEADS) > (r[None, :] // SB_HEADS)
    uo = jnp.concatenate([same_head & later, same_head], axis=1).astype(BF16)

    half = nq // 2
    b_of = lambda s: s // half
    qa_map = lambda s, pt: (b_of(s), s % half, 0)
    qb_map = lambda s, pt: (b_of(s), nq - 1 - s % half, 0)
    seq_map = lambda s, pt: (b_of(s), 0, 0)
    samp_map = lambda s, pt: (s // seq_steps, 0)
    const = lambda s, pt: (0, 0)
    kern = functools.partial(_sb_fused_kernel, tq, nq, ppc, n_pages, t_new, layer * n_pool)
    grid_spec = pltpu.PrefetchScalarGridSpec(
        num_scalar_prefetch=1,
        grid=(n_steps,),
        in_specs=[
            pl.BlockSpec(memory_space=pltpu.SMEM),
            pl.BlockSpec((None, tq, SB_WIDTH), qa_map),
            pl.BlockSpec((None, tq, SB_WIDTH), qb_map),
            pl.BlockSpec((None, seq, SB_WIDTH), seq_map, pipeline_mode=pl.Buffered(1)),
            pl.BlockSpec((None, seq, SB_WIDTH), seq_map, pipeline_mode=pl.Buffered(1)),
            pl.BlockSpec((None, tq, SB_WIDTH), qa_map),
            pl.BlockSpec((None, tq, SB_WIDTH), qb_map),
            pl.BlockSpec((2 * tq, tq), const),
            pl.BlockSpec((t_new, SB_WIDTH), samp_map),
            pl.BlockSpec((page_rows, HEAD_DIM), samp_map),
            pl.BlockSpec((page_rows, HEAD_DIM), samp_map),
            pl.BlockSpec((t_new, SB_WIDTH), samp_map),
            pl.BlockSpec((t_new, page_rows), const),
            pl.BlockSpec((HEAD_DIM, 2 * HEAD_DIM), const),
            pl.BlockSpec(memory_space=pl.ANY),
            pl.BlockSpec(memory_space=pl.ANY),
        ],
        out_specs=[
            pl.BlockSpec((None, seq, SB_WIDTH), seq_map, pipeline_mode=pl.Buffered(1)),
            pl.BlockSpec((t_new, SB_WIDTH), samp_map),
        ],
        scratch_shapes=[
            pltpu.VMEM((tq, SB_WIDTH), F32),
            pltpu.VMEM((SB_HEADS, tq, 1), F32),
            pltpu.VMEM((SB_HEADS * t_new, HEAD_DIM), F32),
            pltpu.VMEM((t_new, HEAD_DIM), F32),
            pltpu.VMEM((SAMPLE_BUFS, ppc, page_rows, HEAD_DIM), F32),
            pltpu.VMEM((SAMPLE_BUFS, ppc, page_rows, HEAD_DIM), F32),
            pltpu.SemaphoreType.DMA((SAMPLE_BUFS, 2)),
        ],
    )
    out_p, out_s = pl.pallas_call(
        kern,
        grid_spec=grid_spec,
        out_shape=[jax.ShapeDtypeStruct((batch, seq, SB_WIDTH), BF16),
                   jax.ShapeDtypeStruct((n_seq * t_new, SB_WIDTH), F32)],
        compiler_params=pltpu.CompilerParams(
            dimension_semantics=("arbitrary",), vmem_limit_bytes=VMEM_LIMIT),
        name="sb_fused",
    )(page_table.reshape(-1), sb_bias, q3, q3, k3, v3, g3, g3, uu,
      q_s, as_page(k_new), as_page(v_new), g_s, bias_lanes, uo, cache_k, cache_v)
    return out_p.reshape(batch * seq, SB_WIDTH), out_s


def _ssd_kernel(length, n_par, *refs):
    per_seq_in, shared, per_seq_out = refs[:5], refs[5:13], refs[13:]
    pre_ref, h0_ref = per_seq_in[3:5]
    ext_ref, st_ref = per_seq_out[3:5]

    @pl.when(pl.program_id(1) == 0)
    def _():
        ext_ref[:, 0:SUBLANES, :] = pre_ref[...]
        st_ref[...] = h0_ref[...]

    for s in range(n_par):
        _ssd_chunk(length, *[r.at[s] for r in per_seq_in[:3]], *shared,
                   *[r.at[s] for r in per_seq_out])


def _pad_rows(a, rows):
    if a.shape[0] == rows:
        return a
    return jnp.concatenate([a, jnp.zeros((rows - a.shape[0], a.shape[1]), a.dtype)], axis=0)


def _ssd_chunk(length, xbc_ref, z_ref, dt_ref, cw_ref, cb_ref, dtb_ref,
               alog_ref, dsk_ref, nw_ref, ltri_ref, e_ref, out_ref, cnew_ref, snew_ref,
               ext_ref, st_ref):
    L = SSD_CHUNK
    P = SSD_HEAD_DIM

    ext_ref[SUBLANES:SUBLANES + L, :] = _pad_rows(xbc_ref[...], L)
    cw = cw_ref[...]
    conv = cb_ref[...]
    for j in range(CONV_WIDTH):
        off = SUBLANES - (CONV_WIDTH - 1) + j
        conv = conv + ext_ref[off:off + L, :] * cw[j:j + 1, :]
    act = _silu(conv)
    tail = ext_ref[length:length + SUBLANES, :]
    cnew_ref[...] = tail
    ext_ref[0:SUBLANES, :] = tail

    xs = act[:, :SSD_WIDTH]
    bm = act[:, SSD_WIDTH:SSD_WIDTH + SSD_GROUPS * SSD_STATE]
    cm = act[:, SSD_WIDTH + SSD_GROUPS * SSD_STATE:]

    x_dt = _pad_rows(dt_ref[...], L) + dtb_ref[...]
    dt = jnp.maximum(x_dt, 0.0) + jnp.log1p(jnp.exp(-jnp.abs(x_dt)))
    if length < L:
        valid = lax.broadcasted_iota(jnp.int32, dt.shape, 0) < length
        dt = jnp.where(valid, dt, 0.0)
    da = dt * (-jnp.exp(alog_ref[...]))
    cs = _dot_exact_rhs(ltri_ref[...], da)
    cs_t = cs.T
    e = e_ref[...]
    dt_x = _dot_exact_lhs(dt, e)
    cs_x = _dot_exact_lhs(cs, e)
    xdt = xs * dt_x
    ecs = jnp.exp(cs_x)
    xw_t = (xdt * jnp.exp(cs_x[L - 1:L, :] - cs_x)).T
    xdt16 = xdt.astype(BF16)

    row = lax.broadcasted_iota(jnp.int32, (L, L), 0)
    col = lax.broadcasted_iota(jnp.int32, (L, L), 1)
    causal = col <= row
    heads_per_group = SSD_HEADS // SSD_GROUPS
    gw = heads_per_group * P
    y_diag, y_off = [], []
    for g in range(SSD_GROUPS):
        bg = bm[:, SSD_STATE * g:SSD_STATE * (g + 1)].astype(BF16)
        cg = cm[:, SSD_STATE * g:SSD_STATE * (g + 1)].astype(BF16)
        cb = _dot_nt(cg, bg)
        prev = st_ref[gw * g:gw * (g + 1), :]
        y_off.append(_dot_nt(cg, prev.astype(BF16)))
        new = _dot(xw_t[gw * g:gw * (g + 1), :].astype(BF16), bg)
        for r in range(heads_per_group):
            h = heads_per_group * g + r
            seg = cs[:, h:h + 1] - cs_t[h:h + 1, :]
            decay = jnp.exp(jnp.where(causal, seg, -jnp.inf))
            y_diag.append(_dot((cb * decay).astype(BF16), xdt16[:, P * h:P * (h + 1)]))
            chunk_decay = jnp.exp(cs[L - 1:L, h:h + 1])
            st_ref[P * h:P * (h + 1), :] = (prev[P * r:P * (r + 1), :] * chunk_decay
                                            + new[P * r:P * (r + 1), :])
    snew_ref[...] = st_ref[...]
    y = (jnp.concatenate(y_diag, axis=1) + jnp.concatenate(y_off, axis=1) * ecs
         + xs * dsk_ref[...])
    gated = y[:length] * _silu(z_ref[...])
    ms = jnp.mean(gated * gated, axis=-1, keepdims=True)
    out_ref[...] = (gated * lax.rsqrt(ms + EPS) * nw_ref[...]).astype(out_ref.dtype)


def _ssd(xbc, zdt, prefix, h0, conv_w, conv_b, dt_bias, a_log, d_skip, ssd_norm_w,
         batch, n_chunks, length, out_dtype):
    L = SSD_CHUNK
    pre = jnp.pad(prefix, ((0, 0), (SUBLANES - (CONV_WIDTH - 1), 0), (0, 0)))
    pad_h = lambda a: jnp.pad(a, (0, HEAD_DIM - SSD_HEADS)).reshape(1, HEAD_DIM)
    ltri = (jnp.arange(L)[:, None] >= jnp.arange(L)[None, :]).astype(BF16)
    expand = (jnp.arange(HEAD_DIM)[:, None] == jnp.arange(SSD_WIDTH)[None, :] // SSD_HEAD_DIM
              ).astype(BF16)
    dsk = jnp.repeat(d_skip, SSD_HEAD_DIM).reshape(1, SSD_WIDTH)
    z_blk = SSD_WIDTH // HEAD_DIM
    n_par = SSD_SEQS_PER_STEP
    assert batch % n_par == 0 and length <= L and (length == L or n_chunks == 1)
    rows = n_chunks * length
    xbc3 = xbc.reshape(batch, rows, XBC_WIDTH)
    zdt3 = zdt.reshape(batch, rows, PROJ_TN)
    const = lambda b, c: (0, 0)
    seq = lambda b, c: (b, 0, 0)
    kern = functools.partial(_ssd_kernel, length, n_par)
    out, conv_new, ssm_new = pl.pallas_call(
        kern,
        grid=(batch // n_par, n_chunks),
        in_specs=[
            pl.BlockSpec((n_par, length, XBC_WIDTH), lambda b, c: (b, c, 0)),
            pl.BlockSpec((n_par, length, SSD_WIDTH), lambda b, c: (b, c, 0)),
            pl.BlockSpec((n_par, length, HEAD_DIM), lambda b, c: (b, c, z_blk)),
            pl.BlockSpec((n_par, SUBLANES, XBC_WIDTH), seq),
            pl.BlockSpec((n_par, SSD_WIDTH, SSD_STATE), seq),
            pl.BlockSpec((CONV_WIDTH, XBC_WIDTH), const),
            pl.BlockSpec((1, XBC_WIDTH), const),
            pl.BlockSpec((1, HEAD_DIM), const),
            pl.BlockSpec((1, HEAD_DIM), const),
            pl.BlockSpec((1, SSD_WIDTH), const),
            pl.BlockSpec((1, SSD_WIDTH), const),
            pl.BlockSpec((L, L), const),
            pl.BlockSpec((HEAD_DIM, SSD_WIDTH), const),
        ],
        out_specs=[
            pl.BlockSpec((n_par, length, SSD_WIDTH), lambda b, c: (b, c, 0)),
            pl.BlockSpec((n_par, SUBLANES, XBC_WIDTH), seq),
            pl.BlockSpec((n_par, SSD_WIDTH, SSD_STATE), seq),
        ],
        out_shape=[
            jax.ShapeDtypeStruct((batch, rows, SSD_WIDTH), out_dtype),
            jax.ShapeDtypeStruct((batch, SUBLANES, XBC_WIDTH), F32),
            jax.ShapeDtypeStruct((batch, SSD_WIDTH, SSD_STATE), F32),
        ],
        scratch_shapes=[pltpu.VMEM((n_par, SUBLANES + L, XBC_WIDTH), F32),
                        pltpu.VMEM((n_par, SSD_WIDTH, SSD_STATE), F32)],
        compiler_params=pltpu.CompilerParams(
            dimension_semantics=("parallel", "arbitrary"), vmem_limit_bytes=VMEM_LIMIT),
        name="ssd_scan",
    )(xbc3, zdt3, zdt3, pre, h0.reshape(batch, SSD_WIDTH, SSD_STATE), conv_w,
      conv_b.reshape(1, XBC_WIDTH), pad_h(dt_bias), pad_h(a_log), dsk,
      ssd_norm_w.reshape(1, SSD_WIDTH), ltri, expand)
    return out.reshape(batch * rows, SSD_WIDTH), conv_new, ssm_new


def _mem_attn_kernel(q_ref, g_ref, k_ref, v_ref, o_ref):
    for h in range(MEM_HEADS):
        cols = slice(HEAD_DIM * h, HEAD_DIM * (h + 1))
        s = _dot_nt(q_ref[:, cols].astype(BF16), k_ref[:, cols].astype(BF16)) * ATTN_SCALE
        p = jnp.exp(s - jnp.max(s, axis=-1, keepdims=True))
        den = jnp.sum(p, axis=-1, keepdims=True)
        o = _dot(p.astype(BF16), v_ref[:, cols].astype(BF16)) / den
        o_ref[:, cols] = (o * _silu(g_ref[:, cols])).astype(o_ref.dtype)


def _mem_attn(q, g, mem_k, mem_v, batch, t, tq, out_dtype):
    nq = t // tq
    return pl.pallas_call(
        _mem_attn_kernel,
        grid=(batch, nq),
        in_specs=[
            pl.BlockSpec((tq, MEM_WIDTH), lambda b, i: (b * nq + i, 0)),
            pl.BlockSpec((tq, MEM_WIDTH), lambda b, i: (b * nq + i, 0)),
            pl.BlockSpec((None, MEM_TOKENS, MEM_WIDTH), lambda b, i: (b, 0, 0)),
            pl.BlockSpec((None, MEM_TOKENS, MEM_WIDTH), lambda b, i: (b, 0, 0)),
        ],
        out_specs=pl.BlockSpec((tq, MEM_WIDTH), lambda b, i: (b * nq + i, 0)),
        out_shape=jax.ShapeDtypeStruct((batch * t, MEM_WIDTH), out_dtype),
        compiler_params=pltpu.CompilerParams(
            dimension_semantics=("parallel", "parallel"), vmem_limit_bytes=VMEM_LIMIT),
        name="mem_attn",
    )(q, g, mem_k, mem_v)


def _out_proj_kernel(x_ref, sb_ref, ssd_ref, mo_ref, w_ref, o_ref):
    mix = jnp.concatenate([sb_ref[...].astype(BF16), ssd_ref[...].astype(BF16),
                           mo_ref[...].astype(BF16)], axis=-1)
    for c in range(0, o_ref.shape[1], PROJ_CHUNK):
        cols = slice(c, c + PROJ_CHUNK)
        o_ref[:, cols] = x_ref[:, cols] + _dot(mix, w_ref[:, cols])


def _out_proj(x, sb, ssd, mo, w_out, tm, tn):
    t, d = x.shape
    return pl.pallas_call(
        _out_proj_kernel,
        grid=(t // tm, d // tn),
        in_specs=[
            pl.BlockSpec((tm, tn), lambda m, n: (m, n)),
            pl.BlockSpec((tm, SB_WIDTH), lambda m, n: (m, 0)),
            pl.BlockSpec((tm, SSD_WIDTH), lambda m, n: (m, 0)),
            pl.BlockSpec((tm, MEM_WIDTH), lambda m, n: (m, 0)),
            pl.BlockSpec((w_out.shape[0], tn), lambda m, n: (0, n)),
        ],
        out_specs=pl.BlockSpec((tm, tn), lambda m, n: (m, n)),
        out_shape=jax.ShapeDtypeStruct((t, d), F32),
        compiler_params=pltpu.CompilerParams(
            dimension_semantics=("parallel", "arbitrary"), vmem_limit_bytes=VMEM_LIMIT),
        name="out_proj",
    )(x, sb, ssd, mo, w_out)


def _in_proj_plan(act_dtype):
    lowp = act_dtype == BF16
    outs, plan = [], []

    def add(width, dtype):
        outs.append((width, dtype))
        return len(outs) - 1

    q = add(SB_WIDTH, act_dtype)
    plan.append(((0, SB_WIDTH, 0, None if lowp else q, q if lowp else None),))
    k32 = add(SB_WIDTH, F32)
    k16 = add(SB_WIDTH, BF16) if lowp else None
    plan.append(((0, SB_WIDTH, 1, k32, k16),))
    v32 = add(SB_WIDTH, F32)
    v16 = add(SB_WIDTH, BF16) if lowp else None
    plan.append(((0, SB_WIDTH, None, v32, v16),))
    g = add(SB_WIDTH, F32)
    plan.append(((0, SB_WIDTH, None, g, None),))
    xbc = add(XBC_WIDTH, F32)
    plan.append(((0, XBC_WIDTH, None, xbc, None),))
    zdt = add(PROJ_TN, F32)
    plan.append(((0, PROJ_TN, None, zdt, None),))
    mq = add(MEM_WIDTH, act_dtype)
    mg = add(MEM_WIDTH, F32)
    plan.append(((0, MEM_WIDTH, 2, None if lowp else mq, mq if lowp else None),
                 (MEM_WIDTH, MEM_WIDTH, None, mg, None)))
    names = dict(q=q, k32=k32, k16=k16, v32=v32, v16=v16, g=g, xbc=xbc, zdt=zdt, mq=mq, mg=mg)
    return tuple(plan), outs, names


_O_Z = 4 * SB_WIDTH
_O_XBC = _O_Z + SSD_WIDTH
_O_DT = _O_XBC + XBC_WIDTH
_O_MEM = _O_DT + SSD_HEADS
_IN_WIDTH = _O_MEM + 2 * MEM_WIDTH
_CAT_WIDTH = 7 * PROJ_TN
W_PREP_COLS = 256
BF16_ROWS = 16


def _w_prep_kernel(w_ref, o_ref):
    def put(dst, src, rows):
        o_ref[dst:dst + rows, :] = w_ref[src:src + rows, :].astype(BF16)

    cols = w_ref.shape[1]
    put(0, 0, _O_Z)
    put(_O_Z, _O_XBC, XBC_WIDTH)
    put(_O_Z + XBC_WIDTH, _O_Z, SSD_WIDTH)
    dt0 = _O_Z + XBC_WIDTH + SSD_WIDTH
    o_ref[dt0:dt0 + BF16_ROWS, :] = jnp.concatenate(
        [w_ref[_O_DT:_O_MEM, :], jnp.zeros((BF16_ROWS - SSD_HEADS, cols), F32)], axis=0
    ).astype(BF16)
    o_ref[dt0 + BF16_ROWS:6 * PROJ_TN, :] = jnp.zeros((6 * PROJ_TN - dt0 - BF16_ROWS, cols), BF16)
    put(6 * PROJ_TN, _O_MEM, 2 * MEM_WIDTH)


def _rearranged_w_in(w_t):
    d = w_t.shape[1]
    assert w_t.shape[0] == _IN_WIDTH and d % W_PREP_COLS == 0
    return pl.pallas_call(
        _w_prep_kernel,
        grid=(d // W_PREP_COLS,),
        in_specs=[pl.BlockSpec((_IN_WIDTH, W_PREP_COLS), lambda c: (0, c))],
        out_specs=pl.BlockSpec((_CAT_WIDTH, W_PREP_COLS), lambda c: (0, c)),
        out_shape=jax.ShapeDtypeStruct((_CAT_WIDTH, d), BF16),
        compiler_params=pltpu.CompilerParams(
            dimension_semantics=("parallel",), vmem_limit_bytes=VMEM_LIMIT),
        name="w_prep",
    )(w_t)


def kernel(x_prompt, x_sample, cache_sb_k, cache_sb_v, state_ssm, state_conv, cache_mem_k,
           cache_mem_v, page_table, mem_prompt, norm_w, w_in, sb_q_norm, sb_k_norm, sb_bias,
           conv_w, conv_b, dt_bias, a_log, d_skip, ssd_norm_w, mem_norm_w, w_mem_kv, mem_q_norm,
           mem_k_norm, w_out):
    depth = w_in.shape[0]
    assert depth == 1
    layer = 0
    bp, sp, d = x_prompt.shape
    bs, ts, _ = x_sample.shape
    n_pool = cache_sb_k.shape[1]
    L = SSD_CHUNK

    w_cat = _rearranged_w_in(w_in[layer].T)
    w_o = w_out[layer].astype(BF16)
    w_kv = w_mem_kv[layer].T.astype(BF16)
    head_norms = jnp.concatenate(
        [sb_q_norm[layer][None], sb_k_norm[layer][None], mem_q_norm[layer][None],
         mem_k_norm[layer][None], jnp.zeros((SUBLANES - 4, HEAD_DIM), F32)], axis=0)
    ssd_params = (conv_w[layer], conv_b[layer], dt_bias[layer], a_log[layer], d_skip[layer],
                  ssd_norm_w[layer])

    xp = x_prompt.reshape(bp * sp, d)
    mem_plan = (((0, MEM_WIDTH, 3, 0, None), (MEM_WIDTH, MEM_WIDTH, None, 1, None)),)
    mk, mv = _proj(mem_prompt.reshape(bp * MEM_TOKENS, d), mem_norm_w[layer], w_kv, head_norms,
                   mem_plan, [(MEM_WIDTH, F32), (MEM_WIDTH, F32)], tm=512)
    plan, outs, nm = _in_proj_plan(BF16)
    pr = _proj(xp, norm_w[layer], w_cat, head_norms, plan, outs, tm=512)
    xs = x_sample.reshape(bs * ts, d)
    plan_s, outs_s, ns = _in_proj_plan(F32)
    ps = _proj(xs, norm_w[layer], w_cat, head_norms, plan_s, outs_s, tm=bs * ts)
    sb, sb_s = _sb_fused(pr[nm['q']], pr[nm['k16']], pr[nm['v16']], pr[nm['g']], bp, sp, 256,
                         ps[ns['q']], ps[ns['k32']], ps[ns['v32']], ps[ns['g']],
                         cache_sb_k, cache_sb_v, layer, page_table, sb_bias[layer], ts)
    ssd, conv_p, ssm_p = _ssd(
        pr[nm['xbc']], pr[nm['zdt']], jnp.zeros((bp, CONV_WIDTH - 1, XBC_WIDTH), F32),
        jnp.zeros((bp, SSD_HEADS, SSD_HEAD_DIM, SSD_STATE), F32), *ssd_params,
        batch=bp, n_chunks=sp // L, length=L, out_dtype=BF16)
    mo = _mem_attn(pr[nm['mq']], pr[nm['mg']], mk.reshape(bp, MEM_TOKENS, MEM_WIDTH),
                   mv.reshape(bp, MEM_TOKENS, MEM_WIDTH), bp, sp, tq=512, out_dtype=BF16)
    yp = _out_proj(xp, sb, ssd, mo, w_o, tm=512, tn=d)

    ssd_s, conv_s, ssm_s = _ssd(
        ps[ns['xbc']], ps[ns['zdt']], state_conv[layer], state_ssm[layer],
        *ssd_params, batch=bs, n_chunks=1, length=ts, out_dtype=F32)
    mo_s = _mem_attn(ps[ns['mq']], ps[ns['mg']],
                     cache_mem_k[layer].reshape(bs, MEM_TOKENS, MEM_WIDTH),
                     cache_mem_v[layer].reshape(bs, MEM_TOKENS, MEM_WIDTH), bs, ts, tq=ts,
                     out_dtype=F32)
    ys = _out_proj(xs, sb_s, ssd_s, mo_s, w_o, tm=bs * ts, tn=d)

    tail = slice(SUBLANES - (CONV_WIDTH - 1), SUBLANES)
    return (
        yp.reshape(bp, sp, d),
        ys.reshape(bs, ts, d),
        pr[nm['k32']].reshape(1, bp, sp, SB_HEADS, HEAD_DIM),
        pr[nm['v32']].reshape(1, bp, sp, SB_HEADS, HEAD_DIM),
        ssm_p.reshape(1, bp, SSD_HEADS, SSD_HEAD_DIM, SSD_STATE),
        conv_p[:, tail][None],
        mk.reshape(1, bp, MEM_TOKENS, MEM_HEADS, HEAD_DIM),
        mv.reshape(1, bp, MEM_TOKENS, MEM_HEADS, HEAD_DIM),
        ps[ns['k32']].reshape(1, bs, ts, SB_HEADS, HEAD_DIM),
        ps[ns['v32']].reshape(1, bs, ts, SB_HEADS, HEAD_DIM),
        ssm_s.reshape(1, bs, SSD_HEADS, SSD_HEAD_DIM, SSD_STATE),
        conv_s[:, tail][None],
    )
```

```python
import functools
import math

import jax
import jax.numpy as jnp
from jax import lax
from jax.experimental import pallas as pl
from jax.experimental.pallas import tpu as pltpu

F32 = jnp.float32
BF16 = jnp.bfloat16

D_MODEL = 2048
SB_HEADS = 8
HEAD_DIM = 128
SB_WIDTH = SB_HEADS * HEAD_DIM
SSD_HEADS = 8
SSD_HEAD_DIM = 64
SSD_WIDTH = SSD_HEADS * SSD_HEAD_DIM
SSD_GROUPS = 2
SSD_STATE = 128
CONV_WIDTH = 4
XBC_WIDTH = SSD_WIDTH + 2 * SSD_GROUPS * SSD_STATE
MEM_TOKENS = 256
MEM_HEADS = 4
MEM_WIDTH = MEM_HEADS * HEAD_DIM
PAGE_SIZE = 128
EPS = 1e-6
ATTN_SCALE = HEAD_DIM ** -0.5
LOG2E = math.log2(math.e)

SSD_CHUNK = 128
PROJ_TN = 1024
PROJ_CHUNK = 256
SAMPLE_BUFS = 3
SSD_SEQS_PER_STEP = 4
SUBLANES = 8
VMEM_LIMIT = 56 * 1024 * 1024

_NT = (((1,), (1,)), ((), ()))


def _dot(a, b):
    return jnp.dot(a, b, preferred_element_type=F32)


def _dot_nt(a, b):
    return lax.dot_general(a, b, _NT, preferred_element_type=F32)


def _split2(x):
    hi = x.astype(BF16)
    lo = (x - hi.astype(F32)).astype(BF16)
    return hi, lo


def _split3(x):
    hi = x.astype(BF16)
    r = x - hi.astype(F32)
    mid = r.astype(BF16)
    lo = (r - mid.astype(F32)).astype(BF16)
    return hi, mid, lo


def _dot_exact_lhs(x, m):
    hi, mid, lo = _split3(x)
    return _dot(hi, m) + _dot(mid, m) + _dot(lo, m)


def _dot_exact_rhs(m, x):
    hi, mid, lo = _split3(x)
    return _dot(m, hi) + _dot(m, mid) + _dot(m, lo)


def _silu(x):
    return x * (1.0 / (1.0 + jnp.exp(-x)))


def _rmsnorm_rows(x_ref, nw_ref, h_ref):
    rows_total = x_ref.shape[0]
    rc = min(rows_total, 64)

    def body(r, carry):
        rows = pl.ds(pl.multiple_of(r * rc, rc), rc)
        xv = x_ref[rows, :]
        ms = jnp.mean(xv * xv, axis=-1, keepdims=True)
        h_ref[rows, :] = (xv * lax.rsqrt(ms + EPS) * nw_ref[...]).astype(BF16)
        return carry

    lax.fori_loop(0, rows_total // rc, body, 0)


def _proj_step(segs, h_ref, w_ref, hn_ref, outs):
    for col0, width, hn_row, o32, o16 in segs:
        for c0 in range(0, width, PROJ_CHUNK):
            y = _dot_nt(h_ref[...], w_ref[col0 + c0:col0 + c0 + PROJ_CHUNK, :])
            for c in range(0, PROJ_CHUNK, HEAD_DIM):
                yc = y[:, c:c + HEAD_DIM]
                if hn_row is not None:
                    ms = jnp.mean(yc * yc, axis=-1, keepdims=True)
                    yc = yc * lax.rsqrt(ms + EPS) * hn_ref[hn_row:hn_row + 1, :]
                cols = slice(c0 + c, c0 + c + HEAD_DIM)
                if o32 is not None:
                    outs[o32][:, cols] = yc
                if o16 is not None:
                    outs[o16][:, cols] = yc.astype(BF16)


def _proj_kernel(plan, n_out, side_plan, n_side, *refs):
    if side_plan is None:
        x_ref, nw_ref, w_ref, hn_ref = refs[:4]
        rest = refs[4:]
    else:
        x_ref, xs_ref, nw_ref, w_ref, hn_ref = refs[:5]
        rest = refs[5:]
    outs, side_outs = rest[:n_out], rest[n_out:n_out + n_side]
    scratch = rest[n_out + n_side:]
    h_ref = scratch[0]
    m = pl.program_id(0)
    n = pl.program_id(1)

    @pl.when(n == 0)
    def _():
        _rmsnorm_rows(x_ref, nw_ref, h_ref)

    if side_plan is not None:
        hs_ref = scratch[1]

        @pl.when((n == 0) & (m == 0))
        def _():
            _rmsnorm_rows(xs_ref, nw_ref, hs_ref)

    for step, segs in enumerate(plan):
        @pl.when(n == step)
        def _(step=step, segs=segs):
            _proj_step(segs, h_ref, w_ref, hn_ref, outs)
            if side_plan is not None:
                @pl.when(m == 0)
                def _():
                    _proj_step(side_plan[step], hs_ref, w_ref, hn_ref, side_outs)


def _proj(x, norm_w, w_t, head_norms, plan, out_defs, tm, side=None):
    t, d = x.shape
    n_steps = len(plan)
    assert w_t.shape == (n_steps * PROJ_TN, d) and t % tm == 0
    row = lambda m, n: (m, 0)
    const = lambda m, n: (0, 0)
    in_specs = [pl.BlockSpec((tm, d), row)]
    operands = [x]
    out_specs = [pl.BlockSpec((tm, w), row) for w, _ in out_defs]
    out_shape = [jax.ShapeDtypeStruct((t, w), dt) for w, dt in out_defs]
    scratch = [pltpu.VMEM((tm, d), BF16)]
    side_plan, n_side = None, 0
    if side is not None:
        x_side, side_plan, side_defs = side
        ts = x_side.shape[0]
        assert len(side_plan) == n_steps
        n_side = len(side_defs)
        in_specs.append(pl.BlockSpec((ts, d), const))
        operands.append(x_side)
        out_specs += [pl.BlockSpec((ts, w), const) for w, _ in side_defs]
        out_shape += [jax.ShapeDtypeStruct((ts, w), dt) for w, dt in side_defs]
        scratch.append(pltpu.VMEM((ts, d), BF16))
    in_specs += [pl.BlockSpec((1, d), const), pl.BlockSpec((PROJ_TN, d), lambda m, n: (n, 0)),
                 pl.BlockSpec((SUBLANES, HEAD_DIM), const)]
    operands += [norm_w.reshape(1, d), w_t, head_norms]
    kern = functools.partial(_proj_kernel, plan, len(out_defs), side_plan, n_side)
    res = pl.pallas_call(
        kern,
        grid=(t // tm, n_steps),
        in_specs=in_specs,
        out_specs=out_specs,
        out_shape=out_shape,
        scratch_shapes=scratch,
        compiler_params=pltpu.CompilerParams(
            dimension_semantics=("arbitrary", "arbitrary"), vmem_limit_bytes=VMEM_LIMIT),
        name="norm_proj",
    )(*operands)
    return res[:len(out_defs)], res[len(out_defs):]


def _log2_fail(z2):
    nz = -z2
    return jnp.minimum(nz, 0.0) - jnp.log(1.0 + jnp.exp2(jnp.minimum(z2, nz))) * LOG2E


def _sb_block_logits(q, ks, uu_incl, bias2, diag):
    z2 = _dot_nt(q, ks) * (ATTN_SCALE * LOG2E) + bias2
    lf = _log2_fail(z2)
    if diag is not None:
        lf = jnp.where(diag, lf, 0.0)
    hi, lo = _split2(lf)
    return z2, _dot(jnp.concatenate([hi, lo], axis=1), uu_incl)


def _sb_fused_kernel(tq, nq, ppc, n_pages, t_new, page_base,
                     pt_ref, bias_ref, qa_ref, qb_ref, k_ref, v_ref, ga_ref, gb_ref, uu_ref,
                     qs_ref, kn_ref, vn_ref, gs_ref, bl_ref, uo_ref, ck_hbm, cv_hbm,
                     op_ref, os_ref, acc_ref, c_ref, accs_ref, cs_ref, kbuf, vbuf, sem):
    step = pl.program_id(0)
    n_steps = pl.num_programs(0)
    pair = lax.rem(step, nq // 2)
    i1 = pair
    i2 = nq - 1 - pair
    chunks_per_seq = n_pages // ppc
    seq_steps = chunks_per_seq // nq
    part = lax.rem(step, seq_steps)
    n_chunks = n_steps * nq

    def chunk_copies(chunk):
        slot = lax.rem(chunk, SAMPLE_BUFS)
        seq = lax.div(chunk, chunks_per_seq)
        first_pos = (n_pages - 1) - lax.rem(chunk, chunks_per_seq) * ppc
        copies = []
        for j in range(ppc):
            page = page_base + pt_ref[seq * n_pages + first_pos - j]
            copies.append(pltpu.make_async_copy(ck_hbm.at[page], kbuf.at[slot, j], sem.at[slot, 0]))
            copies.append(pltpu.make_async_copy(cv_hbm.at[page], vbuf.at[slot, j], sem.at[slot, 1]))
        return copies

    def start_chunk(chunk):
        for cp in chunk_copies(chunk):
            cp.start()

    def wait_chunk(chunk):
        slot = lax.rem(chunk, SAMPLE_BUFS)
        pltpu.make_async_copy(ck_hbm.at[pl.ds(0, ppc)], kbuf.at[slot], sem.at[slot, 0]).wait()
        pltpu.make_async_copy(cv_hbm.at[pl.ds(0, ppc)], vbuf.at[slot], sem.at[slot, 1]).wait()

    def start_next(chunk):
        @pl.when(chunk + SAMPLE_BUFS < n_chunks)
        def _():
            start_chunk(chunk + SAMPLE_BUFS)

    @pl.when(step == 0)
    def _():
        for c in range(SAMPLE_BUFS):
            start_chunk(jnp.int32(c))

    lanes = PAGE_SIZE * SB_HEADS
    n_blk = lanes // HEAD_DIM
    lane = lax.broadcasted_iota(jnp.int32, (t_new, lanes), 1)
    lane_head = lane & (SB_HEADS - 1)
    qs = qs_ref[...]
    q_all = jnp.concatenate([qs[:, HEAD_DIM * h:HEAD_DIM * (h + 1)] for h in range(SB_HEADS)],
                            axis=0).astype(BF16)

    def scores(kpage, mask):
        s_all = _dot_nt(q_all, kpage.astype(BF16))
        sc = s_all[0:t_new, :]
        for h in range(1, SB_HEADS):
            sc = jnp.where(lane_head == h, s_all[t_new * h:t_new * (h + 1), :], sc)
        z2 = sc * (ATTN_SCALE * LOG2E) + bl_ref[...]
        lf = _log2_fail(z2)
        if mask is not None:
            lf = jnp.where(mask, lf, 0.0)
        blocks = jnp.concatenate([lf[:, HEAD_DIM * j:HEAD_DIM * (j + 1)] for j in range(n_blk)],
                                 axis=0)
        hi, lo = _split2(blocks)
        return z2, _dot(jnp.concatenate([hi, lo], axis=1), uo_ref[...])

    def weights(z2, res, mask, run):
        ws = [None] * n_blk
        for j in reversed(range(n_blk)):
            rows = slice(t_new * j, t_new * (j + 1))
            logw = z2[:, HEAD_DIM * j:HEAD_DIM * (j + 1)] + res[rows, :HEAD_DIM]
            if run is not None:
                logw = logw + run
            ws[j] = jnp.exp2(logw)
            tot = res[rows, HEAD_DIM:]
            run = tot if run is None else run + tot
        w = jnp.concatenate(ws, axis=1)
        if mask is not None:
            w = jnp.where(mask, w, 0.0)
        w_all = jnp.concatenate([jnp.where(lane_head == h, w, 0.0) for h in range(SB_HEADS)],
                                axis=0).astype(BF16)
        return w_all, run

    def new_keys():
        mask = (lane >> 3) < lax.broadcasted_iota(jnp.int32, (t_new, lanes), 0)
        z2, res = scores(kn_ref[...], mask)
        w_all, run = weights(z2, res, mask, None)
        cs_ref[...] = run
        accs_ref[...] = _dot(w_all, vn_ref[...].astype(BF16))

    uu = uu_ref[...]
    cols = [slice(HEAD_DIM * h, HEAD_DIM * (h + 1)) for h in range(SB_HEADS)]
    biases = [bias_ref[h] * LOG2E for h in range(SB_HEADS)]

    def section(q_ref, start, diag, first, chunk):
        kb = k_ref[pl.ds(start, tq), :]
        vb = v_ref[pl.ds(start, tq), :]
        n_pages_here = 0 if chunk is None else ppc
        slot = None if chunk is None else lax.rem(chunk, SAMPLE_BUFS)
        if n_pages_here:
            run = cs_ref[...]
            acc = accs_ref[...]
        p_parts, s_parts = [], []
        for idx in range(max(SB_HEADS, n_pages_here)):
            if idx < SB_HEADS:
                p_parts.append(_sb_block_logits(q_ref[:, cols[idx]], kb[:, cols[idx]], uu,
                                                biases[idx], diag))
            if idx < n_pages_here:
                s_parts.append(scores(kbuf[slot, idx], None))
        for idx in range(max(SB_HEADS, n_pages_here)):
            if idx < SB_HEADS:
                z2, incl = p_parts[idx]
                logw = z2 + incl
                if not first:
                    logw = logw + c_ref[idx]
                w = jnp.exp2(logw)
                if diag is not None:
                    w = jnp.where(diag, w, 0.0)
                pv = _dot(w.astype(BF16), vb[:, cols[idx]])
                total = incl[:, 0:1]
                if first:
                    acc_ref[:, cols[idx]] = pv
                    c_ref[idx] = total
                else:
                    acc_ref[:, cols[idx]] += pv
                    c_ref[idx] += total
            if idx < n_pages_here:
                w_all, run = weights(*s_parts[idx], None, run)
                acc = acc + _dot(w_all, vbuf[slot, idx].astype(BF16))
        if n_pages_here:
            cs_ref[...] = run
            accs_ref[...] = acc

    def finish(i, g_ref):
        rows = pl.ds(pl.multiple_of(i * tq, tq), tq)
        op_ref[rows, :] = (acc_ref[...] * _silu(g_ref[...])).astype(op_ref.dtype)

    row = lax.broadcasted_iota(jnp.int32, (tq, tq), 0)
    col = lax.broadcasted_iota(jnp.int32, (tq, tq), 1)
    diag = col < row
    chunk0 = step * nq

    @pl.when(part == 0)
    def _():
        new_keys()

    section(qa_ref, pl.multiple_of(i1 * tq, tq), diag, True, None)

    def body_a(t, carry):
        chunk = chunk0 + t
        wait_chunk(chunk)
        section(qa_ref, pl.multiple_of((i1 - 1 - t) * tq, tq), None, False, chunk)
        start_next(chunk)
        return carry

    lax.fori_loop(0, i1, body_a, 0)
    finish(i1, ga_ref)

    chunk = chunk0 + i1
    wait_chunk(chunk)
    section(qb_ref, pl.multiple_of(i2 * tq, tq), diag, True, chunk)
    start_next(chunk)

    def body_b(t, carry):
        chunk = chunk0 + i1 + 1 + t
        wait_chunk(chunk)
        section(qb_ref, pl.multiple_of((i2 - 1 - t) * tq, tq), None, False, chunk)
        start_next(chunk)
        return carry

    lax.fori_loop(0, i2, body_b, 0)
    finish(i2, gb_ref)

    @pl.when(part == seq_steps - 1)
    def _():
        g = gs_ref[...]
        for h in range(SB_HEADS):
            os_ref[:, cols[h]] = accs_ref[t_new * h:t_new * (h + 1), :] * _silu(g[:, cols[h]])


def _sb_fused(q, k, v, g, batch, seq, tq, q_s, k_new, v_new, g_s, cache_k, cache_v, layer,
              page_table, sb_bias, t_new):
    n_seq, n_pages = page_table.shape
    n_pool = cache_k.shape[1]
    nq = seq // tq
    n_steps = batch * (nq // 2)
    assert nq % 2 == 0 and (n_seq * n_pages) % (n_steps * nq) == 0
    ppc = n_seq * n_pages // (n_steps * nq)
    assert n_pages % (ppc * nq) == 0
    seq_steps = n_pages // (ppc * nq)
    page_rows = PAGE_SIZE * SB_HEADS

    q3, k3, v3, g3 = (a.reshape(batch, seq, SB_WIDTH) for a in (q, k, v, g))
    u = (jnp.arange(tq)[:, None] >= jnp.arange(tq)[None, :]).astype(BF16)
    uu = jnp.concatenate([u, u], axis=0)

    def as_page(a):
        a = a.reshape(n_seq, t_new, SB_HEADS, HEAD_DIM)
        a = jnp.pad(a, ((0, 0), (0, PAGE_SIZE - t_new), (0, 0), (0, 0)))
        return a.reshape(n_seq * page_rows, HEAD_DIM)

    cache_k = cache_k.reshape(-1, page_rows, HEAD_DIM)
    cache_v = cache_v.reshape(-1, page_rows, HEAD_DIM)
    lane_head = jnp.arange(page_rows) % SB_HEADS
    bias_lanes = jnp.broadcast_to((sb_bias * LOG2E)[lane_head][None, :], (t_new, page_rows))
    r = jnp.arange(HEAD_DIM)
    same_head = (r[:, None] % SB_HEADS) == (r[None, :] % SB_HEADS)
    not_earlier = (r[:, None] // SB_HEADS) >= (r[None, :] // SB_HEADS)
    uo = jnp.concatenate([same_head & not_earlier, same_head], axis=1).astype(BF16)
    uo = jnp.concatenate([uo, uo], axis=0)

    half = nq // 2
    b_of = lambda s: s // half
    qa_map = lambda s, pt: (b_of(s), s % half, 0)
    qb_map = lambda s, pt: (b_of(s), nq - 1 - s % half, 0)
    seq_map = lambda s, pt: (b_of(s), 0, 0)
    samp_map = lambda s, pt: (s // seq_steps, 0)
    const = lambda s, pt: (0, 0)
    kern = functools.partial(_sb_fused_kernel, tq, nq, ppc, n_pages, t_new, layer * n_pool)
    grid_spec = pltpu.PrefetchScalarGridSpec(
        num_scalar_prefetch=1,
        grid=(n_steps,),
        in_specs=[
            pl.BlockSpec(memory_space=pltpu.SMEM),
            pl.BlockSpec((None, tq, SB_WIDTH), qa_map),
            pl.BlockSpec((None, tq, SB_WIDTH), qb_map),
            pl.BlockSpec((None, seq, SB_WIDTH), seq_map, pipeline_mode=pl.Buffered(1)),
            pl.BlockSpec((None, seq, SB_WIDTH), seq_map, pipeline_mode=pl.Buffered(1)),
            pl.BlockSpec((None, tq, SB_WIDTH), qa_map),
            pl.BlockSpec((None, tq, SB_WIDTH), qb_map),
            pl.BlockSpec((2 * tq, tq), const),
            pl.BlockSpec((t_new, SB_WIDTH), samp_map),
            pl.BlockSpec((page_rows, HEAD_DIM), samp_map),
            pl.BlockSpec((page_rows, HEAD_DIM), samp_map),
            pl.BlockSpec((t_new, SB_WIDTH), samp_map),
            pl.BlockSpec((t_new, page_rows), const),
            pl.BlockSpec((2 * HEAD_DIM, 2 * HEAD_DIM), const),
            pl.BlockSpec(memory_space=pl.ANY),
            pl.BlockSpec(memory_space=pl.ANY),
        ],
        out_specs=[
            pl.BlockSpec((None, seq, SB_WIDTH), seq_map, pipeline_mode=pl.Buffered(1)),
            pl.BlockSpec((t_new, SB_WIDTH), samp_map),
        ],
        scratch_shapes=[
            pltpu.VMEM((tq, SB_WIDTH), F32),
            pltpu.VMEM((SB_HEADS, tq, 1), F32),
            pltpu.VMEM((SB_HEADS * t_new, HEAD_DIM), F32),
            pltpu.VMEM((t_new, HEAD_DIM), F32),
            pltpu.VMEM((SAMPLE_BUFS, ppc, page_rows, HEAD_DIM), F32),
            pltpu.VMEM((SAMPLE_BUFS, ppc, page_rows, HEAD_DIM), F32),
            pltpu.SemaphoreType.DMA((SAMPLE_BUFS, 2)),
        ],
    )
    out_p, out_s = pl.pallas_call(
        kern,
        grid_spec=grid_spec,
        out_shape=[jax.ShapeDtypeStruct((batch, seq, SB_WIDTH), BF16),
                   jax.ShapeDtypeStruct((n_seq * t_new, SB_WIDTH), F32)],
        compiler_params=pltpu.CompilerParams(
            dimension_semantics=("arbitrary",), vmem_limit_bytes=VMEM_LIMIT),
        name="sb_fused",
    )(page_table.reshape(-1), sb_bias, q3, q3, k3, v3, g3, g3, uu,
      q_s, as_page(k_new), as_page(v_new), g_s, bias_lanes, uo, cache_k, cache_v)
    return out_p.reshape(batch * seq, SB_WIDTH), out_s


def _ssd_kernel(length, n_par, *refs):
    per_seq_in, shared, per_seq_out = refs[:5], refs[5:13], refs[13:]
    pre_ref, h0_ref = per_seq_in[3:5]
    ext_ref, st_ref = per_seq_out[3:5]

    @pl.when(pl.program_id(1) == 0)
    def _():
        ext_ref[:, 0:SUBLANES, :] = pre_ref[...]
        st_ref[...] = h0_ref[...]

    for s in range(n_par):
        _ssd_chunk(length, *[r.at[s] for r in per_seq_in[:3]], *shared,
                   *[r.at[s] for r in per_seq_out])


def _pad_rows(a, rows):
    if a.shape[0] == rows:
        return a
    return jnp.concatenate([a, jnp.zeros((rows - a.shape[0], a.shape[1]), a.dtype)], axis=0)


def _ssd_chunk(length, xbc_ref, z_ref, dt_ref, cw_ref, cb_ref, dtb_ref,
               alog_ref, dsk_ref, nw_ref, ltri_ref, e_ref, out_ref, cnew_ref, snew_ref,
               ext_ref, st_ref):
    L = SSD_CHUNK
    P = SSD_HEAD_DIM

    ext_ref[SUBLANES:SUBLANES + L, :] = _pad_rows(xbc_ref[...], L)
    cw = cw_ref[...]
    conv = cb_ref[...]
    for j in range(CONV_WIDTH):
        off = SUBLANES - (CONV_WIDTH - 1) + j
        conv = conv + ext_ref[off:off + L, :] * cw[j:j + 1, :]
    act = _silu(conv)
    tail = ext_ref[length:length + SUBLANES, :]
    cnew_ref[...] = tail
    ext_ref[0:SUBLANES, :] = tail

    xs = act[:, :SSD_WIDTH]
    bm = act[:, SSD_WIDTH:SSD_WIDTH + SSD_GROUPS * SSD_STATE]
    cm = act[:, SSD_WIDTH + SSD_GROUPS * SSD_STATE:]

    x_dt = _pad_rows(dt_ref[...], L) + dtb_ref[...]
    dt = jnp.maximum(x_dt, 0.0) + jnp.log1p(jnp.exp(-jnp.abs(x_dt)))
    if length < L:
        valid = lax.broadcasted_iota(jnp.int32, dt.shape, 0) < length
        dt = jnp.where(valid, dt, 0.0)
    da = dt * (-jnp.exp(alog_ref[...]))
    cs = _dot_exact_rhs(ltri_ref[...], da)
    cs_t = cs.T
    e = e_ref[...]
    dt_x = _dot_exact_lhs(dt, e)
    cs_x = _dot_exact_lhs(cs, e)
    xdt = xs * dt_x
    ecs = jnp.exp(cs_x)
    xw_t = (xdt * jnp.exp(cs_x[L - 1:L, :] - cs_x)).T
    xdt16 = xdt.astype(BF16)

    row = lax.broadcasted_iota(jnp.int32, (L, L), 0)
    col = lax.broadcasted_iota(jnp.int32, (L, L), 1)
    causal = col <= row
    heads_per_group = SSD_HEADS // SSD_GROUPS
    gw = heads_per_group * P
    y_diag, y_off = [], []
    for g in range(SSD_GROUPS):
        bg = bm[:, SSD_STATE * g:SSD_STATE * (g + 1)].astype(BF16)
        cg = cm[:, SSD_STATE * g:SSD_STATE * (g + 1)].astype(BF16)
        cb = _dot_nt(cg, bg)
        prev = st_ref[gw * g:gw * (g + 1), :]
        y_off.append(_dot_nt(cg, prev.astype(BF16)))
        new = _dot(xw_t[gw * g:gw * (g + 1), :].astype(BF16), bg)
        for r in range(heads_per_group):
            h = heads_per_group * g + r
            seg = cs[:, h:h + 1] - cs_t[h:h + 1, :]
            decay = jnp.exp(jnp.where(causal, seg, -jnp.inf))
            y_diag.append(_dot((cb * decay).astype(BF16), xdt16[:, P * h:P * (h + 1)]))
            chunk_decay = jnp.exp(cs[L - 1:L, h:h + 1])
            st_ref[P * h:P * (h + 1), :] = (prev[P * r:P * (r + 1), :] * chunk_decay
                                            + new[P * r:P * (r + 1), :])
    snew_ref[...] = st_ref[...]
    y = (jnp.concatenate(y_diag, axis=1) + jnp.concatenate(y_off, axis=1) * ecs
         + xs * dsk_ref[...])
    gated = y[:length] * _silu(z_ref[...])
    ms = jnp.mean(gated * gated, axis=-1, keepdims=True)
    out_ref[...] = (gated * lax.rsqrt(ms + EPS) * nw_ref[...]).astype(out_ref.dtype)


def _ssd(xbc, zdt, prefix, h0, conv_w, conv_b, dt_bias, a_log, d_skip, ssd_norm_w,
         batch, n_chunks, length, out_dtype):
    L = SSD_CHUNK
    pre = jnp.pad(prefix, ((0, 0), (SUBLANES - (CONV_WIDTH - 1), 0), (0, 0)))
    pad_h = lambda a: jnp.pad(a, (0, HEAD_DIM - SSD_HEADS)).reshape(1, HEAD_DIM)
    ltri = (jnp.arange(L)[:, None] >= jnp.arange(L)[None, :]).astype(BF16)
    expand = (jnp.arange(HEAD_DIM)[:, None] == jnp.arange(SSD_WIDTH)[None, :] // SSD_HEAD_DIM
              ).astype(BF16)
    dsk = jnp.repeat(d_skip, SSD_HEAD_DIM).reshape(1, SSD_WIDTH)
    z_blk = SSD_WIDTH // HEAD_DIM
    n_par = SSD_SEQS_PER_STEP
    assert batch % n_par == 0 and length <= L and (length == L or n_chunks == 1)
    rows = n_chunks * length
    xbc3 = xbc.reshape(batch, rows, XBC_WIDTH)
    zdt3 = zdt.reshape(batch, rows, PROJ_TN)
    const = lambda b, c: (0, 0)
    seq = lambda b, c: (b, 0, 0)
    kern = functools.partial(_ssd_kernel, length, n_par)
    out, conv_new, ssm_new = pl.pallas_call(
        kern,
        grid=(batch // n_par, n_chunks),
        in_specs=[
            pl.BlockSpec((n_par, length, XBC_WIDTH), lambda b, c: (b, c, 0)),
            pl.BlockSpec((n_par, length, SSD_WIDTH), lambda b, c: (b, c, 0)),
            pl.BlockSpec((n_par, length, HEAD_DIM), lambda b, c: (b, c, z_blk)),
            pl.BlockSpec((n_par, SUBLANES, XBC_WIDTH), seq),
            pl.BlockSpec((n_par, SSD_WIDTH, SSD_STATE), seq),
            pl.BlockSpec((CONV_WIDTH, XBC_WIDTH), const),
            pl.BlockSpec((1, XBC_WIDTH), const),
            pl.BlockSpec((1, HEAD_DIM), const),
            pl.BlockSpec((1, HEAD_DIM), const),
            pl.BlockSpec((1, SSD_WIDTH), const),
            pl.BlockSpec((1, SSD_WIDTH), const),
            pl.BlockSpec((L, L), const),
            pl.BlockSpec((HEAD_DIM, SSD_WIDTH), const),
        ],
        out_specs=[
            pl.BlockSpec((n_par, length, SSD_WIDTH), lambda b, c: (b, c, 0)),
            pl.BlockSpec((n_par, SUBLANES, XBC_WIDTH), seq),
            pl.BlockSpec((n_par, SSD_WIDTH, SSD_STATE), seq),
        ],
        out_shape=[
            jax.ShapeDtypeStruct((batch, rows, SSD_WIDTH), out_dtype),
            jax.ShapeDtypeStruct((batch, SUBLANES, XBC_WIDTH), F32),
            jax.ShapeDtypeStruct((batch, SSD_WIDTH, SSD_STATE), F32),
        ],
        scratch_shapes=[pltpu.VMEM((n_par, SUBLANES + L, XBC_WIDTH), F32),
                        pltpu.VMEM((n_par, SSD_WIDTH, SSD_STATE), F32)],
        compiler_params=pltpu.CompilerParams(
            dimension_semantics=("parallel", "arbitrary"), vmem_limit_bytes=VMEM_LIMIT),
        name="ssd_scan",
    )(xbc3, zdt3, zdt3, pre, h0.reshape(batch, SSD_WIDTH, SSD_STATE), conv_w,
      conv_b.reshape(1, XBC_WIDTH), pad_h(dt_bias), pad_h(a_log), dsk,
      ssd_norm_w.reshape(1, SSD_WIDTH), ltri, expand)
    return out.reshape(batch * rows, SSD_WIDTH), conv_new, ssm_new


def _mem_attn_kernel(q_ref, g_ref, k_ref, v_ref, o_ref):
    for h in range(MEM_HEADS):
        cols = slice(HEAD_DIM * h, HEAD_DIM * (h + 1))
        s = _dot_nt(q_ref[:, cols].astype(BF16), k_ref[:, cols].astype(BF16)) * ATTN_SCALE
        p = jnp.exp(s - jnp.max(s, axis=-1, keepdims=True))
        den = jnp.sum(p, axis=-1, keepdims=True)
        o = _dot(p.astype(BF16), v_ref[:, cols].astype(BF16)) / den
        o_ref[:, cols] = (o * _silu(g_ref[:, cols])).astype(o_ref.dtype)


def _mem_attn(q, g, mem_k, mem_v, batch, t, tq, out_dtype):
    nq = t // tq
    return pl.pallas_call(
        _mem_attn_kernel,
        grid=(batch, nq),
        in_specs=[
            pl.BlockSpec((tq, MEM_WIDTH), lambda b, i: (b * nq + i, 0)),
            pl.BlockSpec((tq, MEM_WIDTH), lambda b, i: (b * nq + i, 0)),
            pl.BlockSpec((None, MEM_TOKENS, MEM_WIDTH), lambda b, i: (b, 0, 0)),
            pl.BlockSpec((None, MEM_TOKENS, MEM_WIDTH), lambda b, i: (b, 0, 0)),
        ],
        out_specs=pl.BlockSpec((tq, MEM_WIDTH), lambda b, i: (b * nq + i, 0)),
        out_shape=jax.ShapeDtypeStruct((batch * t, MEM_WIDTH), out_dtype),
        compiler_params=pltpu.CompilerParams(
            dimension_semantics=("parallel", "parallel"), vmem_limit_bytes=VMEM_LIMIT),
        name="mem_attn",
    )(q, g, mem_k, mem_v)


def _out_proj_kernel(x_ref, sb_ref, ssd_ref, mo_ref, w_ref, o_ref):
    mix = jnp.concatenate([sb_ref[...].astype(BF16), ssd_ref[...].astype(BF16),
                           mo_ref[...].astype(BF16)], axis=-1)
    for c in range(0, o_ref.shape[1], PROJ_CHUNK):
        cols = slice(c, c + PROJ_CHUNK)
        o_ref[:, cols] = x_ref[:, cols] + _dot(mix, w_ref[:, cols])


def _out_proj(x, sb, ssd, mo, w_out, tm, tn):
    t, d = x.shape
    return pl.pallas_call(
        _out_proj_kernel,
        grid=(t // tm, d // tn),
        in_specs=[
            pl.BlockSpec((tm, tn), lambda m, n: (m, n)),
            pl.BlockSpec((tm, SB_WIDTH), lambda m, n: (m, 0)),
            pl.BlockSpec((tm, SSD_WIDTH), lambda m, n: (m, 0)),
            pl.BlockSpec((tm, MEM_WIDTH), lambda m, n: (m, 0)),
            pl.BlockSpec((w_out.shape[0], tn), lambda m, n: (0, n)),
        ],
        out_specs=pl.BlockSpec((tm, tn), lambda m, n: (m, n)),
        out_shape=jax.ShapeDtypeStruct((t, d), F32),
        compiler_params=pltpu.CompilerParams(
            dimension_semantics=("parallel", "arbitrary"), vmem_limit_bytes=VMEM_LIMIT),
        name="out_proj",
    )(x, sb, ssd, mo, w_out)


def _in_proj_plan(act_dtype):
    lowp = act_dtype == BF16
    outs, plan = [], []

    def add(width, dtype):
        outs.append((width, dtype))
        return len(outs) - 1

    q = add(SB_WIDTH, act_dtype)
    plan.append(((0, SB_WIDTH, 0, None if lowp else q, q if lowp else None),))
    k32 = add(SB_WIDTH, F32)
    k16 = add(SB_WIDTH, BF16) if lowp else None
    plan.append(((0, SB_WIDTH, 1, k32, k16),))
    v32 = add(SB_WIDTH, F32)
    v16 = add(SB_WIDTH, BF16) if lowp else None
    plan.append(((0, SB_WIDTH, None, v32, v16),))
    g = add(SB_WIDTH, F32)
    plan.append(((0, SB_WIDTH, None, g, None),))
    xbc = add(XBC_WIDTH, F32)
    plan.append(((0, XBC_WIDTH, None, xbc, None),))
    zdt = add(PROJ_TN, F32)
    plan.append(((0, PROJ_TN, None, zdt, None),))
    mq = add(MEM_WIDTH, act_dtype)
    mg = add(MEM_WIDTH, F32)
    plan.append(((0, MEM_WIDTH, 2, None if lowp else mq, mq if lowp else None),
                 (MEM_WIDTH, MEM_WIDTH, None, mg, None)))
    names = dict(q=q, k32=k32, k16=k16, v32=v32, v16=v16, g=g, xbc=xbc, zdt=zdt, mq=mq, mg=mg)
    return tuple(plan), outs, names


_O_Z = 4 * SB_WIDTH
_O_XBC = _O_Z + SSD_WIDTH
_O_DT = _O_XBC + XBC_WIDTH
_O_MEM = _O_DT + SSD_HEADS
_IN_WIDTH = _O_MEM + 2 * MEM_WIDTH
_CAT_WIDTH = 7 * PROJ_TN
W_PREP_COLS = 256
BF16_ROWS = 16


def _w_prep_kernel(w_ref, o_ref):
    def put(dst, src, rows):
        o_ref[dst:dst + rows, :] = w_ref[src:src + rows, :].astype(BF16)

    cols = w_ref.shape[1]
    put(0, 0, _O_Z)
    put(_O_Z, _O_XBC, XBC_WIDTH)
    put(_O_Z + XBC_WIDTH, _O_Z, SSD_WIDTH)
    dt0 = _O_Z + XBC_WIDTH + SSD_WIDTH
    o_ref[dt0:dt0 + BF16_ROWS, :] = jnp.concatenate(
        [w_ref[_O_DT:_O_MEM, :], jnp.zeros((BF16_ROWS - SSD_HEADS, cols), F32)], axis=0
    ).astype(BF16)
    o_ref[dt0 + BF16_ROWS:6 * PROJ_TN, :] = jnp.zeros((6 * PROJ_TN - dt0 - BF16_ROWS, cols), BF16)
    put(6 * PROJ_TN, _O_MEM, 2 * MEM_WIDTH)


def _rearranged_w_in(w_t):
    d = w_t.shape[1]
    assert w_t.shape[0] == _IN_WIDTH and d % W_PREP_COLS == 0
    return pl.pallas_call(
        _w_prep_kernel,
        grid=(d // W_PREP_COLS,),
        in_specs=[pl.BlockSpec((_IN_WIDTH, W_PREP_COLS), lambda c: (0, c))],
        out_specs=pl.BlockSpec((_CAT_WIDTH, W_PREP_COLS), lambda c: (0, c)),
        out_shape=jax.ShapeDtypeStruct((_CAT_WIDTH, d), BF16),
        compiler_params=pltpu.CompilerParams(
            dimension_semantics=("parallel",), vmem_limit_bytes=VMEM_LIMIT),
        name="w_prep",
    )(w_t)


def kernel(x_prompt, x_sample, cache_sb_k, cache_sb_v, state_ssm, state_conv, cache_mem_k,
           cache_mem_v, page_table, mem_prompt, norm_w, w_in, sb_q_norm, sb_k_norm, sb_bias,
           conv_w, conv_b, dt_bias, a_log, d_skip, ssd_norm_w, mem_norm_w, w_mem_kv, mem_q_norm,
           mem_k_norm, w_out):
    depth = w_in.shape[0]
    assert depth == 1
    layer = 0
    bp, sp, d = x_prompt.shape
    bs, ts, _ = x_sample.shape
    n_pool = cache_sb_k.shape[1]
    L = SSD_CHUNK

    w_cat = _rearranged_w_in(w_in[layer].T)
    w_o = w_out[layer].astype(BF16)
    w_kv = w_mem_kv[layer].T.astype(BF16)
    head_norms = jnp.concatenate(
        [sb_q_norm[layer][None], sb_k_norm[layer][None], mem_q_norm[layer][None],
         mem_k_norm[layer][None], jnp.zeros((SUBLANES - 4, HEAD_DIM), F32)], axis=0)
    ssd_params = (conv_w[layer], conv_b[layer], dt_bias[layer], a_log[layer], d_skip[layer],
                  ssd_norm_w[layer])

    xp = x_prompt.reshape(bp * sp, d)
    mem_plan = (((0, MEM_WIDTH, 3, 0, None), (MEM_WIDTH, MEM_WIDTH, None, 1, None)),)
    (mk, mv), _ = _proj(mem_prompt.reshape(bp * MEM_TOKENS, d), mem_norm_w[layer], w_kv,
                        head_norms, mem_plan, [(MEM_WIDTH, F32), (MEM_WIDTH, F32)], tm=512)
    plan, outs, nm = _in_proj_plan(BF16)
    xs = x_sample.reshape(bs * ts, d)
    plan_s, outs_s, ns = _in_proj_plan(F32)
    pr, ps = _proj(xp, norm_w[layer], w_cat, head_norms, plan, outs, tm=512,
                   side=(xs, plan_s, outs_s))
    sb, sb_s = _sb_fused(pr[nm['q']], pr[nm['k16']], pr[nm['v16']], pr[nm['g']], bp, sp, 256,
                         ps[ns['q']], ps[ns['k32']], ps[ns['v32']], ps[ns['g']],
                         cache_sb_k, cache_sb_v, layer, page_table, sb_bias[layer], ts)
    ssd, conv_p, ssm_p = _ssd(
        pr[nm['xbc']], pr[nm['zdt']], jnp.zeros((bp, CONV_WIDTH - 1, XBC_WIDTH), F32),
        jnp.zeros((bp, SSD_HEADS, SSD_HEAD_DIM, SSD_STATE), F32), *ssd_params,
        batch=bp, n_chunks=sp // L, length=L, out_dtype=BF16)
    mo = _mem_attn(pr[nm['mq']], pr[nm['mg']], mk.reshape(bp, MEM_TOKENS, MEM_WIDTH),
                   mv.reshape(bp, MEM_TOKENS, MEM_WIDTH), bp, sp, tq=512, out_dtype=BF16)
    yp = _out_proj(xp, sb, ssd, mo, w_o, tm=512, tn=d)

    ssd_s, conv_s, ssm_s = _ssd(
        ps[ns['xbc']], ps[ns['zdt']], state_conv[layer], state_ssm[layer],
        *ssd_params, batch=bs, n_chunks=1, length=ts, out_dtype=F32)
    mo_s = _mem_attn(ps[ns['mq']], ps[ns['mg']],
                     cache_mem_k[layer].reshape(bs, MEM_TOKENS, MEM_WIDTH),
                     cache_mem_v[layer].reshape(bs, MEM_TOKENS, MEM_WIDTH), bs, ts, tq=ts,
                     out_dtype=F32)
    ys = _out_proj(xs, sb_s, ssd_s, mo_s, w_o, tm=bs * ts, tn=d)

    tail = slice(SUBLANES - (CONV_WIDTH - 1), SUBLANES)
    return (
        yp.reshape(bp, sp, d),
        ys.reshape(bs, ts, d),
        pr[nm['k32']].reshape(1, bp, sp, SB_HEADS, HEAD_DIM),
        pr[nm['v32']].reshape(1, bp, sp, SB_HEADS, HEAD_DIM),
        ssm_p.reshape(1, bp, SSD_HEADS, SSD_HEAD_DIM, SSD_STATE),
        conv_p[:, tail][None],
        mk.reshape(1, bp, MEM_TOKENS, MEM_HEADS, HEAD_DIM),
        mv.reshape(1, bp, MEM_TOKENS, MEM_HEADS, HEAD_DIM),
        ps[ns['k32']].reshape(1, bs, ts, SB_HEADS, HEAD_DIM),
        ps[ns['v32']].reshape(1, bs, ts, SB_HEADS, HEAD_DIM),
        ssm_s.reshape(1, bs, SSD_HEADS, SSD_HEAD_DIM, SSD_STATE),
        conv_s[:, tail][None],
    )
```

```python
import functools
import math

import jax
import jax.numpy as jnp
from jax import lax
from jax.experimental import pallas as pl
from jax.experimental.pallas import tpu as pltpu

F32 = jnp.float32
BF16 = jnp.bfloat16

D_MODEL = 2048
SB_HEADS = 8
HEAD_DIM = 128
SB_WIDTH = SB_HEADS * HEAD_DIM
SSD_HEADS = 8
SSD_HEAD_DIM = 64
SSD_WIDTH = SSD_HEADS * SSD_HEAD_DIM
SSD_GROUPS = 2
SSD_STATE = 128
CONV_WIDTH = 4
XBC_WIDTH = SSD_WIDTH + 2 * SSD_GROUPS * SSD_STATE
MEM_TOKENS = 256
MEM_HEADS = 4
MEM_WIDTH = MEM_HEADS * HEAD_DIM
PAGE_SIZE = 128
EPS = 1e-6
ATTN_SCALE = HEAD_DIM ** -0.5
LOG2E = math.log2(math.e)

SSD_CHUNK = 128
PROJ_TN = 1024
PROJ_CHUNK = 256
SAMPLE_BUFS = 3
SSD_SEQS_PER_STEP = 4
SUBLANES = 8
VMEM_LIMIT = 56 * 1024 * 1024
VMEM_LIMIT_FUSED = 62 * 1024 * 1024

_NT = (((1,), (1,)), ((), ()))


def _dot(a, b):
    return jnp.dot(a, b, preferred_element_type=F32)


def _dot_nt(a, b):
    return lax.dot_general(a, b, _NT, preferred_element_type=F32)


def _split2(x):
    hi = x.astype(BF16)
    lo = (x - hi.astype(F32)).astype(BF16)
    return hi, lo


def _split3(x):
    hi = x.astype(BF16)
    r = x - hi.astype(F32)
    mid = r.astype(BF16)
    lo = (r - mid.astype(F32)).astype(BF16)
    return hi, mid, lo


def _dot_exact_lhs(x, m):
    hi, mid, lo = _split3(x)
    return _dot(hi, m) + _dot(mid, m) + _dot(lo, m)


def _dot_exact_rhs(m, x):
    hi, mid, lo = _split3(x)
    return _dot(m, hi) + _dot(m, mid) + _dot(m, lo)


def _silu(x):
    return x * (1.0 / (1.0 + jnp.exp(-x)))


def _rmsnorm_rows(x_ref, nw_ref, h_ref):
    rows_total = x_ref.shape[0]
    rc = min(rows_total, 64)

    def body(r, carry):
        rows = pl.ds(pl.multiple_of(r * rc, rc), rc)
        xv = x_ref[rows, :]
        ms = jnp.mean(xv * xv, axis=-1, keepdims=True)
        h_ref[rows, :] = (xv * lax.rsqrt(ms + EPS) * nw_ref[...]).astype(BF16)
        return carry

    lax.fori_loop(0, rows_total // rc, body, 0)


def _proj_step(segs, h_ref, w_ref, hn_ref, outs, w_rows_are_outputs):
    for col0, width, hn_row, o32, o16 in segs:
        for c0 in range(0, width, PROJ_CHUNK):
            chunk = slice(col0 + c0, col0 + c0 + PROJ_CHUNK)
            if w_rows_are_outputs:
                y = _dot_nt(h_ref[...], w_ref[chunk, :].astype(BF16))
            else:
                y = _dot(h_ref[...], w_ref[:, chunk].astype(BF16))
            for c in range(0, PROJ_CHUNK, HEAD_DIM):
                yc = y[:, c:c + HEAD_DIM]
                if hn_row is not None:
                    ms = jnp.mean(yc * yc, axis=-1, keepdims=True)
                    yc = yc * lax.rsqrt(ms + EPS) * hn_ref[hn_row:hn_row + 1, :]
                cols = slice(c0 + c, c0 + c + HEAD_DIM)
                if o32 is not None:
                    outs[o32][:, cols] = yc
                if o16 is not None:
                    outs[o16][:, cols] = yc.astype(BF16)


def _proj_kernel(plan, n_out, side_plan, n_side, w_rows_are_outputs, *refs):
    if side_plan is None:
        x_ref, nw_ref, w_ref, hn_ref = refs[:4]
        rest = refs[4:]
    else:
        x_ref, xs_ref, nw_ref, w_ref, hn_ref = refs[:5]
        rest = refs[5:]
    outs, side_outs = rest[:n_out], rest[n_out:n_out + n_side]
    scratch = rest[n_out + n_side:]
    h_ref = scratch[0]
    m = pl.program_id(0)
    n = pl.program_id(1)

    @pl.when(n == 0)
    def _():
        _rmsnorm_rows(x_ref, nw_ref, h_ref)

    if side_plan is not None:
        hs_ref = scratch[1]

        @pl.when((n == 0) & (m == 0))
        def _():
            _rmsnorm_rows(xs_ref, nw_ref, hs_ref)

    for step, segs in enumerate(plan):
        @pl.when(n == step)
        def _(step=step, segs=segs):
            _proj_step(segs, h_ref, w_ref, hn_ref, outs, w_rows_are_outputs)
            if side_plan is not None:
                @pl.when(m == 0)
                def _():
                    _proj_step(side_plan[step], hs_ref, w_ref, hn_ref, side_outs,
                               w_rows_are_outputs)


def _proj(x, norm_w, w, head_norms, plan, out_defs, tm, side=None, w_rows_are_outputs=True):
    t, d = x.shape
    n_steps = len(plan)
    w_shape = (n_steps * PROJ_TN, d) if w_rows_are_outputs else (d, n_steps * PROJ_TN)
    assert w.shape == w_shape and t % tm == 0
    w_spec = (pl.BlockSpec((PROJ_TN, d), lambda m, n: (n, 0)) if w_rows_are_outputs
              else pl.BlockSpec((d, PROJ_TN), lambda m, n: (0, n)))
    row = lambda m, n: (m, 0)
    const = lambda m, n: (0, 0)
    in_specs = [pl.BlockSpec((tm, d), row)]
    operands = [x]
    out_specs = [pl.BlockSpec((tm, w), row) for w, _ in out_defs]
    out_shape = [jax.ShapeDtypeStruct((t, w), dt) for w, dt in out_defs]
    scratch = [pltpu.VMEM((tm, d), BF16)]
    side_plan, n_side = None, 0
    if side is not None:
        x_side, side_plan, side_defs = side
        ts = x_side.shape[0]
        assert len(side_plan) == n_steps
        n_side = len(side_defs)
        in_specs.append(pl.BlockSpec((ts, d), const))
        operands.append(x_side)
        out_specs += [pl.BlockSpec((ts, w), const) for w, _ in side_defs]
        out_shape += [jax.ShapeDtypeStruct((ts, w), dt) for w, dt in side_defs]
        scratch.append(pltpu.VMEM((ts, d), BF16))
    in_specs += [pl.BlockSpec((1, d), const), w_spec,
                 pl.BlockSpec((SUBLANES, HEAD_DIM), const)]
    operands += [norm_w.reshape(1, d), w, head_norms]
    kern = functools.partial(_proj_kernel, plan, len(out_defs), side_plan, n_side,
                             w_rows_are_outputs)
    res = pl.pallas_call(
        kern,
        grid=(t // tm, n_steps),
        in_specs=in_specs,
        out_specs=out_specs,
        out_shape=out_shape,
        scratch_shapes=scratch,
        compiler_params=pltpu.CompilerParams(
            dimension_semantics=("arbitrary", "arbitrary"), vmem_limit_bytes=VMEM_LIMIT),
        name="norm_proj",
    )(*operands)
    return res[:len(out_defs)], res[len(out_defs):]


def _log2_fail(z2):
    nz = -z2
    return jnp.minimum(nz, 0.0) - jnp.log(1.0 + jnp.exp2(jnp.minimum(z2, nz))) * LOG2E


def _sb_block_logits(q, ks, uu_incl, bias2, diag):
    z2 = _dot_nt(q, ks) * (ATTN_SCALE * LOG2E) + bias2
    lf = _log2_fail(z2)
    if diag is not None:
        lf = jnp.where(diag, lf, 0.0)
    hi, lo = _split2(lf)
    return z2, _dot(jnp.concatenate([hi, lo], axis=1), uu_incl)


def _sb_fused_kernel(tq, nq, ppc, n_pages, t_new, page_base,
                     pt_ref, bias_ref, qa_ref, qb_ref, k_ref, v_ref, ga_ref, gb_ref, uu_ref,
                     qs_ref, kn_ref, vn_ref, gs_ref, bl_ref, uo_ref, ck_hbm, cv_hbm,
                     op_ref, os_ref, acc_ref, c_ref, accs_ref, cs_ref, kbuf, vbuf, sem):
    step = pl.program_id(0)
    n_steps = pl.num_programs(0)
    pair = lax.rem(step, nq // 2)
    i1 = pair
    i2 = nq - 1 - pair
    chunks_per_seq = n_pages // ppc
    seq_steps = chunks_per_seq // nq
    part = lax.rem(step, seq_steps)
    n_chunks = n_steps * nq

    def chunk_copies(chunk):
        slot = lax.rem(chunk, SAMPLE_BUFS)
        seq = lax.div(chunk, chunks_per_seq)
        first_pos = (n_pages - 1) - lax.rem(chunk, chunks_per_seq) * ppc
        copies = []
        for j in range(ppc):
            page = page_base + pt_ref[seq * n_pages + first_pos - j]
            copies.append(pltpu.make_async_copy(ck_hbm.at[page], kbuf.at[slot, j], sem.at[slot, 0]))
            copies.append(pltpu.make_async_copy(cv_hbm.at[page], vbuf.at[slot, j], sem.at[slot, 1]))
        return copies

    def start_chunk(chunk):
        for cp in chunk_copies(chunk):
            cp.start()

    def wait_chunk(chunk):
        slot = lax.rem(chunk, SAMPLE_BUFS)
        pltpu.make_async_copy(ck_hbm.at[pl.ds(0, ppc)], kbuf.at[slot], sem.at[slot, 0]).wait()
        pltpu.make_async_copy(cv_hbm.at[pl.ds(0, ppc)], vbuf.at[slot], sem.at[slot, 1]).wait()

    def start_next(chunk):
        @pl.when(chunk + SAMPLE_BUFS < n_chunks)
        def _():
            start_chunk(chunk + SAMPLE_BUFS)

    @pl.when(step == 0)
    def _():
        for c in range(SAMPLE_BUFS):
            start_chunk(jnp.int32(c))

    lanes = PAGE_SIZE * SB_HEADS
    n_blk = lanes // HEAD_DIM
    lane = lax.broadcasted_iota(jnp.int32, (t_new, lanes), 1)
    lane_head = lane & (SB_HEADS - 1)
    qs = qs_ref[...]
    q_all = jnp.concatenate([qs[:, HEAD_DIM * h:HEAD_DIM * (h + 1)] for h in range(SB_HEADS)],
                            axis=0).astype(BF16)

    def scores(kpage, mask):
        s_all = _dot_nt(q_all, kpage.astype(BF16))
        sc = s_all[0:t_new, :]
        for h in range(1, SB_HEADS):
            sc = jnp.where(lane_head == h, s_all[t_new * h:t_new * (h + 1), :], sc)
        z2 = sc * (ATTN_SCALE * LOG2E) + bl_ref[...]
        lf = _log2_fail(z2)
        if mask is not None:
            lf = jnp.where(mask, lf, 0.0)
        blocks = jnp.concatenate([lf[:, HEAD_DIM * j:HEAD_DIM * (j + 1)] for j in range(n_blk)],
                                 axis=0)
        hi, lo = _split2(blocks)
        return z2, _dot(jnp.concatenate([hi, lo], axis=1), uo_ref[...])

    def weights(z2, res, mask, run):
        ws = [None] * n_blk
        for j in reversed(range(n_blk)):
            rows = slice(t_new * j, t_new * (j + 1))
            logw = z2[:, HEAD_DIM * j:HEAD_DIM * (j + 1)] + res[rows, :HEAD_DIM]
            if run is not None:
                logw = logw + run
            ws[j] = jnp.exp2(logw)
            tot = res[rows, HEAD_DIM:]
            run = tot if run is None else run + tot
        w = jnp.concatenate(ws, axis=1)
        if mask is not None:
            w = jnp.where(mask, w, 0.0)
        w_all = jnp.concatenate([jnp.where(lane_head == h, w, 0.0) for h in range(SB_HEADS)],
                                axis=0).astype(BF16)
        return w_all, run

    def new_keys():
        mask = (lane >> 3) < lax.broadcasted_iota(jnp.int32, (t_new, lanes), 0)
        z2, res = scores(kn_ref[...], mask)
        w_all, run = weights(z2, res, mask, None)
        cs_ref[...] = run
        accs_ref[...] = _dot(w_all, vn_ref[...].astype(BF16))

    uu = uu_ref[...]
    cols = [slice(HEAD_DIM * h, HEAD_DIM * (h + 1)) for h in range(SB_HEADS)]
    biases = [bias_ref[h] * LOG2E for h in range(SB_HEADS)]

    def section(q_ref, start, diag, first, chunk):
        kb = k_ref[pl.ds(start, tq), :]
        vb = v_ref[pl.ds(start, tq), :]
        n_pages_here = 0 if chunk is None else ppc
        slot = None if chunk is None else lax.rem(chunk, SAMPLE_BUFS)
        if n_pages_here:
            run = cs_ref[...]
            acc = accs_ref[...]
        p_parts, s_parts = {}, {}
        n_idx = max(SB_HEADS, n_pages_here)

        raw = [_dot_nt(q_ref[:, cols[h]], kb[:, cols[h]]) for h in range(SB_HEADS)]
        for idx in range(n_pages_here):
            s_parts[idx] = scores(kbuf[slot, idx], None)
        split = []
        for h in range(SB_HEADS):
            z2 = raw[h] * (ATTN_SCALE * LOG2E) + biases[h]
            lf = _log2_fail(z2)
            if diag is not None:
                lf = jnp.where(diag, lf, 0.0)
            hi, lo = _split2(lf)
            split.append((z2, jnp.concatenate([hi, lo], axis=1)))
        for h in range(SB_HEADS):
            p_parts[h] = (split[h][0], _dot(split[h][1], uu))
        for idx in range(n_idx):
            if idx < SB_HEADS:
                z2, incl = p_parts[idx]
                logw = z2 + incl
                if not first:
                    logw = logw + c_ref[idx]
                w = jnp.exp2(logw)
                if diag is not None:
                    w = jnp.where(diag, w, 0.0)
                pv = _dot(w.astype(BF16), vb[:, cols[idx]])
                total = incl[:, 0:1]
                if first:
                    acc_ref[:, cols[idx]] = pv
                    c_ref[idx] = total
                else:
                    acc_ref[:, cols[idx]] += pv
                    c_ref[idx] += total
            if idx < n_pages_here:
                w_all, run = weights(*s_parts[idx], None, run)
                acc = acc + _dot(w_all, vbuf[slot, idx].astype(BF16))
        if n_pages_here:
            cs_ref[...] = run
            accs_ref[...] = acc

    def finish(i, g_ref):
        rows = pl.ds(pl.multiple_of(i * tq, tq), tq)
        op_ref[rows, :] = (acc_ref[...] * _silu(g_ref[...])).astype(op_ref.dtype)

    row = lax.broadcasted_iota(jnp.int32, (tq, tq), 0)
    col = lax.broadcasted_iota(jnp.int32, (tq, tq), 1)
    diag = col < row
    chunk0 = step * nq

    @pl.when(part == 0)
    def _():
        new_keys()

    section(qa_ref, pl.multiple_of(i1 * tq, tq), diag, True, None)

    def body_a(t, carry):
        chunk = chunk0 + t
        wait_chunk(chunk)
        section(qa_ref, pl.multiple_of((i1 - 1 - t) * tq, tq), None, False, chunk)
        start_next(chunk)
        return carry

    lax.fori_loop(0, i1, body_a, 0)
    finish(i1, ga_ref)

    chunk = chunk0 + i1
    wait_chunk(chunk)
    section(qb_ref, pl.multiple_of(i2 * tq, tq), diag, True, chunk)
    start_next(chunk)

    def body_b(t, carry):
        chunk = chunk0 + i1 + 1 + t
        wait_chunk(chunk)
        section(qb_ref, pl.multiple_of((i2 - 1 - t) * tq, tq), None, False, chunk)
        start_next(chunk)
        return carry

    lax.fori_loop(0, i2, body_b, 0)
    finish(i2, gb_ref)

    @pl.when(part == seq_steps - 1)
    def _():
        g = gs_ref[...]
        for h in range(SB_HEADS):
            os_ref[:, cols[h]] = accs_ref[t_new * h:t_new * (h + 1), :] * _silu(g[:, cols[h]])


def _sb_fused(q, k, v, g, batch, seq, tq, q_s, k_new, v_new, g_s, cache_k, cache_v, layer,
              page_table, sb_bias, t_new):
    n_seq, n_pages = page_table.shape
    n_pool = cache_k.shape[1]
    nq = seq // tq
    n_steps = batch * (nq // 2)
    assert nq % 2 == 0 and (n_seq * n_pages) % (n_steps * nq) == 0
    ppc = n_seq * n_pages // (n_steps * nq)
    assert n_pages % (ppc * nq) == 0
    seq_steps = n_pages // (ppc * nq)
    page_rows = PAGE_SIZE * SB_HEADS

    q3, k3, v3, g3 = (a.reshape(batch, seq, SB_WIDTH) for a in (q, k, v, g))
    u = (jnp.arange(tq)[:, None] >= jnp.arange(tq)[None, :]).astype(BF16)
    uu = jnp.concatenate([u, u], axis=0)

    def as_page(a):
        a = a.reshape(n_seq, t_new, SB_HEADS, HEAD_DIM)
        a = jnp.pad(a, ((0, 0), (0, PAGE_SIZE - t_new), (0, 0), (0, 0)))
        return a.reshape(n_seq * page_rows, HEAD_DIM)

    cache_k = cache_k.reshape(-1, page_rows, HEAD_DIM)
    cache_v = cache_v.reshape(-1, page_rows, HEAD_DIM)
    lane_head = jnp.arange(page_rows) % SB_HEADS
    bias_lanes = jnp.broadcast_to((sb_bias * LOG2E)[lane_head][None, :], (t_new, page_rows))
    r = jnp.arange(HEAD_DIM)
    same_head = (r[:, None] % SB_HEADS) == (r[None, :] % SB_HEADS)
    not_earlier = (r[:, None] // SB_HEADS) >= (r[None, :] // SB_HEADS)
    uo = jnp.concatenate([same_head & not_earlier, same_head], axis=1).astype(BF16)
    uo = jnp.concatenate([uo, uo], axis=0)

    half = nq // 2
    b_of = lambda s: s // half
    qa_map = lambda s, pt: (b_of(s), s % half, 0)
    qb_map = lambda s, pt: (b_of(s), nq - 1 - s % half, 0)
    seq_map = lambda s, pt: (b_of(s), 0, 0)
    samp_map = lambda s, pt: (s // seq_steps, 0)
    const = lambda s, pt: (0, 0)
    kern = functools.partial(_sb_fused_kernel, tq, nq, ppc, n_pages, t_new, layer * n_pool)
    grid_spec = pltpu.PrefetchScalarGridSpec(
        num_scalar_prefetch=1,
        grid=(n_steps,),
        in_specs=[
            pl.BlockSpec(memory_space=pltpu.SMEM),
            pl.BlockSpec((None, tq, SB_WIDTH), qa_map),
            pl.BlockSpec((None, tq, SB_WIDTH), qb_map),
            pl.BlockSpec((None, seq, SB_WIDTH), seq_map, pipeline_mode=pl.Buffered(1)),
            pl.BlockSpec((None, seq, SB_WIDTH), seq_map, pipeline_mode=pl.Buffered(1)),
            pl.BlockSpec((None, tq, SB_WIDTH), qa_map),
            pl.BlockSpec((None, tq, SB_WIDTH), qb_map),
            pl.BlockSpec((2 * tq, tq), const),
            pl.BlockSpec((t_new, SB_WIDTH), samp_map),
            pl.BlockSpec((page_rows, HEAD_DIM), samp_map),
            pl.BlockSpec((page_rows, HEAD_DIM), samp_map),
            pl.BlockSpec((t_new, SB_WIDTH), samp_map),
            pl.BlockSpec((t_new, page_rows), const),
            pl.BlockSpec((2 * HEAD_DIM, 2 * HEAD_DIM), const),
            pl.BlockSpec(memory_space=pl.ANY),
            pl.BlockSpec(memory_space=pl.ANY),
        ],
        out_specs=[
            pl.BlockSpec((None, seq, SB_WIDTH), seq_map, pipeline_mode=pl.Buffered(1)),
            pl.BlockSpec((t_new, SB_WIDTH), samp_map),
        ],
        scratch_shapes=[
            pltpu.VMEM((tq, SB_WIDTH), F32),
            pltpu.VMEM((SB_HEADS, tq, 1), F32),
            pltpu.VMEM((SB_HEADS * t_new, HEAD_DIM), F32),
            pltpu.VMEM((t_new, HEAD_DIM), F32),
            pltpu.VMEM((SAMPLE_BUFS, ppc, page_rows, HEAD_DIM), F32),
            pltpu.VMEM((SAMPLE_BUFS, ppc, page_rows, HEAD_DIM), F32),
            pltpu.SemaphoreType.DMA((SAMPLE_BUFS, 2)),
        ],
    )
    out_p, out_s = pl.pallas_call(
        kern,
        grid_spec=grid_spec,
        out_shape=[jax.ShapeDtypeStruct((batch, seq, SB_WIDTH), BF16),
                   jax.ShapeDtypeStruct((n_seq * t_new, SB_WIDTH), F32)],
        compiler_params=pltpu.CompilerParams(
            dimension_semantics=("arbitrary",), vmem_limit_bytes=VMEM_LIMIT_FUSED),
        name="sb_fused",
    )(page_table.reshape(-1), sb_bias, q3, q3, k3, v3, g3, g3, uu,
      q_s, as_page(k_new), as_page(v_new), g_s, bias_lanes, uo, cache_k, cache_v)
    return out_p.reshape(batch * seq, SB_WIDTH), out_s


def _ssd_kernel(length, n_par, *refs):
    per_seq_in, shared, per_seq_out = refs[:5], refs[5:13], refs[13:]
    pre_ref, h0_ref = per_seq_in[3:5]
    ext_ref, st_ref = per_seq_out[3:5]

    @pl.when(pl.program_id(1) == 0)
    def _():
        ext_ref[:, 0:SUBLANES, :] = pre_ref[...]
        st_ref[...] = h0_ref[...]

    for s in range(n_par):
        _ssd_chunk(length, *[r.at[s] for r in per_seq_in[:3]], *shared,
                   *[r.at[s] for r in per_seq_out])


def _pad_rows(a, rows):
    if a.shape[0] == rows:
        return a
    return jnp.concatenate([a, jnp.zeros((rows - a.shape[0], a.shape[1]), a.dtype)], axis=0)


def _ssd_chunk(length, xbc_ref, z_ref, dt_ref, cw_ref, cb_ref, dtb_ref,
               alog_ref, dsk_ref, nw_ref, ltri_ref, e_ref, out_ref, cnew_ref, snew_ref,
               ext_ref, st_ref):
    L = SSD_CHUNK
    P = SSD_HEAD_DIM

    ext_ref[SUBLANES:SUBLANES + L, :] = _pad_rows(xbc_ref[...], L)
    cw = cw_ref[...]
    conv = cb_ref[...]
    for j in range(CONV_WIDTH):
        off = SUBLANES - (CONV_WIDTH - 1) + j
        conv = conv + ext_ref[off:off + L, :] * cw[j:j + 1, :]
    act = _silu(conv)
    tail = ext_ref[length:length + SUBLANES, :]
    cnew_ref[...] = tail
    ext_ref[0:SUBLANES, :] = tail

    xs = act[:, :SSD_WIDTH]
    bm = act[:, SSD_WIDTH:SSD_WIDTH + SSD_GROUPS * SSD_STATE]
    cm = act[:, SSD_WIDTH + SSD_GROUPS * SSD_STATE:]

    x_dt = _pad_rows(dt_ref[...], L) + dtb_ref[...]
    dt = jnp.maximum(x_dt, 0.0) + jnp.log1p(jnp.exp(-jnp.abs(x_dt)))
    if length < L:
        valid = lax.broadcasted_iota(jnp.int32, dt.shape, 0) < length
        dt = jnp.where(valid, dt, 0.0)
    da = dt * (-jnp.exp(alog_ref[...]))
    cs = _dot_exact_rhs(ltri_ref[...], da)
    cs_t = cs.T
    e = e_ref[...]
    dt_x = _dot_exact_lhs(dt, e)
    cs_x = _dot_exact_lhs(cs, e)
    xdt = xs * dt_x
    ecs = jnp.exp(cs_x)
    xw_t = (xdt * jnp.exp(cs_x[L - 1:L, :] - cs_x)).T
    xdt16 = xdt.astype(BF16)

    row = lax.broadcasted_iota(jnp.int32, (L, L), 0)
    col = lax.broadcasted_iota(jnp.int32, (L, L), 1)
    causal = col <= row
    heads_per_group = SSD_HEADS // SSD_GROUPS
    gw = heads_per_group * P
    y_diag, y_off = [], []
    for g in range(SSD_GROUPS):
        bg = bm[:, SSD_STATE * g:SSD_STATE * (g + 1)].astype(BF16)
        cg = cm[:, SSD_STATE * g:SSD_STATE * (g + 1)].astype(BF16)
        cb = _dot_nt(cg, bg)
        prev = st_ref[gw * g:gw * (g + 1), :]
        y_off.append(_dot_nt(cg, prev.astype(BF16)))
        new = _dot(xw_t[gw * g:gw * (g + 1), :].astype(BF16), bg)
        for r in range(heads_per_group):
            h = heads_per_group * g + r
            seg = cs[:, h:h + 1] - cs_t[h:h + 1, :]
            decay = jnp.exp(jnp.where(causal, seg, -jnp.inf))
            y_diag.append(_dot((cb * decay).astype(BF16), xdt16[:, P * h:P * (h + 1)]))
            chunk_decay = jnp.exp(cs[L - 1:L, h:h + 1])
            st_ref[P * h:P * (h + 1), :] = (prev[P * r:P * (r + 1), :] * chunk_decay
                                            + new[P * r:P * (r + 1), :])
    snew_ref[...] = st_ref[...]
    y = (jnp.concatenate(y_diag, axis=1) + jnp.concatenate(y_off, axis=1) * ecs
         + xs * dsk_ref[...])
    gated = y[:length] * _silu(z_ref[...])
    ms = jnp.mean(gated * gated, axis=-1, keepdims=True)
    out_ref[...] = (gated * lax.rsqrt(ms + EPS) * nw_ref[...]).astype(out_ref.dtype)


def _ssd(xbc, zdt, prefix, h0, conv_w, conv_b, dt_bias, a_log, d_skip, ssd_norm_w,
         batch, n_chunks, length, out_dtype):
    L = SSD_CHUNK
    pre = jnp.pad(prefix, ((0, 0), (SUBLANES - (CONV_WIDTH - 1), 0), (0, 0)))
    pad_h = lambda a: jnp.pad(a, (0, HEAD_DIM - SSD_HEADS)).reshape(1, HEAD_DIM)
    ltri = (jnp.arange(L)[:, None] >= jnp.arange(L)[None, :]).astype(BF16)
    expand = (jnp.arange(HEAD_DIM)[:, None] == jnp.arange(SSD_WIDTH)[None, :] // SSD_HEAD_DIM
              ).astype(BF16)
    dsk = jnp.repeat(d_skip, SSD_HEAD_DIM).reshape(1, SSD_WIDTH)
    z_blk = SSD_WIDTH // HEAD_DIM
    n_par = SSD_SEQS_PER_STEP
    assert batch % n_par == 0 and length <= L and (length == L or n_chunks == 1)
    rows = n_chunks * length
    xbc3 = xbc.reshape(batch, rows, XBC_WIDTH)
    zdt3 = zdt.reshape(batch, rows, PROJ_TN)
    const = lambda b, c: (0, 0)
    seq = lambda b, c: (b, 0, 0)
    kern = functools.partial(_ssd_kernel, length, n_par)
    out, conv_new, ssm_new = pl.pallas_call(
        kern,
        grid=(batch // n_par, n_chunks),
        in_specs=[
            pl.BlockSpec((n_par, length, XBC_WIDTH), lambda b, c: (b, c, 0)),
            pl.BlockSpec((n_par, length, SSD_WIDTH), lambda b, c: (b, c, 0)),
            pl.BlockSpec((n_par, length, HEAD_DIM), lambda b, c: (b, c, z_blk)),
            pl.BlockSpec((n_par, SUBLANES, XBC_WIDTH), seq),
            pl.BlockSpec((n_par, SSD_WIDTH, SSD_STATE), seq),
            pl.BlockSpec((CONV_WIDTH, XBC_WIDTH), const),
            pl.BlockSpec((1, XBC_WIDTH), const),
            pl.BlockSpec((1, HEAD_DIM), const),
            pl.BlockSpec((1, HEAD_DIM), const),
            pl.BlockSpec((1, SSD_WIDTH), const),
            pl.BlockSpec((1, SSD_WIDTH), const),
            pl.BlockSpec((L, L), const),
            pl.BlockSpec((HEAD_DIM, SSD_WIDTH), const),
        ],
        out_specs=[
            pl.BlockSpec((n_par, length, SSD_WIDTH), lambda b, c: (b, c, 0)),
            pl.BlockSpec((n_par, SUBLANES, XBC_WIDTH), seq),
            pl.BlockSpec((n_par, SSD_WIDTH, SSD_STATE), seq),
        ],
        out_shape=[
            jax.ShapeDtypeStruct((batch, rows, SSD_WIDTH), out_dtype),
            jax.ShapeDtypeStruct((batch, SUBLANES, XBC_WIDTH), F32),
            jax.ShapeDtypeStruct((batch, SSD_WIDTH, SSD_STATE), F32),
        ],
        scratch_shapes=[pltpu.VMEM((n_par, SUBLANES + L, XBC_WIDTH), F32),
                        pltpu.VMEM((n_par, SSD_WIDTH, SSD_STATE), F32)],
        compiler_params=pltpu.CompilerParams(
            dimension_semantics=("parallel", "arbitrary"), vmem_limit_bytes=VMEM_LIMIT),
        name="ssd_scan",
    )(xbc3, zdt3, zdt3, pre, h0.reshape(batch, SSD_WIDTH, SSD_STATE), conv_w,
      conv_b.reshape(1, XBC_WIDTH), pad_h(dt_bias), pad_h(a_log), dsk,
      ssd_norm_w.reshape(1, SSD_WIDTH), ltri, expand)
    return out.reshape(batch * rows, SSD_WIDTH), conv_new, ssm_new


def _mem_attn_kernel(q_ref, g_ref, k_ref, v_ref, o_ref):
    for h in range(MEM_HEADS):
        cols = slice(HEAD_DIM * h, HEAD_DIM * (h + 1))
        s = _dot_nt(q_ref[:, cols].astype(BF16), k_ref[:, cols].astype(BF16)) * ATTN_SCALE
        p = jnp.exp(s - jnp.max(s, axis=-1, keepdims=True))
        den = jnp.sum(p, axis=-1, keepdims=True)
        o = _dot(p.astype(BF16), v_ref[:, cols].astype(BF16)) / den
        o_ref[:, cols] = (o * _silu(g_ref[:, cols])).astype(o_ref.dtype)


def _mem_attn(q, g, mem_k, mem_v, batch, t, tq, out_dtype):
    nq = t // tq
    return pl.pallas_call(
        _mem_attn_kernel,
        grid=(batch, nq),
        in_specs=[
            pl.BlockSpec((tq, MEM_WIDTH), lambda b, i: (b * nq + i, 0)),
            pl.BlockSpec((tq, MEM_WIDTH), lambda b, i: (b * nq + i, 0)),
            pl.BlockSpec((None, MEM_TOKENS, MEM_WIDTH), lambda b, i: (b, 0, 0)),
            pl.BlockSpec((None, MEM_TOKENS, MEM_WIDTH), lambda b, i: (b, 0, 0)),
        ],
        out_specs=pl.BlockSpec((tq, MEM_WIDTH), lambda b, i: (b * nq + i, 0)),
        out_shape=jax.ShapeDtypeStruct((batch * t, MEM_WIDTH), out_dtype),
        compiler_params=pltpu.CompilerParams(
            dimension_semantics=("parallel", "parallel"), vmem_limit_bytes=VMEM_LIMIT),
        name="mem_attn",
    )(q, g, mem_k, mem_v)


def _out_proj_kernel(x_ref, sb_ref, ssd_ref, mo_ref, w_ref, o_ref):
    mix = jnp.concatenate([sb_ref[...].astype(BF16), ssd_ref[...].astype(BF16),
                           mo_ref[...].astype(BF16)], axis=-1)
    for c in range(0, o_ref.shape[1], PROJ_CHUNK):
        cols = slice(c, c + PROJ_CHUNK)
        o_ref[:, cols] = x_ref[:, cols] + _dot(mix, w_ref[:, cols].astype(BF16))


def _out_proj(x, sb, ssd, mo, w_out, tm, tn):
    t, d = x.shape
    return pl.pallas_call(
        _out_proj_kernel,
        grid=(t // tm, d // tn),
        in_specs=[
            pl.BlockSpec((tm, tn), lambda m, n: (m, n)),
            pl.BlockSpec((tm, SB_WIDTH), lambda m, n: (m, 0)),
            pl.BlockSpec((tm, SSD_WIDTH), lambda m, n: (m, 0)),
            pl.BlockSpec((tm, MEM_WIDTH), lambda m, n: (m, 0)),
            pl.BlockSpec((w_out.shape[0], tn), lambda m, n: (0, n), pipeline_mode=pl.Buffered(1)),
        ],
        out_specs=pl.BlockSpec((tm, tn), lambda m, n: (m, n)),
        out_shape=jax.ShapeDtypeStruct((t, d), F32),
        compiler_params=pltpu.CompilerParams(
            dimension_semantics=("parallel", "arbitrary"), vmem_limit_bytes=VMEM_LIMIT),
        name="out_proj",
    )(x, sb, ssd, mo, w_out)


def _in_proj_plan(act_dtype):
    lowp = act_dtype == BF16
    outs, plan = [], []

    def add(width, dtype):
        outs.append((width, dtype))
        return len(outs) - 1

    q = add(SB_WIDTH, act_dtype)
    plan.append(((0, SB_WIDTH, 0, None if lowp else q, q if lowp else None),))
    k32 = add(SB_WIDTH, F32)
    k16 = add(SB_WIDTH, BF16) if lowp else None
    plan.append(((0, SB_WIDTH, 1, k32, k16),))
    v32 = add(SB_WIDTH, F32)
    v16 = add(SB_WIDTH, BF16) if lowp else None
    plan.append(((0, SB_WIDTH, None, v32, v16),))
    g = add(SB_WIDTH, F32)
    plan.append(((0, SB_WIDTH, None, g, None),))
    xbc = add(XBC_WIDTH, F32)
    plan.append(((0, XBC_WIDTH, None, xbc, None),))
    zdt = add(PROJ_TN, F32)
    plan.append(((0, PROJ_TN, None, zdt, None),))
    mq = add(MEM_WIDTH, act_dtype)
    mg = add(MEM_WIDTH, F32)
    plan.append(((0, MEM_WIDTH, 2, None if lowp else mq, mq if lowp else None),
                 (MEM_WIDTH, MEM_WIDTH, None, mg, None)))
    names = dict(q=q, k32=k32, k16=k16, v32=v32, v16=v16, g=g, xbc=xbc, zdt=zdt, mq=mq, mg=mg)
    return tuple(plan), outs, names


_O_Z = 4 * SB_WIDTH
_O_XBC = _O_Z + SSD_WIDTH
_O_DT = _O_XBC + XBC_WIDTH
_O_MEM = _O_DT + SSD_HEADS
_IN_WIDTH = _O_MEM + 2 * MEM_WIDTH
_CAT_WIDTH = 7 * PROJ_TN
W_PREP_COLS = 256
BF16_ROWS = 16


def _w_prep_kernel(w_ref, o_ref):
    def put(dst, src, rows):
        o_ref[dst:dst + rows, :] = w_ref[src:src + rows, :].astype(BF16)

    cols = w_ref.shape[1]
    put(0, 0, _O_Z)
    put(_O_Z, _O_XBC, XBC_WIDTH)
    put(_O_Z + XBC_WIDTH, _O_Z, SSD_WIDTH)
    dt0 = _O_Z + XBC_WIDTH + SSD_WIDTH
    o_ref[dt0:dt0 + BF16_ROWS, :] = jnp.concatenate(
        [w_ref[_O_DT:_O_MEM, :], jnp.zeros((BF16_ROWS - SSD_HEADS, cols), F32)], axis=0
    ).astype(BF16)
    o_ref[dt0 + BF16_ROWS:6 * PROJ_TN, :] = jnp.zeros((6 * PROJ_TN - dt0 - BF16_ROWS, cols), BF16)
    put(6 * PROJ_TN, _O_MEM, 2 * MEM_WIDTH)


def _rearranged_w_in(w_t):
    d = w_t.shape[1]
    assert w_t.shape[0] == _IN_WIDTH and d % W_PREP_COLS == 0
    return pl.pallas_call(
        _w_prep_kernel,
        grid=(d // W_PREP_COLS,),
        in_specs=[pl.BlockSpec((_IN_WIDTH, W_PREP_COLS), lambda c: (0, c))],
        out_specs=pl.BlockSpec((_CAT_WIDTH, W_PREP_COLS), lambda c: (0, c)),
        out_shape=jax.ShapeDtypeStruct((_CAT_WIDTH, d), BF16),
        compiler_params=pltpu.CompilerParams(
            dimension_semantics=("parallel",), vmem_limit_bytes=VMEM_LIMIT),
        name="w_prep",
    )(w_t)


def kernel(x_prompt, x_sample, cache_sb_k, cache_sb_v, state_ssm, state_conv, cache_mem_k,
           cache_mem_v, page_table, mem_prompt, norm_w, w_in, sb_q_norm, sb_k_norm, sb_bias,
           conv_w, conv_b, dt_bias, a_log, d_skip, ssd_norm_w, mem_norm_w, w_mem_kv, mem_q_norm,
           mem_k_norm, w_out):
    depth = w_in.shape[0]
    assert depth == 1
    layer = 0
    bp, sp, d = x_prompt.shape
    bs, ts, _ = x_sample.shape
    n_pool = cache_sb_k.shape[1]
    L = SSD_CHUNK

    w_cat = _rearranged_w_in(w_in[layer].T)
    w_o = w_out[layer]
    head_norms = jnp.concatenate(
        [sb_q_norm[layer][None], sb_k_norm[layer][None], mem_q_norm[layer][None],
         mem_k_norm[layer][None], jnp.zeros((SUBLANES - 4, HEAD_DIM), F32)], axis=0)
    ssd_params = (conv_w[layer], conv_b[layer], dt_bias[layer], a_log[layer], d_skip[layer],
                  ssd_norm_w[layer])

    xp = x_prompt.reshape(bp * sp, d)
    mem_plan = (((0, MEM_WIDTH, 3, 0, None), (MEM_WIDTH, MEM_WIDTH, None, 1, None)),)
    (mk, mv), _ = _proj(mem_prompt.reshape(bp * MEM_TOKENS, d), mem_norm_w[layer],
                        w_mem_kv[layer], head_norms, mem_plan,
                        [(MEM_WIDTH, F32), (MEM_WIDTH, F32)], tm=512, w_rows_are_outputs=False)
    plan, outs, nm = _in_proj_plan(BF16)
    xs = x_sample.reshape(bs * ts, d)
    plan_s, outs_s, ns = _in_proj_plan(F32)
    pr, ps = _proj(xp, norm_w[layer], w_cat, head_norms, plan, outs, tm=512,
                   side=(xs, plan_s, outs_s))
    sb, sb_s = _sb_fused(pr[nm['q']], pr[nm['k16']], pr[nm['v16']], pr[nm['g']], bp, sp, 256,
                         ps[ns['q']], ps[ns['k32']], ps[ns['v32']], ps[ns['g']],
                         cache_sb_k, cache_sb_v, layer, page_table, sb_bias[layer], ts)
    ssd, conv_p, ssm_p = _ssd(
        pr[nm['xbc']], pr[nm['zdt']], jnp.zeros((bp, CONV_WIDTH - 1, XBC_WIDTH), F32),
        jnp.zeros((bp, SSD_HEADS, SSD_HEAD_DIM, SSD_STATE), F32), *ssd_params,
        batch=bp, n_chunks=sp // L, length=L, out_dtype=BF16)
    mo = _mem_attn(pr[nm['mq']], pr[nm['mg']], mk.reshape(bp, MEM_TOKENS, MEM_WIDTH),
                   mv.reshape(bp, MEM_TOKENS, MEM_WIDTH), bp, sp, tq=512, out_dtype=BF16)
    yp = _out_proj(xp, sb, ssd, mo, w_o, tm=512, tn=d)

    ssd_s, conv_s, ssm_s = _ssd(
        ps[ns['xbc']], ps[ns['zdt']], state_conv[layer], state_ssm[layer],
        *ssd_params, batch=bs, n_chunks=1, length=ts, out_dtype=F32)
    mo_s = _mem_attn(ps[ns['mq']], ps[ns['mg']],
                     cache_mem_k[layer].reshape(bs, MEM_TOKENS, MEM_WIDTH),
                     cache_mem_v[layer].reshape(bs, MEM_TOKENS, MEM_WIDTH), bs, ts, tq=ts,
                     out_dtype=F32)
    ys = _out_proj(xs, sb_s, ssd_s, mo_s, w_o, tm=bs * ts, tn=d)

    tail = slice(SUBLANES - (CONV_WIDTH - 1), SUBLANES)
    return (
        yp.reshape(bp, sp, d),
        ys.reshape(bs, ts, d),
        pr[nm['k32']].reshape(1, bp, sp, SB_HEADS, HEAD_DIM),
        pr[nm['v32']].reshape(1, bp, sp, SB_HEADS, HEAD_DIM),
        ssm_p.reshape(1, bp, SSD_HEADS, SSD_HEAD_DIM, SSD_STATE),
        conv_p[:, tail][None],
        mk.reshape(1, bp, MEM_TOKENS, MEM_HEADS, HEAD_DIM),
        mv.reshape(1, bp, MEM_TOKENS, MEM_HEADS, HEAD_DIM),
        ps[ns['k32']].reshape(1, bs, ts, SB_HEADS, HEAD_DIM),
        ps[ns['v32']].reshape(1, bs, ts, SB_HEADS, HEAD_DIM),
        ssm_s.reshape(1, bs, SSD_HEADS, SSD_HEAD_DIM, SSD_STATE),
        conv_s[:, tail][None],
    )
```

```python
import functools
import math

import jax
import jax.numpy as jnp
from jax import lax
from jax.experimental import pallas as pl
from jax.experimental.pallas import tpu as pltpu

F32 = jnp.float32
BF16 = jnp.bfloat16

D_MODEL = 2048
SB_HEADS = 8
HEAD_DIM = 128
SB_WIDTH = SB_HEADS * HEAD_DIM
SSD_HEADS = 8
SSD_HEAD_DIM = 64
SSD_WIDTH = SSD_HEADS * SSD_HEAD_DIM
SSD_GROUPS = 2
SSD_STATE = 128
CONV_WIDTH = 4
XBC_WIDTH = SSD_WIDTH + 2 * SSD_GROUPS * SSD_STATE
MEM_TOKENS = 256
MEM_HEADS = 4
MEM_WIDTH = MEM_HEADS * HEAD_DIM
PAGE_SIZE = 128
EPS = 1e-6
ATTN_SCALE = HEAD_DIM ** -0.5
LOG2E = math.log2(math.e)

SSD_CHUNK = 128
PROJ_TN = 1024
PROJ_CHUNK = 256
SAMPLE_BUFS = 3
SSD_SEQS_PER_STEP = 4
SUBLANES = 8
VMEM_LIMIT = 56 * 1024 * 1024
VMEM_LIMIT_FUSED = 62 * 1024 * 1024

_NT = (((1,), (1,)), ((), ()))


def _dot(a, b):
    return jnp.dot(a, b, preferred_element_type=F32)


def _dot_nt(a, b):
    return lax.dot_general(a, b, _NT, preferred_element_type=F32)


def _split2(x):
    hi = x.astype(BF16)
    lo = (x - hi.astype(F32)).astype(BF16)
    return hi, lo


def _split3(x):
    hi = x.astype(BF16)
    r = x - hi.astype(F32)
    mid = r.astype(BF16)
    lo = (r - mid.astype(F32)).astype(BF16)
    return hi, mid, lo


def _dot_exact_lhs(x, m):
    hi, mid, lo = _split3(x)
    return _dot(hi, m) + _dot(mid, m) + _dot(lo, m)


def _dot_exact_rhs(m, x):
    hi, mid, lo = _split3(x)
    return _dot(m, hi) + _dot(m, mid) + _dot(m, lo)


def _silu(x):
    return x * (1.0 / (1.0 + jnp.exp(-x)))


def _rmsnorm_rows(x_ref, nw_ref, h_ref):
    rows_total = x_ref.shape[0]
    rc = min(rows_total, 64)

    def body(r, carry):
        rows = pl.ds(pl.multiple_of(r * rc, rc), rc)
        xv = x_ref[rows, :]
        ms = jnp.mean(xv * xv, axis=-1, keepdims=True)
        h_ref[rows, :] = (xv * lax.rsqrt(ms + EPS) * nw_ref[...]).astype(BF16)
        return carry

    lax.fori_loop(0, rows_total // rc, body, 0)


def _proj_step(segs, h_ref, w_ref, hn_ref, outs, w_rows_are_outputs):
    for col0, width, hn_row, o32, o16 in segs:
        for c0 in range(0, width, PROJ_CHUNK):
            chunk = slice(col0 + c0, col0 + c0 + PROJ_CHUNK)
            if w_rows_are_outputs:
                y = _dot_nt(h_ref[...], w_ref[chunk, :].astype(BF16))
            else:
                y = _dot(h_ref[...], w_ref[:, chunk].astype(BF16))
            for c in range(0, PROJ_CHUNK, HEAD_DIM):
                yc = y[:, c:c + HEAD_DIM]
                if hn_row is not None:
                    ms = jnp.mean(yc * yc, axis=-1, keepdims=True)
                    yc = yc * lax.rsqrt(ms + EPS) * hn_ref[hn_row:hn_row + 1, :]
                cols = slice(c0 + c, c0 + c + HEAD_DIM)
                if o32 is not None:
                    outs[o32][:, cols] = yc
                if o16 is not None:
                    outs[o16][:, cols] = yc.astype(BF16)


def _proj_kernel(plan, n_out, side_plan, n_side, w_rows_are_outputs, *refs):
    if side_plan is None:
        x_ref, nw_ref, w_ref, hn_ref = refs[:4]
        rest = refs[4:]
    else:
        x_ref, xs_ref, nw_ref, w_ref, hn_ref = refs[:5]
        rest = refs[5:]
    outs, side_outs = rest[:n_out], rest[n_out:n_out + n_side]
    scratch = rest[n_out + n_side:]
    h_ref = scratch[0]
    m = pl.program_id(0)
    n = pl.program_id(1)

    @pl.when(n == 0)
    def _():
        _rmsnorm_rows(x_ref, nw_ref, h_ref)

    if side_plan is not None:
        hs_ref = scratch[1]

        @pl.when((n == 0) & (m == 0))
        def _():
            _rmsnorm_rows(xs_ref, nw_ref, hs_ref)

    for step, segs in enumerate(plan):
        @pl.when(n == step)
        def _(step=step, segs=segs):
            _proj_step(segs, h_ref, w_ref, hn_ref, outs, w_rows_are_outputs)
            if side_plan is not None:
                @pl.when(m == 0)
                def _():
                    _proj_step(side_plan[step], hs_ref, w_ref, hn_ref, side_outs,
                               w_rows_are_outputs)


def _proj(x, norm_w, w, head_norms, plan, out_defs, tm, side=None, w_rows_are_outputs=True):
    t, d = x.shape
    n_steps = len(plan)
    w_shape = (n_steps * PROJ_TN, d) if w_rows_are_outputs else (d, n_steps * PROJ_TN)
    assert w.shape == w_shape and t % tm == 0
    w_spec = (pl.BlockSpec((PROJ_TN, d), lambda m, n: (n, 0)) if w_rows_are_outputs
              else pl.BlockSpec((d, PROJ_TN), lambda m, n: (0, n)))
    row = lambda m, n: (m, 0)
    const = lambda m, n: (0, 0)
    in_specs = [pl.BlockSpec((tm, d), row)]
    operands = [x]
    out_specs = [pl.BlockSpec((tm, w), row) for w, _ in out_defs]
    out_shape = [jax.ShapeDtypeStruct((t, w), dt) for w, dt in out_defs]
    scratch = [pltpu.VMEM((tm, d), BF16)]
    side_plan, n_side = None, 0
    if side is not None:
        x_side, side_plan, side_defs = side
        ts = x_side.shape[0]
        assert len(side_plan) == n_steps
        n_side = len(side_defs)
        in_specs.append(pl.BlockSpec((ts, d), const))
        operands.append(x_side)
        out_specs += [pl.BlockSpec((ts, w), const) for w, _ in side_defs]
        out_shape += [jax.ShapeDtypeStruct((ts, w), dt) for w, dt in side_defs]
        scratch.append(pltpu.VMEM((ts, d), BF16))
    in_specs += [pl.BlockSpec((1, d), const), w_spec,
                 pl.BlockSpec((SUBLANES, HEAD_DIM), const)]
    operands += [norm_w.reshape(1, d), w, head_norms]
    kern = functools.partial(_proj_kernel, plan, len(out_defs), side_plan, n_side,
                             w_rows_are_outputs)
    res = pl.pallas_call(
        kern,
        grid=(t // tm, n_steps),
        in_specs=in_specs,
        out_specs=out_specs,
        out_shape=out_shape,
        scratch_shapes=scratch,
        compiler_params=pltpu.CompilerParams(
            dimension_semantics=("arbitrary", "arbitrary"), vmem_limit_bytes=VMEM_LIMIT),
        name="norm_proj",
    )(*operands)
    return res[:len(out_defs)], res[len(out_defs):]


def _log2_fail(z2):
    nz = -z2
    return jnp.minimum(nz, 0.0) - jnp.log(1.0 + jnp.exp2(jnp.minimum(z2, nz))) * LOG2E


def _sb_block_logits(q, ks, uu_incl, bias2, diag):
    z2 = _dot_nt(q, ks) * (ATTN_SCALE * LOG2E) + bias2
    lf = _log2_fail(z2)
    if diag is not None:
        lf = jnp.where(diag, lf, 0.0)
    hi, lo = _split2(lf)
    return z2, _dot(jnp.concatenate([hi, lo], axis=1), uu_incl)


def _sb_fused_kernel(tq, nq, ppc, n_pages, t_new, page_base,
                     pt_ref, bias_ref, qa_ref, qb_ref, k_ref, v_ref, ga_ref, gb_ref, uu_ref,
                     qs_ref, kn_ref, vn_ref, gs_ref, bl_ref, uo_ref, ck_hbm, cv_hbm,
                     op_ref, os_ref, acc_ref, c_ref, accs_ref, cs_ref, kbuf, vbuf, sem):
    step = pl.program_id(0)
    n_steps = pl.num_programs(0)
    pair = lax.rem(step, nq // 2)
    i1 = pair
    i2 = nq - 1 - pair
    chunks_per_seq = n_pages // ppc
    seq_steps = chunks_per_seq // nq
    part = lax.rem(step, seq_steps)
    n_chunks = n_steps * nq

    def chunk_copies(chunk):
        slot = lax.rem(chunk, SAMPLE_BUFS)
        seq = lax.div(chunk, chunks_per_seq)
        first_pos = (n_pages - 1) - lax.rem(chunk, chunks_per_seq) * ppc
        copies = []
        for j in range(ppc):
            page = page_base + pt_ref[seq * n_pages + first_pos - j]
            copies.append(pltpu.make_async_copy(ck_hbm.at[page], kbuf.at[slot, j], sem.at[slot, 0]))
            copies.append(pltpu.make_async_copy(cv_hbm.at[page], vbuf.at[slot, j], sem.at[slot, 1]))
        return copies

    def start_chunk(chunk):
        for cp in chunk_copies(chunk):
            cp.start()

    def wait_chunk(chunk):
        slot = lax.rem(chunk, SAMPLE_BUFS)
        pltpu.make_async_copy(ck_hbm.at[pl.ds(0, ppc)], kbuf.at[slot], sem.at[slot, 0]).wait()
        pltpu.make_async_copy(cv_hbm.at[pl.ds(0, ppc)], vbuf.at[slot], sem.at[slot, 1]).wait()

    def start_next(chunk):
        @pl.when(chunk + SAMPLE_BUFS < n_chunks)
        def _():
            start_chunk(chunk + SAMPLE_BUFS)

    @pl.when(step == 0)
    def _():
        for c in range(SAMPLE_BUFS):
            start_chunk(jnp.int32(c))

    lanes = PAGE_SIZE * SB_HEADS
    n_blk = lanes // HEAD_DIM
    lane = lax.broadcasted_iota(jnp.int32, (t_new, lanes), 1)
    lane_head = lane & (SB_HEADS - 1)
    qs = qs_ref[...]
    q_all = jnp.concatenate([qs[:, HEAD_DIM * h:HEAD_DIM * (h + 1)] for h in range(SB_HEADS)],
                            axis=0).astype(BF16)

    def page_scores(kpages, n, mask):
        s_cat = _dot_nt(q_all, kpages.astype(BF16))
        return [scores(s_cat[:, lanes * p:lanes * (p + 1)], mask) for p in range(n)]

    def scores(s_all, mask):
        sc = s_all[0:t_new, :]
        for h in range(1, SB_HEADS):
            sc = jnp.where(lane_head == h, s_all[t_new * h:t_new * (h + 1), :], sc)
        z2 = sc * (ATTN_SCALE * LOG2E) + bl_ref[...]
        lf = _log2_fail(z2)
        if mask is not None:
            lf = jnp.where(mask, lf, 0.0)
        blocks = jnp.concatenate([lf[:, HEAD_DIM * j:HEAD_DIM * (j + 1)] for j in range(n_blk)],
                                 axis=0)
        hi, lo = _split2(blocks)
        return z2, jnp.concatenate([hi, lo], axis=1)

    def stacked_sums(operands, matrix):
        rows = operands[0].shape[0]
        res = _dot(jnp.concatenate(operands, axis=0), matrix)
        return [res[rows * i:rows * (i + 1), :] for i in range(len(operands))]

    def weights(z2, res, mask, run):
        ws = [None] * n_blk
        for j in reversed(range(n_blk)):
            rows = slice(t_new * j, t_new * (j + 1))
            logw = z2[:, HEAD_DIM * j:HEAD_DIM * (j + 1)] + res[rows, :HEAD_DIM]
            if run is not None:
                logw = logw + run
            ws[j] = jnp.exp2(logw)
            tot = res[rows, HEAD_DIM:]
            run = tot if run is None else run + tot
        w = jnp.concatenate(ws, axis=1)
        if mask is not None:
            w = jnp.where(mask, w, 0.0)
        w_all = jnp.concatenate([jnp.where(lane_head == h, w, 0.0) for h in range(SB_HEADS)],
                                axis=0).astype(BF16)
        return w_all, run

    def new_keys():
        mask = (lane >> 3) < lax.broadcasted_iota(jnp.int32, (t_new, lanes), 0)
        (z2, hilo), = page_scores(kn_ref[...], 1, mask)
        res, = stacked_sums([hilo], uo_ref[...])
        w_all, run = weights(z2, res, mask, None)
        cs_ref[...] = run
        accs_ref[...] = _dot(w_all, vn_ref[...].astype(BF16))

    uu = uu_ref[...]
    cols = [slice(HEAD_DIM * h, HEAD_DIM * (h + 1)) for h in range(SB_HEADS)]
    biases = [bias_ref[h] * LOG2E for h in range(SB_HEADS)]

    def section(q_ref, start, diag, first, chunk):
        kb = k_ref[pl.ds(start, tq), :]
        vb = v_ref[pl.ds(start, tq), :]
        n_pages_here = 0 if chunk is None else ppc
        slot = None if chunk is None else lax.rem(chunk, SAMPLE_BUFS)
        if n_pages_here:
            run = cs_ref[...]
            acc = accs_ref[...]
        p_parts, s_parts = {}, {}
        n_idx = max(SB_HEADS, n_pages_here)

        raw = [_dot_nt(q_ref[:, cols[h]], kb[:, cols[h]]) for h in range(SB_HEADS)]
        if n_pages_here:
            s_split = page_scores(kbuf[slot].reshape(ppc * lanes, HEAD_DIM), ppc, None)
        split = []
        for h in range(SB_HEADS):
            z2 = raw[h] * (ATTN_SCALE * LOG2E) + biases[h]
            lf = _log2_fail(z2)
            if diag is not None:
                lf = jnp.where(diag, lf, 0.0)
            hi, lo = _split2(lf)
            split.append((z2, jnp.concatenate([hi, lo], axis=1)))
        incl = stacked_sums([hilo for _, hilo in split], uu)
        for h in range(SB_HEADS):
            p_parts[h] = (split[h][0], incl[h])
        if n_pages_here:
            res = stacked_sums([hilo for _, hilo in s_split], uo_ref[...])
            for idx in range(n_pages_here):
                s_parts[idx] = (s_split[idx][0], res[idx])
        w_pages = []
        for idx in range(n_idx):
            if idx < SB_HEADS:
                z2, incl_h = p_parts[idx]
                logw = z2 + incl_h
                if not first:
                    logw = logw + c_ref[idx]
                w = jnp.exp2(logw)
                if diag is not None:
                    w = jnp.where(diag, w, 0.0)
                pv = _dot(w.astype(BF16), vb[:, cols[idx]])
                total = incl_h[:, 0:1]
                if first:
                    acc_ref[:, cols[idx]] = pv
                    c_ref[idx] = total
                else:
                    acc_ref[:, cols[idx]] += pv
                    c_ref[idx] += total
            if idx < n_pages_here:
                w_all, run = weights(*s_parts[idx], None, run)
                w_pages.append(w_all)
        if n_pages_here:
            cs_ref[...] = run
            accs_ref[...] = acc + _dot(jnp.concatenate(w_pages, axis=1),
                                       vbuf[slot].reshape(ppc * lanes, HEAD_DIM).astype(BF16))

    def finish(i, g_ref):
        rows = pl.ds(pl.multiple_of(i * tq, tq), tq)
        op_ref[rows, :] = (acc_ref[...] * _silu(g_ref[...])).astype(op_ref.dtype)

    row = lax.broadcasted_iota(jnp.int32, (tq, tq), 0)
    col = lax.broadcasted_iota(jnp.int32, (tq, tq), 1)
    diag = col < row
    chunk0 = step * nq

    @pl.when(part == 0)
    def _():
        new_keys()

    section(qa_ref, pl.multiple_of(i1 * tq, tq), diag, True, None)

    def body_a(t, carry):
        chunk = chunk0 + t
        wait_chunk(chunk)
        section(qa_ref, pl.multiple_of((i1 - 1 - t) * tq, tq), None, False, chunk)
        start_next(chunk)
        return carry

    lax.fori_loop(0, i1, body_a, 0)
    finish(i1, ga_ref)

    chunk = chunk0 + i1
    wait_chunk(chunk)
    section(qb_ref, pl.multiple_of(i2 * tq, tq), diag, True, chunk)
    start_next(chunk)

    def body_b(t, carry):
        chunk = chunk0 + i1 + 1 + t
        wait_chunk(chunk)
        section(qb_ref, pl.multiple_of((i2 - 1 - t) * tq, tq), None, False, chunk)
        start_next(chunk)
        return carry

    lax.fori_loop(0, i2, body_b, 0)
    finish(i2, gb_ref)

    @pl.when(part == seq_steps - 1)
    def _():
        g = gs_ref[...]
        for h in range(SB_HEADS):
            os_ref[:, cols[h]] = accs_ref[t_new * h:t_new * (h + 1), :] * _silu(g[:, cols[h]])


def _sb_fused(q, k, v, g, batch, seq, tq, q_s, k_new, v_new, g_s, cache_k, cache_v, layer,
              page_table, sb_bias, t_new):
    n_seq, n_pages = page_table.shape
    n_pool = cache_k.shape[1]
    nq = seq // tq
    n_steps = batch * (nq // 2)
    assert nq % 2 == 0 and (n_seq * n_pages) % (n_steps * nq) == 0
    ppc = n_seq * n_pages // (n_steps * nq)
    assert n_pages % (ppc * nq) == 0
    seq_steps = n_pages // (ppc * nq)
    page_rows = PAGE_SIZE * SB_HEADS

    q3, k3, v3, g3 = (a.reshape(batch, seq, SB_WIDTH) for a in (q, k, v, g))
    u = (jnp.arange(tq)[:, None] >= jnp.arange(tq)[None, :]).astype(BF16)
    uu = jnp.concatenate([u, u], axis=0)

    def as_page(a):
        a = a.reshape(n_seq, t_new, SB_HEADS, HEAD_DIM)
        a = jnp.pad(a, ((0, 0), (0, PAGE_SIZE - t_new), (0, 0), (0, 0)))
        return a.reshape(n_seq * page_rows, HEAD_DIM)

    cache_k = cache_k.reshape(-1, page_rows, HEAD_DIM)
    cache_v = cache_v.reshape(-1, page_rows, HEAD_DIM)
    lane_head = jnp.arange(page_rows) % SB_HEADS
    bias_lanes = jnp.broadcast_to((sb_bias * LOG2E)[lane_head][None, :], (t_new, page_rows))
    r = jnp.arange(HEAD_DIM)
    same_head = (r[:, None] % SB_HEADS) == (r[None, :] % SB_HEADS)
    not_earlier = (r[:, None] // SB_HEADS) >= (r[None, :] // SB_HEADS)
    uo = jnp.concatenate([same_head & not_earlier, same_head], axis=1).astype(BF16)
    uo = jnp.concatenate([uo, uo], axis=0)

    half = nq // 2
    b_of = lambda s: s // half
    qa_map = lambda s, pt: (b_of(s), s % half, 0)
    qb_map = lambda s, pt: (b_of(s), nq - 1 - s % half, 0)
    seq_map = lambda s, pt: (b_of(s), 0, 0)
    samp_map = lambda s, pt: (s // seq_steps, 0)
    const = lambda s, pt: (0, 0)
    kern = functools.partial(_sb_fused_kernel, tq, nq, ppc, n_pages, t_new, layer * n_pool)
    grid_spec = pltpu.PrefetchScalarGridSpec(
        num_scalar_prefetch=1,
        grid=(n_steps,),
        in_specs=[
            pl.BlockSpec(memory_space=pltpu.SMEM),
            pl.BlockSpec((None, tq, SB_WIDTH), qa_map),
            pl.BlockSpec((None, tq, SB_WIDTH), qb_map),
            pl.BlockSpec((None, seq, SB_WIDTH), seq_map, pipeline_mode=pl.Buffered(1)),
            pl.BlockSpec((None, seq, SB_WIDTH), seq_map, pipeline_mode=pl.Buffered(1)),
            pl.BlockSpec((None, tq, SB_WIDTH), qa_map),
            pl.BlockSpec((None, tq, SB_WIDTH), qb_map),
            pl.BlockSpec((2 * tq, tq), const),
            pl.BlockSpec((t_new, SB_WIDTH), samp_map),
            pl.BlockSpec((page_rows, HEAD_DIM), samp_map),
            pl.BlockSpec((page_rows, HEAD_DIM), samp_map),
            pl.BlockSpec((t_new, SB_WIDTH), samp_map),
            pl.BlockSpec((t_new, page_rows), const),
            pl.BlockSpec((2 * HEAD_DIM, 2 * HEAD_DIM), const),
            pl.BlockSpec(memory_space=pl.ANY),
            pl.BlockSpec(memory_space=pl.ANY),
        ],
        out_specs=[
            pl.BlockSpec((None, seq, SB_WIDTH), seq_map, pipeline_mode=pl.Buffered(1)),
            pl.BlockSpec((t_new, SB_WIDTH), samp_map),
        ],
        scratch_shapes=[
            pltpu.VMEM((tq, SB_WIDTH), F32),
            pltpu.VMEM((SB_HEADS, tq, 1), F32),
            pltpu.VMEM((SB_HEADS * t_new, HEAD_DIM), F32),
            pltpu.VMEM((t_new, HEAD_DIM), F32),
            pltpu.VMEM((SAMPLE_BUFS, ppc, page_rows, HEAD_DIM), F32),
            pltpu.VMEM((SAMPLE_BUFS, ppc, page_rows, HEAD_DIM), F32),
            pltpu.SemaphoreType.DMA((SAMPLE_BUFS, 2)),
        ],
    )
    out_p, out_s = pl.pallas_call(
        kern,
        grid_spec=grid_spec,
        out_shape=[jax.ShapeDtypeStruct((batch, seq, SB_WIDTH), BF16),
                   jax.ShapeDtypeStruct((n_seq * t_new, SB_WIDTH), F32)],
        compiler_params=pltpu.CompilerParams(
            dimension_semantics=("arbitrary",), vmem_limit_bytes=VMEM_LIMIT_FUSED),
        name="sb_fused",
    )(page_table.reshape(-1), sb_bias, q3, q3, k3, v3, g3, g3, uu,
      q_s, as_page(k_new), as_page(v_new), g_s, bias_lanes, uo, cache_k, cache_v)
    return out_p.reshape(batch * seq, SB_WIDTH), out_s


def _ssd_kernel(length, n_par, *refs):
    per_seq_in, shared, per_seq_out = refs[:5], refs[5:13], refs[13:]
    pre_ref, h0_ref = per_seq_in[3:5]
    ext_ref, st_ref = per_seq_out[3:5]

    @pl.when(pl.program_id(1) == 0)
    def _():
        ext_ref[:, 0:SUBLANES, :] = pre_ref[...]
        st_ref[...] = h0_ref[...]

    for s in range(n_par):
        _ssd_chunk(length, *[r.at[s] for r in per_seq_in[:3]], *shared,
                   *[r.at[s] for r in per_seq_out])


def _pad_rows(a, rows):
    if a.shape[0] == rows:
        return a
    return jnp.concatenate([a, jnp.zeros((rows - a.shape[0], a.shape[1]), a.dtype)], axis=0)


def _ssd_chunk(length, xbc_ref, z_ref, dt_ref, cw_ref, cb_ref, dtb_ref,
               alog_ref, dsk_ref, nw_ref, ltri_ref, e_ref, out_ref, cnew_ref, snew_ref,
               ext_ref, st_ref):
    L = SSD_CHUNK
    P = SSD_HEAD_DIM

    ext_ref[SUBLANES:SUBLANES + L, :] = _pad_rows(xbc_ref[...], L)
    cw = cw_ref[...]
    conv = cb_ref[...]
    for j in range(CONV_WIDTH):
        off = SUBLANES - (CONV_WIDTH - 1) + j
        conv = conv + ext_ref[off:off + L, :] * cw[j:j + 1, :]
    act = _silu(conv)
    tail = ext_ref[length:length + SUBLANES, :]
    cnew_ref[...] = tail
    ext_ref[0:SUBLANES, :] = tail

    xs = act[:, :SSD_WIDTH]
    bm = act[:, SSD_WIDTH:SSD_WIDTH + SSD_GROUPS * SSD_STATE]
    cm = act[:, SSD_WIDTH + SSD_GROUPS * SSD_STATE:]

    x_dt = _pad_rows(dt_ref[...], L) + dtb_ref[...]
    dt = jnp.maximum(x_dt, 0.0) + jnp.log1p(jnp.exp(-jnp.abs(x_dt)))
    if length < L:
        valid = lax.broadcasted_iota(jnp.int32, dt.shape, 0) < length
        dt = jnp.where(valid, dt, 0.0)
    da = dt * (-jnp.exp(alog_ref[...]))
    cs = _dot_exact_rhs(ltri_ref[...], da)
    cs_t = cs.T
    e = e_ref[...]
    dt_x = _dot_exact_lhs(dt, e)
    cs_x = _dot_exact_lhs(cs, e)
    xdt = xs * dt_x
    ecs = jnp.exp(cs_x)
    xw_t = (xdt * jnp.exp(cs_x[L - 1:L, :] - cs_x)).T
    xdt16 = xdt.astype(BF16)

    row = lax.broadcasted_iota(jnp.int32, (L, L), 0)
    col = lax.broadcasted_iota(jnp.int32, (L, L), 1)
    causal = col <= row
    heads_per_group = SSD_HEADS // SSD_GROUPS
    gw = heads_per_group * P
    y_diag, y_off = [], []
    for g in range(SSD_GROUPS):
        bg = bm[:, SSD_STATE * g:SSD_STATE * (g + 1)].astype(BF16)
        cg = cm[:, SSD_STATE * g:SSD_STATE * (g + 1)].astype(BF16)
        cb = _dot_nt(cg, bg)
        prev = st_ref[gw * g:gw * (g + 1), :]
        y_off.append(_dot_nt(cg, prev.astype(BF16)))
        new = _dot(xw_t[gw * g:gw * (g + 1), :].astype(BF16), bg)
        for r in range(heads_per_group):
            h = heads_per_group * g + r
            seg = cs[:, h:h + 1] - cs_t[h:h + 1, :]
            decay = jnp.exp(jnp.where(causal, seg, -jnp.inf))
            y_diag.append(_dot((cb * decay).astype(BF16), xdt16[:, P * h:P * (h + 1)]))
            chunk_decay = jnp.exp(cs[L - 1:L, h:h + 1])
            st_ref[P * h:P * (h + 1), :] = (prev[P * r:P * (r + 1), :] * chunk_decay
                                            + new[P * r:P * (r + 1), :])
    snew_ref[...] = st_ref[...]
    y = (jnp.concatenate(y_diag, axis=1) + jnp.concatenate(y_off, axis=1) * ecs
         + xs * dsk_ref[...])
    gated = y[:length] * _silu(z_ref[...])
    ms = jnp.mean(gated * gated, axis=-1, keepdims=True)
    out_ref[...] = (gated * lax.rsqrt(ms + EPS) * nw_ref[...]).astype(out_ref.dtype)


def _ssd(xbc, zdt, prefix, h0, conv_w, conv_b, dt_bias, a_log, d_skip, ssd_norm_w,
         batch, n_chunks, length, out_dtype):
    L = SSD_CHUNK
    pre = jnp.pad(prefix, ((0, 0), (SUBLANES - (CONV_WIDTH - 1), 0), (0, 0)))
    pad_h = lambda a: jnp.pad(a, (0, HEAD_DIM - SSD_HEADS)).reshape(1, HEAD_DIM)
    ltri = (jnp.arange(L)[:, None] >= jnp.arange(L)[None, :]).astype(BF16)
    expand = (jnp.arange(HEAD_DIM)[:, None] == jnp.arange(SSD_WIDTH)[None, :] // SSD_HEAD_DIM
              ).astype(BF16)
    dsk = jnp.repeat(d_skip, SSD_HEAD_DIM).reshape(1, SSD_WIDTH)
    z_blk = SSD_WIDTH // HEAD_DIM
    n_par = SSD_SEQS_PER_STEP
    assert batch % n_par == 0 and length <= L and (length == L or n_chunks == 1)
    rows = n_chunks * length
    xbc3 = xbc.reshape(batch, rows, XBC_WIDTH)
    zdt3 = zdt.reshape(batch, rows, PROJ_TN)
    const = lambda b, c: (0, 0)
    seq = lambda b, c: (b, 0, 0)
    kern = functools.partial(_ssd_kernel, length, n_par)
    out, conv_new, ssm_new = pl.pallas_call(
        kern,
        grid=(batch // n_par, n_chunks),
        in_specs=[
            pl.BlockSpec((n_par, length, XBC_WIDTH), lambda b, c: (b, c, 0)),
            pl.BlockSpec((n_par, length, SSD_WIDTH), lambda b, c: (b, c, 0)),
            pl.BlockSpec((n_par, length, HEAD_DIM), lambda b, c: (b, c, z_blk)),
            pl.BlockSpec((n_par, SUBLANES, XBC_WIDTH), seq),
            pl.BlockSpec((n_par, SSD_WIDTH, SSD_STATE), seq),
            pl.BlockSpec((CONV_WIDTH, XBC_WIDTH), const),
            pl.BlockSpec((1, XBC_WIDTH), const),
            pl.BlockSpec((1, HEAD_DIM), const),
            pl.BlockSpec((1, HEAD_DIM), const),
            pl.BlockSpec((1, SSD_WIDTH), const),
            pl.BlockSpec((1, SSD_WIDTH), const),
            pl.BlockSpec((L, L), const),
            pl.BlockSpec((HEAD_DIM, SSD_WIDTH), const),
        ],
        out_specs=[
            pl.BlockSpec((n_par, length, SSD_WIDTH), lambda b, c: (b, c, 0)),
            pl.BlockSpec((n_par, SUBLANES, XBC_WIDTH), seq),
            pl.BlockSpec((n_par, SSD_WIDTH, SSD_STATE), seq),
        ],
        out_shape=[
            jax.ShapeDtypeStruct((batch, rows, SSD_WIDTH), out_dtype),
            jax.ShapeDtypeStruct((batch, SUBLANES, XBC_WIDTH), F32),
            jax.ShapeDtypeStruct((batch, SSD_WIDTH, SSD_STATE), F32),
        ],
        scratch_shapes=[pltpu.VMEM((n_par, SUBLANES + L, XBC_WIDTH), F32),
                        pltpu.VMEM((n_par, SSD_WIDTH, SSD_STATE), F32)],
        compiler_params=pltpu.CompilerParams(
            dimension_semantics=("parallel", "arbitrary"), vmem_limit_bytes=VMEM_LIMIT),
        name="ssd_scan",
    )(xbc3, zdt3, zdt3, pre, h0.reshape(batch, SSD_WIDTH, SSD_STATE), conv_w,
      conv_b.reshape(1, XBC_WIDTH), pad_h(dt_bias), pad_h(a_log), dsk,
      ssd_norm_w.reshape(1, SSD_WIDTH), ltri, expand)
    return out.reshape(batch * rows, SSD_WIDTH), conv_new, ssm_new


def _mem_attn_kernel(q_ref, g_ref, k_ref, v_ref, o_ref):
    for h in range(MEM_HEADS):
        cols = slice(HEAD_DIM * h, HEAD_DIM * (h + 1))
        s = _dot_nt(q_ref[:, cols].astype(BF16), k_ref[:, cols].astype(BF16)) * ATTN_SCALE
        p = jnp.exp(s - jnp.max(s, axis=-1, keepdims=True))
        den = jnp.sum(p, axis=-1, keepdims=True)
        o = _dot(p.astype(BF16), v_ref[:, cols].astype(BF16)) / den
        o_ref[:, cols] = (o * _silu(g_ref[:, cols])).astype(o_ref.dtype)


def _mem_attn(q, g, mem_k, mem_v, batch, t, tq, out_dtype):
    nq = t // tq
    return pl.pallas_call(
        _mem_attn_kernel,
        grid=(batch, nq),
        in_specs=[
            pl.BlockSpec((tq, MEM_WIDTH), lambda b, i: (b * nq + i, 0)),
            pl.BlockSpec((tq, MEM_WIDTH), lambda b, i: (b * nq + i, 0)),
            pl.BlockSpec((None, MEM_TOKENS, MEM_WIDTH), lambda b, i: (b, 0, 0)),
            pl.BlockSpec((None, MEM_TOKENS, MEM_WIDTH), lambda b, i: (b, 0, 0)),
        ],
        out_specs=pl.BlockSpec((tq, MEM_WIDTH), lambda b, i: (b * nq + i, 0)),
        out_shape=jax.ShapeDtypeStruct((batch * t, MEM_WIDTH), out_dtype),
        compiler_params=pltpu.CompilerParams(
            dimension_semantics=("parallel", "parallel"), vmem_limit_bytes=VMEM_LIMIT),
        name="mem_attn",
    )(q, g, mem_k, mem_v)


def _out_proj_kernel(x_ref, sb_ref, ssd_ref, mo_ref, w_ref, o_ref):
    mix = jnp.concatenate([sb_ref[...].astype(BF16), ssd_ref[...].astype(BF16),
                           mo_ref[...].astype(BF16)], axis=-1)
    for c in range(0, o_ref.shape[1], PROJ_CHUNK):
        cols = slice(c, c + PROJ_CHUNK)
        o_ref[:, cols] = x_ref[:, cols] + _dot(mix, w_ref[:, cols].astype(BF16))


def _out_proj(x, sb, ssd, mo, w_out, tm, tn):
    t, d = x.shape
    return pl.pallas_call(
        _out_proj_kernel,
        grid=(t // tm, d // tn),
        in_specs=[
            pl.BlockSpec((tm, tn), lambda m, n: (m, n)),
            pl.BlockSpec((tm, SB_WIDTH), lambda m, n: (m, 0)),
            pl.BlockSpec((tm, SSD_WIDTH), lambda m, n: (m, 0)),
            pl.BlockSpec((tm, MEM_WIDTH), lambda m, n: (m, 0)),
            pl.BlockSpec((w_out.shape[0], tn), lambda m, n: (0, n), pipeline_mode=pl.Buffered(1)),
        ],
        out_specs=pl.BlockSpec((tm, tn), lambda m, n: (m, n)),
        out_shape=jax.ShapeDtypeStruct((t, d), F32),
        compiler_params=pltpu.CompilerParams(
            dimension_semantics=("parallel", "arbitrary"), vmem_limit_bytes=VMEM_LIMIT),
        name="out_proj",
    )(x, sb, ssd, mo, w_out)


def _in_proj_plan(act_dtype):
    lowp = act_dtype == BF16
    outs, plan = [], []

    def add(width, dtype):
        outs.append((width, dtype))
        return len(outs) - 1

    q = add(SB_WIDTH, act_dtype)
    plan.append(((0, SB_WIDTH, 0, None if lowp else q, q if lowp else None),))
    k32 = add(SB_WIDTH, F32)
    k16 = add(SB_WIDTH, BF16) if lowp else None
    plan.append(((0, SB_WIDTH, 1, k32, k16),))
    v32 = add(SB_WIDTH, F32)
    v16 = add(SB_WIDTH, BF16) if lowp else None
    plan.append(((0, SB_WIDTH, None, v32, v16),))
    g = add(SB_WIDTH, F32)
    plan.append(((0, SB_WIDTH, None, g, None),))
    xbc = add(XBC_WIDTH, F32)
    plan.append(((0, XBC_WIDTH, None, xbc, None),))
    zdt = add(PROJ_TN, F32)
    plan.append(((0, PROJ_TN, None, zdt, None),))
    mq = add(MEM_WIDTH, act_dtype)
    mg = add(MEM_WIDTH, F32)
    plan.append(((0, MEM_WIDTH, 2, None if lowp else mq, mq if lowp else None),
                 (MEM_WIDTH, MEM_WIDTH, None, mg, None)))
    names = dict(q=q, k32=k32, k16=k16, v32=v32, v16=v16, g=g, xbc=xbc, zdt=zdt, mq=mq, mg=mg)
    return tuple(plan), outs, names


_O_Z = 4 * SB_WIDTH
_O_XBC = _O_Z + SSD_WIDTH
_O_DT = _O_XBC + XBC_WIDTH
_O_MEM = _O_DT + SSD_HEADS
_IN_WIDTH = _O_MEM + 2 * MEM_WIDTH
_CAT_WIDTH = 7 * PROJ_TN
W_PREP_COLS = 256
BF16_ROWS = 16


def _w_prep_kernel(w_ref, o_ref):
    def put(dst, src, rows):
        o_ref[dst:dst + rows, :] = w_ref[src:src + rows, :].astype(BF16)

    cols = w_ref.shape[1]
    put(0, 0, _O_Z)
    put(_O_Z, _O_XBC, XBC_WIDTH)
    put(_O_Z + XBC_WIDTH, _O_Z, SSD_WIDTH)
    dt0 = _O_Z + XBC_WIDTH + SSD_WIDTH
    o_ref[dt0:dt0 + BF16_ROWS, :] = jnp.concatenate(
        [w_ref[_O_DT:_O_MEM, :], jnp.zeros((BF16_ROWS - SSD_HEADS, cols), F32)], axis=0
    ).astype(BF16)
    o_ref[dt0 + BF16_ROWS:6 * PROJ_TN, :] = jnp.zeros((6 * PROJ_TN - dt0 - BF16_ROWS, cols), BF16)
    put(6 * PROJ_TN, _O_MEM, 2 * MEM_WIDTH)


def _rearranged_w_in(w_t):
    d = w_t.shape[1]
    assert w_t.shape[0] == _IN_WIDTH and d % W_PREP_COLS == 0
    return pl.pallas_call(
        _w_prep_kernel,
        grid=(d // W_PREP_COLS,),
        in_specs=[pl.BlockSpec((_IN_WIDTH, W_PREP_COLS), lambda c: (0, c))],
        out_specs=pl.BlockSpec((_CAT_WIDTH, W_PREP_COLS), lambda c: (0, c)),
        out_shape=jax.ShapeDtypeStruct((_CAT_WIDTH, d), BF16),
        compiler_params=pltpu.CompilerParams(
            dimension_semantics=("parallel",), vmem_limit_bytes=VMEM_LIMIT),
        name="w_prep",
    )(w_t)


def kernel(x_prompt, x_sample, cache_sb_k, cache_sb_v, state_ssm, state_conv, cache_mem_k,
           cache_mem_v, page_table, mem_prompt, norm_w, w_in, sb_q_norm, sb_k_norm, sb_bias,
           conv_w, conv_b, dt_bias, a_log, d_skip, ssd_norm_w, mem_norm_w, w_mem_kv, mem_q_norm,
           mem_k_norm, w_out):
    depth = w_in.shape[0]
    assert depth == 1
    layer = 0
    bp, sp, d = x_prompt.shape
    bs, ts, _ = x_sample.shape
    n_pool = cache_sb_k.shape[1]
    L = SSD_CHUNK

    w_cat = _rearranged_w_in(w_in[layer].T)
    w_o = w_out[layer]
    head_norms = jnp.concatenate(
        [sb_q_norm[layer][None], sb_k_norm[layer][None], mem_q_norm[layer][None],
         mem_k_norm[layer][None], jnp.zeros((SUBLANES - 4, HEAD_DIM), F32)], axis=0)
    ssd_params = (conv_w[layer], conv_b[layer], dt_bias[layer], a_log[layer], d_skip[layer],
                  ssd_norm_w[layer])

    xp = x_prompt.reshape(bp * sp, d)
    mem_plan = (((0, MEM_WIDTH, 3, 0, None), (MEM_WIDTH, MEM_WIDTH, None, 1, None)),)
    (mk, mv), _ = _proj(mem_prompt.reshape(bp * MEM_TOKENS, d), mem_norm_w[layer],
                        w_mem_kv[layer], head_norms, mem_plan,
                        [(MEM_WIDTH, F32), (MEM_WIDTH, F32)], tm=512, w_rows_are_outputs=False)
    plan, outs, nm = _in_proj_plan(BF16)
    xs = x_sample.reshape(bs * ts, d)
    plan_s, outs_s, ns = _in_proj_plan(F32)
    pr, ps = _proj(xp, norm_w[layer], w_cat, head_norms, plan, outs, tm=512,
                   side=(xs, plan_s, outs_s))
    sb, sb_s = _sb_fused(pr[nm['q']], pr[nm['k16']], pr[nm['v16']], pr[nm['g']], bp, sp, 256,
                         ps[ns['q']], ps[ns['k32']], ps[ns['v32']], ps[ns['g']],
                         cache_sb_k, cache_sb_v, layer, page_table, sb_bias[layer], ts)
    ssd, conv_p, ssm_p = _ssd(
        pr[nm['xbc']], pr[nm['zdt']], jnp.zeros((bp, CONV_WIDTH - 1, XBC_WIDTH), F32),
        jnp.zeros((bp, SSD_HEADS, SSD_HEAD_DIM, SSD_STATE), F32), *ssd_params,
        batch=bp, n_chunks=sp // L, length=L, out_dtype=BF16)
    mo = _mem_attn(pr[nm['mq']], pr[nm['mg']], mk.reshape(bp, MEM_TOKENS, MEM_WIDTH),
                   mv.reshape(bp, MEM_TOKENS, MEM_WIDTH), bp, sp, tq=512, out_dtype=BF16)
    yp = _out_proj(xp, sb, ssd, mo, w_o, tm=512, tn=d)

    ssd_s, conv_s, ssm_s = _ssd(
        ps[ns['xbc']], ps[ns['zdt']], state_conv[layer], state_ssm[layer],
        *ssd_params, batch=bs, n_chunks=1, length=ts, out_dtype=F32)
    mo_s = _mem_attn(ps[ns['mq']], ps[ns['mg']],
                     cache_mem_k[layer].reshape(bs, MEM_TOKENS, MEM_WIDTH),
                     cache_mem_v[layer].reshape(bs, MEM_TOKENS, MEM_WIDTH), bs, ts, tq=ts,
                     out_dtype=F32)
    ys = _out_proj(xs, sb_s, ssd_s, mo_s, w_o, tm=bs * ts, tn=d)

    tail = slice(SUBLANES - (CONV_WIDTH - 1), SUBLANES)
    return (
        yp.reshape(bp, sp, d),
        ys.reshape(bs, ts, d),
        pr[nm['k32']].reshape(1, bp, sp, SB_HEADS, HEAD_DIM),
        pr[nm['v32']].reshape(1, bp, sp, SB_HEADS, HEAD_DIM),
        ssm_p.reshape(1, bp, SSD_HEADS, SSD_HEAD_DIM, SSD_STATE),
        conv_p[:, tail][None],
        mk.reshape(1, bp, MEM_TOKENS, MEM_HEADS, HEAD_DIM),
        mv.reshape(1, bp, MEM_TOKENS, MEM_HEADS, HEAD_DIM),
        ps[ns['k32']].reshape(1, bs, ts, SB_HEADS, HEAD_DIM),
        ps[ns['v32']].reshape(1, bs, ts, SB_HEADS, HEAD_DIM),
        ssm_s.reshape(1, bs, SSD_HEADS, SSD_HEAD_DIM, SSD_STATE),
        conv_s[:, tail][None],
    )
```

```python
import functools
import math

import jax
import jax.numpy as jnp
from jax import lax
from jax.experimental import pallas as pl
from jax.experimental.pallas import tpu as pltpu

F32 = jnp.float32
BF16 = jnp.bfloat16

D_MODEL = 2048
SB_HEADS = 8
HEAD_DIM = 128
SB_WIDTH = SB_HEADS * HEAD_DIM
SSD_HEADS = 8
SSD_HEAD_DIM = 64
SSD_WIDTH = SSD_HEADS * SSD_HEAD_DIM
SSD_GROUPS = 2
SSD_STATE = 128
CONV_WIDTH = 4
XBC_WIDTH = SSD_WIDTH + 2 * SSD_GROUPS * SSD_STATE
MEM_TOKENS = 256
MEM_HEADS = 4
MEM_WIDTH = MEM_HEADS * HEAD_DIM
PAGE_SIZE = 128
EPS = 1e-6
ATTN_SCALE = HEAD_DIM ** -0.5
LOG2E = math.log2(math.e)

SSD_CHUNK = 128
PROJ_TN = 1024
PROJ_CHUNK = 256
ZDT_WIDTH = SSD_WIDTH + PROJ_CHUNK
SAMPLE_BUFS = 3
SSD_SEQS_PER_STEP = 4
SUBLANES = 8
VMEM_LIMIT = 56 * 1024 * 1024
VMEM_LIMIT_FUSED = 62 * 1024 * 1024

_NT = (((1,), (1,)), ((), ()))


def _dot(a, b):
    return jnp.dot(a, b, preferred_element_type=F32)


def _dot_nt(a, b):
    return lax.dot_general(a, b, _NT, preferred_element_type=F32)


def _split2(x):
    hi = x.astype(BF16)
    lo = (x - hi.astype(F32)).astype(BF16)
    return hi, lo


def _split3(x):
    hi = x.astype(BF16)
    r = x - hi.astype(F32)
    mid = r.astype(BF16)
    lo = (r - mid.astype(F32)).astype(BF16)
    return hi, mid, lo


def _dots_exact_lhs(xs, m):
    rows = xs[0].shape[0]
    r = _dot(jnp.concatenate([t for x in xs for t in _split3(x)], axis=0), m)
    part = lambda i: r[rows * i:rows * (i + 1), :]
    return [part(3 * i) + part(3 * i + 1) + part(3 * i + 2) for i in range(len(xs))]


def _dot_exact_rhs(m, x):
    n = x.shape[1]
    r = _dot(m, jnp.concatenate(_split3(x), axis=1))
    return r[:, :n] + r[:, n:2 * n] + r[:, 2 * n:]


def _silu(x):
    return x * (1.0 / (1.0 + jnp.exp(-x)))


def _rmsnorm_rows(x_ref, nw_ref, h_ref):
    rows_total = x_ref.shape[0]
    rc = min(rows_total, 64)

    def body(r, carry):
        rows = pl.ds(pl.multiple_of(r * rc, rc), rc)
        xv = x_ref[rows, :]
        ms = jnp.mean(xv * xv, axis=-1, keepdims=True)
        h_ref[rows, :] = (xv * lax.rsqrt(ms + EPS) * nw_ref[...]).astype(BF16)
        return carry

    lax.fori_loop(0, rows_total // rc, body, 0)


def _proj_step(segs, h_ref, w_ref, hn_ref, outs, w_rows_are_outputs):
    for col0, width, hn_row, o32, o16 in segs:
        for c0 in range(0, width, PROJ_CHUNK):
            chunk = slice(col0 + c0, col0 + c0 + PROJ_CHUNK)
            if w_rows_are_outputs:
                y = _dot_nt(h_ref[...], w_ref[chunk, :].astype(BF16))
            else:
                y = _dot(h_ref[...], w_ref[:, chunk].astype(BF16))
            for c in range(0, PROJ_CHUNK, HEAD_DIM):
                yc = y[:, c:c + HEAD_DIM]
                if hn_row is not None:
                    ms = jnp.mean(yc * yc, axis=-1, keepdims=True)
                    yc = yc * lax.rsqrt(ms + EPS) * hn_ref[hn_row:hn_row + 1, :]
                cols = slice(c0 + c, c0 + c + HEAD_DIM)
                if o32 is not None:
                    outs[o32][:, cols] = yc
                if o16 is not None:
                    outs[o16][:, cols] = yc.astype(BF16)


def _proj_kernel(plan, n_out, side_plan, n_side, w_rows_are_outputs, *refs):
    if side_plan is None:
        x_ref, nw_ref, w_ref, hn_ref = refs[:4]
        rest = refs[4:]
    else:
        x_ref, xs_ref, nw_ref, w_ref, hn_ref = refs[:5]
        rest = refs[5:]
    outs, side_outs = rest[:n_out], rest[n_out:n_out + n_side]
    scratch = rest[n_out + n_side:]
    h_ref = scratch[0]
    m = pl.program_id(0)
    n = pl.program_id(1)

    @pl.when(n == 0)
    def _():
        _rmsnorm_rows(x_ref, nw_ref, h_ref)

    if side_plan is not None:
        hs_ref = scratch[1]

        @pl.when((n == 0) & (m == 0))
        def _():
            _rmsnorm_rows(xs_ref, nw_ref, hs_ref)

    for step, segs in enumerate(plan):
        @pl.when(n == step)
        def _(step=step, segs=segs):
            _proj_step(segs, h_ref, w_ref, hn_ref, outs, w_rows_are_outputs)
            if side_plan is not None:
                @pl.when(m == 0)
                def _():
                    _proj_step(side_plan[step], hs_ref, w_ref, hn_ref, side_outs,
                               w_rows_are_outputs)


def _proj(x, norm_w, w, head_norms, plan, out_defs, tm, side=None, w_rows_are_outputs=True):
    t, d = x.shape
    n_steps = len(plan)
    w_shape = (n_steps * PROJ_TN, d) if w_rows_are_outputs else (d, n_steps * PROJ_TN)
    assert w.shape == w_shape and t % tm == 0
    w_spec = (pl.BlockSpec((PROJ_TN, d), lambda m, n: (n, 0)) if w_rows_are_outputs
              else pl.BlockSpec((d, PROJ_TN), lambda m, n: (0, n)))
    row = lambda m, n: (m, 0)
    const = lambda m, n: (0, 0)
    in_specs = [pl.BlockSpec((tm, d), row)]
    operands = [x]
    out_specs = [pl.BlockSpec((tm, w), row) for w, _ in out_defs]
    out_shape = [jax.ShapeDtypeStruct((t, w), dt) for w, dt in out_defs]
    scratch = [pltpu.VMEM((tm, d), BF16)]
    side_plan, n_side = None, 0
    if side is not None:
        x_side, side_plan, side_defs = side
        ts = x_side.shape[0]
        assert len(side_plan) == n_steps
        n_side = len(side_defs)
        in_specs.append(pl.BlockSpec((ts, d), const))
        operands.append(x_side)
        out_specs += [pl.BlockSpec((ts, w), const) for w, _ in side_defs]
        out_shape += [jax.ShapeDtypeStruct((ts, w), dt) for w, dt in side_defs]
        scratch.append(pltpu.VMEM((ts, d), BF16))
    in_specs += [pl.BlockSpec((1, d), const), w_spec,
                 pl.BlockSpec((SUBLANES, HEAD_DIM), const)]
    operands += [norm_w.reshape(1, d), w, head_norms]
    kern = functools.partial(_proj_kernel, plan, len(out_defs), side_plan, n_side,
                             w_rows_are_outputs)
    res = pl.pallas_call(
        kern,
        grid=(t // tm, n_steps),
        in_specs=in_specs,
        out_specs=out_specs,
        out_shape=out_shape,
        scratch_shapes=scratch,
        compiler_params=pltpu.CompilerParams(
            dimension_semantics=("arbitrary", "arbitrary"), vmem_limit_bytes=VMEM_LIMIT),
        name="norm_proj",
    )(*operands)
    return res[:len(out_defs)], res[len(out_defs):]


def _log2_fail(z2):
    nz = -z2
    return jnp.minimum(nz, 0.0) - jnp.log(1.0 + jnp.exp2(jnp.minimum(z2, nz))) * LOG2E


def _sb_fused_kernel(tq, nq, ppc, n_pages, t_new, page_base,
                     pt_ref, bias_ref, qa_ref, qb_ref, k_ref, v_ref, ga_ref, gb_ref, uu_ref,
                     qs_ref, kn_ref, vn_ref, gs_ref, uo_ref, ck_hbm, cv_hbm,
                     op_ref, os_ref, acc_ref, c_ref, accs_ref, cs_ref, kbuf, vbuf, sem):
    step = pl.program_id(0)
    n_steps = pl.num_programs(0)
    pair = lax.rem(step, nq // 2)
    i1 = pair
    i2 = nq - 1 - pair
    chunks_per_seq = n_pages // ppc
    seq_steps = chunks_per_seq // nq
    part = lax.rem(step, seq_steps)
    n_chunks = n_steps * nq

    def chunk_copies(chunk):
        slot = lax.rem(chunk, SAMPLE_BUFS)
        seq = lax.div(chunk, chunks_per_seq)
        first_pos = (n_pages - 1) - lax.rem(chunk, chunks_per_seq) * ppc
        copies = []
        for j in range(ppc):
            page = page_base + pt_ref[seq * n_pages + first_pos - j]
            copies.append(pltpu.make_async_copy(ck_hbm.at[page], kbuf.at[slot, j], sem.at[slot, 0]))
            copies.append(pltpu.make_async_copy(cv_hbm.at[page], vbuf.at[slot, j], sem.at[slot, 1]))
        return copies

    def start_chunk(chunk):
        for cp in chunk_copies(chunk):
            cp.start()

    def wait_chunk(chunk):
        slot = lax.rem(chunk, SAMPLE_BUFS)
        pltpu.make_async_copy(ck_hbm.at[pl.ds(0, ppc)], kbuf.at[slot], sem.at[slot, 0]).wait()
        pltpu.make_async_copy(cv_hbm.at[pl.ds(0, ppc)], vbuf.at[slot], sem.at[slot, 1]).wait()

    def start_next(chunk):
        @pl.when(chunk + SAMPLE_BUFS < n_chunks)
        def _():
            start_chunk(chunk + SAMPLE_BUFS)

    @pl.when(step == 0)
    def _():
        for c in range(SAMPLE_BUFS):
            start_chunk(jnp.int32(c))

    lanes = PAGE_SIZE * SB_HEADS
    n_blk = lanes // HEAD_DIM
    lane = lax.broadcasted_iota(jnp.int32, (t_new, lanes), 1)
    lane_head = lane & (SB_HEADS - 1)
    biases = [bias_ref[h] * LOG2E for h in range(SB_HEADS)]
    bias_lanes = jnp.full((t_new, lanes), biases[0], F32)
    for h in range(1, SB_HEADS):
        bias_lanes = jnp.where(lane_head == h, biases[h], bias_lanes)
    qs = qs_ref[...]
    q_all = jnp.concatenate([qs[:, HEAD_DIM * h:HEAD_DIM * (h + 1)] for h in range(SB_HEADS)],
                            axis=0).astype(BF16)

    def page_scores(kpages, n, mask):
        s_cat = _dot_nt(q_all, kpages.astype(BF16))
        return [scores(s_cat[:, lanes * p:lanes * (p + 1)], mask) for p in range(n)]

    def scores(s_all, mask):
        sc = s_all[0:t_new, :]
        for h in range(1, SB_HEADS):
            sc = jnp.where(lane_head == h, s_all[t_new * h:t_new * (h + 1), :], sc)
        z2 = sc * (ATTN_SCALE * LOG2E) + bias_lanes
        lf = _log2_fail(z2)
        if mask is not None:
            lf = jnp.where(mask, lf, 0.0)
        blocks = jnp.concatenate([lf[:, HEAD_DIM * j:HEAD_DIM * (j + 1)] for j in range(n_blk)],
                                 axis=0)
        hi, lo = _split2(blocks)
        return z2, jnp.concatenate([hi, lo], axis=1)

    def stacked_sums(operands, matrix):
        rows = operands[0].shape[0]
        res = _dot(jnp.concatenate(operands, axis=0), matrix)
        return [res[rows * i:rows * (i + 1), :] for i in range(len(operands))]

    def weights(z2, res, mask, run):
        ws = [None] * n_blk
        for j in reversed(range(n_blk)):
            rows = slice(t_new * j, t_new * (j + 1))
            logw = z2[:, HEAD_DIM * j:HEAD_DIM * (j + 1)] + res[rows, :HEAD_DIM]
            if run is not None:
                logw = logw + run
            ws[j] = jnp.exp2(logw)
            tot = res[rows, HEAD_DIM:]
            run = tot if run is None else run + tot
        w = jnp.concatenate(ws, axis=1)
        if mask is not None:
            w = jnp.where(mask, w, 0.0)
        w_all = jnp.concatenate([jnp.where(lane_head == h, w, 0.0) for h in range(SB_HEADS)],
                                axis=0).astype(BF16)
        return w_all, run

    def new_keys():
        mask = (lane >> 3) < lax.broadcasted_iota(jnp.int32, (t_new, lanes), 0)
        (z2, hilo), = page_scores(kn_ref[...], 1, mask)
        res, = stacked_sums([hilo], uo_ref[...])
        w_all, run = weights(z2, res, mask, None)
        cs_ref[...] = run
        accs_ref[...] = _dot(w_all, vn_ref[...].astype(BF16))

    uu = uu_ref[...]
    cols = [slice(HEAD_DIM * h, HEAD_DIM * (h + 1)) for h in range(SB_HEADS)]

    def section(q_ref, start, diag, first, chunk):
        kb = k_ref[pl.ds(start, tq), :]
        vb = v_ref[pl.ds(start, tq), :]
        n_pages_here = 0 if chunk is None else ppc
        slot = None if chunk is None else lax.rem(chunk, SAMPLE_BUFS)
        if n_pages_here:
            run = cs_ref[...]
            acc = accs_ref[...]
        p_parts, s_parts = {}, {}
        n_idx = max(SB_HEADS, n_pages_here)

        raw = [_dot_nt(q_ref[:, cols[h]], kb[:, cols[h]]) for h in range(SB_HEADS)]
        if n_pages_here:
            s_split = page_scores(kbuf[slot].reshape(ppc * lanes, HEAD_DIM), ppc, None)
        split = []
        for h in range(SB_HEADS):
            z2 = raw[h] * (ATTN_SCALE * LOG2E) + biases[h]
            lf = _log2_fail(z2)
            if diag is not None:
                lf = jnp.where(diag, lf, 0.0)
            hi, lo = _split2(lf)
            split.append((z2, jnp.concatenate([hi, lo], axis=1)))
        incl = stacked_sums([hilo for _, hilo in split], uu)
        for h in range(SB_HEADS):
            p_parts[h] = (split[h][0], incl[h])
        if n_pages_here:
            res = stacked_sums([hilo for _, hilo in s_split], uo_ref[...])
            for idx in range(n_pages_here):
                s_parts[idx] = (s_split[idx][0], res[idx])
        w_pages = []
        for idx in range(n_idx):
            if idx < SB_HEADS:
                z2, incl_h = p_parts[idx]
                logw = z2 + incl_h
                if not first:
                    logw = logw + c_ref[idx]
                w = jnp.exp2(logw)
                if diag is not None:
                    w = jnp.where(diag, w, 0.0)
                pv = _dot(w.astype(BF16), vb[:, cols[idx]])
                total = incl_h[:, 0:1]
                if first:
                    acc_ref[:, cols[idx]] = pv
                    c_ref[idx] = total
                else:
                    acc_ref[:, cols[idx]] += pv
                    c_ref[idx] += total
            if idx < n_pages_here:
                w_all, run = weights(*s_parts[idx], None, run)
                w_pages.append(w_all)
        if n_pages_here:
            cs_ref[...] = run
            accs_ref[...] = acc + _dot(jnp.concatenate(w_pages, axis=1),
                                       vbuf[slot].reshape(ppc * lanes, HEAD_DIM).astype(BF16))

    def finish(i, g_ref):
        rows = pl.ds(pl.multiple_of(i * tq, tq), tq)
        op_ref[rows, :] = (acc_ref[...] * _silu(g_ref[...])).astype(op_ref.dtype)

    row = lax.broadcasted_iota(jnp.int32, (tq, tq), 0)
    col = lax.broadcasted_iota(jnp.int32, (tq, tq), 1)
    diag = col < row
    chunk0 = step * nq

    @pl.when(part == 0)
    def _():
        new_keys()

    section(qa_ref, pl.multiple_of(i1 * tq, tq), diag, True, None)

    def body_a(t, carry):
        chunk = chunk0 + t
        wait_chunk(chunk)
        section(qa_ref, pl.multiple_of((i1 - 1 - t) * tq, tq), None, False, chunk)
        start_next(chunk)
        return carry

    lax.fori_loop(0, i1, body_a, 0)
    finish(i1, ga_ref)

    chunk = chunk0 + i1
    wait_chunk(chunk)
    section(qb_ref, pl.multiple_of(i2 * tq, tq), diag, True, chunk)
    start_next(chunk)

    def body_b(t, carry):
        chunk = chunk0 + i1 + 1 + t
        wait_chunk(chunk)
        section(qb_ref, pl.multiple_of((i2 - 1 - t) * tq, tq), None, False, chunk)
        start_next(chunk)
        return carry

    lax.fori_loop(0, i2, body_b, 0)
    finish(i2, gb_ref)

    @pl.when(part == seq_steps - 1)
    def _():
        g = gs_ref[...]
        for h in range(SB_HEADS):
            os_ref[:, cols[h]] = accs_ref[t_new * h:t_new * (h + 1), :] * _silu(g[:, cols[h]])


def _sb_fused(q, k, v, g, batch, seq, tq, q_s, k_new, v_new, g_s, cache_k, cache_v, layer,
              page_table, sb_bias, t_new):
    n_seq, n_pages = page_table.shape
    n_pool = cache_k.shape[1]
    nq = seq // tq
    n_steps = batch * (nq // 2)
    assert nq % 2 == 0 and (n_seq * n_pages) % (n_steps * nq) == 0
    ppc = n_seq * n_pages // (n_steps * nq)
    assert n_pages % (ppc * nq) == 0
    seq_steps = n_pages // (ppc * nq)
    page_rows = PAGE_SIZE * SB_HEADS

    q3, k3, v3, g3 = (a.reshape(batch, seq, SB_WIDTH) for a in (q, k, v, g))
    u = (jnp.arange(tq)[:, None] >= jnp.arange(tq)[None, :]).astype(BF16)
    uu = jnp.concatenate([u, u], axis=0)

    def as_page(a):
        a = a.reshape(n_seq, t_new, SB_HEADS, HEAD_DIM)
        a = jnp.pad(a, ((0, 0), (0, PAGE_SIZE - t_new), (0, 0), (0, 0)))
        return a.reshape(n_seq * page_rows, HEAD_DIM)

    cache_k = cache_k.reshape(-1, page_rows, HEAD_DIM)
    cache_v = cache_v.reshape(-1, page_rows, HEAD_DIM)
    r = jnp.arange(HEAD_DIM)
    same_head = (r[:, None] % SB_HEADS) == (r[None, :] % SB_HEADS)
    not_earlier = (r[:, None] // SB_HEADS) >= (r[None, :] // SB_HEADS)
    uo = jnp.concatenate([same_head & not_earlier, same_head], axis=1).astype(BF16)
    uo = jnp.concatenate([uo, uo], axis=0)

    half = nq // 2
    b_of = lambda s: s // half
    qa_map = lambda s, pt: (b_of(s), s % half, 0)
    qb_map = lambda s, pt: (b_of(s), nq - 1 - s % half, 0)
    seq_map = lambda s, pt: (b_of(s), 0, 0)
    samp_map = lambda s, pt: (s // seq_steps, 0)
    const = lambda s, pt: (0, 0)
    kern = functools.partial(_sb_fused_kernel, tq, nq, ppc, n_pages, t_new, layer * n_pool)
    grid_spec = pltpu.PrefetchScalarGridSpec(
        num_scalar_prefetch=1,
        grid=(n_steps,),
        in_specs=[
            pl.BlockSpec(memory_space=pltpu.SMEM),
            pl.BlockSpec((None, tq, SB_WIDTH), qa_map),
            pl.BlockSpec((None, tq, SB_WIDTH), qb_map),
            pl.BlockSpec((None, seq, SB_WIDTH), seq_map, pipeline_mode=pl.Buffered(1)),
            pl.BlockSpec((None, seq, SB_WIDTH), seq_map, pipeline_mode=pl.Buffered(1)),
            pl.BlockSpec((None, tq, SB_WIDTH), qa_map),
            pl.BlockSpec((None, tq, SB_WIDTH), qb_map),
            pl.BlockSpec((2 * tq, tq), const),
            pl.BlockSpec((t_new, SB_WIDTH), samp_map),
            pl.BlockSpec((page_rows, HEAD_DIM), samp_map),
            pl.BlockSpec((page_rows, HEAD_DIM), samp_map),
            pl.BlockSpec((t_new, SB_WIDTH), samp_map),
            pl.BlockSpec((2 * HEAD_DIM, 2 * HEAD_DIM), const),
            pl.BlockSpec(memory_space=pl.ANY),
            pl.BlockSpec(memory_space=pl.ANY),
        ],
        out_specs=[
            pl.BlockSpec((None, seq, SB_WIDTH), seq_map, pipeline_mode=pl.Buffered(1)),
            pl.BlockSpec((t_new, SB_WIDTH), samp_map),
        ],
        scratch_shapes=[
            pltpu.VMEM((tq, SB_WIDTH), F32),
            pltpu.VMEM((SB_HEADS, tq, 1), F32),
            pltpu.VMEM((SB_HEADS * t_new, HEAD_DIM), F32),
            pltpu.VMEM((t_new, HEAD_DIM), F32),
            pltpu.VMEM((SAMPLE_BUFS, ppc, page_rows, HEAD_DIM), F32),
            pltpu.VMEM((SAMPLE_BUFS, ppc, page_rows, HEAD_DIM), F32),
            pltpu.SemaphoreType.DMA((SAMPLE_BUFS, 2)),
        ],
    )
    out_p, out_s = pl.pallas_call(
        kern,
        grid_spec=grid_spec,
        out_shape=[jax.ShapeDtypeStruct((batch, seq, SB_WIDTH), BF16),
                   jax.ShapeDtypeStruct((n_seq * t_new, SB_WIDTH), F32)],
        compiler_params=pltpu.CompilerParams(
            dimension_semantics=("arbitrary",), vmem_limit_bytes=VMEM_LIMIT_FUSED),
        name="sb_fused",
    )(page_table.reshape(-1), sb_bias, q3, q3, k3, v3, g3, g3, uu,
      q_s, as_page(k_new), as_page(v_new), g_s, uo, cache_k, cache_v)
    return out_p.reshape(batch * seq, SB_WIDTH), out_s


def _ssd_kernel(length, n_par, *refs):
    per_seq_in, shared, per_seq_out = refs[:5], refs[5:13], refs[13:]
    pre_ref, h0_ref = per_seq_in[3:5]
    ext_ref, st_ref = per_seq_out[3:5]

    @pl.when(pl.program_id(1) == 0)
    def _():
        ext_ref[:, 0:SUBLANES, :] = pre_ref[...]
        st_ref[...] = h0_ref[...]

    for s in range(n_par):
        _ssd_chunk(length, *[r.at[s] for r in per_seq_in[:3]], *shared,
                   *[r.at[s] for r in per_seq_out])


def _pad_rows(a, rows):
    if a.shape[0] == rows:
        return a
    return jnp.concatenate([a, jnp.zeros((rows - a.shape[0], a.shape[1]), a.dtype)], axis=0)


def _ssd_chunk(length, xbc_ref, z_ref, dt_ref, cw_ref, cb_ref, dtb_ref,
               alog_ref, dsk_ref, nw_ref, ltri_ref, e_ref, out_ref, cnew_ref, snew_ref,
               ext_ref, st_ref):
    L = SSD_CHUNK
    P = SSD_HEAD_DIM

    ext_ref[SUBLANES:SUBLANES + L, :] = _pad_rows(xbc_ref[...], L)
    cw = cw_ref[...]
    conv = cb_ref[...]
    for j in range(CONV_WIDTH):
        off = SUBLANES - (CONV_WIDTH - 1) + j
        conv = conv + ext_ref[off:off + L, :] * cw[j:j + 1, :]
    act = _silu(conv)
    tail = ext_ref[length:length + SUBLANES, :]
    cnew_ref[...] = tail
    ext_ref[0:SUBLANES, :] = tail

    xs = act[:, :SSD_WIDTH]
    bm = act[:, SSD_WIDTH:SSD_WIDTH + SSD_GROUPS * SSD_STATE]
    cm = act[:, SSD_WIDTH + SSD_GROUPS * SSD_STATE:]

    x_dt = _pad_rows(dt_ref[...], L) + dtb_ref[...]
    dt = jnp.maximum(x_dt, 0.0) + jnp.log1p(jnp.exp(-jnp.abs(x_dt)))
    if length < L:
        valid = lax.broadcasted_iota(jnp.int32, dt.shape, 0) < length
        dt = jnp.where(valid, dt, 0.0)
    da = dt * (-jnp.exp(alog_ref[...]))
    cs = _dot_exact_rhs(ltri_ref[...], da)
    cs_t = cs.T
    e = e_ref[...]
    dt_x, cs_x = _dots_exact_lhs([dt, cs], e)
    xdt = xs * dt_x
    ecs = jnp.exp(cs_x)
    xw_t = (xdt * jnp.exp(cs_x[L - 1:L, :] - cs_x)).T
    xdt16 = xdt.astype(BF16)

    row = lax.broadcasted_iota(jnp.int32, (L, L), 0)
    col = lax.broadcasted_iota(jnp.int32, (L, L), 1)
    causal = col <= row
    heads_per_group = SSD_HEADS // SSD_GROUPS
    gw = heads_per_group * P
    y_diag, y_off = [], []
    for g in range(SSD_GROUPS):
        bg = bm[:, SSD_STATE * g:SSD_STATE * (g + 1)].astype(BF16)
        cg = cm[:, SSD_STATE * g:SSD_STATE * (g + 1)].astype(BF16)
        cb = _dot_nt(cg, bg)
        prev = st_ref[gw * g:gw * (g + 1), :]
        y_off.append(_dot_nt(cg, prev.astype(BF16)))
        new = _dot(xw_t[gw * g:gw * (g + 1), :].astype(BF16), bg)
        for r in range(heads_per_group):
            h = heads_per_group * g + r
            seg = cs[:, h:h + 1] - cs_t[h:h + 1, :]
            decay = jnp.exp(jnp.where(causal, seg, -jnp.inf))
            y_diag.append(_dot((cb * decay).astype(BF16), xdt16[:, P * h:P * (h + 1)]))
            chunk_decay = jnp.exp(cs[L - 1:L, h:h + 1])
            st_ref[P * h:P * (h + 1), :] = (prev[P * r:P * (r + 1), :] * chunk_decay
                                            + new[P * r:P * (r + 1), :])
    snew_ref[...] = st_ref[...]
    y = (jnp.concatenate(y_diag, axis=1) + jnp.concatenate(y_off, axis=1) * ecs
         + xs * dsk_ref[...])
    gated = y[:length] * _silu(z_ref[...])
    ms = jnp.mean(gated * gated, axis=-1, keepdims=True)
    out_ref[...] = (gated * lax.rsqrt(ms + EPS) * nw_ref[...]).astype(out_ref.dtype)


def _ssd(xbc, zdt, prefix, h0, conv_w, conv_b, dt_bias, a_log, d_skip, ssd_norm_w,
         batch, n_chunks, length, out_dtype):
    L = SSD_CHUNK
    pre = jnp.pad(prefix, ((0, 0), (SUBLANES - (CONV_WIDTH - 1), 0), (0, 0)))
    pad_h = lambda a: jnp.pad(a, (0, HEAD_DIM - SSD_HEADS)).reshape(1, HEAD_DIM)
    ltri = (jnp.arange(L)[:, None] >= jnp.arange(L)[None, :]).astype(BF16)
    expand = (jnp.arange(HEAD_DIM)[:, None] == jnp.arange(SSD_WIDTH)[None, :] // SSD_HEAD_DIM
              ).astype(BF16)
    dsk = jnp.repeat(d_skip, SSD_HEAD_DIM).reshape(1, SSD_WIDTH)
    z_blk = SSD_WIDTH // HEAD_DIM
    n_par = SSD_SEQS_PER_STEP
    assert batch % n_par == 0 and length <= L and (length == L or n_chunks == 1)
    rows = n_chunks * length
    xbc3 = xbc.reshape(batch, rows, XBC_WIDTH)
    zdt3 = zdt.reshape(batch, rows, ZDT_WIDTH)
    const = lambda b, c: (0, 0)
    seq = lambda b, c: (b, 0, 0)
    kern = functools.partial(_ssd_kernel, length, n_par)
    out, conv_new, ssm_new = pl.pallas_call(
        kern,
        grid=(batch // n_par, n_chunks),
        in_specs=[
            pl.BlockSpec((n_par, length, XBC_WIDTH), lambda b, c: (b, c, 0)),
            pl.BlockSpec((n_par, length, SSD_WIDTH), lambda b, c: (b, c, 0)),
            pl.BlockSpec((n_par, length, HEAD_DIM), lambda b, c: (b, c, z_blk)),
            pl.BlockSpec((n_par, SUBLANES, XBC_WIDTH), seq),
            pl.BlockSpec((n_par, SSD_WIDTH, SSD_STATE), seq),
            pl.BlockSpec((CONV_WIDTH, XBC_WIDTH), const),
            pl.BlockSpec((1, XBC_WIDTH), const),
            pl.BlockSpec((1, HEAD_DIM), const),
            pl.BlockSpec((1, HEAD_DIM), const),
            pl.BlockSpec((1, SSD_WIDTH), const),
            pl.BlockSpec((1, SSD_WIDTH), const),
            pl.BlockSpec((L, L), const),
            pl.BlockSpec((HEAD_DIM, SSD_WIDTH), const),
        ],
        out_specs=[
            pl.BlockSpec((n_par, length, SSD_WIDTH), lambda b, c: (b, c, 0)),
            pl.BlockSpec((n_par, SUBLANES, XBC_WIDTH), seq),
            pl.BlockSpec((n_par, SSD_WIDTH, SSD_STATE), seq),
        ],
        out_shape=[
            jax.ShapeDtypeStruct((batch, rows, SSD_WIDTH), out_dtype),
            jax.ShapeDtypeStruct((batch, SUBLANES, XBC_WIDTH), F32),
            jax.ShapeDtypeStruct((batch, SSD_WIDTH, SSD_STATE), F32),
        ],
        scratch_shapes=[pltpu.VMEM((n_par, SUBLANES + L, XBC_WIDTH), F32),
                        pltpu.VMEM((n_par, SSD_WIDTH, SSD_STATE), F32)],
        compiler_params=pltpu.CompilerParams(
            dimension_semantics=("parallel", "arbitrary"), vmem_limit_bytes=VMEM_LIMIT),
        name="ssd_scan",
    )(xbc3, zdt3, zdt3, pre, h0.reshape(batch, SSD_WIDTH, SSD_STATE), conv_w,
      conv_b.reshape(1, XBC_WIDTH), pad_h(dt_bias), pad_h(a_log), dsk,
      ssd_norm_w.reshape(1, SSD_WIDTH), ltri, expand)
    return out.reshape(batch * rows, SSD_WIDTH), conv_new, ssm_new


def _mem_attn_kernel(q_ref, g_ref, k_ref, v_ref, o_ref):
    for h in range(MEM_HEADS):
        cols = slice(HEAD_DIM * h, HEAD_DIM * (h + 1))
        s = _dot_nt(q_ref[:, cols].astype(BF16), k_ref[:, cols].astype(BF16)) * ATTN_SCALE
        p = jnp.exp(s - jnp.max(s, axis=-1, keepdims=True))
        den = jnp.sum(p, axis=-1, keepdims=True)
        o = _dot(p.astype(BF16), v_ref[:, cols].astype(BF16)) / den
        o_ref[:, cols] = (o * _silu(g_ref[:, cols])).astype(o_ref.dtype)


def _mem_attn(q, g, mem_k, mem_v, batch, t, tq, out_dtype):
    nq = t // tq
    return pl.pallas_call(
        _mem_attn_kernel,
        grid=(batch, nq),
        in_specs=[
            pl.BlockSpec((tq, MEM_WIDTH), lambda b, i: (b * nq + i, 0)),
            pl.BlockSpec((tq, MEM_WIDTH), lambda b, i: (b * nq + i, 0)),
            pl.BlockSpec((None, MEM_TOKENS, MEM_WIDTH), lambda b, i: (b, 0, 0)),
            pl.BlockSpec((None, MEM_TOKENS, MEM_WIDTH), lambda b, i: (b, 0, 0)),
        ],
        out_specs=pl.BlockSpec((tq, MEM_WIDTH), lambda b, i: (b * nq + i, 0)),
        out_shape=jax.ShapeDtypeStruct((batch * t, MEM_WIDTH), out_dtype),
        compiler_params=pltpu.CompilerParams(
            dimension_semantics=("parallel", "parallel"), vmem_limit_bytes=VMEM_LIMIT),
        name="mem_attn",
    )(q, g, mem_k, mem_v)


def _mem_attn_rows_kernel(t, q_ref, g_ref, k_ref, v_ref, o_ref):
    q = q_ref[...]
    g = g_ref[...]
    cols = [slice(HEAD_DIM * h, HEAD_DIM * (h + 1)) for h in range(MEM_HEADS)]
    q_all = jnp.concatenate([q[:, c] for c in cols], axis=0).astype(BF16)
    s = _dot_nt(q_all, k_ref[...].astype(BF16)) * ATTN_SCALE
    n_rows = s.shape[1]
    row_head = jnp.concatenate([jnp.full((t, n_rows), h, jnp.int32) for h in range(MEM_HEADS)],
                               axis=0)
    lane_head = lax.broadcasted_iota(jnp.int32, s.shape, 1) & (MEM_HEADS - 1)
    s = jnp.where(row_head == lane_head, s, -jnp.inf)
    p = jnp.exp(s - jnp.max(s, axis=-1, keepdims=True))
    den = jnp.sum(p, axis=-1, keepdims=True)
    o = _dot(p.astype(BF16), v_ref[...].astype(BF16)) / den
    for h in range(MEM_HEADS):
        o_ref[:, cols[h]] = (o[t * h:t * (h + 1), :] * _silu(g[:, cols[h]])).astype(o_ref.dtype)


def _mem_attn_rows(q, g, mem_k, mem_v, batch, t, out_dtype):
    rows = MEM_TOKENS * MEM_HEADS
    kern = functools.partial(_mem_attn_rows_kernel, t)
    return pl.pallas_call(
        kern,
        grid=(batch,),
        in_specs=[
            pl.BlockSpec((t, MEM_WIDTH), lambda b: (b, 0)),
            pl.BlockSpec((t, MEM_WIDTH), lambda b: (b, 0)),
            pl.BlockSpec((None, rows, HEAD_DIM), lambda b: (b, 0, 0)),
            pl.BlockSpec((None, rows, HEAD_DIM), lambda b: (b, 0, 0)),
        ],
        out_specs=pl.BlockSpec((t, MEM_WIDTH), lambda b: (b, 0)),
        out_shape=jax.ShapeDtypeStruct((batch * t, MEM_WIDTH), out_dtype),
        compiler_params=pltpu.CompilerParams(
            dimension_semantics=("parallel",), vmem_limit_bytes=VMEM_LIMIT),
        name="mem_attn_rows",
    )(q, g, mem_k, mem_v)


def _out_proj_kernel(x_ref, sb_ref, ssd_ref, mo_ref, w_ref, o_ref):
    mix = jnp.concatenate([sb_ref[...].astype(BF16), ssd_ref[...].astype(BF16),
                           mo_ref[...].astype(BF16)], axis=-1)
    for c in range(0, o_ref.shape[1], PROJ_CHUNK):
        cols = slice(c, c + PROJ_CHUNK)
        o_ref[:, cols] = x_ref[:, cols] + _dot(mix, w_ref[:, cols].astype(BF16))


def _out_proj(x, sb, ssd, mo, w_out, tm, tn):
    t, d = x.shape
    return pl.pallas_call(
        _out_proj_kernel,
        grid=(t // tm, d // tn),
        in_specs=[
            pl.BlockSpec((tm, tn), lambda m, n: (m, n)),
            pl.BlockSpec((tm, SB_WIDTH), lambda m, n: (m, 0)),
            pl.BlockSpec((tm, SSD_WIDTH), lambda m, n: (m, 0)),
            pl.BlockSpec((tm, MEM_WIDTH), lambda m, n: (m, 0)),
            pl.BlockSpec((w_out.shape[0], tn), lambda m, n: (0, n), pipeline_mode=pl.Buffered(1)),
        ],
        out_specs=pl.BlockSpec((tm, tn), lambda m, n: (m, n)),
        out_shape=jax.ShapeDtypeStruct((t, d), F32),
        compiler_params=pltpu.CompilerParams(
            dimension_semantics=("parallel", "arbitrary"), vmem_limit_bytes=VMEM_LIMIT),
        name="out_proj",
    )(x, sb, ssd, mo, w_out)


def _in_proj_plan(act_dtype):
    lowp = act_dtype == BF16
    outs, plan = [], []

    def add(width, dtype):
        outs.append((width, dtype))
        return len(outs) - 1

    q = add(SB_WIDTH, act_dtype)
    plan.append(((0, SB_WIDTH, 0, None if lowp else q, q if lowp else None),))
    k32 = add(SB_WIDTH, F32)
    k16 = add(SB_WIDTH, BF16) if lowp else None
    plan.append(((0, SB_WIDTH, 1, k32, k16),))
    v32 = add(SB_WIDTH, F32)
    v16 = add(SB_WIDTH, BF16) if lowp else None
    plan.append(((0, SB_WIDTH, None, v32, v16),))
    g = add(SB_WIDTH, F32)
    plan.append(((0, SB_WIDTH, None, g, None),))
    xbc = add(XBC_WIDTH, F32)
    plan.append(((0, XBC_WIDTH, None, xbc, None),))
    zdt = add(ZDT_WIDTH, F32)
    plan.append(((0, ZDT_WIDTH, None, zdt, None),))
    mq = add(MEM_WIDTH, act_dtype)
    mg = add(MEM_WIDTH, F32)
    plan.append(((0, MEM_WIDTH, 2, None if lowp else mq, mq if lowp else None),
                 (MEM_WIDTH, MEM_WIDTH, None, mg, None)))
    names = dict(q=q, k32=k32, k16=k16, v32=v32, v16=v16, g=g, xbc=xbc, zdt=zdt, mq=mq, mg=mg)
    return tuple(plan), outs, names


_O_Z = 4 * SB_WIDTH
_O_XBC = _O_Z + SSD_WIDTH
_O_DT = _O_XBC + XBC_WIDTH
_O_MEM = _O_DT + SSD_HEADS
_IN_WIDTH = _O_MEM + 2 * MEM_WIDTH
_CAT_WIDTH = 7 * PROJ_TN
W_PREP_COLS = 256
BF16_ROWS = 16


def _w_prep_kernel(w_ref, o_ref):
    def put(dst, src, rows):
        o_ref[dst:dst + rows, :] = w_ref[src:src + rows, :].astype(BF16)

    cols = w_ref.shape[1]
    put(0, 0, _O_Z)
    put(_O_Z, _O_XBC, XBC_WIDTH)
    put(_O_Z + XBC_WIDTH, _O_Z, SSD_WIDTH)
    dt0 = _O_Z + XBC_WIDTH + SSD_WIDTH
    o_ref[dt0:dt0 + BF16_ROWS, :] = jnp.concatenate(
        [w_ref[_O_DT:_O_MEM, :], jnp.zeros((BF16_ROWS - SSD_HEADS, cols), F32)], axis=0
    ).astype(BF16)
    o_ref[dt0 + BF16_ROWS:6 * PROJ_TN, :] = jnp.zeros((6 * PROJ_TN - dt0 - BF16_ROWS, cols), BF16)
    put(6 * PROJ_TN, _O_MEM, 2 * MEM_WIDTH)


def _rearranged_w_in(w_t):
    d = w_t.shape[1]
    assert w_t.shape[0] == _IN_WIDTH and d % W_PREP_COLS == 0
    return pl.pallas_call(
        _w_prep_kernel,
        grid=(d // W_PREP_COLS,),
        in_specs=[pl.BlockSpec((_IN_WIDTH, W_PREP_COLS), lambda c: (0, c))],
        out_specs=pl.BlockSpec((_CAT_WIDTH, W_PREP_COLS), lambda c: (0, c)),
        out_shape=jax.ShapeDtypeStruct((_CAT_WIDTH, d), BF16),
        compiler_params=pltpu.CompilerParams(
            dimension_semantics=("parallel",), vmem_limit_bytes=VMEM_LIMIT),
        name="w_prep",
    )(w_t)


def kernel(x_prompt, x_sample, cache_sb_k, cache_sb_v, state_ssm, state_conv, cache_mem_k,
           cache_mem_v, page_table, mem_prompt, norm_w, w_in, sb_q_norm, sb_k_norm, sb_bias,
           conv_w, conv_b, dt_bias, a_log, d_skip, ssd_norm_w, mem_norm_w, w_mem_kv, mem_q_norm,
           mem_k_norm, w_out):
    depth = w_in.shape[0]
    assert depth == 1
    layer = 0
    bp, sp, d = x_prompt.shape
    bs, ts, _ = x_sample.shape
    n_pool = cache_sb_k.shape[1]
    L = SSD_CHUNK

    w_cat = _rearranged_w_in(w_in[layer].T)
    w_o = w_out[layer]
    head_norms = jnp.concatenate(
        [sb_q_norm[layer][None], sb_k_norm[layer][None], mem_q_norm[layer][None],
         mem_k_norm[layer][None], jnp.zeros((SUBLANES - 4, HEAD_DIM), F32)], axis=0)
    ssd_params = (conv_w[layer], conv_b[layer], dt_bias[layer], a_log[layer], d_skip[layer],
                  ssd_norm_w[layer])

    xp = x_prompt.reshape(bp * sp, d)
    mem_plan = (((0, MEM_WIDTH, 3, 0, None), (MEM_WIDTH, MEM_WIDTH, None, 1, None)),)
    (mk, mv), _ = _proj(mem_prompt.reshape(bp * MEM_TOKENS, d), mem_norm_w[layer],
                        w_mem_kv[layer], head_norms, mem_plan,
                        [(MEM_WIDTH, F32), (MEM_WIDTH, F32)], tm=512, w_rows_are_outputs=False)
    plan, outs, nm = _in_proj_plan(BF16)
    xs = x_sample.reshape(bs * ts, d)
    plan_s, outs_s, ns = _in_proj_plan(F32)
    pr, ps = _proj(xp, norm_w[layer], w_cat, head_norms, plan, outs, tm=512,
                   side=(xs, plan_s, outs_s))
    sb, sb_s = _sb_fused(pr[nm['q']], pr[nm['k16']], pr[nm['v16']], pr[nm['g']], bp, sp, 256,
                         ps[ns['q']], ps[ns['k32']], ps[ns['v32']], ps[ns['g']],
                         cache_sb_k, cache_sb_v, layer, page_table, sb_bias[layer], ts)
    ssd, conv_p, ssm_p = _ssd(
        pr[nm['xbc']], pr[nm['zdt']], jnp.zeros((bp, CONV_WIDTH - 1, XBC_WIDTH), F32),
        jnp.zeros((bp, SSD_HEADS, SSD_HEAD_DIM, SSD_STATE), F32), *ssd_params,
        batch=bp, n_chunks=sp // L, length=L, out_dtype=BF16)
    mo = _mem_attn(pr[nm['mq']], pr[nm['mg']], mk.reshape(bp, MEM_TOKENS, MEM_WIDTH),
                   mv.reshape(bp, MEM_TOKENS, MEM_WIDTH), bp, sp, tq=512, out_dtype=BF16)
    yp = _out_proj(xp, sb, ssd, mo, w_o, tm=512, tn=d)

    ssd_s, conv_s, ssm_s = _ssd(
        ps[ns['xbc']], ps[ns['zdt']], state_conv[layer], state_ssm[layer],
        *ssd_params, batch=bs, n_chunks=1, length=ts, out_dtype=F32)
    mo_s = _mem_attn_rows(ps[ns['mq']], ps[ns['mg']],
                          cache_mem_k[layer].reshape(bs, MEM_TOKENS * MEM_HEADS, HEAD_DIM),
                          cache_mem_v[layer].reshape(bs, MEM_TOKENS * MEM_HEADS, HEAD_DIM),
                          bs, ts, out_dtype=F32)
    ys = _out_proj(xs, sb_s, ssd_s, mo_s, w_o, tm=bs * ts, tn=d)

    tail = slice(SUBLANES - (CONV_WIDTH - 1), SUBLANES)
    return (
        yp.reshape(bp, sp, d),
        ys.reshape(bs, ts, d),
        pr[nm['k32']].reshape(1, bp, sp, SB_HEADS, HEAD_DIM),
        pr[nm['v32']].reshape(1, bp, sp, SB_HEADS, HEAD_DIM),
        ssm_p.reshape(1, bp, SSD_HEADS, SSD_HEAD_DIM, SSD_STATE),
        conv_p[:, tail][None],
        mk.reshape(1, bp, MEM_TOKENS, MEM_HEADS, HEAD_DIM),
        mv.reshape(1, bp, MEM_TOKENS, MEM_HEADS, HEAD_DIM),
        ps[ns['k32']].reshape(1, bs, ts, SB_HEADS, HEAD_DIM),
        ps[ns['v32']].reshape(1, bs, ts, SB_HEADS, HEAD_DIM),
        ssm_s.reshape(1, bs, SSD_HEADS, SSD_HEAD_DIM, SSD_STATE),
        conv_s[:, tail][None],
    )
```

```python
import functools
import math

import jax
import jax.numpy as jnp
from jax import lax
from jax.experimental import pallas as pl
from jax.experimental.pallas import tpu as pltpu

F32 = jnp.float32
BF16 = jnp.bfloat16

SB_HEADS = 8
HEAD_DIM = 128
SB_WIDTH = SB_HEADS * HEAD_DIM
SSD_HEADS = 8
SSD_HEAD_DIM = 64
SSD_WIDTH = SSD_HEADS * SSD_HEAD_DIM
SSD_GROUPS = 2
SSD_STATE = 128
CONV_WIDTH = 4
XBC_WIDTH = SSD_WIDTH + 2 * SSD_GROUPS * SSD_STATE
MEM_TOKENS = 256
MEM_HEADS = 4
MEM_WIDTH = MEM_HEADS * HEAD_DIM
PAGE_SIZE = 128
EPS = 1e-6
ATTN_SCALE = HEAD_DIM ** -0.5
LOG2E = math.log2(math.e)

SSD_CHUNK = 128
PROJ_TN = 1024
PROJ_CHUNK = 256
ZDT_WIDTH = SSD_WIDTH + PROJ_CHUNK
SAMPLE_BUFS = 3
SSD_SEQS_PER_STEP = 4
SUBLANES = 8
SB_HEAD_BITS = SB_HEADS.bit_length() - 1
NORM_ROW_CHUNK = 128
PROJ_TM = 512
OUT_TM = 512
ATTN_TQ = 256
MEM_TQ = 512
VMEM_LIMIT = 56 * 1024 * 1024
VMEM_LIMIT_FUSED = 62 * 1024 * 1024

_NT = (((1,), (1,)), ((), ()))


def _dot(a, b):
    return jnp.dot(a, b, preferred_element_type=F32)


def _dot_nt(a, b):
    return lax.dot_general(a, b, _NT, preferred_element_type=F32)


def _split2(x):
    hi = x.astype(BF16)
    lo = (x - hi.astype(F32)).astype(BF16)
    return hi, lo


def _split3(x):
    hi = x.astype(BF16)
    r = x - hi.astype(F32)
    mid = r.astype(BF16)
    lo = (r - mid.astype(F32)).astype(BF16)
    return hi, mid, lo


def _dots_exact_lhs(xs, m):
    rows = xs[0].shape[0]
    r = _dot(jnp.concatenate([t for x in xs for t in _split3(x)], axis=0), m)
    part = lambda i: r[rows * i:rows * (i + 1), :]
    return [part(3 * i) + part(3 * i + 1) + part(3 * i + 2) for i in range(len(xs))]


def _dot_exact_rhs(m, x):
    n = x.shape[1]
    r = _dot(m, jnp.concatenate(_split3(x), axis=1))
    return r[:, :n] + r[:, n:2 * n] + r[:, 2 * n:]


def _silu(x):
    return x * (1.0 / (1.0 + jnp.exp(-x)))


def _rmsnorm_rows(x_ref, nw_ref, h_ref):
    rows_total = x_ref.shape[0]
    rc = min(rows_total, NORM_ROW_CHUNK)

    def body(r, carry):
        rows = pl.ds(pl.multiple_of(r * rc, rc), rc)
        xv = x_ref[rows, :]
        ms = jnp.mean(xv * xv, axis=-1, keepdims=True)
        h_ref[rows, :] = (xv * lax.rsqrt(ms + EPS) * nw_ref[...]).astype(BF16)
        return carry

    lax.fori_loop(0, rows_total // rc, body, 0)


def _proj_step(segs, h_ref, w_ref, hn_ref, outs, w_rows_are_outputs):
    for col0, width, hn_row, o32, o16 in segs:
        for c0 in range(0, width, PROJ_CHUNK):
            chunk = slice(col0 + c0, col0 + c0 + PROJ_CHUNK)
            if w_rows_are_outputs:
                y = _dot_nt(h_ref[...], w_ref[chunk, :].astype(BF16))
            else:
                y = _dot(h_ref[...], w_ref[:, chunk].astype(BF16))
            for c in range(0, PROJ_CHUNK, HEAD_DIM):
                yc = y[:, c:c + HEAD_DIM]
                if hn_row is not None:
                    ms = jnp.mean(yc * yc, axis=-1, keepdims=True)
                    yc = yc * lax.rsqrt(ms + EPS) * hn_ref[hn_row:hn_row + 1, :]
                cols = slice(c0 + c, c0 + c + HEAD_DIM)
                if o32 is not None:
                    outs[o32][:, cols] = yc
                if o16 is not None:
                    outs[o16][:, cols] = yc.astype(BF16)


def _proj_kernel(plan, n_out, side_plan, n_side, w_rows_are_outputs, *refs):
    if side_plan is None:
        x_ref, nw_ref, w_ref, hn_ref = refs[:4]
        rest = refs[4:]
    else:
        x_ref, xs_ref, nw_ref, w_ref, hn_ref = refs[:5]
        rest = refs[5:]
    outs, side_outs = rest[:n_out], rest[n_out:n_out + n_side]
    scratch = rest[n_out + n_side:]
    h_ref = scratch[0]
    m = pl.program_id(0)
    n = pl.program_id(1)

    @pl.when(n == 0)
    def _():
        _rmsnorm_rows(x_ref, nw_ref, h_ref)

    if side_plan is not None:
        hs_ref = scratch[1]

        @pl.when((n == 0) & (m == 0))
        def _():
            _rmsnorm_rows(xs_ref, nw_ref, hs_ref)

    for step, segs in enumerate(plan):
        @pl.when(n == step)
        def _(step=step, segs=segs):
            _proj_step(segs, h_ref, w_ref, hn_ref, outs, w_rows_are_outputs)
            if side_plan is not None:
                @pl.when(m == 0)
                def _():
                    _proj_step(side_plan[step], hs_ref, w_ref, hn_ref, side_outs,
                               w_rows_are_outputs)


def _proj(x, norm_w, w, head_norms, plan, out_defs, tm, side=None, w_rows_are_outputs=True):
    t, d = x.shape
    n_steps = len(plan)
    w_shape = (n_steps * PROJ_TN, d) if w_rows_are_outputs else (d, n_steps * PROJ_TN)
    assert w.shape == w_shape and t % tm == 0
    w_spec = (pl.BlockSpec((PROJ_TN, d), lambda m, n: (n, 0)) if w_rows_are_outputs
              else pl.BlockSpec((d, PROJ_TN), lambda m, n: (0, n)))
    row = lambda m, n: (m, 0)
    const = lambda m, n: (0, 0)
    in_specs = [pl.BlockSpec((tm, d), row)]
    operands = [x]
    out_specs = [pl.BlockSpec((tm, w), row) for w, _ in out_defs]
    out_shape = [jax.ShapeDtypeStruct((t, w), dt) for w, dt in out_defs]
    scratch = [pltpu.VMEM((tm, d), BF16)]
    side_plan, n_side = None, 0
    if side is not None:
        x_side, side_plan, side_defs = side
        ts = x_side.shape[0]
        assert len(side_plan) == n_steps
        n_side = len(side_defs)
        in_specs.append(pl.BlockSpec((ts, d), const))
        operands.append(x_side)
        out_specs += [pl.BlockSpec((ts, w), const) for w, _ in side_defs]
        out_shape += [jax.ShapeDtypeStruct((ts, w), dt) for w, dt in side_defs]
        scratch.append(pltpu.VMEM((ts, d), BF16))
    in_specs += [pl.BlockSpec((1, d), const), w_spec,
                 pl.BlockSpec((SUBLANES, HEAD_DIM), const)]
    operands += [norm_w.reshape(1, d), w, head_norms]
    kern = functools.partial(_proj_kernel, plan, len(out_defs), side_plan, n_side,
                             w_rows_are_outputs)
    res = pl.pallas_call(
        kern,
        grid=(t // tm, n_steps),
        in_specs=in_specs,
        out_specs=out_specs,
        out_shape=out_shape,
        scratch_shapes=scratch,
        compiler_params=pltpu.CompilerParams(
            dimension_semantics=("arbitrary", "arbitrary"), vmem_limit_bytes=VMEM_LIMIT),
        name="norm_proj",
    )(*operands)
    return res[:len(out_defs)], res[len(out_defs):]


def _log2_fail(z2):
    nz = -z2
    return jnp.minimum(nz, 0.0) - jnp.log(1.0 + jnp.exp2(jnp.minimum(z2, nz))) * LOG2E


def _sb_fused_kernel(tq, nq, ppc, n_pages, t_new, page_base,
                     pt_ref, bias_ref, qa_ref, qb_ref, k_ref, v_ref, ga_ref, gb_ref, uu_ref,
                     qs_ref, kn_ref, vn_ref, gs_ref, uo_ref, ck_hbm, cv_hbm,
                     op_ref, os_ref, acc_ref, c_ref, accs_ref, cs_ref, kbuf, vbuf, sem):
    step = pl.program_id(0)
    n_steps = pl.num_programs(0)
    pair = lax.rem(step, nq // 2)
    i1 = pair
    i2 = nq - 1 - pair
    chunks_per_seq = n_pages // ppc
    seq_steps = chunks_per_seq // nq
    part = lax.rem(step, seq_steps)
    n_chunks = n_steps * nq

    def chunk_copies(chunk):
        slot = lax.rem(chunk, SAMPLE_BUFS)
        seq = lax.div(chunk, chunks_per_seq)
        first_pos = (n_pages - 1) - lax.rem(chunk, chunks_per_seq) * ppc
        copies = []
        for j in range(ppc):
            page = page_base + pt_ref[seq * n_pages + first_pos - j]
            copies.append(pltpu.make_async_copy(ck_hbm.at[page], kbuf.at[slot, j], sem.at[slot, 0]))
            copies.append(pltpu.make_async_copy(cv_hbm.at[page], vbuf.at[slot, j], sem.at[slot, 1]))
        return copies

    def start_chunk(chunk):
        for cp in chunk_copies(chunk):
            cp.start()

    def wait_chunk(chunk):
        slot = lax.rem(chunk, SAMPLE_BUFS)
        pltpu.make_async_copy(ck_hbm.at[pl.ds(0, ppc)], kbuf.at[slot], sem.at[slot, 0]).wait()
        pltpu.make_async_copy(cv_hbm.at[pl.ds(0, ppc)], vbuf.at[slot], sem.at[slot, 1]).wait()

    def start_next(chunk):
        @pl.when(chunk + SAMPLE_BUFS < n_chunks)
        def _():
            start_chunk(chunk + SAMPLE_BUFS)

    @pl.when(step == 0)
    def _():
        for c in range(SAMPLE_BUFS):
            start_chunk(jnp.int32(c))

    lanes = PAGE_SIZE * SB_HEADS
    n_blk = lanes // HEAD_DIM
    lane = lax.broadcasted_iota(jnp.int32, (t_new, lanes), 1)
    lane_head = lane & (SB_HEADS - 1)
    biases = [bias_ref[h] * LOG2E for h in range(SB_HEADS)]
    bias_lanes = jnp.full((t_new, lanes), biases[0], F32)
    for h in range(1, SB_HEADS):
        bias_lanes = jnp.where(lane_head == h, biases[h], bias_lanes)
    qs = qs_ref[...]
    q_all = jnp.concatenate([qs[:, HEAD_DIM * h:HEAD_DIM * (h + 1)] for h in range(SB_HEADS)],
                            axis=0).astype(BF16)

    def page_scores(kpages, n, mask):
        s_cat = _dot_nt(q_all, kpages.astype(BF16))
        return [scores(s_cat[:, lanes * p:lanes * (p + 1)], mask) for p in range(n)]

    def scores(s_all, mask):
        sc = s_all[0:t_new, :]
        for h in range(1, SB_HEADS):
            sc = jnp.where(lane_head == h, s_all[t_new * h:t_new * (h + 1), :], sc)
        z2 = sc * (ATTN_SCALE * LOG2E) + bias_lanes
        lf = _log2_fail(z2)
        if mask is not None:
            lf = jnp.where(mask, lf, 0.0)
        blocks = jnp.concatenate([lf[:, HEAD_DIM * j:HEAD_DIM * (j + 1)] for j in range(n_blk)],
                                 axis=0)
        hi, lo = _split2(blocks)
        return z2, jnp.concatenate([hi, lo], axis=1)

    def stacked_sums(operands, matrix):
        rows = operands[0].shape[0]
        res = _dot(jnp.concatenate(operands, axis=0), matrix)
        return [res[rows * i:rows * (i + 1), :] for i in range(len(operands))]

    def weights(z2, res, mask, run):
        ws = [None] * n_blk
        for j in reversed(range(n_blk)):
            rows = slice(t_new * j, t_new * (j + 1))
            logw = z2[:, HEAD_DIM * j:HEAD_DIM * (j + 1)] + res[rows, :HEAD_DIM]
            if run is not None:
                logw = logw + run
            ws[j] = jnp.exp2(logw)
            tot = res[rows, HEAD_DIM:]
            run = tot if run is None else run + tot
        w = jnp.concatenate(ws, axis=1)
        if mask is not None:
            w = jnp.where(mask, w, 0.0)
        w_all = jnp.concatenate([jnp.where(lane_head == h, w, 0.0) for h in range(SB_HEADS)],
                                axis=0).astype(BF16)
        return w_all, run

    def new_keys():
        mask = (lane >> SB_HEAD_BITS) < lax.broadcasted_iota(jnp.int32, (t_new, lanes), 0)
        (z2, hilo), = page_scores(kn_ref[...], 1, mask)
        res, = stacked_sums([hilo], uo_ref[...])
        w_all, run = weights(z2, res, mask, None)
        cs_ref[...] = run
        accs_ref[...] = _dot(w_all, vn_ref[...].astype(BF16))

    uu = uu_ref[...]
    cols = [slice(HEAD_DIM * h, HEAD_DIM * (h + 1)) for h in range(SB_HEADS)]

    def section(q_ref, start, diag, first, chunk):
        kb = k_ref[pl.ds(start, tq), :]
        vb = v_ref[pl.ds(start, tq), :]
        n_pages_here = 0 if chunk is None else ppc
        slot = None if chunk is None else lax.rem(chunk, SAMPLE_BUFS)
        if n_pages_here:
            run = cs_ref[...]
            acc = accs_ref[...]
        p_parts, s_parts = {}, {}
        n_idx = max(SB_HEADS, n_pages_here)

        raw = [_dot_nt(q_ref[:, cols[h]], kb[:, cols[h]]) for h in range(SB_HEADS)]
        if n_pages_here:
            s_split = page_scores(kbuf[slot].reshape(ppc * lanes, HEAD_DIM), ppc, None)
        split = []
        for h in range(SB_HEADS):
            z2 = raw[h] * (ATTN_SCALE * LOG2E) + biases[h]
            lf = _log2_fail(z2)
            if diag is not None:
                lf = jnp.where(diag, lf, 0.0)
            hi, lo = _split2(lf)
            split.append((z2, jnp.concatenate([hi, lo], axis=1)))
        incl = stacked_sums([hilo for _, hilo in split], uu)
        for h in range(SB_HEADS):
            p_parts[h] = (split[h][0], incl[h])
        if n_pages_here:
            res = stacked_sums([hilo for _, hilo in s_split], uo_ref[...])
            for idx in range(n_pages_here):
                s_parts[idx] = (s_split[idx][0], res[idx])
        w_pages = []
        for idx in range(n_idx):
            if idx < SB_HEADS:
                z2, incl_h = p_parts[idx]
                logw = z2 + incl_h
                if not first:
                    logw = logw + c_ref[idx]
                w = jnp.exp2(logw)
                if diag is not None:
                    w = jnp.where(diag, w, 0.0)
                pv = _dot(w.astype(BF16), vb[:, cols[idx]])
                total = incl_h[:, 0:1]
                if first:
                    acc_ref[:, cols[idx]] = pv
                    c_ref[idx] = total
                else:
                    acc_ref[:, cols[idx]] += pv
                    c_ref[idx] += total
            if idx < n_pages_here:
                w_all, run = weights(*s_parts[idx], None, run)
                w_pages.append(w_all)
        if n_pages_here:
            cs_ref[...] = run
            accs_ref[...] = acc + _dot(jnp.concatenate(w_pages, axis=1),
                                       vbuf[slot].reshape(ppc * lanes, HEAD_DIM).astype(BF16))

    def finish(i, g_ref):
        rows = pl.ds(pl.multiple_of(i * tq, tq), tq)
        op_ref[rows, :] = (acc_ref[...] * _silu(g_ref[...])).astype(op_ref.dtype)

    row = lax.broadcasted_iota(jnp.int32, (tq, tq), 0)
    col = lax.broadcasted_iota(jnp.int32, (tq, tq), 1)
    diag = col < row
    chunk0 = step * nq

    @pl.when(part == 0)
    def _():
        new_keys()

    section(qa_ref, pl.multiple_of(i1 * tq, tq), diag, True, None)

    def body_a(t, carry):
        chunk = chunk0 + t
        wait_chunk(chunk)
        section(qa_ref, pl.multiple_of((i1 - 1 - t) * tq, tq), None, False, chunk)
        start_next(chunk)
        return carry

    lax.fori_loop(0, i1, body_a, 0)
    finish(i1, ga_ref)

    chunk = chunk0 + i1
    wait_chunk(chunk)
    section(qb_ref, pl.multiple_of(i2 * tq, tq), diag, True, chunk)
    start_next(chunk)

    def body_b(t, carry):
        chunk = chunk0 + i1 + 1 + t
        wait_chunk(chunk)
        section(qb_ref, pl.multiple_of((i2 - 1 - t) * tq, tq), None, False, chunk)
        start_next(chunk)
        return carry

    lax.fori_loop(0, i2, body_b, 0)
    finish(i2, gb_ref)

    @pl.when(part == seq_steps - 1)
    def _():
        g = gs_ref[...]
        for h in range(SB_HEADS):
            os_ref[:, cols[h]] = accs_ref[t_new * h:t_new * (h + 1), :] * _silu(g[:, cols[h]])


def _sb_fused(q, k, v, g, batch, seq, tq, q_s, k_new, v_new, g_s, cache_k, cache_v, layer,
              page_table, sb_bias, t_new):
    n_seq, n_pages = page_table.shape
    n_pool = cache_k.shape[1]
    nq = seq // tq
    n_steps = batch * (nq // 2)
    assert nq % 2 == 0 and (n_seq * n_pages) % (n_steps * nq) == 0
    ppc = n_seq * n_pages // (n_steps * nq)
    assert n_pages % (ppc * nq) == 0
    seq_steps = n_pages // (ppc * nq)
    page_rows = PAGE_SIZE * SB_HEADS

    q3, k3, v3, g3 = (a.reshape(batch, seq, SB_WIDTH) for a in (q, k, v, g))
    u = (jnp.arange(tq)[:, None] >= jnp.arange(tq)[None, :]).astype(BF16)
    uu = jnp.concatenate([u, u], axis=0)

    def as_page(a):
        a = a.reshape(n_seq, t_new, SB_HEADS, HEAD_DIM)
        a = jnp.pad(a, ((0, 0), (0, PAGE_SIZE - t_new), (0, 0), (0, 0)))
        return a.reshape(n_seq * page_rows, HEAD_DIM)

    cache_k = cache_k.reshape(-1, page_rows, HEAD_DIM)
    cache_v = cache_v.reshape(-1, page_rows, HEAD_DIM)
    r = jnp.arange(HEAD_DIM)
    same_head = (r[:, None] % SB_HEADS) == (r[None, :] % SB_HEADS)
    not_earlier = (r[:, None] // SB_HEADS) >= (r[None, :] // SB_HEADS)
    uo = jnp.concatenate([same_head & not_earlier, same_head], axis=1).astype(BF16)
    uo = jnp.concatenate([uo, uo], axis=0)

    half = nq // 2
    b_of = lambda s: s // half
    qa_map = lambda s, pt: (b_of(s), s % half, 0)
    qb_map = lambda s, pt: (b_of(s), nq - 1 - s % half, 0)
    seq_map = lambda s, pt: (b_of(s), 0, 0)
    samp_map = lambda s, pt: (s // seq_steps, 0)
    const = lambda s, pt: (0, 0)
    kern = functools.partial(_sb_fused_kernel, tq, nq, ppc, n_pages, t_new, layer * n_pool)
    grid_spec = pltpu.PrefetchScalarGridSpec(
        num_scalar_prefetch=1,
        grid=(n_steps,),
        in_specs=[
            pl.BlockSpec(memory_space=pltpu.SMEM),
            pl.BlockSpec((None, tq, SB_WIDTH), qa_map),
            pl.BlockSpec((None, tq, SB_WIDTH), qb_map),
            pl.BlockSpec((None, seq, SB_WIDTH), seq_map, pipeline_mode=pl.Buffered(1)),
            pl.BlockSpec((None, seq, SB_WIDTH), seq_map, pipeline_mode=pl.Buffered(1)),
            pl.BlockSpec((None, tq, SB_WIDTH), qa_map),
            pl.BlockSpec((None, tq, SB_WIDTH), qb_map),
            pl.BlockSpec((2 * tq, tq), const),
            pl.BlockSpec((t_new, SB_WIDTH), samp_map),
            pl.BlockSpec((page_rows, HEAD_DIM), samp_map),
            pl.BlockSpec((page_rows, HEAD_DIM), samp_map),
            pl.BlockSpec((t_new, SB_WIDTH), samp_map),
            pl.BlockSpec((2 * HEAD_DIM, 2 * HEAD_DIM), const),
            pl.BlockSpec(memory_space=pl.ANY),
            pl.BlockSpec(memory_space=pl.ANY),
        ],
        out_specs=[
            pl.BlockSpec((None, seq, SB_WIDTH), seq_map, pipeline_mode=pl.Buffered(1)),
            pl.BlockSpec((t_new, SB_WIDTH), samp_map),
        ],
        scratch_shapes=[
            pltpu.VMEM((tq, SB_WIDTH), F32),
            pltpu.VMEM((SB_HEADS, tq, 1), F32),
            pltpu.VMEM((SB_HEADS * t_new, HEAD_DIM), F32),
            pltpu.VMEM((t_new, HEAD_DIM), F32),
            pltpu.VMEM((SAMPLE_BUFS, ppc, page_rows, HEAD_DIM), F32),
            pltpu.VMEM((SAMPLE_BUFS, ppc, page_rows, HEAD_DIM), F32),
            pltpu.SemaphoreType.DMA((SAMPLE_BUFS, 2)),
        ],
    )
    out_p, out_s = pl.pallas_call(
        kern,
        grid_spec=grid_spec,
        out_shape=[jax.ShapeDtypeStruct((batch, seq, SB_WIDTH), BF16),
                   jax.ShapeDtypeStruct((n_seq * t_new, SB_WIDTH), F32)],
        compiler_params=pltpu.CompilerParams(
            dimension_semantics=("arbitrary",), vmem_limit_bytes=VMEM_LIMIT_FUSED),
        name="sb_fused",
    )(page_table.reshape(-1), sb_bias, q3, q3, k3, v3, g3, g3, uu,
      q_s, as_page(k_new), as_page(v_new), g_s, uo, cache_k, cache_v)
    return out_p.reshape(batch * seq, SB_WIDTH), out_s


def _ssd_kernel(length, n_par, *refs):
    per_seq_in, shared, per_seq_out = refs[:5], refs[5:13], refs[13:]
    pre_ref, h0_ref = per_seq_in[3:5]
    ext_ref, st_ref = per_seq_out[3:5]

    @pl.when(pl.program_id(1) == 0)
    def _():
        ext_ref[:, 0:SUBLANES, :] = pre_ref[...]
        st_ref[...] = h0_ref[...]

    for s in range(n_par):
        _ssd_chunk(length, *[r.at[s] for r in per_seq_in[:3]], *shared,
                   *[r.at[s] for r in per_seq_out])


def _pad_rows(a, rows):
    if a.shape[0] == rows:
        return a
    return jnp.concatenate([a, jnp.zeros((rows - a.shape[0], a.shape[1]), a.dtype)], axis=0)


def _ssd_chunk(length, xbc_ref, z_ref, dt_ref, cw_ref, cb_ref, dtb_ref,
               alog_ref, dsk_ref, nw_ref, ltri_ref, e_ref, out_ref, cnew_ref, snew_ref,
               ext_ref, st_ref):
    L = SSD_CHUNK
    P = SSD_HEAD_DIM

    ext_ref[SUBLANES:SUBLANES + L, :] = _pad_rows(xbc_ref[...], L)
    cw = cw_ref[...]
    conv = cb_ref[...]
    for j in range(CONV_WIDTH):
        off = SUBLANES - (CONV_WIDTH - 1) + j
        conv = conv + ext_ref[off:off + L, :] * cw[j:j + 1, :]
    act = _silu(conv)
    tail = ext_ref[length:length + SUBLANES, :]
    cnew_ref[...] = tail
    ext_ref[0:SUBLANES, :] = tail

    xs = act[:, :SSD_WIDTH]
    bm = act[:, SSD_WIDTH:SSD_WIDTH + SSD_GROUPS * SSD_STATE]
    cm = act[:, SSD_WIDTH + SSD_GROUPS * SSD_STATE:]

    x_dt = _pad_rows(dt_ref[...], L) + dtb_ref[...]
    dt = jnp.maximum(x_dt, 0.0) + jnp.log1p(jnp.exp(-jnp.abs(x_dt)))
    if length < L:
        valid = lax.broadcasted_iota(jnp.int32, dt.shape, 0) < length
        dt = jnp.where(valid, dt, 0.0)
    da = dt * (-jnp.exp(alog_ref[...]))
    cs = _dot_exact_rhs(ltri_ref[...], da)
    cs_t = cs.T
    e = e_ref[...]
    dt_x, cs_x = _dots_exact_lhs([dt, cs], e)
    xdt = xs * dt_x
    ecs = jnp.exp(cs_x)
    xw_t = (xdt * jnp.exp(cs_x[L - 1:L, :] - cs_x)).T
    xdt16 = xdt.astype(BF16)

    row = lax.broadcasted_iota(jnp.int32, (L, L), 0)
    col = lax.broadcasted_iota(jnp.int32, (L, L), 1)
    causal = col <= row
    heads_per_group = SSD_HEADS // SSD_GROUPS
    gw = heads_per_group * P
    y_diag, y_off = [], []
    for g in range(SSD_GROUPS):
        bg = bm[:, SSD_STATE * g:SSD_STATE * (g + 1)].astype(BF16)
        cg = cm[:, SSD_STATE * g:SSD_STATE * (g + 1)].astype(BF16)
        cb = _dot_nt(cg, bg)
        prev = st_ref[gw * g:gw * (g + 1), :]
        y_off.append(_dot_nt(cg, prev.astype(BF16)))
        new = _dot(xw_t[gw * g:gw * (g + 1), :].astype(BF16), bg)
        for r in range(heads_per_group):
            h = heads_per_group * g + r
            seg = cs[:, h:h + 1] - cs_t[h:h + 1, :]
            decay = jnp.exp(jnp.where(causal, seg, -jnp.inf))
            y_diag.append(_dot((cb * decay).astype(BF16), xdt16[:, P * h:P * (h + 1)]))
            chunk_decay = jnp.exp(cs[L - 1:L, h:h + 1])
            st_ref[P * h:P * (h + 1), :] = (prev[P * r:P * (r + 1), :] * chunk_decay
                                            + new[P * r:P * (r + 1), :])
    snew_ref[...] = st_ref[...]
    y = (jnp.concatenate(y_diag, axis=1) + jnp.concatenate(y_off, axis=1) * ecs
         + xs * dsk_ref[...])
    gated = y[:length] * _silu(z_ref[...])
    ms = jnp.mean(gated * gated, axis=-1, keepdims=True)
    out_ref[...] = (gated * lax.rsqrt(ms + EPS) * nw_ref[...]).astype(out_ref.dtype)


def _ssd(xbc, zdt, prefix, h0, conv_w, conv_b, dt_bias, a_log, d_skip, ssd_norm_w,
         batch, n_chunks, length, out_dtype):
    L = SSD_CHUNK
    pre = jnp.pad(prefix, ((0, 0), (SUBLANES - (CONV_WIDTH - 1), 0), (0, 0)))
    pad_h = lambda a: jnp.pad(a, (0, HEAD_DIM - SSD_HEADS)).reshape(1, HEAD_DIM)
    ltri = (jnp.arange(L)[:, None] >= jnp.arange(L)[None, :]).astype(BF16)
    expand = (jnp.arange(HEAD_DIM)[:, None] == jnp.arange(SSD_WIDTH)[None, :] // SSD_HEAD_DIM
              ).astype(BF16)
    dsk = jnp.repeat(d_skip, SSD_HEAD_DIM).reshape(1, SSD_WIDTH)
    z_blk = SSD_WIDTH // HEAD_DIM
    n_par = SSD_SEQS_PER_STEP
    assert batch % n_par == 0 and length <= L and (length == L or n_chunks == 1)
    rows = n_chunks * length
    xbc3 = xbc.reshape(batch, rows, XBC_WIDTH)
    zdt3 = zdt.reshape(batch, rows, ZDT_WIDTH)
    const = lambda b, c: (0, 0)
    seq = lambda b, c: (b, 0, 0)
    kern = functools.partial(_ssd_kernel, length, n_par)
    out, conv_new, ssm_new = pl.pallas_call(
        kern,
        grid=(batch // n_par, n_chunks),
        in_specs=[
            pl.BlockSpec((n_par, length, XBC_WIDTH), lambda b, c: (b, c, 0)),
            pl.BlockSpec((n_par, length, SSD_WIDTH), lambda b, c: (b, c, 0)),
            pl.BlockSpec((n_par, length, HEAD_DIM), lambda b, c: (b, c, z_blk)),
            pl.BlockSpec((n_par, SUBLANES, XBC_WIDTH), seq),
            pl.BlockSpec((n_par, SSD_WIDTH, SSD_STATE), seq),
            pl.BlockSpec((CONV_WIDTH, XBC_WIDTH), const),
            pl.BlockSpec((1, XBC_WIDTH), const),
            pl.BlockSpec((1, HEAD_DIM), const),
            pl.BlockSpec((1, HEAD_DIM), const),
            pl.BlockSpec((1, SSD_WIDTH), const),
            pl.BlockSpec((1, SSD_WIDTH), const),
            pl.BlockSpec((L, L), const),
            pl.BlockSpec((HEAD_DIM, SSD_WIDTH), const),
        ],
        out_specs=[
            pl.BlockSpec((n_par, length, SSD_WIDTH), lambda b, c: (b, c, 0)),
            pl.BlockSpec((n_par, SUBLANES, XBC_WIDTH), seq),
            pl.BlockSpec((n_par, SSD_WIDTH, SSD_STATE), seq),
        ],
        out_shape=[
            jax.ShapeDtypeStruct((batch, rows, SSD_WIDTH), out_dtype),
            jax.ShapeDtypeStruct((batch, SUBLANES, XBC_WIDTH), F32),
            jax.ShapeDtypeStruct((batch, SSD_WIDTH, SSD_STATE), F32),
        ],
        scratch_shapes=[pltpu.VMEM((n_par, SUBLANES + L, XBC_WIDTH), F32),
                        pltpu.VMEM((n_par, SSD_WIDTH, SSD_STATE), F32)],
        compiler_params=pltpu.CompilerParams(
            dimension_semantics=("parallel", "arbitrary"), vmem_limit_bytes=VMEM_LIMIT),
        name="ssd_scan",
    )(xbc3, zdt3, zdt3, pre, h0.reshape(batch, SSD_WIDTH, SSD_STATE), conv_w,
      conv_b.reshape(1, XBC_WIDTH), pad_h(dt_bias), pad_h(a_log), dsk,
      ssd_norm_w.reshape(1, SSD_WIDTH), ltri, expand)
    return out.reshape(batch * rows, SSD_WIDTH), conv_new, ssm_new


def _mem_attn_kernel(q_ref, g_ref, k_ref, v_ref, o_ref):
    for h in range(MEM_HEADS):
        cols = slice(HEAD_DIM * h, HEAD_DIM * (h + 1))
        s = _dot_nt(q_ref[:, cols].astype(BF16), k_ref[:, cols].astype(BF16)) * ATTN_SCALE
        p = jnp.exp(s - jnp.max(s, axis=-1, keepdims=True))
        den = jnp.sum(p, axis=-1, keepdims=True)
        o = _dot(p.astype(BF16), v_ref[:, cols].astype(BF16)) / den
        o_ref[:, cols] = (o * _silu(g_ref[:, cols])).astype(o_ref.dtype)


def _mem_attn(q, g, mem_k, mem_v, batch, t, tq, out_dtype):
    nq = t // tq
    return pl.pallas_call(
        _mem_attn_kernel,
        grid=(batch, nq),
        in_specs=[
            pl.BlockSpec((tq, MEM_WIDTH), lambda b, i: (b * nq + i, 0)),
            pl.BlockSpec((tq, MEM_WIDTH), lambda b, i: (b * nq + i, 0)),
            pl.BlockSpec((None, MEM_TOKENS, MEM_WIDTH), lambda b, i: (b, 0, 0)),
            pl.BlockSpec((None, MEM_TOKENS, MEM_WIDTH), lambda b, i: (b, 0, 0)),
        ],
        out_specs=pl.BlockSpec((tq, MEM_WIDTH), lambda b, i: (b * nq + i, 0)),
        out_shape=jax.ShapeDtypeStruct((batch * t, MEM_WIDTH), out_dtype),
        compiler_params=pltpu.CompilerParams(
            dimension_semantics=("parallel", "parallel"), vmem_limit_bytes=VMEM_LIMIT),
        name="mem_attn",
    )(q, g, mem_k, mem_v)


def _mem_attn_rows_kernel(t, q_ref, g_ref, k_ref, v_ref, o_ref):
    q = q_ref[...]
    g = g_ref[...]
    cols = [slice(HEAD_DIM * h, HEAD_DIM * (h + 1)) for h in range(MEM_HEADS)]
    q_all = jnp.concatenate([q[:, c] for c in cols], axis=0).astype(BF16)
    s = _dot_nt(q_all, k_ref[...].astype(BF16)) * ATTN_SCALE
    n_rows = s.shape[1]
    row_head = jnp.concatenate([jnp.full((t, n_rows), h, jnp.int32) for h in range(MEM_HEADS)],
                               axis=0)
    lane_head = lax.broadcasted_iota(jnp.int32, s.shape, 1) & (MEM_HEADS - 1)
    s = jnp.where(row_head == lane_head, s, -jnp.inf)
    p = jnp.exp(s - jnp.max(s, axis=-1, keepdims=True))
    den = jnp.sum(p, axis=-1, keepdims=True)
    o = _dot(p.astype(BF16), v_ref[...].astype(BF16)) / den
    for h in range(MEM_HEADS):
        o_ref[:, cols[h]] = (o[t * h:t * (h + 1), :] * _silu(g[:, cols[h]])).astype(o_ref.dtype)


def _mem_attn_rows(q, g, mem_k, mem_v, batch, t, out_dtype):
    rows = MEM_TOKENS * MEM_HEADS
    kern = functools.partial(_mem_attn_rows_kernel, t)
    return pl.pallas_call(
        kern,
        grid=(batch,),
        in_specs=[
            pl.BlockSpec((t, MEM_WIDTH), lambda b: (b, 0)),
            pl.BlockSpec((t, MEM_WIDTH), lambda b: (b, 0)),
            pl.BlockSpec((None, rows, HEAD_DIM), lambda b: (b, 0, 0)),
            pl.BlockSpec((None, rows, HEAD_DIM), lambda b: (b, 0, 0)),
        ],
        out_specs=pl.BlockSpec((t, MEM_WIDTH), lambda b: (b, 0)),
        out_shape=jax.ShapeDtypeStruct((batch * t, MEM_WIDTH), out_dtype),
        compiler_params=pltpu.CompilerParams(
            dimension_semantics=("parallel",), vmem_limit_bytes=VMEM_LIMIT),
        name="mem_attn_rows",
    )(q, g, mem_k, mem_v)


def _out_proj_kernel(x_ref, sb_ref, ssd_ref, mo_ref, w_ref, o_ref):
    mix = jnp.concatenate([sb_ref[...].astype(BF16), ssd_ref[...].astype(BF16),
                           mo_ref[...].astype(BF16)], axis=-1)
    for c in range(0, o_ref.shape[1], PROJ_CHUNK):
        cols = slice(c, c + PROJ_CHUNK)
        o_ref[:, cols] = x_ref[:, cols] + _dot(mix, w_ref[:, cols].astype(BF16))


def _out_proj(x, sb, ssd, mo, w_out, tm, tn):
    t, d = x.shape
    return pl.pallas_call(
        _out_proj_kernel,
        grid=(t // tm, d // tn),
        in_specs=[
            pl.BlockSpec((tm, tn), lambda m, n: (m, n)),
            pl.BlockSpec((tm, SB_WIDTH), lambda m, n: (m, 0)),
            pl.BlockSpec((tm, SSD_WIDTH), lambda m, n: (m, 0)),
            pl.BlockSpec((tm, MEM_WIDTH), lambda m, n: (m, 0)),
            pl.BlockSpec((w_out.shape[0], tn), lambda m, n: (0, n), pipeline_mode=pl.Buffered(1)),
        ],
        out_specs=pl.BlockSpec((tm, tn), lambda m, n: (m, n)),
        out_shape=jax.ShapeDtypeStruct((t, d), F32),
        compiler_params=pltpu.CompilerParams(
            dimension_semantics=("parallel", "arbitrary"), vmem_limit_bytes=VMEM_LIMIT),
        name="out_proj",
    )(x, sb, ssd, mo, w_out)


def _in_proj_plan(act_dtype):
    lowp = act_dtype == BF16
    outs, plan = [], []

    def add(width, dtype):
        outs.append((width, dtype))
        return len(outs) - 1

    q = add(SB_WIDTH, act_dtype)
    plan.append(((0, SB_WIDTH, 0, None if lowp else q, q if lowp else None),))
    k32 = add(SB_WIDTH, F32)
    k16 = add(SB_WIDTH, BF16) if lowp else None
    plan.append(((0, SB_WIDTH, 1, k32, k16),))
    v32 = add(SB_WIDTH, F32)
    v16 = add(SB_WIDTH, BF16) if lowp else None
    plan.append(((0, SB_WIDTH, None, v32, v16),))
    g = add(SB_WIDTH, F32)
    plan.append(((0, SB_WIDTH, None, g, None),))
    xbc = add(XBC_WIDTH, F32)
    plan.append(((0, XBC_WIDTH, None, xbc, None),))
    zdt = add(ZDT_WIDTH, F32)
    plan.append(((0, ZDT_WIDTH, None, zdt, None),))
    mq = add(MEM_WIDTH, act_dtype)
    mg = add(MEM_WIDTH, F32)
    plan.append(((0, MEM_WIDTH, 2, None if lowp else mq, mq if lowp else None),
                 (MEM_WIDTH, MEM_WIDTH, None, mg, None)))
    names = dict(q=q, k32=k32, k16=k16, v32=v32, v16=v16, g=g, xbc=xbc, zdt=zdt, mq=mq, mg=mg)
    return tuple(plan), outs, names


_O_Z = 4 * SB_WIDTH
_O_XBC = _O_Z + SSD_WIDTH
_O_DT = _O_XBC + XBC_WIDTH
_O_MEM = _O_DT + SSD_HEADS
_IN_WIDTH = _O_MEM + 2 * MEM_WIDTH
_CAT_WIDTH = 7 * PROJ_TN
W_PREP_COLS = 256
BF16_ROWS = 16


def _w_prep_kernel(w_ref, o_ref):
    def put(dst, src, rows):
        o_ref[dst:dst + rows, :] = w_ref[src:src + rows, :].astype(BF16)

    cols = w_ref.shape[1]
    put(0, 0, _O_Z)
    put(_O_Z, _O_XBC, XBC_WIDTH)
    put(_O_Z + XBC_WIDTH, _O_Z, SSD_WIDTH)
    dt0 = _O_Z + XBC_WIDTH + SSD_WIDTH
    o_ref[dt0:dt0 + BF16_ROWS, :] = jnp.concatenate(
        [w_ref[_O_DT:_O_MEM, :], jnp.zeros((BF16_ROWS - SSD_HEADS, cols), F32)], axis=0
    ).astype(BF16)
    o_ref[dt0 + BF16_ROWS:6 * PROJ_TN, :] = jnp.zeros((6 * PROJ_TN - dt0 - BF16_ROWS, cols), BF16)
    put(6 * PROJ_TN, _O_MEM, 2 * MEM_WIDTH)


def _rearranged_w_in(w_t):
    d = w_t.shape[1]
    assert w_t.shape[0] == _IN_WIDTH and d % W_PREP_COLS == 0
    return pl.pallas_call(
        _w_prep_kernel,
        grid=(d // W_PREP_COLS,),
        in_specs=[pl.BlockSpec((_IN_WIDTH, W_PREP_COLS), lambda c: (0, c))],
        out_specs=pl.BlockSpec((_CAT_WIDTH, W_PREP_COLS), lambda c: (0, c)),
        out_shape=jax.ShapeDtypeStruct((_CAT_WIDTH, d), BF16),
        compiler_params=pltpu.CompilerParams(
            dimension_semantics=("parallel",), vmem_limit_bytes=VMEM_LIMIT),
        name="w_prep",
    )(w_t)


def kernel(x_prompt, x_sample, cache_sb_k, cache_sb_v, state_ssm, state_conv, cache_mem_k,
           cache_mem_v, page_table, mem_prompt, norm_w, w_in, sb_q_norm, sb_k_norm, sb_bias,
           conv_w, conv_b, dt_bias, a_log, d_skip, ssd_norm_w, mem_norm_w, w_mem_kv, mem_q_norm,
           mem_k_norm, w_out):
    depth = w_in.shape[0]
    assert depth == 1
    layer = 0
    bp, sp, d = x_prompt.shape
    bs, ts, _ = x_sample.shape
    n_pool = cache_sb_k.shape[1]
    L = SSD_CHUNK

    w_cat = _rearranged_w_in(w_in[layer].T)
    w_o = w_out[layer]
    head_norms = jnp.concatenate(
        [sb_q_norm[layer][None], sb_k_norm[layer][None], mem_q_norm[layer][None],
         mem_k_norm[layer][None], jnp.zeros((SUBLANES - 4, HEAD_DIM), F32)], axis=0)
    ssd_params = (conv_w[layer], conv_b[layer], dt_bias[layer], a_log[layer], d_skip[layer],
                  ssd_norm_w[layer])

    xp = x_prompt.reshape(bp * sp, d)
    mem_plan = (((0, MEM_WIDTH, 3, 0, None), (MEM_WIDTH, MEM_WIDTH, None, 1, None)),)
    (mk, mv), _ = _proj(mem_prompt.reshape(bp * MEM_TOKENS, d), mem_norm_w[layer],
                        w_mem_kv[layer], head_norms, mem_plan,
                        [(MEM_WIDTH, F32), (MEM_WIDTH, F32)], tm=PROJ_TM,
                        w_rows_are_outputs=False)
    plan, outs, nm = _in_proj_plan(BF16)
    xs = x_sample.reshape(bs * ts, d)
    plan_s, outs_s, ns = _in_proj_plan(F32)
    pr, ps = _proj(xp, norm_w[layer], w_cat, head_norms, plan, outs, tm=PROJ_TM,
                   side=(xs, plan_s, outs_s))
    sb, sb_s = _sb_fused(pr[nm['q']], pr[nm['k16']], pr[nm['v16']], pr[nm['g']], bp, sp, ATTN_TQ,
                         ps[ns['q']], ps[ns['k32']], ps[ns['v32']], ps[ns['g']],
                         cache_sb_k, cache_sb_v, layer, page_table, sb_bias[layer], ts)
    ssd, conv_p, ssm_p = _ssd(
        pr[nm['xbc']], pr[nm['zdt']], jnp.zeros((bp, CONV_WIDTH - 1, XBC_WIDTH), F32),
        jnp.zeros((bp, SSD_HEADS, SSD_HEAD_DIM, SSD_STATE), F32), *ssd_params,
        batch=bp, n_chunks=sp // L, length=L, out_dtype=BF16)
    mo = _mem_attn(pr[nm['mq']], pr[nm['mg']], mk.reshape(bp, MEM_TOKENS, MEM_WIDTH),
                   mv.reshape(bp, MEM_TOKENS, MEM_WIDTH), bp, sp, tq=MEM_TQ, out_dtype=BF16)
    yp = _out_proj(xp, sb, ssd, mo, w_o, tm=OUT_TM, tn=d)

    ssd_s, conv_s, ssm_s = _ssd(
        ps[ns['xbc']], ps[ns['zdt']], state_conv[layer], state_ssm[layer],
        *ssd_params, batch=bs, n_chunks=1, length=ts, out_dtype=F32)
    mo_s = _mem_attn_rows(ps[ns['mq']], ps[ns['mg']],
                          cache_mem_k[layer].reshape(bs, MEM_TOKENS * MEM_HEADS, HEAD_DIM),
                          cache_mem_v[layer].reshape(bs, MEM_TOKENS * MEM_HEADS, HEAD_DIM),
                          bs, ts, out_dtype=F32)
    ys = _out_proj(xs, sb_s, ssd_s, mo_s, w_o, tm=bs * ts, tn=d)

    tail = slice(SUBLANES - (CONV_WIDTH - 1), SUBLANES)
    return (
        yp.reshape(bp, sp, d),
        ys.reshape(bs, ts, d),
        pr[nm['k32']].reshape(1, bp, sp, SB_HEADS, HEAD_DIM),
        pr[nm['v32']].reshape(1, bp, sp, SB_HEADS, HEAD_DIM),
        ssm_p.reshape(1, bp, SSD_HEADS, SSD_HEAD_DIM, SSD_STATE),
        conv_p[:, tail][None],
        mk.reshape(1, bp, MEM_TOKENS, MEM_HEADS, HEAD_DIM),
        mv.reshape(1, bp, MEM_TOKENS, MEM_HEADS, HEAD_DIM),
        ps[ns['k32']].reshape(1, bs, ts, SB_HEADS, HEAD_DIM),
        ps[ns['v32']].reshape(1, bs, ts, SB_HEADS, HEAD_DIM),
        ssm_s.reshape(1, bs, SSD_HEADS, SSD_HEAD_DIM, SSD_STATE),
        conv_s[:, tail][None],
    )
```

```python
import functools
import math

import jax
import jax.numpy as jnp
from jax import lax
from jax.experimental import pallas as pl
from jax.experimental.pallas import tpu as pltpu

F32 = jnp.float32
BF16 = jnp.bfloat16

SB_HEADS = 8
HEAD_DIM = 128
SB_WIDTH = SB_HEADS * HEAD_DIM
SSD_HEADS = 8
SSD_HEAD_DIM = 64
SSD_WIDTH = SSD_HEADS * SSD_HEAD_DIM
SSD_GROUPS = 2
SSD_STATE = 128
CONV_WIDTH = 4
XBC_WIDTH = SSD_WIDTH + 2 * SSD_GROUPS * SSD_STATE
MEM_TOKENS = 256
MEM_HEADS = 4
MEM_WIDTH = MEM_HEADS * HEAD_DIM
PAGE_SIZE = 128
EPS = 1e-6
ATTN_SCALE = HEAD_DIM ** -0.5
LOG2E = math.log2(math.e)

SSD_CHUNK = 128
PROJ_TN = 1024
PROJ_CHUNK = 256
ZDT_WIDTH = SSD_WIDTH + PROJ_CHUNK
SAMPLE_BUFS = 3
SSD_SEQS_PER_STEP = 4
SUBLANES = 8
SB_HEAD_BITS = SB_HEADS.bit_length() - 1
NORM_ROW_CHUNK = 128
PROJ_TM = 512
OUT_TM = 512
ATTN_TQ = 256
MEM_TQ = 512
VMEM_LIMIT = 56 * 1024 * 1024
VMEM_LIMIT_FUSED = 62 * 1024 * 1024

_NT = (((1,), (1,)), ((), ()))


def _dot(a, b):
    return jnp.dot(a, b, preferred_element_type=F32)


def _dot_nt(a, b):
    return lax.dot_general(a, b, _NT, preferred_element_type=F32)


def _split2(x):
    hi = x.astype(BF16)
    lo = (x - hi.astype(F32)).astype(BF16)
    return hi, lo


def _split3(x):
    hi = x.astype(BF16)
    r = x - hi.astype(F32)
    mid = r.astype(BF16)
    lo = (r - mid.astype(F32)).astype(BF16)
    return hi, mid, lo


def _dots_exact_lhs(xs, m):
    rows = xs[0].shape[0]
    r = _dot(jnp.concatenate([t for x in xs for t in _split3(x)], axis=0), m)
    part = lambda i: r[rows * i:rows * (i + 1), :]
    return [part(3 * i) + part(3 * i + 1) + part(3 * i + 2) for i in range(len(xs))]


def _dot_exact_rhs(m, x):
    n = x.shape[1]
    r = _dot(m, jnp.concatenate(_split3(x), axis=1))
    return r[:, :n] + r[:, n:2 * n] + r[:, 2 * n:]


def _silu(x):
    return x * (1.0 / (1.0 + jnp.exp(-x)))


def _rmsnorm_kernel(x_ref, nw_ref, h_ref):
    rows_total = x_ref.shape[0]
    rc = min(rows_total, NORM_ROW_CHUNK)

    def body(r, carry):
        rows = pl.ds(pl.multiple_of(r * rc, rc), rc)
        xv = x_ref[rows, :]
        ms = jnp.mean(xv * xv, axis=-1, keepdims=True)
        h_ref[rows, :] = (xv * lax.rsqrt(ms + EPS) * nw_ref[...]).astype(BF16)
        return carry

    lax.fori_loop(0, rows_total // rc, body, 0)


def _rmsnorm(x, norm_w, tm):
    t, d = x.shape
    assert t % tm == 0
    return pl.pallas_call(
        _rmsnorm_kernel,
        grid=(t // tm,),
        in_specs=[pl.BlockSpec((tm, d), lambda m: (m, 0)), pl.BlockSpec((1, d), lambda m: (0, 0))],
        out_specs=pl.BlockSpec((tm, d), lambda m: (m, 0)),
        out_shape=jax.ShapeDtypeStruct((t, d), BF16),
        compiler_params=pltpu.CompilerParams(
            dimension_semantics=("parallel",), vmem_limit_bytes=VMEM_LIMIT),
        name="rmsnorm",
    )(x, norm_w.reshape(1, d))


def _proj_step(segs, h_ref, w_ref, hn_ref, outs, w_rows_are_outputs):
    for col0, width, hn_row, o32, o16 in segs:
        for c0 in range(0, width, PROJ_CHUNK):
            chunk = slice(col0 + c0, col0 + c0 + PROJ_CHUNK)
            if w_rows_are_outputs:
                y = _dot_nt(h_ref[...], w_ref[chunk, :].astype(BF16))
            else:
                y = _dot(h_ref[...], w_ref[:, chunk].astype(BF16))
            for c in range(0, PROJ_CHUNK, HEAD_DIM):
                yc = y[:, c:c + HEAD_DIM]
                if hn_row is not None:
                    ms = jnp.mean(yc * yc, axis=-1, keepdims=True)
                    yc = yc * lax.rsqrt(ms + EPS) * hn_ref[hn_row:hn_row + 1, :]
                cols = slice(c0 + c, c0 + c + HEAD_DIM)
                if o32 is not None:
                    outs[o32][:, cols] = yc
                if o16 is not None:
                    outs[o16][:, cols] = yc.astype(BF16)


def _proj_kernel(plan, n_out, side_plan, n_side, w_rows_are_outputs, *refs):
    if side_plan is None:
        h_ref, w_ref, hn_ref = refs[:3]
        rest = refs[3:]
    else:
        h_ref, hs_ref, w_ref, hn_ref = refs[:4]
        rest = refs[4:]
    outs, side_outs = rest[:n_out], rest[n_out:n_out + n_side]
    n = pl.program_id(0)
    m = pl.program_id(1)

    for step, segs in enumerate(plan):
        @pl.when(n == step)
        def _(step=step, segs=segs):
            _proj_step(segs, h_ref, w_ref, hn_ref, outs, w_rows_are_outputs)
            if side_plan is not None:
                @pl.when(m == 0)
                def _():
                    _proj_step(side_plan[step], hs_ref, w_ref, hn_ref, side_outs,
                               w_rows_are_outputs)


def _proj(h, w, head_norms, plan, out_defs, tm, side=None, w_rows_are_outputs=True):
    t, d = h.shape
    n_steps = len(plan)
    m_tiles = t // tm
    w_shape = (n_steps * PROJ_TN, d) if w_rows_are_outputs else (d, n_steps * PROJ_TN)
    assert w.shape == w_shape and t % tm == 0
    w_spec = (pl.BlockSpec((PROJ_TN, d), lambda n, m: (n, 0)) if w_rows_are_outputs
              else pl.BlockSpec((d, PROJ_TN), lambda n, m: (0, n)))
    const = lambda n, m: (0, 0)

    step_of = {}
    for step, segs in enumerate(plan):
        for seg in segs:
            for o in seg[3:5]:
                if o is not None:
                    assert step_of.setdefault(o, step) == step
    assert sorted(step_of) == list(range(len(out_defs)))

    def out_map(step):
        return lambda n, m: (jnp.where(n == step, m, jnp.where(n < step, 0, m_tiles - 1)), 0)

    in_specs = [pl.BlockSpec((tm, d), lambda n, m: (m, 0))]
    operands = [h]
    out_specs = [pl.BlockSpec((tm, wd), out_map(step_of[i])) for i, (wd, _) in enumerate(out_defs)]
    out_shape = [jax.ShapeDtypeStruct((t, wd), dt) for wd, dt in out_defs]
    side_plan, n_side = None, 0
    if side is not None:
        h_side, side_plan, side_defs = side
        ts = h_side.shape[0]
        assert len(side_plan) == n_steps
        n_side = len(side_defs)
        in_specs.append(pl.BlockSpec((ts, d), const))
        operands.append(h_side)
        out_specs += [pl.BlockSpec((ts, wd), const) for wd, _ in side_defs]
        out_shape += [jax.ShapeDtypeStruct((ts, wd), dt) for wd, dt in side_defs]
    in_specs += [w_spec, pl.BlockSpec((SUBLANES, HEAD_DIM), const)]
    operands += [w, head_norms]
    kern = functools.partial(_proj_kernel, plan, len(out_defs), side_plan, n_side,
                             w_rows_are_outputs)
    res = pl.pallas_call(
        kern,
        grid=(n_steps, m_tiles),
        in_specs=in_specs,
        out_specs=out_specs,
        out_shape=out_shape,
        compiler_params=pltpu.CompilerParams(
            dimension_semantics=("arbitrary", "arbitrary"), vmem_limit_bytes=VMEM_LIMIT),
        name="proj",
    )(*operands)
    return res[:len(out_defs)], res[len(out_defs):]


def _log2_fail(z2):
    nz = -z2
    return jnp.minimum(nz, 0.0) - jnp.log(1.0 + jnp.exp2(jnp.minimum(z2, nz))) * LOG2E


def _sb_fused_kernel(tq, nq, ppc, n_pages, t_new, page_base,
                     pt_ref, bias_ref, qa_ref, qb_ref, k_ref, v_ref, ga_ref, gb_ref, uu_ref,
                     qs_ref, kn_ref, vn_ref, gs_ref, uo_ref, ck_hbm, cv_hbm,
                     op_ref, os_ref, acc_ref, c_ref, accs_ref, cs_ref, kbuf, vbuf, sem):
    step = pl.program_id(0)
    n_steps = pl.num_programs(0)
    pair = lax.rem(step, nq // 2)
    i1 = pair
    i2 = nq - 1 - pair
    chunks_per_seq = n_pages // ppc
    seq_steps = chunks_per_seq // nq
    part = lax.rem(step, seq_steps)
    n_chunks = n_steps * nq

    def chunk_copies(chunk):
        slot = lax.rem(chunk, SAMPLE_BUFS)
        seq = lax.div(chunk, chunks_per_seq)
        first_pos = (n_pages - 1) - lax.rem(chunk, chunks_per_seq) * ppc
        copies = []
        for j in range(ppc):
            page = page_base + pt_ref[seq * n_pages + first_pos - j]
            copies.append(pltpu.make_async_copy(ck_hbm.at[page], kbuf.at[slot, j], sem.at[slot, 0]))
            copies.append(pltpu.make_async_copy(cv_hbm.at[page], vbuf.at[slot, j], sem.at[slot, 1]))
        return copies

    def start_chunk(chunk):
        for cp in chunk_copies(chunk):
            cp.start()

    def wait_chunk(chunk):
        slot = lax.rem(chunk, SAMPLE_BUFS)
        pltpu.make_async_copy(ck_hbm.at[pl.ds(0, ppc)], kbuf.at[slot], sem.at[slot, 0]).wait()
        pltpu.make_async_copy(cv_hbm.at[pl.ds(0, ppc)], vbuf.at[slot], sem.at[slot, 1]).wait()

    def start_next(chunk):
        @pl.when(chunk + SAMPLE_BUFS < n_chunks)
        def _():
            start_chunk(chunk + SAMPLE_BUFS)

    @pl.when(step == 0)
    def _():
        for c in range(SAMPLE_BUFS):
            start_chunk(jnp.int32(c))

    lanes = PAGE_SIZE * SB_HEADS
    n_blk = lanes // HEAD_DIM
    lane = lax.broadcasted_iota(jnp.int32, (t_new, lanes), 1)
    lane_head = lane & (SB_HEADS - 1)
    biases = [bias_ref[h] * LOG2E for h in range(SB_HEADS)]
    bias_lanes = jnp.full((t_new, lanes), biases[0], F32)
    for h in range(1, SB_HEADS):
        bias_lanes = jnp.where(lane_head == h, biases[h], bias_lanes)
    qs = qs_ref[...]
    q_all = jnp.concatenate([qs[:, HEAD_DIM * h:HEAD_DIM * (h + 1)] for h in range(SB_HEADS)],
                            axis=0).astype(BF16)

    def page_scores(kpages, n, mask):
        s_cat = _dot_nt(q_all, kpages.astype(BF16))
        return [scores(s_cat[:, lanes * p:lanes * (p + 1)], mask) for p in range(n)]

    def scores(s_all, mask):
        sc = s_all[0:t_new, :]
        for h in range(1, SB_HEADS):
            sc = jnp.where(lane_head == h, s_all[t_new * h:t_new * (h + 1), :], sc)
        z2 = sc * (ATTN_SCALE * LOG2E) + bias_lanes
        lf = _log2_fail(z2)
        if mask is not None:
            lf = jnp.where(mask, lf, 0.0)
        blocks = jnp.concatenate([lf[:, HEAD_DIM * j:HEAD_DIM * (j + 1)] for j in range(n_blk)],
                                 axis=0)
        hi, lo = _split2(blocks)
        return z2, jnp.concatenate([hi, lo], axis=1)

    def stacked_sums(operands, matrix):
        rows = operands[0].shape[0]
        res = _dot(jnp.concatenate(operands, axis=0), matrix)
        return [res[rows * i:rows * (i + 1), :] for i in range(len(operands))]

    def weights(z2, res, mask, run):
        ws = [None] * n_blk
        for j in reversed(range(n_blk)):
            rows = slice(t_new * j, t_new * (j + 1))
            logw = z2[:, HEAD_DIM * j:HEAD_DIM * (j + 1)] + res[rows, :HEAD_DIM]
            if run is not None:
                logw = logw + run
            ws[j] = jnp.exp2(logw)
            tot = res[rows, HEAD_DIM:]
            run = tot if run is None else run + tot
        w = jnp.concatenate(ws, axis=1)
        if mask is not None:
            w = jnp.where(mask, w, 0.0)
        w_all = jnp.concatenate([jnp.where(lane_head == h, w, 0.0) for h in range(SB_HEADS)],
                                axis=0).astype(BF16)
        return w_all, run

    def new_keys():
        mask = (lane >> SB_HEAD_BITS) < lax.broadcasted_iota(jnp.int32, (t_new, lanes), 0)
        (z2, hilo), = page_scores(kn_ref[...], 1, mask)
        res, = stacked_sums([hilo], uo_ref[...])
        w_all, run = weights(z2, res, mask, None)
        cs_ref[...] = run
        accs_ref[...] = _dot(w_all, vn_ref[...].astype(BF16))

    uu = uu_ref[...]
    cols = [slice(HEAD_DIM * h, HEAD_DIM * (h + 1)) for h in range(SB_HEADS)]

    def section(q_ref, start, diag, first, chunk):
        kb = k_ref[pl.ds(start, tq), :]
        vb = v_ref[pl.ds(start, tq), :]
        n_pages_here = 0 if chunk is None else ppc
        slot = None if chunk is None else lax.rem(chunk, SAMPLE_BUFS)
        if n_pages_here:
            run = cs_ref[...]
            acc = accs_ref[...]
        p_parts, s_parts = {}, {}
        n_idx = max(SB_HEADS, n_pages_here)

        raw = [_dot_nt(q_ref[:, cols[h]], kb[:, cols[h]]) for h in range(SB_HEADS)]
        if n_pages_here:
            s_split = page_scores(kbuf[slot].reshape(ppc * lanes, HEAD_DIM), ppc, None)
        split = []
        for h in range(SB_HEADS):
            z2 = raw[h] * (ATTN_SCALE * LOG2E) + biases[h]
            lf = _log2_fail(z2)
            if diag is not None:
                lf = jnp.where(diag, lf, 0.0)
            hi, lo = _split2(lf)
            split.append((z2, jnp.concatenate([hi, lo], axis=1)))
        incl = stacked_sums([hilo for _, hilo in split], uu)
        for h in range(SB_HEADS):
            p_parts[h] = (split[h][0], incl[h])
        if n_pages_here:
            res = stacked_sums([hilo for _, hilo in s_split], uo_ref[...])
            for idx in range(n_pages_here):
                s_parts[idx] = (s_split[idx][0], res[idx])
        w_pages = []
        for idx in range(n_idx):
            if idx < SB_HEADS:
                z2, incl_h = p_parts[idx]
                logw = z2 + incl_h
                if not first:
                    logw = logw + c_ref[idx]
                w = jnp.exp2(logw)
                if diag is not None:
                    w = jnp.where(diag, w, 0.0)
                pv = _dot(w.astype(BF16), vb[:, cols[idx]])
                total = incl_h[:, 0:1]
                if first:
                    acc_ref[:, cols[idx]] = pv
                    c_ref[idx] = total
                else:
                    acc_ref[:, cols[idx]] += pv
                    c_ref[idx] += total
            if idx < n_pages_here:
                w_all, run = weights(*s_parts[idx], None, run)
                w_pages.append(w_all)
        if n_pages_here:
            cs_ref[...] = run
            accs_ref[...] = acc + _dot(jnp.concatenate(w_pages, axis=1),
                                       vbuf[slot].reshape(ppc * lanes, HEAD_DIM).astype(BF16))

    def finish(i, g_ref):
        rows = pl.ds(pl.multiple_of(i * tq, tq), tq)
        op_ref[rows, :] = (acc_ref[...] * _silu(g_ref[...].astype(F32))).astype(op_ref.dtype)

    row = lax.broadcasted_iota(jnp.int32, (tq, tq), 0)
    col = lax.broadcasted_iota(jnp.int32, (tq, tq), 1)
    diag = col < row
    chunk0 = step * nq

    @pl.when(part == 0)
    def _():
        new_keys()

    section(qa_ref, pl.multiple_of(i1 * tq, tq), diag, True, None)

    def body_a(t, carry):
        chunk = chunk0 + t
        wait_chunk(chunk)
        section(qa_ref, pl.multiple_of((i1 - 1 - t) * tq, tq), None, False, chunk)
        start_next(chunk)
        return carry

    lax.fori_loop(0, i1, body_a, 0)
    finish(i1, ga_ref)

    chunk = chunk0 + i1
    wait_chunk(chunk)
    section(qb_ref, pl.multiple_of(i2 * tq, tq), diag, True, chunk)
    start_next(chunk)

    def body_b(t, carry):
        chunk = chunk0 + i1 + 1 + t
        wait_chunk(chunk)
        section(qb_ref, pl.multiple_of((i2 - 1 - t) * tq, tq), None, False, chunk)
        start_next(chunk)
        return carry

    lax.fori_loop(0, i2, body_b, 0)
    finish(i2, gb_ref)

    @pl.when(part == seq_steps - 1)
    def _():
        g = gs_ref[...]
        for h in range(SB_HEADS):
            os_ref[:, cols[h]] = accs_ref[t_new * h:t_new * (h + 1), :] * _silu(g[:, cols[h]])


def _sb_fused(q, k, v, g, batch, seq, tq, q_s, k_new, v_new, g_s, cache_k, cache_v, layer,
              page_table, sb_bias, t_new):
    n_seq, n_pages = page_table.shape
    n_pool = cache_k.shape[1]
    nq = seq // tq
    n_steps = batch * (nq // 2)
    assert nq % 2 == 0 and (n_seq * n_pages) % (n_steps * nq) == 0
    ppc = n_seq * n_pages // (n_steps * nq)
    assert n_pages % (ppc * nq) == 0
    seq_steps = n_pages // (ppc * nq)
    page_rows = PAGE_SIZE * SB_HEADS

    q3, k3, v3, g3 = (a.reshape(batch, seq, SB_WIDTH) for a in (q, k, v, g))
    u = (jnp.arange(tq)[:, None] >= jnp.arange(tq)[None, :]).astype(BF16)
    uu = jnp.concatenate([u, u], axis=0)

    def as_page(a):
        a = a.reshape(n_seq, t_new, SB_HEADS, HEAD_DIM)
        a = jnp.pad(a, ((0, 0), (0, PAGE_SIZE - t_new), (0, 0), (0, 0)))
        return a.reshape(n_seq * page_rows, HEAD_DIM)

    cache_k = cache_k.reshape(-1, page_rows, HEAD_DIM)
    cache_v = cache_v.reshape(-1, page_rows, HEAD_DIM)
    r = jnp.arange(HEAD_DIM)
    same_head = (r[:, None] % SB_HEADS) == (r[None, :] % SB_HEADS)
    not_earlier = (r[:, None] // SB_HEADS) >= (r[None, :] // SB_HEADS)
    uo = jnp.concatenate([same_head & not_earlier, same_head], axis=1).astype(BF16)
    uo = jnp.concatenate([uo, uo], axis=0)

    half = nq // 2
    b_of = lambda s: s // half
    qa_map = lambda s, pt: (b_of(s), s % half, 0)
    qb_map = lambda s, pt: (b_of(s), nq - 1 - s % half, 0)
    seq_map = lambda s, pt: (b_of(s), 0, 0)
    samp_map = lambda s, pt: (s // seq_steps, 0)
    const = lambda s, pt: (0, 0)
    kern = functools.partial(_sb_fused_kernel, tq, nq, ppc, n_pages, t_new, layer * n_pool)
    grid_spec = pltpu.PrefetchScalarGridSpec(
        num_scalar_prefetch=1,
        grid=(n_steps,),
        in_specs=[
            pl.BlockSpec(memory_space=pltpu.SMEM),
            pl.BlockSpec((None, tq, SB_WIDTH), qa_map),
            pl.BlockSpec((None, tq, SB_WIDTH), qb_map),
            pl.BlockSpec((None, seq, SB_WIDTH), seq_map, pipeline_mode=pl.Buffered(1)),
            pl.BlockSpec((None, seq, SB_WIDTH), seq_map, pipeline_mode=pl.Buffered(1)),
            pl.BlockSpec((None, tq, SB_WIDTH), qa_map),
            pl.BlockSpec((None, tq, SB_WIDTH), qb_map),
            pl.BlockSpec((2 * tq, tq), const),
            pl.BlockSpec((t_new, SB_WIDTH), samp_map),
            pl.BlockSpec((page_rows, HEAD_DIM), samp_map),
            pl.BlockSpec((page_rows, HEAD_DIM), samp_map),
            pl.BlockSpec((t_new, SB_WIDTH), samp_map),
            pl.BlockSpec((2 * HEAD_DIM, 2 * HEAD_DIM), const),
            pl.BlockSpec(memory_space=pl.ANY),
            pl.BlockSpec(memory_space=pl.ANY),
        ],
        out_specs=[
            pl.BlockSpec((None, seq, SB_WIDTH), seq_map, pipeline_mode=pl.Buffered(1)),
            pl.BlockSpec((t_new, SB_WIDTH), samp_map),
        ],
        scratch_shapes=[
            pltpu.VMEM((tq, SB_WIDTH), F32),
            pltpu.VMEM((SB_HEADS, tq, 1), F32),
            pltpu.VMEM((SB_HEADS * t_new, HEAD_DIM), F32),
            pltpu.VMEM((t_new, HEAD_DIM), F32),
            pltpu.VMEM((SAMPLE_BUFS, ppc, page_rows, HEAD_DIM), F32),
            pltpu.VMEM((SAMPLE_BUFS, ppc, page_rows, HEAD_DIM), F32),
            pltpu.SemaphoreType.DMA((SAMPLE_BUFS, 2)),
        ],
    )
    out_p, out_s = pl.pallas_call(
        kern,
        grid_spec=grid_spec,
        out_shape=[jax.ShapeDtypeStruct((batch, seq, SB_WIDTH), BF16),
                   jax.ShapeDtypeStruct((n_seq * t_new, SB_WIDTH), F32)],
        compiler_params=pltpu.CompilerParams(
            dimension_semantics=("arbitrary",), vmem_limit_bytes=VMEM_LIMIT_FUSED),
        name="sb_fused",
    )(page_table.reshape(-1), sb_bias, q3, q3, k3, v3, g3, g3, uu,
      q_s, as_page(k_new), as_page(v_new), g_s, uo, cache_k, cache_v)
    return out_p.reshape(batch * seq, SB_WIDTH), out_s


def _ssd_kernel(length, n_par, *refs):
    per_seq_in, shared, per_seq_out = refs[:5], refs[5:13], refs[13:]
    pre_ref, h0_ref = per_seq_in[3:5]
    ext_ref, st_ref = per_seq_out[3:5]

    @pl.when(pl.program_id(1) == 0)
    def _():
        ext_ref[:, 0:SUBLANES, :] = pre_ref[...]
        st_ref[...] = h0_ref[...]

    for s in range(n_par):
        _ssd_chunk(length, *[r.at[s] for r in per_seq_in[:3]], *shared,
                   *[r.at[s] for r in per_seq_out])


def _pad_rows(a, rows):
    if a.shape[0] == rows:
        return a
    return jnp.concatenate([a, jnp.zeros((rows - a.shape[0], a.shape[1]), a.dtype)], axis=0)


def _ssd_chunk(length, xbc_ref, z_ref, dt_ref, cw_ref, cb_ref, dtb_ref,
               alog_ref, dsk_ref, nw_ref, ltri_ref, e_ref, out_ref, cnew_ref, snew_ref,
               ext_ref, st_ref):
    L = SSD_CHUNK
    P = SSD_HEAD_DIM

    ext_ref[SUBLANES:SUBLANES + L, :] = _pad_rows(xbc_ref[...], L)
    cw = cw_ref[...]
    conv = cb_ref[...]
    for j in range(CONV_WIDTH):
        off = SUBLANES - (CONV_WIDTH - 1) + j
        conv = conv + ext_ref[off:off + L, :] * cw[j:j + 1, :]
    act = _silu(conv)
    tail = ext_ref[length:length + SUBLANES, :]
    cnew_ref[...] = tail
    ext_ref[0:SUBLANES, :] = tail

    xs = act[:, :SSD_WIDTH]
    bm = act[:, SSD_WIDTH:SSD_WIDTH + SSD_GROUPS * SSD_STATE]
    cm = act[:, SSD_WIDTH + SSD_GROUPS * SSD_STATE:]

    x_dt = _pad_rows(dt_ref[...], L) + dtb_ref[...]
    dt = jnp.maximum(x_dt, 0.0) + jnp.log1p(jnp.exp(-jnp.abs(x_dt)))
    if length < L:
        valid = lax.broadcasted_iota(jnp.int32, dt.shape, 0) < length
        dt = jnp.where(valid, dt, 0.0)
    da = dt * (-jnp.exp(alog_ref[...]))
    cs = _dot_exact_rhs(ltri_ref[...], da)
    cs_t = cs.T
    e = e_ref[...]
    dt_x, cs_x = _dots_exact_lhs([dt, cs], e)
    xdt = xs * dt_x
    ecs = jnp.exp(cs_x)
    xw_t = (xdt * jnp.exp(cs_x[L - 1:L, :] - cs_x)).T
    xdt16 = xdt.astype(BF16)

    row = lax.broadcasted_iota(jnp.int32, (L, L), 0)
    col = lax.broadcasted_iota(jnp.int32, (L, L), 1)
    causal = col <= row
    heads_per_group = SSD_HEADS // SSD_GROUPS
    gw = heads_per_group * P
    y_diag, y_off = [], []
    for g in range(SSD_GROUPS):
        bg = bm[:, SSD_STATE * g:SSD_STATE * (g + 1)].astype(BF16)
        cg = cm[:, SSD_STATE * g:SSD_STATE * (g + 1)].astype(BF16)
        cb = _dot_nt(cg, bg)
        prev = st_ref[gw * g:gw * (g + 1), :]
        y_off.append(_dot_nt(cg, prev.astype(BF16)))
        new = _dot(xw_t[gw * g:gw * (g + 1), :].astype(BF16), bg)
        for r in range(heads_per_group):
            h = heads_per_group * g + r
            seg = cs[:, h:h + 1] - cs_t[h:h + 1, :]
            decay = jnp.exp(jnp.where(causal, seg, -jnp.inf))
            y_diag.append(_dot((cb * decay).astype(BF16), xdt16[:, P * h:P * (h + 1)]))
            chunk_decay = jnp.exp(cs[L - 1:L, h:h + 1])
            st_ref[P * h:P * (h + 1), :] = (prev[P * r:P * (r + 1), :] * chunk_decay
                                            + new[P * r:P * (r + 1), :])
    snew_ref[...] = st_ref[...]
    y = (jnp.concatenate(y_diag, axis=1) + jnp.concatenate(y_off, axis=1) * ecs
         + xs * dsk_ref[...])
    gated = y[:length] * _silu(z_ref[...])
    ms = jnp.mean(gated * gated, axis=-1, keepdims=True)
    out_ref[...] = (gated * lax.rsqrt(ms + EPS) * nw_ref[...]).astype(out_ref.dtype)


def _ssd(xbc, zdt, prefix, h0, conv_w, conv_b, dt_bias, a_log, d_skip, ssd_norm_w,
         batch, n_chunks, length, out_dtype):
    L = SSD_CHUNK
    pre = jnp.pad(prefix, ((0, 0), (SUBLANES - (CONV_WIDTH - 1), 0), (0, 0)))
    pad_h = lambda a: jnp.pad(a, (0, HEAD_DIM - SSD_HEADS)).reshape(1, HEAD_DIM)
    ltri = (jnp.arange(L)[:, None] >= jnp.arange(L)[None, :]).astype(BF16)
    expand = (jnp.arange(HEAD_DIM)[:, None] == jnp.arange(SSD_WIDTH)[None, :] // SSD_HEAD_DIM
              ).astype(BF16)
    dsk = jnp.repeat(d_skip, SSD_HEAD_DIM).reshape(1, SSD_WIDTH)
    z_blk = SSD_WIDTH // HEAD_DIM
    n_par = SSD_SEQS_PER_STEP
    assert batch % n_par == 0 and length <= L and (length == L or n_chunks == 1)
    rows = n_chunks * length
    xbc3 = xbc.reshape(batch, rows, XBC_WIDTH)
    zdt3 = zdt.reshape(batch, rows, ZDT_WIDTH)
    const = lambda b, c: (0, 0)
    seq = lambda b, c: (b, 0, 0)
    kern = functools.partial(_ssd_kernel, length, n_par)
    out, conv_new, ssm_new = pl.pallas_call(
        kern,
        grid=(batch // n_par, n_chunks),
        in_specs=[
            pl.BlockSpec((n_par, length, XBC_WIDTH), lambda b, c: (b, c, 0)),
            pl.BlockSpec((n_par, length, SSD_WIDTH), lambda b, c: (b, c, 0)),
            pl.BlockSpec((n_par, length, HEAD_DIM), lambda b, c: (b, c, z_blk)),
            pl.BlockSpec((n_par, SUBLANES, XBC_WIDTH), seq),
            pl.BlockSpec((n_par, SSD_WIDTH, SSD_STATE), seq),
            pl.BlockSpec((CONV_WIDTH, XBC_WIDTH), const),
            pl.BlockSpec((1, XBC_WIDTH), const),
            pl.BlockSpec((1, HEAD_DIM), const),
            pl.BlockSpec((1, HEAD_DIM), const),
            pl.BlockSpec((1, SSD_WIDTH), const),
            pl.BlockSpec((1, SSD_WIDTH), const),
            pl.BlockSpec((L, L), const),
            pl.BlockSpec((HEAD_DIM, SSD_WIDTH), const),
        ],
        out_specs=[
            pl.BlockSpec((n_par, length, SSD_WIDTH), lambda b, c: (b, c, 0)),
            pl.BlockSpec((n_par, SUBLANES, XBC_WIDTH), seq),
            pl.BlockSpec((n_par, SSD_WIDTH, SSD_STATE), seq),
        ],
        out_shape=[
            jax.ShapeDtypeStruct((batch, rows, SSD_WIDTH), out_dtype),
            jax.ShapeDtypeStruct((batch, SUBLANES, XBC_WIDTH), F32),
            jax.ShapeDtypeStruct((batch, SSD_WIDTH, SSD_STATE), F32),
        ],
        scratch_shapes=[pltpu.VMEM((n_par, SUBLANES + L, XBC_WIDTH), F32),
                        pltpu.VMEM((n_par, SSD_WIDTH, SSD_STATE), F32)],
        compiler_params=pltpu.CompilerParams(
            dimension_semantics=("parallel", "arbitrary"), vmem_limit_bytes=VMEM_LIMIT),
        name="ssd_scan",
    )(xbc3, zdt3, zdt3, pre, h0.reshape(batch, SSD_WIDTH, SSD_STATE), conv_w,
      conv_b.reshape(1, XBC_WIDTH), pad_h(dt_bias), pad_h(a_log), dsk,
      ssd_norm_w.reshape(1, SSD_WIDTH), ltri, expand)
    return out.reshape(batch * rows, SSD_WIDTH), conv_new, ssm_new


def _mem_attn_kernel(q_ref, g_ref, k_ref, v_ref, o_ref):
    for h in range(MEM_HEADS):
        cols = slice(HEAD_DIM * h, HEAD_DIM * (h + 1))
        s = _dot_nt(q_ref[:, cols].astype(BF16), k_ref[:, cols].astype(BF16)) * ATTN_SCALE
        p = jnp.exp(s - jnp.max(s, axis=-1, keepdims=True))
        den = jnp.sum(p, axis=-1, keepdims=True)
        o = _dot(p.astype(BF16), v_ref[:, cols].astype(BF16)) / den
        o_ref[:, cols] = (o * _silu(g_ref[:, cols].astype(F32))).astype(o_ref.dtype)


def _mem_attn(q, g, mem_k, mem_v, batch, t, tq, out_dtype):
    nq = t // tq
    return pl.pallas_call(
        _mem_attn_kernel,
        grid=(batch, nq),
        in_specs=[
            pl.BlockSpec((tq, MEM_WIDTH), lambda b, i: (b * nq + i, 0)),
            pl.BlockSpec((tq, MEM_WIDTH), lambda b, i: (b * nq + i, 0)),
            pl.BlockSpec((None, MEM_TOKENS, MEM_WIDTH), lambda b, i: (b, 0, 0)),
            pl.BlockSpec((None, MEM_TOKENS, MEM_WIDTH), lambda b, i: (b, 0, 0)),
        ],
        out_specs=pl.BlockSpec((tq, MEM_WIDTH), lambda b, i: (b * nq + i, 0)),
        out_shape=jax.ShapeDtypeStruct((batch * t, MEM_WIDTH), out_dtype),
        compiler_params=pltpu.CompilerParams(
            dimension_semantics=("parallel", "parallel"), vmem_limit_bytes=VMEM_LIMIT),
        name="mem_attn",
    )(q, g, mem_k, mem_v)


def _mem_attn_rows_kernel(t, q_ref, g_ref, k_ref, v_ref, o_ref):
    q = q_ref[...]
    g = g_ref[...]
    cols = [slice(HEAD_DIM * h, HEAD_DIM * (h + 1)) for h in range(MEM_HEADS)]
    q_all = jnp.concatenate([q[:, c] for c in cols], axis=0).astype(BF16)
    s = _dot_nt(q_all, k_ref[...].astype(BF16)) * ATTN_SCALE
    n_rows = s.shape[1]
    row_head = jnp.concatenate([jnp.full((t, n_rows), h, jnp.int32) for h in range(MEM_HEADS)],
                               axis=0)
    lane_head = lax.broadcasted_iota(jnp.int32, s.shape, 1) & (MEM_HEADS - 1)
    s = jnp.where(row_head == lane_head, s, -jnp.inf)
    p = jnp.exp(s - jnp.max(s, axis=-1, keepdims=True))
    den = jnp.sum(p, axis=-1, keepdims=True)
    o = _dot(p.astype(BF16), v_ref[...].astype(BF16)) / den
    for h in range(MEM_HEADS):
        o_ref[:, cols[h]] = (o[t * h:t * (h + 1), :] * _silu(g[:, cols[h]])).astype(o_ref.dtype)


def _mem_attn_rows(q, g, mem_k, mem_v, batch, t, out_dtype):
    rows = MEM_TOKENS * MEM_HEADS
    kern = functools.partial(_mem_attn_rows_kernel, t)
    return pl.pallas_call(
        kern,
        grid=(batch,),
        in_specs=[
            pl.BlockSpec((t, MEM_WIDTH), lambda b: (b, 0)),
            pl.BlockSpec((t, MEM_WIDTH), lambda b: (b, 0)),
            pl.BlockSpec((None, rows, HEAD_DIM), lambda b: (b, 0, 0)),
            pl.BlockSpec((None, rows, HEAD_DIM), lambda b: (b, 0, 0)),
        ],
        out_specs=pl.BlockSpec((t, MEM_WIDTH), lambda b: (b, 0)),
        out_shape=jax.ShapeDtypeStruct((batch * t, MEM_WIDTH), out_dtype),
        compiler_params=pltpu.CompilerParams(
            dimension_semantics=("parallel",), vmem_limit_bytes=VMEM_LIMIT),
        name="mem_attn_rows",
    )(q, g, mem_k, mem_v)


def _out_proj_kernel(x_ref, sb_ref, ssd_ref, mo_ref, w_ref, o_ref):
    mix = jnp.concatenate([sb_ref[...].astype(BF16), ssd_ref[...].astype(BF16),
                           mo_ref[...].astype(BF16)], axis=-1)
    for c in range(0, o_ref.shape[1], PROJ_CHUNK):
        cols = slice(c, c + PROJ_CHUNK)
        o_ref[:, cols] = x_ref[:, cols] + _dot(mix, w_ref[:, cols].astype(BF16))


def _out_proj(x, sb, ssd, mo, w_out, tm, tn):
    t, d = x.shape
    return pl.pallas_call(
        _out_proj_kernel,
        grid=(t // tm, d // tn),
        in_specs=[
            pl.BlockSpec((tm, tn), lambda m, n: (m, n)),
            pl.BlockSpec((tm, SB_WIDTH), lambda m, n: (m, 0)),
            pl.BlockSpec((tm, SSD_WIDTH), lambda m, n: (m, 0)),
            pl.BlockSpec((tm, MEM_WIDTH), lambda m, n: (m, 0)),
            pl.BlockSpec((w_out.shape[0], tn), lambda m, n: (0, n), pipeline_mode=pl.Buffered(1)),
        ],
        out_specs=pl.BlockSpec((tm, tn), lambda m, n: (m, n)),
        out_shape=jax.ShapeDtypeStruct((t, d), F32),
        compiler_params=pltpu.CompilerParams(
            dimension_semantics=("parallel", "arbitrary"), vmem_limit_bytes=VMEM_LIMIT),
        name="out_proj",
    )(x, sb, ssd, mo, w_out)


def _in_proj_plan(act_dtype):
    lowp = act_dtype == BF16
    outs, plan = [], []

    def add(width, dtype):
        outs.append((width, dtype))
        return len(outs) - 1

    q = add(SB_WIDTH, act_dtype)
    plan.append(((0, SB_WIDTH, 0, None if lowp else q, q if lowp else None),))
    k32 = add(SB_WIDTH, F32)
    k16 = add(SB_WIDTH, BF16) if lowp else None
    plan.append(((0, SB_WIDTH, 1, k32, k16),))
    v32 = add(SB_WIDTH, F32)
    v16 = add(SB_WIDTH, BF16) if lowp else None
    plan.append(((0, SB_WIDTH, None, v32, v16),))
    g = add(SB_WIDTH, act_dtype)
    plan.append(((0, SB_WIDTH, None, None if lowp else g, g if lowp else None),))
    xbc = add(XBC_WIDTH, F32)
    plan.append(((0, XBC_WIDTH, None, xbc, None),))
    zdt = add(ZDT_WIDTH, F32)
    plan.append(((0, ZDT_WIDTH, None, zdt, None),))
    mq = add(MEM_WIDTH, act_dtype)
    mg = add(MEM_WIDTH, act_dtype)
    plan.append(((0, MEM_WIDTH, 2, None if lowp else mq, mq if lowp else None),
                 (MEM_WIDTH, MEM_WIDTH, None, None if lowp else mg, mg if lowp else None)))
    names = dict(q=q, k32=k32, k16=k16, v32=v32, v16=v16, g=g, xbc=xbc, zdt=zdt, mq=mq, mg=mg)
    return tuple(plan), outs, names


_O_Z = 4 * SB_WIDTH
_O_XBC = _O_Z + SSD_WIDTH
_O_DT = _O_XBC + XBC_WIDTH
_O_MEM = _O_DT + SSD_HEADS
_IN_WIDTH = _O_MEM + 2 * MEM_WIDTH
_CAT_WIDTH = 7 * PROJ_TN
W_PREP_COLS = 256
BF16_ROWS = 16


def _w_prep_kernel(w_ref, o_ref):
    def put(dst, src, rows):
        o_ref[dst:dst + rows, :] = w_ref[src:src + rows, :].astype(BF16)

    cols = w_ref.shape[1]
    put(0, 0, _O_Z)
    put(_O_Z, _O_XBC, XBC_WIDTH)
    put(_O_Z + XBC_WIDTH, _O_Z, SSD_WIDTH)
    dt0 = _O_Z + XBC_WIDTH + SSD_WIDTH
    o_ref[dt0:dt0 + BF16_ROWS, :] = jnp.concatenate(
        [w_ref[_O_DT:_O_MEM, :], jnp.zeros((BF16_ROWS - SSD_HEADS, cols), F32)], axis=0
    ).astype(BF16)
    o_ref[dt0 + BF16_ROWS:6 * PROJ_TN, :] = jnp.zeros((6 * PROJ_TN - dt0 - BF16_ROWS, cols), BF16)
    put(6 * PROJ_TN, _O_MEM, 2 * MEM_WIDTH)


def _rearranged_w_in(w_t):
    d = w_t.shape[1]
    assert w_t.shape[0] == _IN_WIDTH and d % W_PREP_COLS == 0
    return pl.pallas_call(
        _w_prep_kernel,
        grid=(d // W_PREP_COLS,),
        in_specs=[pl.BlockSpec((_IN_WIDTH, W_PREP_COLS), lambda c: (0, c))],
        out_specs=pl.BlockSpec((_CAT_WIDTH, W_PREP_COLS), lambda c: (0, c)),
        out_shape=jax.ShapeDtypeStruct((_CAT_WIDTH, d), BF16),
        compiler_params=pltpu.CompilerParams(
            dimension_semantics=("parallel",), vmem_limit_bytes=VMEM_LIMIT),
        name="w_prep",
    )(w_t)


def kernel(x_prompt, x_sample, cache_sb_k, cache_sb_v, state_ssm, state_conv, cache_mem_k,
           cache_mem_v, page_table, mem_prompt, norm_w, w_in, sb_q_norm, sb_k_norm, sb_bias,
           conv_w, conv_b, dt_bias, a_log, d_skip, ssd_norm_w, mem_norm_w, w_mem_kv, mem_q_norm,
           mem_k_norm, w_out):
    depth = w_in.shape[0]
    assert depth == 1
    layer = 0
    bp, sp, d = x_prompt.shape
    bs, ts, _ = x_sample.shape
    n_pool = cache_sb_k.shape[1]
    L = SSD_CHUNK

    w_cat = _rearranged_w_in(w_in[layer].T)
    w_o = w_out[layer]
    head_norms = jnp.concatenate(
        [sb_q_norm[layer][None], sb_k_norm[layer][None], mem_q_norm[layer][None],
         mem_k_norm[layer][None], jnp.zeros((SUBLANES - 4, HEAD_DIM), F32)], axis=0)
    ssd_params = (conv_w[layer], conv_b[layer], dt_bias[layer], a_log[layer], d_skip[layer],
                  ssd_norm_w[layer])

    xp = x_prompt.reshape(bp * sp, d)
    mem_plan = (((0, MEM_WIDTH, 3, 0, None), (MEM_WIDTH, MEM_WIDTH, None, 1, None)),)
    h_mem = _rmsnorm(mem_prompt.reshape(bp * MEM_TOKENS, d), mem_norm_w[layer], PROJ_TM)
    (mk, mv), _ = _proj(h_mem, w_mem_kv[layer], head_norms, mem_plan,
                        [(MEM_WIDTH, F32), (MEM_WIDTH, F32)], tm=PROJ_TM,
                        w_rows_are_outputs=False)
    plan, outs, nm = _in_proj_plan(BF16)
    xs = x_sample.reshape(bs * ts, d)
    plan_s, outs_s, ns = _in_proj_plan(F32)
    pr, ps = _proj(_rmsnorm(xp, norm_w[layer], PROJ_TM), w_cat, head_norms, plan, outs,
                   tm=PROJ_TM, side=(_rmsnorm(xs, norm_w[layer], bs * ts), plan_s, outs_s))
    sb, sb_s = _sb_fused(pr[nm['q']], pr[nm['k16']], pr[nm['v16']], pr[nm['g']], bp, sp, ATTN_TQ,
                         ps[ns['q']], ps[ns['k32']], ps[ns['v32']], ps[ns['g']],
                         cache_sb_k, cache_sb_v, layer, page_table, sb_bias[layer], ts)
    ssd, conv_p, ssm_p = _ssd(
        pr[nm['xbc']], pr[nm['zdt']], jnp.zeros((bp, CONV_WIDTH - 1, XBC_WIDTH), F32),
        jnp.zeros((bp, SSD_HEADS, SSD_HEAD_DIM, SSD_STATE), F32), *ssd_params,
        batch=bp, n_chunks=sp // L, length=L, out_dtype=BF16)
    mo = _mem_attn(pr[nm['mq']], pr[nm['mg']], mk.reshape(bp, MEM_TOKENS, MEM_WIDTH),
                   mv.reshape(bp, MEM_TOKENS, MEM_WIDTH), bp, sp, tq=MEM_TQ, out_dtype=BF16)
    yp = _out_proj(xp, sb, ssd, mo, w_o, tm=OUT_TM, tn=d)

    ssd_s, conv_s, ssm_s = _ssd(
        ps[ns['xbc']], ps[ns['zdt']], state_conv[layer], state_ssm[layer],
        *ssd_params, batch=bs, n_chunks=1, length=ts, out_dtype=F32)
    mo_s = _mem_attn_rows(ps[ns['mq']], ps[ns['mg']],
                          cache_mem_k[layer].reshape(bs, MEM_TOKENS * MEM_HEADS, HEAD_DIM),
                          cache_mem_v[layer].reshape(bs, MEM_TOKENS * MEM_HEADS, HEAD_DIM),
                          bs, ts, out_dtype=F32)
    ys = _out_proj(xs, sb_s, ssd_s, mo_s, w_o, tm=bs * ts, tn=d)

    tail = slice(SUBLANES - (CONV_WIDTH - 1), SUBLANES)
    return (
        yp.reshape(bp, sp, d),
        ys.reshape(bs, ts, d),
        pr[nm['k32']].reshape(1, bp, sp, SB_HEADS, HEAD_DIM),
        pr[nm['v32']].reshape(1, bp, sp, SB_HEADS, HEAD_DIM),
        ssm_p.reshape(1, bp, SSD_HEADS, SSD_HEAD_DIM, SSD_STATE),
        conv_p[:, tail][None],
        mk.reshape(1, bp, MEM_TOKENS, MEM_HEADS, HEAD_DIM),
        mv.reshape(1, bp, MEM_TOKENS, MEM_HEADS, HEAD_DIM),
        ps[ns['k32']].reshape(1, bs, ts, SB_HEADS, HEAD_DIM),
        ps[ns['v32']].reshape(1, bs, ts, SB_HEADS, HEAD_DIM),
        ssm_s.reshape(1, bs, SSD_HEADS, SSD_HEAD_DIM, SSD_STATE),
        conv_s[:, tail][None],
    )
```

```python
import functools
import math

import jax
import jax.numpy as jnp
from jax import lax
from jax.experimental import pallas as pl
from jax.experimental.pallas import tpu as pltpu

F32 = jnp.float32
BF16 = jnp.bfloat16

SB_HEADS = 8
HEAD_DIM = 128
SB_WIDTH = SB_HEADS * HEAD_DIM
SSD_HEADS = 8
SSD_HEAD_DIM = 64
SSD_WIDTH = SSD_HEADS * SSD_HEAD_DIM
SSD_GROUPS = 2
SSD_STATE = 128
CONV_WIDTH = 4
XBC_WIDTH = SSD_WIDTH + 2 * SSD_GROUPS * SSD_STATE
MEM_TOKENS = 256
MEM_HEADS = 4
MEM_WIDTH = MEM_HEADS * HEAD_DIM
PAGE_SIZE = 128
EPS = 1e-6
ATTN_SCALE = HEAD_DIM ** -0.5
LOG2E = math.log2(math.e)

SSD_CHUNK = 128
PROJ_TN = 1024
PROJ_CHUNK = 256
ZDT_WIDTH = SSD_WIDTH + PROJ_CHUNK
SAMPLE_BUFS = 3
SSD_SEQS_PER_STEP = 4
SUBLANES = 8
SB_HEAD_BITS = SB_HEADS.bit_length() - 1
NORM_ROW_CHUNK = 128
PROJ_TM = 512
OUT_TM = 512
ATTN_TQ = 256
MEM_TQ = 512
VMEM_LIMIT = 56 * 1024 * 1024
VMEM_LIMIT_FUSED = 62 * 1024 * 1024

_NT = (((1,), (1,)), ((), ()))


def _dot(a, b):
    return jnp.dot(a, b, preferred_element_type=F32)


def _dot_nt(a, b):
    return lax.dot_general(a, b, _NT, preferred_element_type=F32)


def _split2(x):
    hi = x.astype(BF16)
    lo = (x - hi.astype(F32)).astype(BF16)
    return hi, lo


def _split3(x):
    hi = x.astype(BF16)
    r = x - hi.astype(F32)
    mid = r.astype(BF16)
    lo = (r - mid.astype(F32)).astype(BF16)
    return hi, mid, lo


def _dots_exact_lhs(xs, m):
    rows = xs[0].shape[0]
    r = _dot(jnp.concatenate([t for x in xs for t in _split3(x)], axis=0), m)
    part = lambda i: r[rows * i:rows * (i + 1), :]
    return [part(3 * i) + part(3 * i + 1) + part(3 * i + 2) for i in range(len(xs))]


def _dot_exact_rhs(m, x):
    n = x.shape[1]
    r = _dot(m, jnp.concatenate(_split3(x), axis=1))
    return r[:, :n] + r[:, n:2 * n] + r[:, 2 * n:]


def _silu(x):
    return x * (1.0 / (1.0 + jnp.exp(-x)))


def _rmsnorm_rows(x_ref, nw_ref, h_ref):
    rows_total = x_ref.shape[0]
    rc = min(rows_total, NORM_ROW_CHUNK)

    def body(r, carry):
        rows = pl.ds(pl.multiple_of(r * rc, rc), rc)
        xv = x_ref[rows, :]
        ms = jnp.mean(xv * xv, axis=-1, keepdims=True)
        h_ref[rows, :] = (xv * lax.rsqrt(ms + EPS) * nw_ref[...]).astype(BF16)
        return carry

    lax.fori_loop(0, rows_total // rc, body, 0)


def _proj_step(segs, h_ref, w_ref, hn_ref, outs, w_rows_are_outputs):
    for col0, width, hn_row, o32, o16, *silu_cols in segs:
        for c0 in range(0, width, PROJ_CHUNK):
            chunk = slice(col0 + c0, col0 + c0 + PROJ_CHUNK)
            if w_rows_are_outputs:
                y = _dot_nt(h_ref[...], w_ref[chunk, :].astype(BF16))
            else:
                y = _dot(h_ref[...], w_ref[:, chunk].astype(BF16))
            for c in range(0, PROJ_CHUNK, HEAD_DIM):
                yc = y[:, c:c + HEAD_DIM]
                if hn_row is not None:
                    ms = jnp.mean(yc * yc, axis=-1, keepdims=True)
                    yc = yc * lax.rsqrt(ms + EPS) * hn_ref[hn_row:hn_row + 1, :]
                if silu_cols and c0 + c < silu_cols[0]:
                    yc = _silu(yc)
                cols = slice(c0 + c, c0 + c + HEAD_DIM)
                if o32 is not None:
                    outs[o32][:, cols] = yc
                if o16 is not None:
                    outs[o16][:, cols] = yc.astype(BF16)


def _proj_kernel(plan, n_out, side_plan, n_side, w_rows_are_outputs, *refs):
    if side_plan is None:
        x_ref, nw_ref, w_ref, hn_ref = refs[:4]
        rest = refs[4:]
    else:
        x_ref, xs_ref, nw_ref, w_ref, hn_ref = refs[:5]
        rest = refs[5:]
    outs, side_outs = rest[:n_out], rest[n_out:n_out + n_side]
    scratch = rest[n_out + n_side:]
    h_ref = scratch[0]
    m = pl.program_id(0)
    n = pl.program_id(1)

    @pl.when(n == 0)
    def _():
        _rmsnorm_rows(x_ref, nw_ref, h_ref)

    if side_plan is not None:
        hs_ref = scratch[1]

        @pl.when((n == 0) & (m == 0))
        def _():
            _rmsnorm_rows(xs_ref, nw_ref, hs_ref)

    for step, segs in enumerate(plan):
        @pl.when(n == step)
        def _(step=step, segs=segs):
            _proj_step(segs, h_ref, w_ref, hn_ref, outs, w_rows_are_outputs)
            if side_plan is not None:
                @pl.when(m == 0)
                def _():
                    _proj_step(side_plan[step], hs_ref, w_ref, hn_ref, side_outs,
                               w_rows_are_outputs)


def _proj(x, norm_w, w, head_norms, plan, out_defs, tm, side=None, w_rows_are_outputs=True):
    t, d = x.shape
    n_steps = len(plan)
    w_shape = (n_steps * PROJ_TN, d) if w_rows_are_outputs else (d, n_steps * PROJ_TN)
    assert w.shape == w_shape and t % tm == 0
    w_spec = (pl.BlockSpec((PROJ_TN, d), lambda m, n: (n, 0)) if w_rows_are_outputs
              else pl.BlockSpec((d, PROJ_TN), lambda m, n: (0, n)))
    row = lambda m, n: (m, 0)
    const = lambda m, n: (0, 0)
    in_specs = [pl.BlockSpec((tm, d), row)]
    operands = [x]
    out_specs = [pl.BlockSpec((tm, w), row) for w, _ in out_defs]
    out_shape = [jax.ShapeDtypeStruct((t, w), dt) for w, dt in out_defs]
    scratch = [pltpu.VMEM((tm, d), BF16)]
    side_plan, n_side = None, 0
    if side is not None:
        x_side, side_plan, side_defs = side
        ts = x_side.shape[0]
        assert len(side_plan) == n_steps
        n_side = len(side_defs)
        in_specs.append(pl.BlockSpec((ts, d), const))
        operands.append(x_side)
        out_specs += [pl.BlockSpec((ts, w), const) for w, _ in side_defs]
        out_shape += [jax.ShapeDtypeStruct((ts, w), dt) for w, dt in side_defs]
        scratch.append(pltpu.VMEM((ts, d), BF16))
    in_specs += [pl.BlockSpec((1, d), const), w_spec,
                 pl.BlockSpec((SUBLANES, HEAD_DIM), const)]
    operands += [norm_w.reshape(1, d), w, head_norms]
    kern = functools.partial(_proj_kernel, plan, len(out_defs), side_plan, n_side,
                             w_rows_are_outputs)
    res = pl.pallas_call(
        kern,
        grid=(t // tm, n_steps),
        in_specs=in_specs,
        out_specs=out_specs,
        out_shape=out_shape,
        scratch_shapes=scratch,
        compiler_params=pltpu.CompilerParams(
            dimension_semantics=("arbitrary", "arbitrary"), vmem_limit_bytes=VMEM_LIMIT),
        name="norm_proj",
    )(*operands)
    return res[:len(out_defs)], res[len(out_defs):]


def _log2_fail(z2):
    nz = -z2
    return jnp.minimum(nz, 0.0) - jnp.log(1.0 + jnp.exp2(jnp.minimum(z2, nz))) * LOG2E


def _sb_fused_kernel(tq, nq, ppc, n_pages, t_new, page_base,
                     pt_ref, bias_ref, qa_ref, qb_ref, k_ref, v_ref, ga_ref, gb_ref, uu_ref,
                     qs_ref, kn_ref, vn_ref, gs_ref, uo_ref, ck_hbm, cv_hbm,
                     op_ref, os_ref, acc_ref, c_ref, accs_ref, cs_ref, kbuf, vbuf, sem):
    step = pl.program_id(0)
    n_steps = pl.num_programs(0)
    pair = lax.rem(step, nq // 2)
    i1 = pair
    i2 = nq - 1 - pair
    chunks_per_seq = n_pages // ppc
    seq_steps = chunks_per_seq // nq
    part = lax.rem(step, seq_steps)
    n_chunks = n_steps * nq

    def chunk_copies(chunk):
        slot = lax.rem(chunk, SAMPLE_BUFS)
        seq = lax.div(chunk, chunks_per_seq)
        first_pos = (n_pages - 1) - lax.rem(chunk, chunks_per_seq) * ppc
        copies = []
        for j in range(ppc):
            page = page_base + pt_ref[seq * n_pages + first_pos - j]
            copies.append(pltpu.make_async_copy(ck_hbm.at[page], kbuf.at[slot, j], sem.at[slot, 0]))
            copies.append(pltpu.make_async_copy(cv_hbm.at[page], vbuf.at[slot, j], sem.at[slot, 1]))
        return copies

    def start_chunk(chunk):
        for cp in chunk_copies(chunk):
            cp.start()

    def wait_chunk(chunk):
        slot = lax.rem(chunk, SAMPLE_BUFS)
        pltpu.make_async_copy(ck_hbm.at[pl.ds(0, ppc)], kbuf.at[slot], sem.at[slot, 0]).wait()
        pltpu.make_async_copy(cv_hbm.at[pl.ds(0, ppc)], vbuf.at[slot], sem.at[slot, 1]).wait()

    def start_next(chunk):
        @pl.when(chunk + SAMPLE_BUFS < n_chunks)
        def _():
            start_chunk(chunk + SAMPLE_BUFS)

    @pl.when(step == 0)
    def _():
        for c in range(SAMPLE_BUFS):
            start_chunk(jnp.int32(c))

    lanes = PAGE_SIZE * SB_HEADS
    n_blk = lanes // HEAD_DIM
    lane = lax.broadcasted_iota(jnp.int32, (t_new, lanes), 1)
    lane_head = lane & (SB_HEADS - 1)
    biases = [bias_ref[h] * LOG2E for h in range(SB_HEADS)]
    bias_lanes = jnp.full((t_new, lanes), biases[0], F32)
    for h in range(1, SB_HEADS):
        bias_lanes = jnp.where(lane_head == h, biases[h], bias_lanes)
    qs = qs_ref[...]
    q_all = jnp.concatenate([qs[:, HEAD_DIM * h:HEAD_DIM * (h + 1)] for h in range(SB_HEADS)],
                            axis=0).astype(BF16)

    def page_scores(kpages, n, mask):
        s_cat = _dot_nt(q_all, kpages.astype(BF16))
        return [scores(s_cat[:, lanes * p:lanes * (p + 1)], mask) for p in range(n)]

    def scores(s_all, mask):
        sc = s_all[0:t_new, :]
        for h in range(1, SB_HEADS):
            sc = jnp.where(lane_head == h, s_all[t_new * h:t_new * (h + 1), :], sc)
        z2 = sc * (ATTN_SCALE * LOG2E) + bias_lanes
        lf = _log2_fail(z2)
        if mask is not None:
            lf = jnp.where(mask, lf, 0.0)
        blocks = jnp.concatenate([lf[:, HEAD_DIM * j:HEAD_DIM * (j + 1)] for j in range(n_blk)],
                                 axis=0)
        hi, lo = _split2(blocks)
        return z2, jnp.concatenate([hi, lo], axis=1)

    def stacked_sums(operands, matrix):
        rows = operands[0].shape[0]
        res = _dot(jnp.concatenate(operands, axis=0), matrix)
        return [res[rows * i:rows * (i + 1), :] for i in range(len(operands))]

    def weights(z2, res, mask, run):
        ws = [None] * n_blk
        for j in reversed(range(n_blk)):
            rows = slice(t_new * j, t_new * (j + 1))
            logw = z2[:, HEAD_DIM * j:HEAD_DIM * (j + 1)] + res[rows, :HEAD_DIM]
            if run is not None:
                logw = logw + run
            ws[j] = jnp.exp2(logw)
            tot = res[rows, HEAD_DIM:]
            run = tot if run is None else run + tot
        w = jnp.concatenate(ws, axis=1)
        if mask is not None:
            w = jnp.where(mask, w, 0.0)
        w_all = jnp.concatenate([jnp.where(lane_head == h, w, 0.0) for h in range(SB_HEADS)],
                                axis=0).astype(BF16)
        return w_all, run

    def new_keys():
        mask = (lane >> SB_HEAD_BITS) < lax.broadcasted_iota(jnp.int32, (t_new, lanes), 0)
        (z2, hilo), = page_scores(kn_ref[...], 1, mask)
        res, = stacked_sums([hilo], uo_ref[...])
        w_all, run = weights(z2, res, mask, None)
        cs_ref[...] = run
        accs_ref[...] = _dot(w_all, vn_ref[...].astype(BF16))

    uu = uu_ref[...]
    cols = [slice(HEAD_DIM * h, HEAD_DIM * (h + 1)) for h in range(SB_HEADS)]

    def section(q_ref, start, diag, first, chunk):
        kb = k_ref[pl.ds(start, tq), :]
        vb = v_ref[pl.ds(start, tq), :]
        n_pages_here = 0 if chunk is None else ppc
        slot = None if chunk is None else lax.rem(chunk, SAMPLE_BUFS)
        if n_pages_here:
            run = cs_ref[...]
            acc = accs_ref[...]
        p_parts, s_parts = {}, {}
        n_idx = max(SB_HEADS, n_pages_here)

        raw = [_dot_nt(q_ref[:, cols[h]], kb[:, cols[h]]) for h in range(SB_HEADS)]
        if n_pages_here:
            s_split = page_scores(kbuf[slot].reshape(ppc * lanes, HEAD_DIM), ppc, None)
        split = []
        for h in range(SB_HEADS):
            z2 = raw[h] * (ATTN_SCALE * LOG2E) + biases[h]
            lf = _log2_fail(z2)
            if diag is not None:
                lf = jnp.where(diag, lf, 0.0)
            hi, lo = _split2(lf)
            split.append((z2, jnp.concatenate([hi, lo], axis=1)))
        incl = stacked_sums([hilo for _, hilo in split], uu)
        for h in range(SB_HEADS):
            p_parts[h] = (split[h][0], incl[h])
        if n_pages_here:
            res = stacked_sums([hilo for _, hilo in s_split], uo_ref[...])
            for idx in range(n_pages_here):
                s_parts[idx] = (s_split[idx][0], res[idx])
        w_pages = []
        for idx in range(n_idx):
            if idx < SB_HEADS:
                z2, incl_h = p_parts[idx]
                logw = z2 + incl_h
                if not first:
                    logw = logw + c_ref[idx]
                w = jnp.exp2(logw)
                if diag is not None:
                    w = jnp.where(diag, w, 0.0)
                pv = _dot(w.astype(BF16), vb[:, cols[idx]])
                total = incl_h[:, 0:1]
                if first:
                    acc_ref[:, cols[idx]] = pv
                    c_ref[idx] = total
                else:
                    acc_ref[:, cols[idx]] += pv
                    c_ref[idx] += total
            if idx < n_pages_here:
                w_all, run = weights(*s_parts[idx], None, run)
                w_pages.append(w_all)
        if n_pages_here:
            cs_ref[...] = run
            accs_ref[...] = acc + _dot(jnp.concatenate(w_pages, axis=1),
                                       vbuf[slot].reshape(ppc * lanes, HEAD_DIM).astype(BF16))

    def finish(i, g_ref):
        rows = pl.ds(pl.multiple_of(i * tq, tq), tq)
        op_ref[rows, :] = (acc_ref[...] * g_ref[...]).astype(op_ref.dtype)

    row = lax.broadcasted_iota(jnp.int32, (tq, tq), 0)
    col = lax.broadcasted_iota(jnp.int32, (tq, tq), 1)
    diag = col < row
    chunk0 = step * nq

    @pl.when(part == 0)
    def _():
        new_keys()

    section(qa_ref, pl.multiple_of(i1 * tq, tq), diag, True, None)

    def body_a(t, carry):
        chunk = chunk0 + t
        wait_chunk(chunk)
        section(qa_ref, pl.multiple_of((i1 - 1 - t) * tq, tq), None, False, chunk)
        start_next(chunk)
        return carry

    lax.fori_loop(0, i1, body_a, 0)
    finish(i1, ga_ref)

    chunk = chunk0 + i1
    wait_chunk(chunk)
    section(qb_ref, pl.multiple_of(i2 * tq, tq), diag, True, chunk)
    start_next(chunk)

    def body_b(t, carry):
        chunk = chunk0 + i1 + 1 + t
        wait_chunk(chunk)
        section(qb_ref, pl.multiple_of((i2 - 1 - t) * tq, tq), None, False, chunk)
        start_next(chunk)
        return carry

    lax.fori_loop(0, i2, body_b, 0)
    finish(i2, gb_ref)

    @pl.when(part == seq_steps - 1)
    def _():
        g = gs_ref[...]
        for h in range(SB_HEADS):
            os_ref[:, cols[h]] = accs_ref[t_new * h:t_new * (h + 1), :] * g[:, cols[h]]


def _sb_fused(q, k, v, g, batch, seq, tq, q_s, k_new, v_new, g_s, cache_k, cache_v, layer,
              page_table, sb_bias, t_new):
    n_seq, n_pages = page_table.shape
    n_pool = cache_k.shape[1]
    nq = seq // tq
    n_steps = batch * (nq // 2)
    assert nq % 2 == 0 and (n_seq * n_pages) % (n_steps * nq) == 0
    ppc = n_seq * n_pages // (n_steps * nq)
    assert n_pages % (ppc * nq) == 0
    seq_steps = n_pages // (ppc * nq)
    page_rows = PAGE_SIZE * SB_HEADS

    q3, k3, v3, g3 = (a.reshape(batch, seq, SB_WIDTH) for a in (q, k, v, g))
    u = (jnp.arange(tq)[:, None] >= jnp.arange(tq)[None, :]).astype(BF16)
    uu = jnp.concatenate([u, u], axis=0)

    def as_page(a):
        a = a.reshape(n_seq, t_new, SB_HEADS, HEAD_DIM)
        a = jnp.pad(a, ((0, 0), (0, PAGE_SIZE - t_new), (0, 0), (0, 0)))
        return a.reshape(n_seq * page_rows, HEAD_DIM)

    cache_k = cache_k.reshape(-1, page_rows, HEAD_DIM)
    cache_v = cache_v.reshape(-1, page_rows, HEAD_DIM)
    r = jnp.arange(HEAD_DIM)
    same_head = (r[:, None] % SB_HEADS) == (r[None, :] % SB_HEADS)
    not_earlier = (r[:, None] // SB_HEADS) >= (r[None, :] // SB_HEADS)
    uo = jnp.concatenate([same_head & not_earlier, same_head], axis=1).astype(BF16)
    uo = jnp.concatenate([uo, uo], axis=0)

    half = nq // 2
    b_of = lambda s: s // half
    qa_map = lambda s, pt: (b_of(s), s % half, 0)
    qb_map = lambda s, pt: (b_of(s), nq - 1 - s % half, 0)
    seq_map = lambda s, pt: (b_of(s), 0, 0)
    samp_map = lambda s, pt: (s // seq_steps, 0)
    const = lambda s, pt: (0, 0)
    kern = functools.partial(_sb_fused_kernel, tq, nq, ppc, n_pages, t_new, layer * n_pool)
    grid_spec = pltpu.PrefetchScalarGridSpec(
        num_scalar_prefetch=1,
        grid=(n_steps,),
        in_specs=[
            pl.BlockSpec(memory_space=pltpu.SMEM),
            pl.BlockSpec((None, tq, SB_WIDTH), qa_map),
            pl.BlockSpec((None, tq, SB_WIDTH), qb_map),
            pl.BlockSpec((None, seq, SB_WIDTH), seq_map, pipeline_mode=pl.Buffered(1)),
            pl.BlockSpec((None, seq, SB_WIDTH), seq_map, pipeline_mode=pl.Buffered(1)),
            pl.BlockSpec((None, tq, SB_WIDTH), qa_map),
            pl.BlockSpec((None, tq, SB_WIDTH), qb_map),
            pl.BlockSpec((2 * tq, tq), const),
            pl.BlockSpec((t_new, SB_WIDTH), samp_map),
            pl.BlockSpec((page_rows, HEAD_DIM), samp_map),
            pl.BlockSpec((page_rows, HEAD_DIM), samp_map),
            pl.BlockSpec((t_new, SB_WIDTH), samp_map),
            pl.BlockSpec((2 * HEAD_DIM, 2 * HEAD_DIM), const),
            pl.BlockSpec(memory_space=pl.ANY),
            pl.BlockSpec(memory_space=pl.ANY),
        ],
        out_specs=[
            pl.BlockSpec((None, seq, SB_WIDTH), seq_map, pipeline_mode=pl.Buffered(1)),
            pl.BlockSpec((t_new, SB_WIDTH), samp_map),
        ],
        scratch_shapes=[
            pltpu.VMEM((tq, SB_WIDTH), F32),
            pltpu.VMEM((SB_HEADS, tq, 1), F32),
            pltpu.VMEM((SB_HEADS * t_new, HEAD_DIM), F32),
            pltpu.VMEM((t_new, HEAD_DIM), F32),
            pltpu.VMEM((SAMPLE_BUFS, ppc, page_rows, HEAD_DIM), F32),
            pltpu.VMEM((SAMPLE_BUFS, ppc, page_rows, HEAD_DIM), F32),
            pltpu.SemaphoreType.DMA((SAMPLE_BUFS, 2)),
        ],
    )
    out_p, out_s = pl.pallas_call(
        kern,
        grid_spec=grid_spec,
        out_shape=[jax.ShapeDtypeStruct((batch, seq, SB_WIDTH), BF16),
                   jax.ShapeDtypeStruct((n_seq * t_new, SB_WIDTH), F32)],
        compiler_params=pltpu.CompilerParams(
            dimension_semantics=("arbitrary",), vmem_limit_bytes=VMEM_LIMIT_FUSED),
        name="sb_fused",
    )(page_table.reshape(-1), sb_bias, q3, q3, k3, v3, g3, g3, uu,
      q_s, as_page(k_new), as_page(v_new), g_s, uo, cache_k, cache_v)
    return out_p.reshape(batch * seq, SB_WIDTH), out_s


def _ssd_kernel(length, n_par, *refs):
    per_seq_in, shared, per_seq_out = refs[:5], refs[5:13], refs[13:]
    pre_ref, h0_ref = per_seq_in[3:5]
    ext_ref, st_ref = per_seq_out[3:5]

    @pl.when(pl.program_id(1) == 0)
    def _():
        ext_ref[:, 0:SUBLANES, :] = pre_ref[...]
        st_ref[...] = h0_ref[...]

    for s in range(n_par):
        _ssd_chunk(length, *[r.at[s] for r in per_seq_in[:3]], *shared,
                   *[r.at[s] for r in per_seq_out])


def _pad_rows(a, rows):
    if a.shape[0] == rows:
        return a
    return jnp.concatenate([a, jnp.zeros((rows - a.shape[0], a.shape[1]), a.dtype)], axis=0)


def _ssd_chunk(length, xbc_ref, z_ref, dt_ref, cw_ref, cb_ref, dtb_ref,
               alog_ref, dsk_ref, nw_ref, ltri_ref, e_ref, out_ref, cnew_ref, snew_ref,
               ext_ref, st_ref):
    L = SSD_CHUNK
    P = SSD_HEAD_DIM

    ext_ref[SUBLANES:SUBLANES + L, :] = _pad_rows(xbc_ref[...], L)
    cw = cw_ref[...]
    conv = cb_ref[...]
    for j in range(CONV_WIDTH):
        off = SUBLANES - (CONV_WIDTH - 1) + j
        conv = conv + ext_ref[off:off + L, :] * cw[j:j + 1, :]
    act = _silu(conv)
    tail = ext_ref[length:length + SUBLANES, :]
    cnew_ref[...] = tail
    ext_ref[0:SUBLANES, :] = tail

    xs = act[:, :SSD_WIDTH]
    bm = act[:, SSD_WIDTH:SSD_WIDTH + SSD_GROUPS * SSD_STATE]
    cm = act[:, SSD_WIDTH + SSD_GROUPS * SSD_STATE:]

    x_dt = _pad_rows(dt_ref[...], L) + dtb_ref[...]
    dt = jnp.maximum(x_dt, 0.0) + jnp.log1p(jnp.exp(-jnp.abs(x_dt)))
    if length < L:
        valid = lax.broadcasted_iota(jnp.int32, dt.shape, 0) < length
        dt = jnp.where(valid, dt, 0.0)
    da = dt * (-jnp.exp(alog_ref[...]))
    cs = _dot_exact_rhs(ltri_ref[...], da)
    cs_t = cs.T
    e = e_ref[...]
    dt_x, cs_x = _dots_exact_lhs([dt, cs], e)
    xdt = xs * dt_x
    ecs = jnp.exp(cs_x)
    xw_t = (xdt * jnp.exp(cs_x[L - 1:L, :] - cs_x)).T
    xdt16 = xdt.astype(BF16)

    row = lax.broadcasted_iota(jnp.int32, (L, L), 0)
    col = lax.broadcasted_iota(jnp.int32, (L, L), 1)
    causal = col <= row
    heads_per_group = SSD_HEADS // SSD_GROUPS
    gw = heads_per_group * P
    y_diag, y_off = [], []
    for g in range(SSD_GROUPS):
        bg = bm[:, SSD_STATE * g:SSD_STATE * (g + 1)].astype(BF16)
        cg = cm[:, SSD_STATE * g:SSD_STATE * (g + 1)].astype(BF16)
        cb = _dot_nt(cg, bg)
        prev = st_ref[gw * g:gw * (g + 1), :]
        y_off.append(_dot_nt(cg, prev.astype(BF16)))
        new = _dot(xw_t[gw * g:gw * (g + 1), :].astype(BF16), bg)
        for r in range(heads_per_group):
            h = heads_per_group * g + r
            seg = cs[:, h:h + 1] - cs_t[h:h + 1, :]
            decay = jnp.exp(jnp.where(causal, seg, -jnp.inf))
            y_diag.append(_dot((cb * decay).astype(BF16), xdt16[:, P * h:P * (h + 1)]))
            chunk_decay = jnp.exp(cs[L - 1:L, h:h + 1])
            st_ref[P * h:P * (h + 1), :] = (prev[P * r:P * (r + 1), :] * chunk_decay
                                            + new[P * r:P * (r + 1), :])
    snew_ref[...] = st_ref[...]
    y = (jnp.concatenate(y_diag, axis=1) + jnp.concatenate(y_off, axis=1) * ecs
         + xs * dsk_ref[...])
    gated = y[:length] * z_ref[...]
    ms = jnp.mean(gated * gated, axis=-1, keepdims=True)
    out_ref[...] = (gated * lax.rsqrt(ms + EPS) * nw_ref[...]).astype(out_ref.dtype)


def _ssd(xbc, zdt, prefix, h0, conv_w, conv_b, dt_bias, a_log, d_skip, ssd_norm_w,
         batch, n_chunks, length, out_dtype):
    L = SSD_CHUNK
    pre = jnp.pad(prefix, ((0, 0), (SUBLANES - (CONV_WIDTH - 1), 0), (0, 0)))
    pad_h = lambda a: jnp.pad(a, (0, HEAD_DIM - SSD_HEADS)).reshape(1, HEAD_DIM)
    ltri = (jnp.arange(L)[:, None] >= jnp.arange(L)[None, :]).astype(BF16)
    expand = (jnp.arange(HEAD_DIM)[:, None] == jnp.arange(SSD_WIDTH)[None, :] // SSD_HEAD_DIM
              ).astype(BF16)
    dsk = jnp.repeat(d_skip, SSD_HEAD_DIM).reshape(1, SSD_WIDTH)
    z_blk = SSD_WIDTH // HEAD_DIM
    n_par = SSD_SEQS_PER_STEP
    assert batch % n_par == 0 and length <= L and (length == L or n_chunks == 1)
    rows = n_chunks * length
    xbc3 = xbc.reshape(batch, rows, XBC_WIDTH)
    zdt3 = zdt.reshape(batch, rows, ZDT_WIDTH)
    const = lambda b, c: (0, 0)
    seq = lambda b, c: (b, 0, 0)
    kern = functools.partial(_ssd_kernel, length, n_par)
    out, conv_new, ssm_new = pl.pallas_call(
        kern,
        grid=(batch // n_par, n_chunks),
        in_specs=[
            pl.BlockSpec((n_par, length, XBC_WIDTH), lambda b, c: (b, c, 0)),
            pl.BlockSpec((n_par, length, SSD_WIDTH), lambda b, c: (b, c, 0)),
            pl.BlockSpec((n_par, length, HEAD_DIM), lambda b, c: (b, c, z_blk)),
            pl.BlockSpec((n_par, SUBLANES, XBC_WIDTH), seq),
            pl.BlockSpec((n_par, SSD_WIDTH, SSD_STATE), seq),
            pl.BlockSpec((CONV_WIDTH, XBC_WIDTH), const),
            pl.BlockSpec((1, XBC_WIDTH), const),
            pl.BlockSpec((1, HEAD_DIM), const),
            pl.BlockSpec((1, HEAD_DIM), const),
            pl.BlockSpec((1, SSD_WIDTH), const),
            pl.BlockSpec((1, SSD_WIDTH), const),
            pl.BlockSpec((L, L), const),
            pl.BlockSpec((HEAD_DIM, SSD_WIDTH), const),
        ],
        out_specs=[
            pl.BlockSpec((n_par, length, SSD_WIDTH), lambda b, c: (b, c, 0)),
            pl.BlockSpec((n_par, SUBLANES, XBC_WIDTH), seq),
            pl.BlockSpec((n_par, SSD_WIDTH, SSD_STATE), seq),
        ],
        out_shape=[
            jax.ShapeDtypeStruct((batch, rows, SSD_WIDTH), out_dtype),
            jax.ShapeDtypeStruct((batch, SUBLANES, XBC_WIDTH), F32),
            jax.ShapeDtypeStruct((batch, SSD_WIDTH, SSD_STATE), F32),
        ],
        scratch_shapes=[pltpu.VMEM((n_par, SUBLANES + L, XBC_WIDTH), F32),
                        pltpu.VMEM((n_par, SSD_WIDTH, SSD_STATE), F32)],
        compiler_params=pltpu.CompilerParams(
            dimension_semantics=("parallel", "arbitrary"), vmem_limit_bytes=VMEM_LIMIT),
        name="ssd_scan",
    )(xbc3, zdt3, zdt3, pre, h0.reshape(batch, SSD_WIDTH, SSD_STATE), conv_w,
      conv_b.reshape(1, XBC_WIDTH), pad_h(dt_bias), pad_h(a_log), dsk,
      ssd_norm_w.reshape(1, SSD_WIDTH), ltri, expand)
    return out.reshape(batch * rows, SSD_WIDTH), conv_new, ssm_new


def _mem_attn_kernel(q_ref, g_ref, k_ref, v_ref, o_ref):
    for h in range(MEM_HEADS):
        cols = slice(HEAD_DIM * h, HEAD_DIM * (h + 1))
        s = _dot_nt(q_ref[:, cols].astype(BF16), k_ref[:, cols].astype(BF16)) * ATTN_SCALE
        p = jnp.exp(s - jnp.max(s, axis=-1, keepdims=True))
        den = jnp.sum(p, axis=-1, keepdims=True)
        o = _dot(p.astype(BF16), v_ref[:, cols].astype(BF16)) / den
        o_ref[:, cols] = (o * g_ref[:, cols]).astype(o_ref.dtype)


def _mem_attn(q, g, mem_k, mem_v, batch, t, tq, out_dtype):
    nq = t // tq
    return pl.pallas_call(
        _mem_attn_kernel,
        grid=(batch, nq),
        in_specs=[
            pl.BlockSpec((tq, MEM_WIDTH), lambda b, i: (b * nq + i, 0)),
            pl.BlockSpec((tq, MEM_WIDTH), lambda b, i: (b * nq + i, 0)),
            pl.BlockSpec((None, MEM_TOKENS, MEM_WIDTH), lambda b, i: (b, 0, 0)),
            pl.BlockSpec((None, MEM_TOKENS, MEM_WIDTH), lambda b, i: (b, 0, 0)),
        ],
        out_specs=pl.BlockSpec((tq, MEM_WIDTH), lambda b, i: (b * nq + i, 0)),
        out_shape=jax.ShapeDtypeStruct((batch * t, MEM_WIDTH), out_dtype),
        compiler_params=pltpu.CompilerParams(
            dimension_semantics=("parallel", "parallel"), vmem_limit_bytes=VMEM_LIMIT),
        name="mem_attn",
    )(q, g, mem_k, mem_v)


def _mem_attn_rows_kernel(t, q_ref, g_ref, k_ref, v_ref, o_ref):
    q = q_ref[...]
    g = g_ref[...]
    cols = [slice(HEAD_DIM * h, HEAD_DIM * (h + 1)) for h in range(MEM_HEADS)]
    q_all = jnp.concatenate([q[:, c] for c in cols], axis=0).astype(BF16)
    s = _dot_nt(q_all, k_ref[...].astype(BF16)) * ATTN_SCALE
    n_rows = s.shape[1]
    row_head = jnp.concatenate([jnp.full((t, n_rows), h, jnp.int32) for h in range(MEM_HEADS)],
                               axis=0)
    lane_head = lax.broadcasted_iota(jnp.int32, s.shape, 1) & (MEM_HEADS - 1)
    s = jnp.where(row_head == lane_head, s, -jnp.inf)
    p = jnp.exp(s - jnp.max(s, axis=-1, keepdims=True))
    den = jnp.sum(p, axis=-1, keepdims=True)
    o = _dot(p.astype(BF16), v_ref[...].astype(BF16)) / den
    for h in range(MEM_HEADS):
        o_ref[:, cols[h]] = (o[t * h:t * (h + 1), :] * g[:, cols[h]]).astype(o_ref.dtype)


def _mem_attn_rows(q, g, mem_k, mem_v, batch, t, out_dtype):
    rows = MEM_TOKENS * MEM_HEADS
    kern = functools.partial(_mem_attn_rows_kernel, t)
    return pl.pallas_call(
        kern,
        grid=(batch,),
        in_specs=[
            pl.BlockSpec((t, MEM_WIDTH), lambda b: (b, 0)),
            pl.BlockSpec((t, MEM_WIDTH), lambda b: (b, 0)),
            pl.BlockSpec((None, rows, HEAD_DIM), lambda b: (b, 0, 0)),
            pl.BlockSpec((None, rows, HEAD_DIM), lambda b: (b, 0, 0)),
        ],
        out_specs=pl.BlockSpec((t, MEM_WIDTH), lambda b: (b, 0)),
        out_shape=jax.ShapeDtypeStruct((batch * t, MEM_WIDTH), out_dtype),
        compiler_params=pltpu.CompilerParams(
            dimension_semantics=("parallel",), vmem_limit_bytes=VMEM_LIMIT),
        name="mem_attn_rows",
    )(q, g, mem_k, mem_v)


def _out_proj_kernel(x_ref, sb_ref, ssd_ref, mo_ref, w_ref, o_ref):
    mix = jnp.concatenate([sb_ref[...].astype(BF16), ssd_ref[...].astype(BF16),
                           mo_ref[...].astype(BF16)], axis=-1)
    for c in range(0, o_ref.shape[1], PROJ_CHUNK):
        cols = slice(c, c + PROJ_CHUNK)
        o_ref[:, cols] = x_ref[:, cols] + _dot(mix, w_ref[:, cols].astype(BF16))


def _out_proj(x, sb, ssd, mo, w_out, tm, tn):
    t, d = x.shape
    return pl.pallas_call(
        _out_proj_kernel,
        grid=(t // tm, d // tn),
        in_specs=[
            pl.BlockSpec((tm, tn), lambda m, n: (m, n)),
            pl.BlockSpec((tm, SB_WIDTH), lambda m, n: (m, 0)),
            pl.BlockSpec((tm, SSD_WIDTH), lambda m, n: (m, 0)),
            pl.BlockSpec((tm, MEM_WIDTH), lambda m, n: (m, 0)),
            pl.BlockSpec((w_out.shape[0], tn), lambda m, n: (0, n), pipeline_mode=pl.Buffered(1)),
        ],
        out_specs=pl.BlockSpec((tm, tn), lambda m, n: (m, n)),
        out_shape=jax.ShapeDtypeStruct((t, d), F32),
        compiler_params=pltpu.CompilerParams(
            dimension_semantics=("parallel", "arbitrary"), vmem_limit_bytes=VMEM_LIMIT),
        name="out_proj",
    )(x, sb, ssd, mo, w_out)


def _in_proj_plan(act_dtype):
    lowp = act_dtype == BF16
    outs, plan = [], []

    def add(width, dtype):
        outs.append((width, dtype))
        return len(outs) - 1

    q = add(SB_WIDTH, act_dtype)
    plan.append(((0, SB_WIDTH, 0, None if lowp else q, q if lowp else None),))
    k32 = add(SB_WIDTH, F32)
    k16 = add(SB_WIDTH, BF16) if lowp else None
    plan.append(((0, SB_WIDTH, 1, k32, k16),))
    v32 = add(SB_WIDTH, F32)
    v16 = add(SB_WIDTH, BF16) if lowp else None
    plan.append(((0, SB_WIDTH, None, v32, v16),))
    g = add(SB_WIDTH, F32)
    plan.append(((0, SB_WIDTH, None, g, None, SB_WIDTH),))
    xbc = add(XBC_WIDTH, F32)
    plan.append(((0, XBC_WIDTH, None, xbc, None),))
    zdt = add(ZDT_WIDTH, F32)
    plan.append(((0, ZDT_WIDTH, None, zdt, None, SSD_WIDTH),))
    mq = add(MEM_WIDTH, act_dtype)
    mg = add(MEM_WIDTH, F32)
    plan.append(((0, MEM_WIDTH, 2, None if lowp else mq, mq if lowp else None),
                 (MEM_WIDTH, MEM_WIDTH, None, mg, None, MEM_WIDTH)))
    names = dict(q=q, k32=k32, k16=k16, v32=v32, v16=v16, g=g, xbc=xbc, zdt=zdt, mq=mq, mg=mg)
    return tuple(plan), outs, names


_O_Z = 4 * SB_WIDTH
_O_XBC = _O_Z + SSD_WIDTH
_O_DT = _O_XBC + XBC_WIDTH
_O_MEM = _O_DT + SSD_HEADS
_IN_WIDTH = _O_MEM + 2 * MEM_WIDTH
_CAT_WIDTH = 7 * PROJ_TN
W_PREP_COLS = 256
BF16_ROWS = 16


def _w_prep_kernel(w_ref, o_ref):
    def put(dst, src, rows):
        o_ref[dst:dst + rows, :] = w_ref[src:src + rows, :].astype(BF16)

    cols = w_ref.shape[1]
    put(0, 0, _O_Z)
    put(_O_Z, _O_XBC, XBC_WIDTH)
    put(_O_Z + XBC_WIDTH, _O_Z, SSD_WIDTH)
    dt0 = _O_Z + XBC_WIDTH + SSD_WIDTH
    o_ref[dt0:dt0 + BF16_ROWS, :] = jnp.concatenate(
        [w_ref[_O_DT:_O_MEM, :], jnp.zeros((BF16_ROWS - SSD_HEADS, cols), F32)], axis=0
    ).astype(BF16)
    o_ref[dt0 + BF16_ROWS:6 * PROJ_TN, :] = jnp.zeros((6 * PROJ_TN - dt0 - BF16_ROWS, cols), BF16)
    put(6 * PROJ_TN, _O_MEM, 2 * MEM_WIDTH)


def _rearranged_w_in(w_t):
    d = w_t.shape[1]
    assert w_t.shape[0] == _IN_WIDTH and d % W_PREP_COLS == 0
    return pl.pallas_call(
        _w_prep_kernel,
        grid=(d // W_PREP_COLS,),
        in_specs=[pl.BlockSpec((_IN_WIDTH, W_PREP_COLS), lambda c: (0, c))],
        out_specs=pl.BlockSpec((_CAT_WIDTH, W_PREP_COLS), lambda c: (0, c)),
        out_shape=jax.ShapeDtypeStruct((_CAT_WIDTH, d), BF16),
        compiler_params=pltpu.CompilerParams(
            dimension_semantics=("parallel",), vmem_limit_bytes=VMEM_LIMIT),
        name="w_prep",
    )(w_t)


def kernel(x_prompt, x_sample, cache_sb_k, cache_sb_v, state_ssm, state_conv, cache_mem_k,
           cache_mem_v, page_table, mem_prompt, norm_w, w_in, sb_q_norm, sb_k_norm, sb_bias,
           conv_w, conv_b, dt_bias, a_log, d_skip, ssd_norm_w, mem_norm_w, w_mem_kv, mem_q_norm,
           mem_k_norm, w_out):
    depth = w_in.shape[0]
    assert depth == 1
    layer = 0
    bp, sp, d = x_prompt.shape
    bs, ts, _ = x_sample.shape
    n_pool = cache_sb_k.shape[1]
    L = SSD_CHUNK

    w_cat = _rearranged_w_in(w_in[layer].T)
    w_o = w_out[layer]
    head_norms = jnp.concatenate(
        [sb_q_norm[layer][None], sb_k_norm[layer][None], mem_q_norm[layer][None],
         mem_k_norm[layer][None], jnp.zeros((SUBLANES - 4, HEAD_DIM), F32)], axis=0)
    ssd_params = (conv_w[layer], conv_b[layer], dt_bias[layer], a_log[layer], d_skip[layer],
                  ssd_norm_w[layer])

    xp = x_prompt.reshape(bp * sp, d)
    mem_plan = (((0, MEM_WIDTH, 3, 0, None), (MEM_WIDTH, MEM_WIDTH, None, 1, None)),)
    (mk, mv), _ = _proj(mem_prompt.reshape(bp * MEM_TOKENS, d), mem_norm_w[layer],
                        w_mem_kv[layer], head_norms, mem_plan,
                        [(MEM_WIDTH, F32), (MEM_WIDTH, F32)], tm=PROJ_TM,
                        w_rows_are_outputs=False)
    plan, outs, nm = _in_proj_plan(BF16)
    xs = x_sample.reshape(bs * ts, d)
    plan_s, outs_s, ns = _in_proj_plan(F32)
    pr, ps = _proj(xp, norm_w[layer], w_cat, head_norms, plan, outs, tm=PROJ_TM,
                   side=(xs, plan_s, outs_s))
    sb, sb_s = _sb_fused(pr[nm['q']], pr[nm['k16']], pr[nm['v16']], pr[nm['g']], bp, sp, ATTN_TQ,
                         ps[ns['q']], ps[ns['k32']], ps[ns['v32']], ps[ns['g']],
                         cache_sb_k, cache_sb_v, layer, page_table, sb_bias[layer], ts)
    ssd, conv_p, ssm_p = _ssd(
        pr[nm['xbc']], pr[nm['zdt']], jnp.zeros((bp, CONV_WIDTH - 1, XBC_WIDTH), F32),
        jnp.zeros((bp, SSD_HEADS, SSD_HEAD_DIM, SSD_STATE), F32), *ssd_params,
        batch=bp, n_chunks=sp // L, length=L, out_dtype=BF16)
    mo = _mem_attn(pr[nm['mq']], pr[nm['mg']], mk.reshape(bp, MEM_TOKENS, MEM_WIDTH),
                   mv.reshape(bp, MEM_TOKENS, MEM_WIDTH), bp, sp, tq=MEM_TQ, out_dtype=BF16)
    yp = _out_proj(xp, sb, ssd, mo, w_o, tm=OUT_TM, tn=d)

    ssd_s, conv_s, ssm_s = _ssd(
        ps[ns['xbc']], ps[ns['zdt']], state_conv[layer], state_ssm[layer],
        *ssd_params, batch=bs, n_chunks=1, length=ts, out_dtype=F32)
    mo_s = _mem_attn_rows(ps[ns['mq']], ps[ns['mg']],
                          cache_mem_k[layer].reshape(bs, MEM_TOKENS * MEM_HEADS, HEAD_DIM),
                          cache_mem_v[layer].reshape(bs, MEM_TOKENS * MEM_HEADS, HEAD_DIM),
                          bs, ts, out_dtype=F32)
    ys = _out_proj(xs, sb_s, ssd_s, mo_s, w_o, tm=bs * ts, tn=d)

    tail = slice(SUBLANES - (CONV_WIDTH - 1), SUBLANES)
    return (
        yp.reshape(bp, sp, d),
        ys.reshape(bs, ts, d),
        pr[nm['k32']].reshape(1, bp, sp, SB_HEADS, HEAD_DIM),
        pr[nm['v32']].reshape(1, bp, sp, SB_HEADS, HEAD_DIM),
        ssm_p.reshape(1, bp, SSD_HEADS, SSD_HEAD_DIM, SSD_STATE),
        conv_p[:, tail][None],
        mk.reshape(1, bp, MEM_TOKENS, MEM_HEADS, HEAD_DIM),
        mv.reshape(1, bp, MEM_TOKENS, MEM_HEADS, HEAD_DIM),
        ps[ns['k32']].reshape(1, bs, ts, SB_HEADS, HEAD_DIM),
        ps[ns['v32']].reshape(1, bs, ts, SB_HEADS, HEAD_DIM),
        ssm_s.reshape(1, bs, SSD_HEADS, SSD_HEAD_DIM, SSD_STATE),
        conv_s[:, tail][None],
    )
```

```python
import functools
import math

import jax
import jax.numpy as jnp
from jax import lax
from jax.experimental import pallas as pl
from jax.experimental.pallas import tpu as pltpu

F32 = jnp.float32
BF16 = jnp.bfloat16

SB_HEADS = 8
HEAD_DIM = 128
SB_WIDTH = SB_HEADS * HEAD_DIM
SSD_HEADS = 8
SSD_HEAD_DIM = 64
SSD_WIDTH = SSD_HEADS * SSD_HEAD_DIM
SSD_GROUPS = 2
SSD_STATE = 128
CONV_WIDTH = 4
XBC_WIDTH = SSD_WIDTH + 2 * SSD_GROUPS * SSD_STATE
MEM_TOKENS = 256
MEM_HEADS = 4
MEM_WIDTH = MEM_HEADS * HEAD_DIM
PAGE_SIZE = 128
EPS = 1e-6
ATTN_SCALE = HEAD_DIM ** -0.5
LOG2E = math.log2(math.e)

SSD_CHUNK = 128
PROJ_TN = 1024
PROJ_CHUNK = 256
ZDT_WIDTH = SSD_WIDTH + PROJ_CHUNK
SAMPLE_BUFS = 3
SSD_SEQS_PER_STEP = 4
SUBLANES = 8
SB_HEAD_BITS = SB_HEADS.bit_length() - 1
NORM_ROW_CHUNK = 128
PROJ_TM = 512
OUT_TM = 512
ATTN_TQ = 256
MEM_TQ = 512
VMEM_LIMIT = 56 * 1024 * 1024
VMEM_LIMIT_FUSED = 62 * 1024 * 1024

_NT = (((1,), (1,)), ((), ()))


def _dot(a, b):
    return jnp.dot(a, b, preferred_element_type=F32)


def _dot_nt(a, b):
    return lax.dot_general(a, b, _NT, preferred_element_type=F32)


def _split2(x):
    hi = x.astype(BF16)
    lo = (x - hi.astype(F32)).astype(BF16)
    return hi, lo


def _split3(x):
    hi = x.astype(BF16)
    r = x - hi.astype(F32)
    mid = r.astype(BF16)
    lo = (r - mid.astype(F32)).astype(BF16)
    return hi, mid, lo


def _dots_exact_lhs(xs, m):
    rows = xs[0].shape[0]
    r = _dot(jnp.concatenate([t for x in xs for t in _split3(x)], axis=0), m)
    part = lambda i: r[rows * i:rows * (i + 1), :]
    return [part(3 * i) + part(3 * i + 1) + part(3 * i + 2) for i in range(len(xs))]


def _dot_exact_rhs(m, x):
    n = x.shape[1]
    r = _dot(m, jnp.concatenate(_split3(x), axis=1))
    return r[:, :n] + r[:, n:2 * n] + r[:, 2 * n:]


def _silu(x):
    return x * (1.0 / (1.0 + jnp.exp(-x)))


def _rmsnorm_rows(x_ref, nw_ref, h_ref):
    rows_total = x_ref.shape[0]
    rc = min(rows_total, NORM_ROW_CHUNK)

    def body(r, carry):
        rows = pl.ds(pl.multiple_of(r * rc, rc), rc)
        xv = x_ref[rows, :]
        ms = jnp.mean(xv * xv, axis=-1, keepdims=True)
        h_ref[rows, :] = (xv * lax.rsqrt(ms + EPS) * nw_ref[...]).astype(BF16)
        return carry

    lax.fori_loop(0, rows_total // rc, body, 0)


def _proj_step(segs, h_ref, w_ref, hn_ref, outs, w_rows_are_outputs):
    for col0, width, hn_row, o32, o16 in segs:
        for c0 in range(0, width, PROJ_CHUNK):
            chunk = slice(col0 + c0, col0 + c0 + PROJ_CHUNK)
            if w_rows_are_outputs:
                y = _dot_nt(h_ref[...], w_ref[chunk, :].astype(BF16))
            else:
                y = _dot(h_ref[...], w_ref[:, chunk].astype(BF16))
            for c in range(0, PROJ_CHUNK, HEAD_DIM):
                yc = y[:, c:c + HEAD_DIM]
                if hn_row is not None:
                    ms = jnp.mean(yc * yc, axis=-1, keepdims=True)
                    yc = yc * lax.rsqrt(ms + EPS) * hn_ref[hn_row:hn_row + 1, :]
                cols = slice(c0 + c, c0 + c + HEAD_DIM)
                if o32 is not None:
                    outs[o32][:, cols] = yc
                if o16 is not None:
                    outs[o16][:, cols] = yc.astype(BF16)


def _proj_kernel(plan, n_out, side_plan, n_side, w_rows_are_outputs, *refs):
    if side_plan is None:
        x_ref, nw_ref, w_ref, hn_ref = refs[:4]
        rest = refs[4:]
    else:
        x_ref, xs_ref, nw_ref, w_ref, hn_ref = refs[:5]
        rest = refs[5:]
    outs, side_outs = rest[:n_out], rest[n_out:n_out + n_side]
    scratch = rest[n_out + n_side:]
    h_ref = scratch[0]
    m = pl.program_id(0)
    n = pl.program_id(1)

    @pl.when(n == 0)
    def _():
        _rmsnorm_rows(x_ref, nw_ref, h_ref)

    if side_plan is not None:
        hs_ref = scratch[1]

        @pl.when((n == 0) & (m == 0))
        def _():
            _rmsnorm_rows(xs_ref, nw_ref, hs_ref)

    for step, segs in enumerate(plan):
        @pl.when(n == step)
        def _(step=step, segs=segs):
            _proj_step(segs, h_ref, w_ref, hn_ref, outs, w_rows_are_outputs)
            if side_plan is not None:
                @pl.when(m == 0)
                def _():
                    _proj_step(side_plan[step], hs_ref, w_ref, hn_ref, side_outs,
                               w_rows_are_outputs)


def _proj(x, norm_w, w, head_norms, plan, out_defs, tm, side=None, w_rows_are_outputs=True):
    t, d = x.shape
    n_steps = len(plan)
    w_shape = (n_steps * PROJ_TN, d) if w_rows_are_outputs else (d, n_steps * PROJ_TN)
    assert w.shape == w_shape and t % tm == 0
    w_spec = (pl.BlockSpec((PROJ_TN, d), lambda m, n: (n, 0)) if w_rows_are_outputs
              else pl.BlockSpec((d, PROJ_TN), lambda m, n: (0, n)))
    row = lambda m, n: (m, 0)
    const = lambda m, n: (0, 0)
    in_specs = [pl.BlockSpec((tm, d), row)]
    operands = [x]
    out_specs = [pl.BlockSpec((tm, w), row) for w, _ in out_defs]
    out_shape = [jax.ShapeDtypeStruct((t, w), dt) for w, dt in out_defs]
    scratch = [pltpu.VMEM((tm, d), BF16)]
    side_plan, n_side = None, 0
    if side is not None:
        x_side, side_plan, side_defs = side
        ts = x_side.shape[0]
        assert len(side_plan) == n_steps
        n_side = len(side_defs)
        in_specs.append(pl.BlockSpec((ts, d), const))
        operands.append(x_side)
        out_specs += [pl.BlockSpec((ts, w), const) for w, _ in side_defs]
        out_shape += [jax.ShapeDtypeStruct((ts, w), dt) for w, dt in side_defs]
        scratch.append(pltpu.VMEM((ts, d), BF16))
    in_specs += [pl.BlockSpec((1, d), const), w_spec,
                 pl.BlockSpec((SUBLANES, HEAD_DIM), const)]
    operands += [norm_w.reshape(1, d), w, head_norms]
    kern = functools.partial(_proj_kernel, plan, len(out_defs), side_plan, n_side,
                             w_rows_are_outputs)
    res = pl.pallas_call(
        kern,
        grid=(t // tm, n_steps),
        in_specs=in_specs,
        out_specs=out_specs,
        out_shape=out_shape,
        scratch_shapes=scratch,
        compiler_params=pltpu.CompilerParams(
            dimension_semantics=("arbitrary", "arbitrary"), vmem_limit_bytes=VMEM_LIMIT),
        name="norm_proj",
    )(*operands)
    return res[:len(out_defs)], res[len(out_defs):]


def _log2_fail(z2):
    nz = -z2
    return jnp.minimum(nz, 0.0) - jnp.log(1.0 + jnp.exp2(jnp.minimum(z2, nz))) * LOG2E


def _sb_fused_kernel(tq, nq, ppc, n_pages, t_new, page_base,
                     pt_ref, bias_ref, qa_ref, qb_ref, k_ref, v_ref, ga_ref, gb_ref, uu_ref,
                     qs_ref, kn_ref, vn_ref, gs_ref, uo_ref, ck_hbm, cv_hbm,
                     op_ref, os_ref, acc_ref, c_ref, accs_ref, cs_ref, kbuf, vbuf, sem):
    step = pl.program_id(0)
    n_steps = pl.num_programs(0)
    pair = lax.rem(step, nq // 2)
    i1 = pair
    i2 = nq - 1 - pair
    chunks_per_seq = n_pages // ppc
    seq_steps = chunks_per_seq // nq
    part = lax.rem(step, seq_steps)
    n_chunks = n_steps * nq

    def chunk_copies(chunk):
        slot = lax.rem(chunk, SAMPLE_BUFS)
        seq = lax.div(chunk, chunks_per_seq)
        first_pos = (n_pages - 1) - lax.rem(chunk, chunks_per_seq) * ppc
        copies = []
        for j in range(ppc):
            page = page_base + pt_ref[seq * n_pages + first_pos - j]
            copies.append(pltpu.make_async_copy(ck_hbm.at[page], kbuf.at[slot, j], sem.at[slot, 0]))
            copies.append(pltpu.make_async_copy(cv_hbm.at[page], vbuf.at[slot, j], sem.at[slot, 1]))
        return copies

    def start_chunk(chunk):
        for cp in chunk_copies(chunk):
            cp.start()

    def wait_chunk(chunk):
        slot = lax.rem(chunk, SAMPLE_BUFS)
        pltpu.make_async_copy(ck_hbm.at[pl.ds(0, ppc)], kbuf.at[slot], sem.at[slot, 0]).wait()
        pltpu.make_async_copy(cv_hbm.at[pl.ds(0, ppc)], vbuf.at[slot], sem.at[slot, 1]).wait()

    def start_next(chunk):
        @pl.when(chunk + SAMPLE_BUFS < n_chunks)
        def _():
            start_chunk(chunk + SAMPLE_BUFS)

    @pl.when(step == 0)
    def _():
        for c in range(SAMPLE_BUFS):
            start_chunk(jnp.int32(c))

    lanes = PAGE_SIZE * SB_HEADS
    lane =lax.broadcasted_iota(jnp.int32, (t_new, lanes), 1)
    lane_head = lane & (SB_HEADS - 1)
    biases = [bias_ref[h] * LOG2E for h in range(SB_HEADS)]
    bias_lanes = jnp.full((t_new, lanes), biases[0], F32)
    for h in range(1, SB_HEADS):
        bias_lanes = jnp.where(lane_head == h, biases[h], bias_lanes)
    qs = qs_ref[...]
    q_all = jnp.concatenate([qs[:, HEAD_DIM * h:HEAD_DIM * (h + 1)] for h in range(SB_HEADS)],
                            axis=0).astype(BF16)

    def page_scores(kpages, n, mask):
        s_cat = _dot_nt(q_all, kpages.astype(BF16))
        width = kpages.shape[0] // n
        return [scores(s_cat[:, width * p:width * (p + 1)], mask) for p in range(n)]

    def scores(s_all, mask):
        width = s_all.shape[1]
        head = lane_head[:, :width]
        sc = s_all[0:t_new, :]
        for h in range(1, SB_HEADS):
            sc = jnp.where(head == h, s_all[t_new * h:t_new * (h + 1), :], sc)
        z2 = sc * (ATTN_SCALE * LOG2E) + bias_lanes[:, :width]
        lf = _log2_fail(z2)
        if mask is not None:
            lf = jnp.where(mask, lf, 0.0)
        blocks = jnp.concatenate([lf[:, HEAD_DIM * j:HEAD_DIM * (j + 1)]
                                  for j in range(width // HEAD_DIM)], axis=0)
        hi, lo = _split2(blocks)
        return z2, jnp.concatenate([hi, lo], axis=1)

    def stacked_sums(operands, matrix):
        rows = operands[0].shape[0]
        res = _dot(jnp.concatenate(operands, axis=0), matrix)
        return [res[rows * i:rows * (i + 1), :] for i in range(len(operands))]

    def weights(z2, res, mask, run):
        width = z2.shape[1]
        head = lane_head[:, :width]
        n_blk = width // HEAD_DIM
        ws = [None] * n_blk
        for j in reversed(range(n_blk)):
            rows = slice(t_new * j, t_new * (j + 1))
            logw = z2[:, HEAD_DIM * j:HEAD_DIM * (j + 1)] + res[rows, :HEAD_DIM]
            if run is not None:
                logw = logw + run
            ws[j] = jnp.exp2(logw)
            tot = res[rows, HEAD_DIM:]
            run = tot if run is None else run + tot
        w = jnp.concatenate(ws, axis=1)
        if mask is not None:
            w = jnp.where(mask, w, 0.0)
        w_all = jnp.concatenate([jnp.where(head == h, w, 0.0) for h in range(SB_HEADS)],
                                axis=0).astype(BF16)
        return w_all, run

    def new_keys():
        width = kn_ref.shape[0]
        mask = ((lane >> SB_HEAD_BITS) < lax.broadcasted_iota(jnp.int32, (t_new, lanes), 0)
                )[:, :width]
        (z2, hilo), = page_scores(kn_ref[...], 1, mask)
        res, = stacked_sums([hilo], uo_ref[...])
        w_all, run = weights(z2, res, mask, None)
        cs_ref[...] = run
        accs_ref[...] = _dot(w_all, vn_ref[...].astype(BF16))

    uu = uu_ref[...]
    cols = [slice(HEAD_DIM * h, HEAD_DIM * (h + 1)) for h in range(SB_HEADS)]

    def section(q_ref, start, diag, first, chunk):
        kb = k_ref[pl.ds(start, tq), :]
        vb = v_ref[pl.ds(start, tq), :]
        n_pages_here = 0 if chunk is None else ppc
        slot = None if chunk is None else lax.rem(chunk, SAMPLE_BUFS)
        if n_pages_here:
            run = cs_ref[...]
            acc = accs_ref[...]
        p_parts, s_parts = {}, {}
        n_idx = max(SB_HEADS, n_pages_here)

        raw = [_dot_nt(q_ref[:, cols[h]], kb[:, cols[h]]) for h in range(SB_HEADS)]
        if n_pages_here:
            s_split = page_scores(kbuf[slot].reshape(ppc * lanes, HEAD_DIM), ppc, None)
        split = []
        for h in range(SB_HEADS):
            z2 = raw[h] * (ATTN_SCALE * LOG2E) + biases[h]
            lf = _log2_fail(z2)
            if diag is not None:
                lf = jnp.where(diag, lf, 0.0)
            hi, lo = _split2(lf)
            split.append((z2, jnp.concatenate([hi, lo], axis=1)))
        incl = stacked_sums([hilo for _, hilo in split], uu)
        for h in range(SB_HEADS):
            p_parts[h] = (split[h][0], incl[h])
        if n_pages_here:
            res = stacked_sums([hilo for _, hilo in s_split], uo_ref[...])
            for idx in range(n_pages_here):
                s_parts[idx] = (s_split[idx][0], res[idx])
        w_pages = []
        for idx in range(n_idx):
            if idx < SB_HEADS:
                z2, incl_h = p_parts[idx]
                logw = z2 + incl_h
                if not first:
                    logw = logw + c_ref[idx]
                w = jnp.exp2(logw)
                if diag is not None:
                    w = jnp.where(diag, w, 0.0)
                pv = _dot(w.astype(BF16), vb[:, cols[idx]])
                total = incl_h[:, 0:1]
                if first:
                    acc_ref[:, cols[idx]] = pv
                    c_ref[idx] = total
                else:
                    acc_ref[:, cols[idx]] += pv
                    c_ref[idx] += total
            if idx < n_pages_here:
                w_all, run = weights(*s_parts[idx], None, run)
                w_pages.append(w_all)
        if n_pages_here:
            cs_ref[...] = run
            accs_ref[...] = acc + _dot(jnp.concatenate(w_pages, axis=1),
                                       vbuf[slot].reshape(ppc * lanes, HEAD_DIM).astype(BF16))

    def finish(i, g_ref):
        rows = pl.ds(pl.multiple_of(i * tq, tq), tq)
        op_ref[rows, :] = (acc_ref[...] * _silu(g_ref[...])).astype(op_ref.dtype)

    row = lax.broadcasted_iota(jnp.int32, (tq, tq), 0)
    col = lax.broadcasted_iota(jnp.int32, (tq, tq), 1)
    diag = col < row
    chunk0 = step * nq

    @pl.when(part == 0)
    def _():
        new_keys()

    section(qa_ref, pl.multiple_of(i1 * tq, tq), diag, True, None)

    def body_a(t, carry):
        chunk = chunk0 + t
        wait_chunk(chunk)
        section(qa_ref, pl.multiple_of((i1 - 1 - t) * tq, tq), None, False, chunk)
        start_next(chunk)
        return carry

    lax.fori_loop(0, i1, body_a, 0)
    finish(i1, ga_ref)

    chunk = chunk0 + i1
    wait_chunk(chunk)
    section(qb_ref, pl.multiple_of(i2 * tq, tq), diag, True, chunk)
    start_next(chunk)

    def body_b(t, carry):
        chunk = chunk0 + i1 + 1 + t
        wait_chunk(chunk)
        section(qb_ref, pl.multiple_of((i2 - 1 - t) * tq, tq), None, False, chunk)
        start_next(chunk)
        return carry

    lax.fori_loop(0, i2, body_b, 0)
    finish(i2, gb_ref)

    @pl.when(part == seq_steps - 1)
    def _():
        g = gs_ref[...]
        for h in range(SB_HEADS):
            os_ref[:, cols[h]] = accs_ref[t_new * h:t_new * (h + 1), :] * _silu(g[:, cols[h]])


def _sb_fused(q, k, v, g, batch, seq, tq, q_s, k_new, v_new, g_s, cache_k, cache_v, layer,
              page_table, sb_bias, t_new):
    n_seq, n_pages = page_table.shape
    n_pool = cache_k.shape[1]
    nq = seq // tq
    n_steps = batch * (nq // 2)
    assert nq % 2 == 0 and (n_seq * n_pages) % (n_steps * nq) == 0
    ppc = n_seq * n_pages // (n_steps * nq)
    assert n_pages % (ppc * nq) == 0
    seq_steps = n_pages // (ppc * nq)
    page_rows = PAGE_SIZE * SB_HEADS

    q3, k3, v3, g3 = (a.reshape(batch, seq, SB_WIDTH) for a in (q, k, v, g))
    u = (jnp.arange(tq)[:, None] >= jnp.arange(tq)[None, :]).astype(BF16)
    uu = jnp.concatenate([u, u], axis=0)

    new_tokens = -(-t_new * SB_HEADS // HEAD_DIM) * HEAD_DIM // SB_HEADS
    new_rows = new_tokens * SB_HEADS

    def as_page(a):
        a = a.reshape(n_seq, t_new, SB_HEADS, HEAD_DIM)
        a = jnp.pad(a, ((0, 0), (0, new_tokens - t_new), (0, 0), (0, 0)))
        return a.reshape(n_seq * new_rows, HEAD_DIM)

    cache_k = cache_k.reshape(-1, page_rows, HEAD_DIM)
    cache_v = cache_v.reshape(-1, page_rows, HEAD_DIM)
    r = jnp.arange(HEAD_DIM)
    same_head = (r[:, None] % SB_HEADS) == (r[None, :] % SB_HEADS)
    not_earlier = (r[:, None] // SB_HEADS) >= (r[None, :] // SB_HEADS)
    uo = jnp.concatenate([same_head & not_earlier, same_head], axis=1).astype(BF16)
    uo = jnp.concatenate([uo, uo], axis=0)

    half = nq // 2
    b_of = lambda s: s // half
    qa_map = lambda s, pt: (b_of(s), s % half, 0)
    qb_map = lambda s, pt: (b_of(s), nq - 1 - s % half, 0)
    seq_map = lambda s, pt: (b_of(s), 0, 0)
    samp_map = lambda s, pt: (s // seq_steps, 0)
    const = lambda s, pt: (0, 0)
    kern = functools.partial(_sb_fused_kernel, tq, nq, ppc, n_pages, t_new, layer * n_pool)
    grid_spec = pltpu.PrefetchScalarGridSpec(
        num_scalar_prefetch=1,
        grid=(n_steps,),
        in_specs=[
            pl.BlockSpec(memory_space=pltpu.SMEM),
            pl.BlockSpec((None, tq, SB_WIDTH), qa_map),
            pl.BlockSpec((None, tq, SB_WIDTH), qb_map),
            pl.BlockSpec((None, seq, SB_WIDTH), seq_map, pipeline_mode=pl.Buffered(1)),
            pl.BlockSpec((None, seq, SB_WIDTH), seq_map, pipeline_mode=pl.Buffered(1)),
            pl.BlockSpec((None, tq, SB_WIDTH), qa_map),
            pl.BlockSpec((None, tq, SB_WIDTH), qb_map),
            pl.BlockSpec((2 * tq, tq), const),
            pl.BlockSpec((t_new, SB_WIDTH), samp_map),
            pl.BlockSpec((new_rows, HEAD_DIM), samp_map),
            pl.BlockSpec((new_rows, HEAD_DIM), samp_map),
            pl.BlockSpec((t_new, SB_WIDTH), samp_map),
            pl.BlockSpec((2 * HEAD_DIM, 2 * HEAD_DIM), const),
            pl.BlockSpec(memory_space=pl.ANY),
            pl.BlockSpec(memory_space=pl.ANY),
        ],
        out_specs=[
            pl.BlockSpec((None, seq, SB_WIDTH), seq_map, pipeline_mode=pl.Buffered(1)),
            pl.BlockSpec((t_new, SB_WIDTH), samp_map),
        ],
        scratch_shapes=[
            pltpu.VMEM((tq, SB_WIDTH), F32),
            pltpu.VMEM((SB_HEADS, tq, 1), F32),
            pltpu.VMEM((SB_HEADS * t_new, HEAD_DIM), F32),
            pltpu.VMEM((t_new, HEAD_DIM), F32),
            pltpu.VMEM((SAMPLE_BUFS, ppc, page_rows, HEAD_DIM), F32),
            pltpu.VMEM((SAMPLE_BUFS, ppc, page_rows, HEAD_DIM), F32),
            pltpu.SemaphoreType.DMA((SAMPLE_BUFS, 2)),
        ],
    )
    out_p, out_s = pl.pallas_call(
        kern,
        grid_spec=grid_spec,
        out_shape=[jax.ShapeDtypeStruct((batch, seq, SB_WIDTH), BF16),
                   jax.ShapeDtypeStruct((n_seq * t_new, SB_WIDTH), F32)],
        compiler_params=pltpu.CompilerParams(
            dimension_semantics=("arbitrary",), vmem_limit_bytes=VMEM_LIMIT_FUSED),
        name="sb_fused",
    )(page_table.reshape(-1), sb_bias, q3, q3, k3, v3, g3, g3, uu,
      q_s, as_page(k_new), as_page(v_new), g_s, uo, cache_k, cache_v)
    return out_p.reshape(batch * seq, SB_WIDTH), out_s


def _ssd_kernel(length, n_par, *refs):
    per_seq_in, shared, per_seq_out = refs[:5], refs[5:13], refs[13:]
    pre_ref, h0_ref = per_seq_in[3:5]
    ext_ref, st_ref = per_seq_out[3:5]

    @pl.when(pl.program_id(1) == 0)
    def _():
        ext_ref[:, 0:SUBLANES, :] = pre_ref[...]
        st_ref[...] = h0_ref[...]

    for s in range(n_par):
        _ssd_chunk(length, *[r.at[s] for r in per_seq_in[:3]], *shared,
                   *[r.at[s] for r in per_seq_out])


def _pad_rows(a, rows):
    if a.shape[0] == rows:
        return a
    return jnp.concatenate([a, jnp.zeros((rows - a.shape[0], a.shape[1]), a.dtype)], axis=0)


def _ssd_chunk(length, xbc_ref, z_ref, dt_ref, cw_ref, cb_ref, dtb_ref,
               alog_ref, dsk_ref, nw_ref, ltri_ref, e_ref, out_ref, cnew_ref, snew_ref,
               ext_ref, st_ref):
    L = SSD_CHUNK
    P = SSD_HEAD_DIM

    ext_ref[SUBLANES:SUBLANES + L, :] = _pad_rows(xbc_ref[...], L)
    cw = cw_ref[...]
    conv = cb_ref[...]
    for j in range(CONV_WIDTH):
        off = SUBLANES - (CONV_WIDTH - 1) + j
        conv = conv + ext_ref[off:off + L, :] * cw[j:j + 1, :]
    act = _silu(conv)
    tail = ext_ref[length:length + SUBLANES, :]
    cnew_ref[...] = tail
    ext_ref[0:SUBLANES, :] = tail

    xs = act[:, :SSD_WIDTH]
    bm = act[:, SSD_WIDTH:SSD_WIDTH + SSD_GROUPS * SSD_STATE]
    cm = act[:, SSD_WIDTH + SSD_GROUPS * SSD_STATE:]

    x_dt = _pad_rows(dt_ref[...], L) + dtb_ref[...]
    dt = jnp.maximum(x_dt, 0.0) + jnp.log1p(jnp.exp(-jnp.abs(x_dt)))
    if length < L:
        valid = lax.broadcasted_iota(jnp.int32, dt.shape, 0) < length
        dt = jnp.where(valid, dt, 0.0)
    da = dt * (-jnp.exp(alog_ref[...]))
    cs = _dot_exact_rhs(ltri_ref[...], da)
    cs_t = cs.T
    e = e_ref[...]
    dt_x, cs_x = _dots_exact_lhs([dt, cs], e)
    xdt = xs * dt_x
    ecs = jnp.exp(cs_x)
    xw_t = (xdt * jnp.exp(cs_x[L - 1:L, :] - cs_x)).T
    xdt16 = xdt.astype(BF16)

    row = lax.broadcasted_iota(jnp.int32, (L, L), 0)
    col = lax.broadcasted_iota(jnp.int32, (L, L), 1)
    causal = col <= row
    heads_per_group = SSD_HEADS // SSD_GROUPS
    gw = heads_per_group * P
    y_diag, y_off = [], []
    for g in range(SSD_GROUPS):
        bg = bm[:, SSD_STATE * g:SSD_STATE * (g + 1)].astype(BF16)
        cg = cm[:, SSD_STATE * g:SSD_STATE * (g + 1)].astype(BF16)
        cb = _dot_nt(cg, bg)
        prev = st_ref[gw * g:gw * (g + 1), :]
        y_off.append(_dot_nt(cg, prev.astype(BF16)))
        new = _dot(xw_t[gw * g:gw * (g + 1), :].astype(BF16), bg)
        for r in range(heads_per_group):
            h = heads_per_group * g + r
            seg = cs[:, h:h + 1] - cs_t[h:h + 1, :]
            decay = jnp.exp(jnp.where(causal, seg, -jnp.inf))
            y_diag.append(_dot((cb * decay).astype(BF16), xdt16[:, P * h:P * (h + 1)]))
            chunk_decay = jnp.exp(cs[L - 1:L, h:h + 1])
            st_ref[P * h:P * (h + 1), :] = (prev[P * r:P * (r + 1), :] * chunk_decay
                                            + new[P * r:P * (r + 1), :])
    snew_ref[...] = st_ref[...]
    y = (jnp.concatenate(y_diag, axis=1) + jnp.concatenate(y_off, axis=1) * ecs
         + xs * dsk_ref[...])
    gated = y[:length] * _silu(z_ref[...])
    ms = jnp.mean(gated * gated, axis=-1, keepdims=True)
    out_ref[...] = (gated * lax.rsqrt(ms + EPS) * nw_ref[...]).astype(out_ref.dtype)


def _ssd(xbc, zdt, prefix, h0, conv_w, conv_b, dt_bias, a_log, d_skip, ssd_norm_w,
         batch, n_chunks, length, out_dtype):
    L = SSD_CHUNK
    pre = jnp.pad(prefix, ((0, 0), (SUBLANES - (CONV_WIDTH - 1), 0), (0, 0)))
    pad_h = lambda a: jnp.pad(a, (0, HEAD_DIM - SSD_HEADS)).reshape(1, HEAD_DIM)
    ltri = (jnp.arange(L)[:, None] >= jnp.arange(L)[None, :]).astype(BF16)
    expand = (jnp.arange(HEAD_DIM)[:, None] == jnp.arange(SSD_WIDTH)[None, :] // SSD_HEAD_DIM
              ).astype(BF16)
    dsk = jnp.repeat(d_skip, SSD_HEAD_DIM).reshape(1, SSD_WIDTH)
    z_blk = SSD_WIDTH // HEAD_DIM
    n_par = SSD_SEQS_PER_STEP
    assert batch % n_par == 0 and length <= L and (length == L or n_chunks == 1)
    rows = n_chunks * length
    xbc3 = xbc.reshape(batch, rows, XBC_WIDTH)
    zdt3 = zdt.reshape(batch, rows, ZDT_WIDTH)
    const = lambda b, c: (0, 0)
    seq = lambda b, c: (b, 0, 0)
    kern = functools.partial(_ssd_kernel, length, n_par)
    out, conv_new, ssm_new = pl.pallas_call(
        kern,
        grid=(batch // n_par, n_chunks),
        in_specs=[
            pl.BlockSpec((n_par, length, XBC_WIDTH), lambda b, c: (b, c, 0)),
            pl.BlockSpec((n_par, length, SSD_WIDTH), lambda b, c: (b, c, 0)),
            pl.BlockSpec((n_par, length, HEAD_DIM), lambda b, c: (b, c, z_blk)),
            pl.BlockSpec((n_par, SUBLANES, XBC_WIDTH), seq),
            pl.BlockSpec((n_par, SSD_WIDTH, SSD_STATE), seq),
            pl.BlockSpec((CONV_WIDTH, XBC_WIDTH), const),
            pl.BlockSpec((1, XBC_WIDTH), const),
            pl.BlockSpec((1, HEAD_DIM), const),
            pl.BlockSpec((1, HEAD_DIM), const),
            pl.BlockSpec((1, SSD_WIDTH), const),
            pl.BlockSpec((1, SSD_WIDTH), const),
            pl.BlockSpec((L, L), const),
            pl.BlockSpec((HEAD_DIM, SSD_WIDTH), const),
        ],
        out_specs=[
            pl.BlockSpec((n_par, length, SSD_WIDTH), lambda b, c: (b, c, 0)),
            pl.BlockSpec((n_par, SUBLANES, XBC_WIDTH), seq),
            pl.BlockSpec((n_par, SSD_WIDTH, SSD_STATE), seq),
        ],
        out_shape=[
            jax.ShapeDtypeStruct((batch, rows, SSD_WIDTH), out_dtype),
            jax.ShapeDtypeStruct((batch, SUBLANES, XBC_WIDTH), F32),
            jax.ShapeDtypeStruct((batch, SSD_WIDTH, SSD_STATE), F32),
        ],
        scratch_shapes=[pltpu.VMEM((n_par, SUBLANES + L, XBC_WIDTH), F32),
                        pltpu.VMEM((n_par, SSD_WIDTH, SSD_STATE), F32)],
        compiler_params=pltpu.CompilerParams(
            dimension_semantics=("parallel", "arbitrary"), vmem_limit_bytes=VMEM_LIMIT),
        name="ssd_scan",
    )(xbc3, zdt3, zdt3, pre, h0.reshape(batch, SSD_WIDTH, SSD_STATE), conv_w,
      conv_b.reshape(1, XBC_WIDTH), pad_h(dt_bias), pad_h(a_log), dsk,
      ssd_norm_w.reshape(1, SSD_WIDTH), ltri, expand)
    return out.reshape(batch * rows, SSD_WIDTH), conv_new, ssm_new


def _mem_attn_kernel(q_ref, g_ref, k_ref, v_ref, o_ref):
    for h in range(MEM_HEADS):
        cols = slice(HEAD_DIM * h, HEAD_DIM * (h + 1))
        s = _dot_nt(q_ref[:, cols].astype(BF16), k_ref[:, cols].astype(BF16)) * ATTN_SCALE
        p = jnp.exp(s - jnp.max(s, axis=-1, keepdims=True))
        den = jnp.sum(p, axis=-1, keepdims=True)
        o = _dot(p.astype(BF16), v_ref[:, cols].astype(BF16)) / den
        o_ref[:, cols] = (o * _silu(g_ref[:, cols])).astype(o_ref.dtype)


def _mem_attn(q, g, mem_k, mem_v, batch, t, tq, out_dtype):
    nq = t // tq
    return pl.pallas_call(
        _mem_attn_kernel,
        grid=(batch, nq),
        in_specs=[
            pl.BlockSpec((tq, MEM_WIDTH), lambda b, i: (b * nq + i, 0)),
            pl.BlockSpec((tq, MEM_WIDTH), lambda b, i: (b * nq + i, 0)),
            pl.BlockSpec((None, MEM_TOKENS, MEM_WIDTH), lambda b, i: (b, 0, 0)),
            pl.BlockSpec((None, MEM_TOKENS, MEM_WIDTH), lambda b, i: (b, 0, 0)),
        ],
        out_specs=pl.BlockSpec((tq, MEM_WIDTH), lambda b, i: (b * nq + i, 0)),
        out_shape=jax.ShapeDtypeStruct((batch * t, MEM_WIDTH), out_dtype),
        compiler_params=pltpu.CompilerParams(
            dimension_semantics=("parallel", "parallel"), vmem_limit_bytes=VMEM_LIMIT),
        name="mem_attn",
    )(q, g, mem_k, mem_v)


def _mem_attn_rows_kernel(t, q_ref, g_ref, k_ref, v_ref, o_ref):
    q = q_ref[...]
    g = g_ref[...]
    cols = [slice(HEAD_DIM * h, HEAD_DIM * (h + 1)) for h in range(MEM_HEADS)]
    q_all = jnp.concatenate([q[:, c] for c in cols], axis=0).astype(BF16)
    s = _dot_nt(q_all, k_ref[...].astype(BF16)) * ATTN_SCALE
    n_rows = s.shape[1]
    row_head = jnp.concatenate([jnp.full((t, n_rows), h, jnp.int32) for h in range(MEM_HEADS)],
                               axis=0)
    lane_head = lax.broadcasted_iota(jnp.int32, s.shape, 1) & (MEM_HEADS - 1)
    s = jnp.where(row_head == lane_head, s, -jnp.inf)
    p = jnp.exp(s - jnp.max(s, axis=-1, keepdims=True))
    den = jnp.sum(p, axis=-1, keepdims=True)
    o = _dot(p.astype(BF16), v_ref[...].astype(BF16)) / den
    for h in range(MEM_HEADS):
        o_ref[:, cols[h]] = (o[t * h:t * (h + 1), :] * _silu(g[:, cols[h]])).astype(o_ref.dtype)


def _mem_attn_rows(q, g, mem_k, mem_v, batch, t, out_dtype):
    rows = MEM_TOKENS * MEM_HEADS
    kern = functools.partial(_mem_attn_rows_kernel, t)
    return pl.pallas_call(
        kern,
        grid=(batch,),
        in_specs=[
            pl.BlockSpec((t, MEM_WIDTH), lambda b: (b, 0)),
            pl.BlockSpec((t, MEM_WIDTH), lambda b: (b, 0)),
            pl.BlockSpec((None, rows, HEAD_DIM), lambda b: (b, 0, 0)),
            pl.BlockSpec((None, rows, HEAD_DIM), lambda b: (b, 0, 0)),
        ],
        out_specs=pl.BlockSpec((t, MEM_WIDTH), lambda b: (b, 0)),
        out_shape=jax.ShapeDtypeStruct((batch * t, MEM_WIDTH), out_dtype),
        compiler_params=pltpu.CompilerParams(
            dimension_semantics=("parallel",), vmem_limit_bytes=VMEM_LIMIT),
        name="mem_attn_rows",
    )(q, g, mem_k, mem_v)


def _out_proj_kernel(x_ref, sb_ref, ssd_ref, mo_ref, w_ref, o_ref):
    mix = jnp.concatenate([sb_ref[...].astype(BF16), ssd_ref[...].astype(BF16),
                           mo_ref[...].astype(BF16)], axis=-1)
    for c in range(0, o_ref.shape[1], PROJ_CHUNK):
        cols = slice(c, c + PROJ_CHUNK)
        o_ref[:, cols] = x_ref[:, cols] + _dot(mix, w_ref[:, cols].astype(BF16))


def _out_proj(x, sb, ssd, mo, w_out, tm, tn):
    t, d = x.shape
    return pl.pallas_call(
        _out_proj_kernel,
        grid=(t // tm, d // tn),
        in_specs=[
            pl.BlockSpec((tm, tn), lambda m, n: (m, n)),
            pl.BlockSpec((tm, SB_WIDTH), lambda m, n: (m, 0)),
            pl.BlockSpec((tm, SSD_WIDTH), lambda m, n: (m, 0)),
            pl.BlockSpec((tm, MEM_WIDTH), lambda m, n: (m, 0)),
            pl.BlockSpec((w_out.shape[0], tn), lambda m, n: (0, n), pipeline_mode=pl.Buffered(1)),
        ],
        out_specs=pl.BlockSpec((tm, tn), lambda m, n: (m, n)),
        out_shape=jax.ShapeDtypeStruct((t, d), F32),
        compiler_params=pltpu.CompilerParams(
            dimension_semantics=("parallel", "arbitrary"), vmem_limit_bytes=VMEM_LIMIT),
        name="out_proj",
    )(x, sb, ssd, mo, w_out)


def _in_proj_plan(act_dtype):
    lowp = act_dtype == BF16
    outs, plan = [], []

    def add(width, dtype):
        outs.append((width, dtype))
        return len(outs) - 1

    q = add(SB_WIDTH, act_dtype)
    plan.append(((0, SB_WIDTH, 0, None if lowp else q, q if lowp else None),))
    k32 = add(SB_WIDTH, F32)
    k16 = add(SB_WIDTH, BF16) if lowp else None
    plan.append(((0, SB_WIDTH, 1, k32, k16),))
    v32 = add(SB_WIDTH, F32)
    v16 = add(SB_WIDTH, BF16) if lowp else None
    plan.append(((0, SB_WIDTH, None, v32, v16),))
    g = add(SB_WIDTH, F32)
    plan.append(((0, SB_WIDTH, None, g, None),))
    xbc = add(XBC_WIDTH, F32)
    plan.append(((0, XBC_WIDTH, None, xbc, None),))
    zdt = add(ZDT_WIDTH, F32)
    plan.append(((0, ZDT_WIDTH, None, zdt, None),))
    mq = add(MEM_WIDTH, act_dtype)
    mg = add(MEM_WIDTH, F32)
    plan.append(((0, MEM_WIDTH, 2, None if lowp else mq, mq if lowp else None),
                 (MEM_WIDTH, MEM_WIDTH, None, mg, None)))
    names = dict(q=q, k32=k32, k16=k16, v32=v32, v16=v16, g=g, xbc=xbc, zdt=zdt, mq=mq, mg=mg)
    return tuple(plan), outs, names


_O_Z = 4 * SB_WIDTH
_O_XBC = _O_Z + SSD_WIDTH
_O_DT = _O_XBC + XBC_WIDTH
_O_MEM = _O_DT + SSD_HEADS
_IN_WIDTH = _O_MEM + 2 * MEM_WIDTH
_CAT_WIDTH = 7 * PROJ_TN
W_PREP_COLS = 256
BF16_ROWS = 16


def _w_prep_kernel(w_ref, o_ref):
    def put(dst, src, rows):
        o_ref[dst:dst + rows, :] = w_ref[src:src + rows, :].astype(BF16)

    cols = w_ref.shape[1]
    put(0, 0, _O_Z)
    put(_O_Z, _O_XBC, XBC_WIDTH)
    put(_O_Z + XBC_WIDTH, _O_Z, SSD_WIDTH)
    dt0 = _O_Z + XBC_WIDTH + SSD_WIDTH
    o_ref[dt0:dt0 + BF16_ROWS, :] = jnp.concatenate(
        [w_ref[_O_DT:_O_MEM, :], jnp.zeros((BF16_ROWS - SSD_HEADS, cols), F32)], axis=0
    ).astype(BF16)
    o_ref[dt0 + BF16_ROWS:6 * PROJ_TN, :] = jnp.zeros((6 * PROJ_TN - dt0 - BF16_ROWS, cols), BF16)
    put(6 * PROJ_TN, _O_MEM, 2 * MEM_WIDTH)


def _rearranged_w_in(w_t):
    d = w_t.shape[1]
    assert w_t.shape[0] == _IN_WIDTH and d % W_PREP_COLS == 0
    return pl.pallas_call(
        _w_prep_kernel,
        grid=(d // W_PREP_COLS,),
        in_specs=[pl.BlockSpec((_IN_WIDTH, W_PREP_COLS), lambda c: (0, c))],
        out_specs=pl.BlockSpec((_CAT_WIDTH, W_PREP_COLS), lambda c: (0, c)),
        out_shape=jax.ShapeDtypeStruct((_CAT_WIDTH, d), BF16),
        compiler_params=pltpu.CompilerParams(
            dimension_semantics=("parallel",), vmem_limit_bytes=VMEM_LIMIT),
        name="w_prep",
    )(w_t)


def kernel(x_prompt, x_sample, cache_sb_k, cache_sb_v, state_ssm, state_conv, cache_mem_k,
           cache_mem_v, page_table, mem_prompt, norm_w, w_in, sb_q_norm, sb_k_norm, sb_bias,
           conv_w, conv_b, dt_bias, a_log, d_skip, ssd_norm_w, mem_norm_w, w_mem_kv, mem_q_norm,
           mem_k_norm, w_out):
    depth = w_in.shape[0]
    assert depth == 1
    layer = 0
    bp, sp, d = x_prompt.shape
    bs, ts, _ = x_sample.shape
    n_pool = cache_sb_k.shape[1]
    L = SSD_CHUNK

    w_cat = _rearranged_w_in(w_in[layer].T)
    w_o = w_out[layer]
    head_norms = jnp.concatenate(
        [sb_q_norm[layer][None], sb_k_norm[layer][None], mem_q_norm[layer][None],
         mem_k_norm[layer][None], jnp.zeros((SUBLANES - 4, HEAD_DIM), F32)], axis=0)
    ssd_params = (conv_w[layer], conv_b[layer], dt_bias[layer], a_log[layer], d_skip[layer],
                  ssd_norm_w[layer])

    xp = x_prompt.reshape(bp * sp, d)
    mem_plan = (((0, MEM_WIDTH, 3, 0, None), (MEM_WIDTH, MEM_WIDTH, None, 1, None)),)
    (mk, mv), _ = _proj(mem_prompt.reshape(bp * MEM_TOKENS, d), mem_norm_w[layer],
                        w_mem_kv[layer], head_norms, mem_plan,
                        [(MEM_WIDTH, F32), (MEM_WIDTH, F32)], tm=PROJ_TM,
                        w_rows_are_outputs=False)
    plan, outs, nm = _in_proj_plan(BF16)
    xs = x_sample.reshape(bs * ts, d)
    plan_s, outs_s, ns = _in_proj_plan(F32)
    pr, ps = _proj(xp, norm_w[layer], w_cat, head_norms, plan, outs, tm=PROJ_TM,
                   side=(xs, plan_s, outs_s))
    sb, sb_s = _sb_fused(pr[nm['q']], pr[nm['k16']], pr[nm['v16']], pr[nm['g']], bp, sp, ATTN_TQ,
                         ps[ns['q']], ps[ns['k32']], ps[ns['v32']], ps[ns['g']],
                         cache_sb_k, cache_sb_v, layer, page_table, sb_bias[layer], ts)
    ssd, conv_p, ssm_p = _ssd(
        pr[nm['xbc']], pr[nm['zdt']], jnp.zeros((bp, CONV_WIDTH - 1, XBC_WIDTH), F32),
        jnp.zeros((bp, SSD_HEADS, SSD_HEAD_DIM, SSD_STATE), F32), *ssd_params,
        batch=bp, n_chunks=sp // L, length=L, out_dtype=BF16)
    mo = _mem_attn(pr[nm['mq']], pr[nm['mg']], mk.reshape(bp, MEM_TOKENS, MEM_WIDTH),
                   mv.reshape(bp, MEM_TOKENS, MEM_WIDTH), bp, sp, tq=MEM_TQ, out_dtype=BF16)
    yp = _out_proj(xp, sb, ssd, mo, w_o, tm=OUT_TM, tn=d)

    ssd_s, conv_s, ssm_s = _ssd(
        ps[ns['xbc']], ps[ns['zdt']], state_conv[layer], state_ssm[layer],
        *ssd_params, batch=bs, n_chunks=1, length=ts, out_dtype=F32)
    mo_s = _mem_attn_rows(ps[ns['mq']], ps[ns['mg']],
                          cache_mem_k[layer].reshape(bs, MEM_TOKENS * MEM_HEADS, HEAD_DIM),
                          cache_mem_v[layer].reshape(bs, MEM_TOKENS * MEM_HEADS, HEAD_DIM),
                          bs, ts, out_dtype=F32)
    ys = _out_proj(xs, sb_s, ssd_s, mo_s, w_o, tm=bs * ts, tn=d)

    tail = slice(SUBLANES - (CONV_WIDTH - 1), SUBLANES)
    return (
        yp.reshape(bp, sp, d),
        ys.reshape(bs, ts, d),
        pr[nm['k32']].reshape(1, bp, sp, SB_HEADS, HEAD_DIM),
        pr[nm['v32']].reshape(1, bp, sp, SB_HEADS, HEAD_DIM),
        ssm_p.reshape(1, bp, SSD_HEADS, SSD_HEAD_DIM, SSD_STATE),
        conv_p[:, tail][None],
        mk.reshape(1, bp, MEM_TOKENS, MEM_HEADS, HEAD_DIM),
        mv.reshape(1, bp, MEM_TOKENS, MEM_HEADS, HEAD_DIM),
        ps[ns['k32']].reshape(1, bs, ts, SB_HEADS, HEAD_DIM),
        ps[ns['v32']].reshape(1, bs, ts, SB_HEADS, HEAD_DIM),
        ssm_s.reshape(1, bs, SSD_HEADS, SSD_HEAD_DIM, SSD_STATE),
        conv_s[:, tail][None],
    )
```

```python
import functools
import math

import jax
import jax.numpy as jnp
from jax import lax
from jax.experimental import pallas as pl
from jax.experimental.pallas import tpu as pltpu

F32 = jnp.float32
BF16 = jnp.bfloat16

SB_HEADS = 8
HEAD_DIM = 128
SB_WIDTH = SB_HEADS * HEAD_DIM
SSD_HEADS = 8
SSD_HEAD_DIM = 64
SSD_WIDTH = SSD_HEADS * SSD_HEAD_DIM
SSD_GROUPS = 2
SSD_STATE = 128
CONV_WIDTH = 4
XBC_WIDTH = SSD_WIDTH + 2 * SSD_GROUPS * SSD_STATE
MEM_TOKENS = 256
MEM_HEADS = 4
MEM_WIDTH = MEM_HEADS * HEAD_DIM
PAGE_SIZE = 128
EPS = 1e-6
ATTN_SCALE = HEAD_DIM ** -0.5
LOG2E = math.log2(math.e)

SSD_CHUNK = 128
PROJ_TN = 1024
PROJ_CHUNK = 256
ZDT_WIDTH = SSD_WIDTH + PROJ_CHUNK
SAMPLE_BUFS = 3
SSD_SEQS_PER_STEP = 4
SUBLANES = 8
SB_HEAD_BITS = SB_HEADS.bit_length() - 1
NORM_ROW_CHUNK = 128
PROJ_TM = 512
OUT_TM = 512
ATTN_TQ = 256
MEM_TQ = 512
VMEM_LIMIT = 56 * 1024 * 1024
VMEM_LIMIT_FUSED = 62 * 1024 * 1024

_NT = (((1,), (1,)), ((), ()))


def _dot(a, b):
    return jnp.dot(a, b, preferred_element_type=F32)


def _dot_nt(a, b):
    return lax.dot_general(a, b, _NT, preferred_element_type=F32)


def _split2(x):
    hi = x.astype(BF16)
    lo = (x - hi.astype(F32)).astype(BF16)
    return hi, lo


def _split3(x):
    hi = x.astype(BF16)
    r = x - hi.astype(F32)
    mid = r.astype(BF16)
    lo = (r - mid.astype(F32)).astype(BF16)
    return hi, mid, lo


def _dots_exact_lhs(xs, m):
    rows = xs[0].shape[0]
    r = _dot(jnp.concatenate([t for x in xs for t in _split3(x)], axis=0), m)
    part = lambda i: r[rows * i:rows * (i + 1), :]
    return [part(3 * i) + part(3 * i + 1) + part(3 * i + 2) for i in range(len(xs))]


def _dot_exact_rhs(m, x):
    n = x.shape[1]
    r = _dot(m, jnp.concatenate(_split3(x), axis=1))
    return r[:, :n] + r[:, n:2 * n] + r[:, 2 * n:]


def _silu(x):
    return x * (1.0 / (1.0 + jnp.exp(-x)))


def _rmsnorm_rows(x_ref, nw_ref, h_ref):
    rows_total = x_ref.shape[0]
    rc = min(rows_total, NORM_ROW_CHUNK)

    def body(r, carry):
        rows = pl.ds(pl.multiple_of(r * rc, rc), rc)
        xv = x_ref[rows, :]
        ms = jnp.mean(xv * xv, axis=-1, keepdims=True)
        h_ref[rows, :] = (xv * lax.rsqrt(ms + EPS) * nw_ref[...]).astype(BF16)
        return carry

    lax.fori_loop(0, rows_total // rc, body, 0)


def _proj_step(segs, h_ref, w_ref, hn_ref, outs, w_rows_are_outputs):
    for col0, width, hn_row, o32, o16 in segs:
        for c0 in range(0, width, PROJ_CHUNK):
            chunk = slice(col0 + c0, col0 + c0 + PROJ_CHUNK)
            if w_rows_are_outputs:
                y = _dot_nt(h_ref[...], w_ref[chunk, :].astype(BF16))
            else:
                y = _dot(h_ref[...], w_ref[:, chunk].astype(BF16))
            for c in range(0, PROJ_CHUNK, HEAD_DIM):
                yc = y[:, c:c + HEAD_DIM]
                if hn_row is not None:
                    ms = jnp.mean(yc * yc, axis=-1, keepdims=True)
                    yc = yc * lax.rsqrt(ms + EPS) * hn_ref[hn_row:hn_row + 1, :]
                cols = slice(c0 + c, c0 + c + HEAD_DIM)
                if o32 is not None:
                    outs[o32][:, cols] = yc
                if o16 is not None:
                    outs[o16][:, cols] = yc.astype(BF16)


def _proj_kernel(plan, n_out, side_plan, n_side, w_rows_are_outputs, *refs):
    if side_plan is None:
        x_ref, nw_ref, w_ref, hn_ref = refs[:4]
        rest = refs[4:]
    else:
        x_ref, xs_ref, nw_ref, w_ref, hn_ref = refs[:5]
        rest = refs[5:]
    outs, side_outs = rest[:n_out], rest[n_out:n_out + n_side]
    scratch = rest[n_out + n_side:]
    h_ref = scratch[0]
    m = pl.program_id(0)
    n = pl.program_id(1)

    @pl.when(n == 0)
    def _():
        _rmsnorm_rows(x_ref, nw_ref, h_ref)

    if side_plan is not None:
        hs_ref = scratch[1]

        @pl.when((n == 0) & (m == 0))
        def _():
            _rmsnorm_rows(xs_ref, nw_ref, hs_ref)

    for step, segs in enumerate(plan):
        @pl.when(n == step)
        def _(step=step, segs=segs):
            _proj_step(segs, h_ref, w_ref, hn_ref, outs, w_rows_are_outputs)
            if side_plan is not None:
                @pl.when(m == 0)
                def _():
                    _proj_step(side_plan[step], hs_ref, w_ref, hn_ref, side_outs,
                               w_rows_are_outputs)


def _proj(x, norm_w, w, head_norms, plan, out_defs, tm, side=None, w_rows_are_outputs=True):
    t, d = x.shape
    n_steps = len(plan)
    w_shape = (n_steps * PROJ_TN, d) if w_rows_are_outputs else (d, n_steps * PROJ_TN)
    assert w.shape == w_shape and t % tm == 0
    w_spec = (pl.BlockSpec((PROJ_TN, d), lambda m, n: (n, 0)) if w_rows_are_outputs
              else pl.BlockSpec((d, PROJ_TN), lambda m, n: (0, n)))
    row = lambda m, n: (m, 0)
    const = lambda m, n: (0, 0)
    in_specs = [pl.BlockSpec((tm, d), row)]
    operands = [x]

    last_tile = t // tm - 1
    step_of = {}
    for step, segs in enumerate(plan):
        for seg in segs:
            for o in seg[3:5]:
                if o is not None:
                    assert step_of.setdefault(o, step) == step
    assert sorted(step_of) == list(range(len(out_defs)))

    def out_map(step):
        return lambda m, n: (jnp.where(n > step, jnp.minimum(m + 1, last_tile), m), 0)

    out_specs = [pl.BlockSpec((tm, w), out_map(step_of[i])) for i, (w, _) in enumerate(out_defs)]
    out_shape = [jax.ShapeDtypeStruct((t, w), dt) for w, dt in out_defs]
    scratch = [pltpu.VMEM((tm, d), BF16)]
    side_plan, n_side = None, 0
    if side is not None:
        x_side, side_plan, side_defs = side
        ts = x_side.shape[0]
        assert len(side_plan) == n_steps
        n_side = len(side_defs)
        in_specs.append(pl.BlockSpec((ts, d), const))
        operands.append(x_side)
        out_specs += [pl.BlockSpec((ts, w), const) for w, _ in side_defs]
        out_shape += [jax.ShapeDtypeStruct((ts, w), dt) for w, dt in side_defs]
        scratch.append(pltpu.VMEM((ts, d), BF16))
    in_specs += [pl.BlockSpec((1, d), const), w_spec,
                 pl.BlockSpec((SUBLANES, HEAD_DIM), const)]
    operands += [norm_w.reshape(1, d), w, head_norms]
    kern = functools.partial(_proj_kernel, plan, len(out_defs), side_plan, n_side,
                             w_rows_are_outputs)
    res = pl.pallas_call(
        kern,
        grid=(t // tm, n_steps),
        in_specs=in_specs,
        out_specs=out_specs,
        out_shape=out_shape,
        scratch_shapes=scratch,
        compiler_params=pltpu.CompilerParams(
            dimension_semantics=("arbitrary", "arbitrary"), vmem_limit_bytes=VMEM_LIMIT),
        name="norm_proj",
    )(*operands)
    return res[:len(out_defs)], res[len(out_defs):]


def _log2_fail(z2):
    nz = -z2
    return jnp.minimum(nz, 0.0) - jnp.log(1.0 + jnp.exp2(jnp.minimum(z2, nz))) * LOG2E


def _sb_fused_kernel(tq, nq, ppc, n_pages, t_new, page_base,
                     pt_ref, bias_ref, qa_ref, qb_ref, k_ref, v_ref, ga_ref, gb_ref, uu_ref,
                     qs_ref, kn_ref, vn_ref, gs_ref, uo_ref, ck_hbm, cv_hbm,
                     op_ref, os_ref, acc_ref, c_ref, accs_ref, cs_ref, kbuf, vbuf, sem):
    step = pl.program_id(0)
    n_steps = pl.num_programs(0)
    pair = lax.rem(step, nq // 2)
    i1 = pair
    i2 = nq - 1 - pair
    chunks_per_seq = n_pages // ppc
    seq_steps = chunks_per_seq // nq
    part = lax.rem(step, seq_steps)
    n_chunks = n_steps * nq

    def chunk_copies(chunk):
        slot = lax.rem(chunk, SAMPLE_BUFS)
        seq = lax.div(chunk, chunks_per_seq)
        first_pos = (n_pages - 1) - lax.rem(chunk, chunks_per_seq) * ppc
        copies = []
        for j in range(ppc):
            page = page_base + pt_ref[seq * n_pages + first_pos - j]
            copies.append(pltpu.make_async_copy(ck_hbm.at[page], kbuf.at[slot, j], sem.at[slot, 0]))
            copies.append(pltpu.make_async_copy(cv_hbm.at[page], vbuf.at[slot, j], sem.at[slot, 1]))
        return copies

    def start_chunk(chunk):
        for cp in chunk_copies(chunk):
            cp.start()

    def wait_chunk(chunk):
        slot = lax.rem(chunk, SAMPLE_BUFS)
        pltpu.make_async_copy(ck_hbm.at[pl.ds(0, ppc)], kbuf.at[slot], sem.at[slot, 0]).wait()
        pltpu.make_async_copy(cv_hbm.at[pl.ds(0, ppc)], vbuf.at[slot], sem.at[slot, 1]).wait()

    def start_next(chunk):
        @pl.when(chunk + SAMPLE_BUFS < n_chunks)
        def _():
            start_chunk(chunk + SAMPLE_BUFS)

    @pl.when(step == 0)
    def _():
        for c in range(SAMPLE_BUFS):
            start_chunk(jnp.int32(c))

    lanes = PAGE_SIZE * SB_HEADS
    lane =lax.broadcasted_iota(jnp.int32, (t_new, lanes), 1)
    lane_head = lane & (SB_HEADS - 1)
    biases = [bias_ref[h] * LOG2E for h in range(SB_HEADS)]
    bias_lanes = jnp.full((t_new, lanes), biases[0], F32)
    for h in range(1, SB_HEADS):
        bias_lanes = jnp.where(lane_head == h, biases[h], bias_lanes)
    qs = qs_ref[...]
    q_all = jnp.concatenate([qs[:, HEAD_DIM * h:HEAD_DIM * (h + 1)] for h in range(SB_HEADS)],
                            axis=0).astype(BF16)

    def page_scores(kpages, n, mask):
        s_cat = _dot_nt(q_all, kpages.astype(BF16))
        width = kpages.shape[0] // n
        return [scores(s_cat[:, width * p:width * (p + 1)], mask) for p in range(n)]

    def scores(s_all, mask):
        width = s_all.shape[1]
        head = lane_head[:, :width]
        sc = s_all[0:t_new, :]
        for h in range(1, SB_HEADS):
            sc = jnp.where(head == h, s_all[t_new * h:t_new * (h + 1), :], sc)
        z2 = sc * (ATTN_SCALE * LOG2E) + bias_lanes[:, :width]
        lf = _log2_fail(z2)
        if mask is not None:
            lf = jnp.where(mask, lf, 0.0)
        blocks = jnp.concatenate([lf[:, HEAD_DIM * j:HEAD_DIM * (j + 1)]
                                  for j in range(width // HEAD_DIM)], axis=0)
        hi, lo = _split2(blocks)
        return z2, jnp.concatenate([hi, lo], axis=1)

    def stacked_sums(operands, matrix):
        rows = operands[0].shape[0]
        res = _dot(jnp.concatenate(operands, axis=0), matrix)
        return [res[rows * i:rows * (i + 1), :] for i in range(len(operands))]

    def weights(z2, res, mask, run):
        width = z2.shape[1]
        head = lane_head[:, :width]
        n_blk = width // HEAD_DIM
        ws = [None] * n_blk
        for j in reversed(range(n_blk)):
            rows = slice(t_new * j, t_new * (j + 1))
            logw = z2[:, HEAD_DIM * j:HEAD_DIM * (j + 1)] + res[rows, :HEAD_DIM]
            if run is not None:
                logw = logw + run
            ws[j] = jnp.exp2(logw)
            tot = res[rows, HEAD_DIM:]
            run = tot if run is None else run + tot
        w = jnp.concatenate(ws, axis=1)
        if mask is not None:
            w = jnp.where(mask, w, 0.0)
        w_all = jnp.concatenate([jnp.where(head == h, w, 0.0) for h in range(SB_HEADS)],
                                axis=0).astype(BF16)
        return w_all, run

    def new_keys():
        width = kn_ref.shape[0]
        mask = ((lane >> SB_HEAD_BITS) < lax.broadcasted_iota(jnp.int32, (t_new, lanes), 0)
                )[:, :width]
        (z2, hilo), = page_scores(kn_ref[...], 1, mask)
        res, = stacked_sums([hilo], uo_ref[...])
        w_all, run = weights(z2, res, mask, None)
        cs_ref[...] = run
        accs_ref[...] = _dot(w_all, vn_ref[...].astype(BF16))

    uu = uu_ref[...]
    cols = [slice(HEAD_DIM * h, HEAD_DIM * (h + 1)) for h in range(SB_HEADS)]

    def section(q_ref, start, diag, first, chunk):
        kb = k_ref[pl.ds(start, tq), :]
        vb = v_ref[pl.ds(start, tq), :]
        n_pages_here = 0 if chunk is None else ppc
        slot = None if chunk is None else lax.rem(chunk, SAMPLE_BUFS)
        if n_pages_here:
            run = cs_ref[...]
            acc = accs_ref[...]
        p_parts, s_parts = {}, {}
        n_idx = max(SB_HEADS, n_pages_here)

        raw = [_dot_nt(q_ref[:, cols[h]], kb[:, cols[h]]) for h in range(SB_HEADS)]
        if n_pages_here:
            s_split = page_scores(kbuf[slot].reshape(ppc * lanes, HEAD_DIM), ppc, None)
        split = []
        for h in range(SB_HEADS):
            z2 = raw[h] * (ATTN_SCALE * LOG2E) + biases[h]
            lf = _log2_fail(z2)
            if diag is not None:
                lf = jnp.where(diag, lf, 0.0)
            hi, lo = _split2(lf)
            split.append((z2, jnp.concatenate([hi, lo], axis=1)))
        incl = stacked_sums([hilo for _, hilo in split], uu)
        for h in range(SB_HEADS):
            p_parts[h] = (split[h][0], incl[h])
        if n_pages_here:
            res = stacked_sums([hilo for _, hilo in s_split], uo_ref[...])
            for idx in range(n_pages_here):
                s_parts[idx] = (s_split[idx][0], res[idx])
        w_pages = []
        for idx in range(n_idx):
            if idx < SB_HEADS:
                z2, incl_h = p_parts[idx]
                logw = z2 + incl_h
                if not first:
                    logw = logw + c_ref[idx]
                w = jnp.exp2(logw)
                if diag is not None:
                    w = jnp.where(diag, w, 0.0)
                pv = _dot(w.astype(BF16), vb[:, cols[idx]])
                total = incl_h[:, 0:1]
                if first:
                    acc_ref[:, cols[idx]] = pv
                    c_ref[idx] = total
                else:
                    acc_ref[:, cols[idx]] += pv
                    c_ref[idx] += total
            if idx < n_pages_here:
                w_all, run = weights(*s_parts[idx], None, run)
                w_pages.append(w_all)
        if n_pages_here:
            cs_ref[...] = run
            accs_ref[...] = acc + _dot(jnp.concatenate(w_pages, axis=1),
                                       vbuf[slot].reshape(ppc * lanes, HEAD_DIM).astype(BF16))

    def finish(i, g_ref):
        rows = pl.ds(pl.multiple_of(i * tq, tq), tq)
        op_ref[rows, :] = (acc_ref[...] * _silu(g_ref[...])).astype(op_ref.dtype)

    row = lax.broadcasted_iota(jnp.int32, (tq, tq), 0)
    col = lax.broadcasted_iota(jnp.int32, (tq, tq), 1)
    diag = col < row
    chunk0 = step * nq

    @pl.when(part == 0)
    def _():
        new_keys()

    section(qa_ref, pl.multiple_of(i1 * tq, tq), diag, True, None)

    def body_a(t, carry):
        chunk = chunk0 + t
        wait_chunk(chunk)
        section(qa_ref, pl.multiple_of((i1 - 1 - t) * tq, tq), None, False, chunk)
        start_next(chunk)
        return carry

    lax.fori_loop(0, i1, body_a, 0)
    finish(i1, ga_ref)

    chunk = chunk0 + i1
    wait_chunk(chunk)
    section(qb_ref, pl.multiple_of(i2 * tq, tq), diag, True, chunk)
    start_next(chunk)

    def body_b(t, carry):
        chunk = chunk0 + i1 + 1 + t
        wait_chunk(chunk)
        section(qb_ref, pl.multiple_of((i2 - 1 - t) * tq, tq), None, False, chunk)
        start_next(chunk)
        return carry

    lax.fori_loop(0, i2, body_b, 0)
    finish(i2, gb_ref)

    @pl.when(part == seq_steps - 1)
    def _():
        g = gs_ref[...]
        for h in range(SB_HEADS):
            os_ref[:, cols[h]] = accs_ref[t_new * h:t_new * (h + 1), :] * _silu(g[:, cols[h]])


def _sb_fused(q, k, v, g, batch, seq, tq, q_s, k_new, v_new, g_s, cache_k, cache_v, layer,
              page_table, sb_bias, t_new):
    n_seq, n_pages = page_table.shape
    n_pool = cache_k.shape[1]
    nq = seq // tq
    n_steps = batch * (nq // 2)
    assert nq % 2 == 0 and (n_seq * n_pages) % (n_steps * nq) == 0
    ppc = n_seq * n_pages // (n_steps * nq)
    assert n_pages % (ppc * nq) == 0
    seq_steps = n_pages // (ppc * nq)
    page_rows = PAGE_SIZE * SB_HEADS

    q3, k3, v3, g3 = (a.reshape(batch, seq, SB_WIDTH) for a in (q, k, v, g))
    u = (jnp.arange(tq)[:, None] >= jnp.arange(tq)[None, :]).astype(BF16)
    uu = jnp.concatenate([u, u], axis=0)

    new_tokens = -(-t_new * SB_HEADS // HEAD_DIM) * HEAD_DIM // SB_HEADS
    new_rows = new_tokens * SB_HEADS

    def as_page(a):
        a = a.reshape(n_seq, t_new, SB_HEADS, HEAD_DIM)
        a = jnp.pad(a, ((0, 0), (0, new_tokens - t_new), (0, 0), (0, 0)))
        return a.reshape(n_seq * new_rows, HEAD_DIM)

    cache_k = cache_k.reshape(-1, page_rows, HEAD_DIM)
    cache_v = cache_v.reshape(-1, page_rows, HEAD_DIM)
    r = jnp.arange(HEAD_DIM)
    same_head = (r[:, None] % SB_HEADS) == (r[None, :] % SB_HEADS)
    not_earlier = (r[:, None] // SB_HEADS) >= (r[None, :] // SB_HEADS)
    uo = jnp.concatenate([same_head & not_earlier, same_head], axis=1).astype(BF16)
    uo = jnp.concatenate([uo, uo], axis=0)

    half = nq // 2
    b_of = lambda s: s // half
    qa_map = lambda s, pt: (b_of(s), s % half, 0)
    qb_map = lambda s, pt: (b_of(s), nq - 1 - s % half, 0)
    seq_map = lambda s, pt: (b_of(s), 0, 0)
    samp_map = lambda s, pt: (s // seq_steps, 0)
    const = lambda s, pt: (0, 0)
    kern = functools.partial(_sb_fused_kernel, tq, nq, ppc, n_pages, t_new, layer * n_pool)
    grid_spec = pltpu.PrefetchScalarGridSpec(
        num_scalar_prefetch=1,
        grid=(n_steps,),
        in_specs=[
            pl.BlockSpec(memory_space=pltpu.SMEM),
            pl.BlockSpec((None, tq, SB_WIDTH), qa_map),
            pl.BlockSpec((None, tq, SB_WIDTH), qb_map),
            pl.BlockSpec((None, seq, SB_WIDTH), seq_map, pipeline_mode=pl.Buffered(1)),
            pl.BlockSpec((None, seq, SB_WIDTH), seq_map, pipeline_mode=pl.Buffered(1)),
            pl.BlockSpec((None, tq, SB_WIDTH), qa_map),
            pl.BlockSpec((None, tq, SB_WIDTH), qb_map),
            pl.BlockSpec((2 * tq, tq), const),
            pl.BlockSpec((t_new, SB_WIDTH), samp_map),
            pl.BlockSpec((new_rows, HEAD_DIM), samp_map),
            pl.BlockSpec((new_rows, HEAD_DIM), samp_map),
            pl.BlockSpec((t_new, SB_WIDTH), samp_map),
            pl.BlockSpec((2 * HEAD_DIM, 2 * HEAD_DIM), const),
            pl.BlockSpec(memory_space=pl.ANY),
            pl.BlockSpec(memory_space=pl.ANY),
        ],
        out_specs=[
            pl.BlockSpec((None, seq, SB_WIDTH), seq_map, pipeline_mode=pl.Buffered(1)),
            pl.BlockSpec((t_new, SB_WIDTH), samp_map),
        ],
        scratch_shapes=[
            pltpu.VMEM((tq, SB_WIDTH), F32),
            pltpu.VMEM((SB_HEADS, tq, 1), F32),
            pltpu.VMEM((SB_HEADS * t_new, HEAD_DIM), F32),
            pltpu.VMEM((t_new, HEAD_DIM), F32),
            pltpu.VMEM((SAMPLE_BUFS, ppc, page_rows, HEAD_DIM), F32),
            pltpu.VMEM((SAMPLE_BUFS, ppc, page_rows, HEAD_DIM), F32),
            pltpu.SemaphoreType.DMA((SAMPLE_BUFS, 2)),
        ],
    )
    out_p, out_s = pl.pallas_call(
        kern,
        grid_spec=grid_spec,
        out_shape=[jax.ShapeDtypeStruct((batch, seq, SB_WIDTH), BF16),
                   jax.ShapeDtypeStruct((n_seq * t_new, SB_WIDTH), F32)],
        compiler_params=pltpu.CompilerParams(
            dimension_semantics=("arbitrary",), vmem_limit_bytes=VMEM_LIMIT_FUSED),
        name="sb_fused",
    )(page_table.reshape(-1), sb_bias, q3, q3, k3, v3, g3, g3, uu,
      q_s, as_page(k_new), as_page(v_new), g_s, uo, cache_k, cache_v)
    return out_p.reshape(batch * seq, SB_WIDTH), out_s


def _ssd_kernel(length, n_par, *refs):
    per_seq_in, shared, per_seq_out = refs[:5], refs[5:13], refs[13:]
    pre_ref, h0_ref = per_seq_in[3:5]
    ext_ref, st_ref = per_seq_out[3:5]

    @pl.when(pl.program_id(1) == 0)
    def _():
        ext_ref[:, 0:SUBLANES, :] = pre_ref[...]
        st_ref[...] = h0_ref[...]

    for s in range(n_par):
        _ssd_chunk(length, *[r.at[s] for r in per_seq_in[:3]], *shared,
                   *[r.at[s] for r in per_seq_out])


def _pad_rows(a, rows):
    if a.shape[0] == rows:
        return a
    return jnp.concatenate([a, jnp.zeros((rows - a.shape[0], a.shape[1]), a.dtype)], axis=0)


def _ssd_chunk(length, xbc_ref, z_ref, dt_ref, cw_ref, cb_ref, dtb_ref,
               alog_ref, dsk_ref, nw_ref, ltri_ref, e_ref, out_ref, cnew_ref, snew_ref,
               ext_ref, st_ref):
    L = SSD_CHUNK
    P = SSD_HEAD_DIM

    ext_ref[SUBLANES:SUBLANES + L, :] = _pad_rows(xbc_ref[...], L)
    cw = cw_ref[...]
    conv = cb_ref[...]
    for j in range(CONV_WIDTH):
        off = SUBLANES - (CONV_WIDTH - 1) + j
        conv = conv + ext_ref[off:off + L, :] * cw[j:j + 1, :]
    act = _silu(conv)
    tail = ext_ref[length:length + SUBLANES, :]
    cnew_ref[...] = tail
    ext_ref[0:SUBLANES, :] = tail

    xs = act[:, :SSD_WIDTH]
    bm = act[:, SSD_WIDTH:SSD_WIDTH + SSD_GROUPS * SSD_STATE]
    cm = act[:, SSD_WIDTH + SSD_GROUPS * SSD_STATE:]

    x_dt = _pad_rows(dt_ref[...], L) + dtb_ref[...]
    dt = jnp.maximum(x_dt, 0.0) + jnp.log1p(jnp.exp(-jnp.abs(x_dt)))
    if length < L:
        valid = lax.broadcasted_iota(jnp.int32, dt.shape, 0) < length
        dt = jnp.where(valid, dt, 0.0)
    da = dt * (-jnp.exp(alog_ref[...]))
    cs = _dot_exact_rhs(ltri_ref[...], da)
    cs_t = cs.T
    e = e_ref[...]
    dt_x, cs_x = _dots_exact_lhs([dt, cs], e)
    xdt = xs * dt_x
    ecs = jnp.exp(cs_x)
    xw_t = (xdt * jnp.exp(cs_x[L - 1:L, :] - cs_x)).T
    xdt16 = xdt.astype(BF16)

    row = lax.broadcasted_iota(jnp.int32, (L, L), 0)
    col = lax.broadcasted_iota(jnp.int32, (L, L), 1)
    causal = col <= row
    heads_per_group = SSD_HEADS // SSD_GROUPS
    gw = heads_per_group * P
    y_diag, y_off = [], []
    for g in range(SSD_GROUPS):
        bg = bm[:, SSD_STATE * g:SSD_STATE * (g + 1)].astype(BF16)
        cg = cm[:, SSD_STATE * g:SSD_STATE * (g + 1)].astype(BF16)
        cb = _dot_nt(cg, bg)
        prev = st_ref[gw * g:gw * (g + 1), :]
        y_off.append(_dot_nt(cg, prev.astype(BF16)))
        new = _dot(xw_t[gw * g:gw * (g + 1), :].astype(BF16), bg)
        for r in range(heads_per_group):
            h = heads_per_group * g + r
            seg = cs[:, h:h + 1] - cs_t[h:h + 1, :]
            decay = jnp.exp(jnp.where(causal, seg, -jnp.inf))
            y_diag.append(_dot((cb * decay).astype(BF16), xdt16[:, P * h:P * (h + 1)]))
            chunk_decay = jnp.exp(cs[L - 1:L, h:h + 1])
            st_ref[P * h:P * (h + 1), :] = (prev[P * r:P * (r + 1), :] * chunk_decay
                                            + new[P * r:P * (r + 1), :])
    snew_ref[...] = st_ref[...]
    y = (jnp.concatenate(y_diag, axis=1) + jnp.concatenate(y_off, axis=1) * ecs
         + xs * dsk_ref[...])
    gated = y[:length] * _silu(z_ref[...])
    ms = jnp.mean(gated * gated, axis=-1, keepdims=True)
    out_ref[...] = (gated * lax.rsqrt(ms + EPS) * nw_ref[...]).astype(out_ref.dtype)


def _ssd(xbc, zdt, prefix, h0, conv_w, conv_b, dt_bias, a_log, d_skip, ssd_norm_w,
         batch, n_chunks, length, out_dtype):
    L = SSD_CHUNK
    pre = jnp.pad(prefix, ((0, 0), (SUBLANES - (CONV_WIDTH - 1), 0), (0, 0)))
    pad_h = lambda a: jnp.pad(a, (0, HEAD_DIM - SSD_HEADS)).reshape(1, HEAD_DIM)
    ltri = (jnp.arange(L)[:, None] >= jnp.arange(L)[None, :]).astype(BF16)
    expand = (jnp.arange(HEAD_DIM)[:, None] == jnp.arange(SSD_WIDTH)[None, :] // SSD_HEAD_DIM
              ).astype(BF16)
    dsk = jnp.repeat(d_skip, SSD_HEAD_DIM).reshape(1, SSD_WIDTH)
    z_blk = SSD_WIDTH // HEAD_DIM
    n_par = SSD_SEQS_PER_STEP
    assert batch % n_par == 0 and length <= L and (length == L or n_chunks == 1)
    rows = n_chunks * length
    xbc3 = xbc.reshape(batch, rows, XBC_WIDTH)
    zdt3 = zdt.reshape(batch, rows, ZDT_WIDTH)
    const = lambda b, c: (0, 0)
    seq = lambda b, c: (b, 0, 0)
    kern = functools.partial(_ssd_kernel, length, n_par)
    out, conv_new, ssm_new = pl.pallas_call(
        kern,
        grid=(batch // n_par, n_chunks),
        in_specs=[
            pl.BlockSpec((n_par, length, XBC_WIDTH), lambda b, c: (b, c, 0)),
            pl.BlockSpec((n_par, length, SSD_WIDTH), lambda b, c: (b, c, 0)),
            pl.BlockSpec((n_par, length, HEAD_DIM), lambda b, c: (b, c, z_blk)),
            pl.BlockSpec((n_par, SUBLANES, XBC_WIDTH), seq),
            pl.BlockSpec((n_par, SSD_WIDTH, SSD_STATE), seq),
            pl.BlockSpec((CONV_WIDTH, XBC_WIDTH), const),
            pl.BlockSpec((1, XBC_WIDTH), const),
            pl.BlockSpec((1, HEAD_DIM), const),
            pl.BlockSpec((1, HEAD_DIM), const),
            pl.BlockSpec((1, SSD_WIDTH), const),
            pl.BlockSpec((1, SSD_WIDTH), const),
            pl.BlockSpec((L, L), const),
            pl.BlockSpec((HEAD_DIM, SSD_WIDTH), const),
        ],
        out_specs=[
            pl.BlockSpec((n_par, length, SSD_WIDTH), lambda b, c: (b, c, 0)),
            pl.BlockSpec((n_par, SUBLANES, XBC_WIDTH), seq),
            pl.BlockSpec((n_par, SSD_WIDTH, SSD_STATE), seq),
        ],
        out_shape=[
            jax.ShapeDtypeStruct((batch, rows, SSD_WIDTH), out_dtype),
            jax.ShapeDtypeStruct((batch, SUBLANES, XBC_WIDTH), F32),
            jax.ShapeDtypeStruct((batch, SSD_WIDTH, SSD_STATE), F32),
        ],
        scratch_shapes=[pltpu.VMEM((n_par, SUBLANES + L, XBC_WIDTH), F32),
                        pltpu.VMEM((n_par, SSD_WIDTH, SSD_STATE), F32)],
        compiler_params=pltpu.CompilerParams(
            dimension_semantics=("parallel", "arbitrary"), vmem_limit_bytes=VMEM_LIMIT),
        name="ssd_scan",
    )(xbc3, zdt3, zdt3, pre, h0.reshape(batch, SSD_WIDTH, SSD_STATE), conv_w,
      conv_b.reshape(1, XBC_WIDTH), pad_h(dt_bias), pad_h(a_log), dsk,
      ssd_norm_w.reshape(1, SSD_WIDTH), ltri, expand)
    return out.reshape(batch * rows, SSD_WIDTH), conv_new, ssm_new


def _mem_attn_kernel(q_ref, g_ref, k_ref, v_ref, o_ref):
    for h in range(MEM_HEADS):
        cols = slice(HEAD_DIM * h, HEAD_DIM * (h + 1))
        s = _dot_nt(q_ref[:, cols].astype(BF16), k_ref[:, cols].astype(BF16)) * ATTN_SCALE
        p = jnp.exp(s - jnp.max(s, axis=-1, keepdims=True))
        den = jnp.sum(p, axis=-1, keepdims=True)
        o = _dot(p.astype(BF16), v_ref[:, cols].astype(BF16)) / den
        o_ref[:, cols] = (o * _silu(g_ref[:, cols])).astype(o_ref.dtype)


def _mem_attn(q, g, mem_k, mem_v, batch, t, tq, out_dtype):
    nq = t // tq
    return pl.pallas_call(
        _mem_attn_kernel,
        grid=(batch, nq),
        in_specs=[
            pl.BlockSpec((tq, MEM_WIDTH), lambda b, i: (b * nq + i, 0)),
            pl.BlockSpec((tq, MEM_WIDTH), lambda b, i: (b * nq + i, 0)),
            pl.BlockSpec((None, MEM_TOKENS, MEM_WIDTH), lambda b, i: (b, 0, 0)),
            pl.BlockSpec((None, MEM_TOKENS, MEM_WIDTH), lambda b, i: (b, 0, 0)),
        ],
        out_specs=pl.BlockSpec((tq, MEM_WIDTH), lambda b, i: (b * nq + i, 0)),
        out_shape=jax.ShapeDtypeStruct((batch * t, MEM_WIDTH), out_dtype),
        compiler_params=pltpu.CompilerParams(
            dimension_semantics=("parallel", "parallel"), vmem_limit_bytes=VMEM_LIMIT),
        name="mem_attn",
    )(q, g, mem_k, mem_v)


def _mem_attn_rows_kernel(t, q_ref, g_ref, k_ref, v_ref, o_ref):
    q = q_ref[...]
    g = g_ref[...]
    cols = [slice(HEAD_DIM * h, HEAD_DIM * (h + 1)) for h in range(MEM_HEADS)]
    q_all = jnp.concatenate([q[:, c] for c in cols], axis=0).astype(BF16)
    s = _dot_nt(q_all, k_ref[...].astype(BF16)) * ATTN_SCALE
    n_rows = s.shape[1]
    row_head = jnp.concatenate([jnp.full((t, n_rows), h, jnp.int32) for h in range(MEM_HEADS)],
                               axis=0)
    lane_head = lax.broadcasted_iota(jnp.int32, s.shape, 1) & (MEM_HEADS - 1)
    s = jnp.where(row_head == lane_head, s, -jnp.inf)
    p = jnp.exp(s - jnp.max(s, axis=-1, keepdims=True))
    den = jnp.sum(p, axis=-1, keepdims=True)
    o = _dot(p.astype(BF16), v_ref[...].astype(BF16)) / den
    for h in range(MEM_HEADS):
        o_ref[:, cols[h]] = (o[t * h:t * (h + 1), :] * _silu(g[:, cols[h]])).astype(o_ref.dtype)


def _mem_attn_rows(q, g, mem_k, mem_v, batch, t, out_dtype):
    rows = MEM_TOKENS * MEM_HEADS
    kern = functools.partial(_mem_attn_rows_kernel, t)
    return pl.pallas_call(
        kern,
        grid=(batch,),
        in_specs=[
            pl.BlockSpec((t, MEM_WIDTH), lambda b: (b, 0)),
            pl.BlockSpec((t, MEM_WIDTH), lambda b: (b, 0)),
            pl.BlockSpec((None, rows, HEAD_DIM), lambda b: (b, 0, 0)),
            pl.BlockSpec((None, rows, HEAD_DIM), lambda b: (b, 0, 0)),
        ],
        out_specs=pl.BlockSpec((t, MEM_WIDTH), lambda b: (b, 0)),
        out_shape=jax.ShapeDtypeStruct((batch * t, MEM_WIDTH), out_dtype),
        compiler_params=pltpu.CompilerParams(
            dimension_semantics=("parallel",), vmem_limit_bytes=VMEM_LIMIT),
        name="mem_attn_rows",
    )(q, g, mem_k, mem_v)


def _out_proj_kernel(x_ref, sb_ref, ssd_ref, mo_ref, w_ref, o_ref):
    mix = jnp.concatenate([sb_ref[...].astype(BF16), ssd_ref[...].astype(BF16),
                           mo_ref[...].astype(BF16)], axis=-1)
    for c in range(0, o_ref.shape[1], PROJ_CHUNK):
        cols = slice(c, c + PROJ_CHUNK)
        o_ref[:, cols] = x_ref[:, cols] + _dot(mix, w_ref[:, cols].astype(BF16))


def _out_proj(x, sb, ssd, mo, w_out, tm, tn):
    t, d = x.shape
    return pl.pallas_call(
        _out_proj_kernel,
        grid=(t // tm, d // tn),
        in_specs=[
            pl.BlockSpec((tm, tn), lambda m, n: (m, n)),
            pl.BlockSpec((tm, SB_WIDTH), lambda m, n: (m, 0)),
            pl.BlockSpec((tm, SSD_WIDTH), lambda m, n: (m, 0)),
            pl.BlockSpec((tm, MEM_WIDTH), lambda m, n: (m, 0)),
            pl.BlockSpec((w_out.shape[0], tn), lambda m, n: (0, n), pipeline_mode=pl.Buffered(1)),
        ],
        out_specs=pl.BlockSpec((tm, tn), lambda m, n: (m, n)),
        out_shape=jax.ShapeDtypeStruct((t, d), F32),
        compiler_params=pltpu.CompilerParams(
            dimension_semantics=("parallel", "arbitrary"), vmem_limit_bytes=VMEM_LIMIT),
        name="out_proj",
    )(x, sb, ssd, mo, w_out)


def _in_proj_plan(act_dtype):
    lowp = act_dtype == BF16
    outs, plan = [], []

    def add(width, dtype):
        outs.append((width, dtype))
        return len(outs) - 1

    q = add(SB_WIDTH, act_dtype)
    plan.append(((0, SB_WIDTH, 0, None if lowp else q, q if lowp else None),))
    k32 = add(SB_WIDTH, F32)
    k16 = add(SB_WIDTH, BF16) if lowp else None
    plan.append(((0, SB_WIDTH, 1, k32, k16),))
    v32 = add(SB_WIDTH, F32)
    v16 = add(SB_WIDTH, BF16) if lowp else None
    plan.append(((0, SB_WIDTH, None, v32, v16),))
    g = add(SB_WIDTH, F32)
    plan.append(((0, SB_WIDTH, None, g, None),))
    xbc = add(XBC_WIDTH, F32)
    plan.append(((0, XBC_WIDTH, None, xbc, None),))
    zdt = add(ZDT_WIDTH, F32)
    plan.append(((0, ZDT_WIDTH, None, zdt, None),))
    mq = add(MEM_WIDTH, act_dtype)
    mg = add(MEM_WIDTH, F32)
    plan.append(((0, MEM_WIDTH, 2, None if lowp else mq, mq if lowp else None),
                 (MEM_WIDTH, MEM_WIDTH, None, mg, None)))
    names = dict(q=q, k32=k32, k16=k16, v32=v32, v16=v16, g=g, xbc=xbc, zdt=zdt, mq=mq, mg=mg)
    return tuple(plan), outs, names


_O_Z = 4 * SB_WIDTH
_O_XBC = _O_Z + SSD_WIDTH
_O_DT = _O_XBC + XBC_WIDTH
_O_MEM = _O_DT + SSD_HEADS
_IN_WIDTH = _O_MEM + 2 * MEM_WIDTH
_CAT_WIDTH = 7 * PROJ_TN
W_PREP_COLS = 256
BF16_ROWS = 16


def _w_prep_kernel(w_ref, o_ref):
    def put(dst, src, rows):
        o_ref[dst:dst + rows, :] = w_ref[src:src + rows, :].astype(BF16)

    cols = w_ref.shape[1]
    put(0, 0, _O_Z)
    put(_O_Z, _O_XBC, XBC_WIDTH)
    put(_O_Z + XBC_WIDTH, _O_Z, SSD_WIDTH)
    dt0 = _O_Z + XBC_WIDTH + SSD_WIDTH
    o_ref[dt0:dt0 + BF16_ROWS, :] = jnp.concatenate(
        [w_ref[_O_DT:_O_MEM, :], jnp.zeros((BF16_ROWS - SSD_HEADS, cols), F32)], axis=0
    ).astype(BF16)
    o_ref[dt0 + BF16_ROWS:6 * PROJ_TN, :] = jnp.zeros((6 * PROJ_TN - dt0 - BF16_ROWS, cols), BF16)
    put(6 * PROJ_TN, _O_MEM, 2 * MEM_WIDTH)


def _rearranged_w_in(w_t):
    d = w_t.shape[1]
    assert w_t.shape[0] == _IN_WIDTH and d % W_PREP_COLS == 0
    return pl.pallas_call(
        _w_prep_kernel,
        grid=(d // W_PREP_COLS,),
        in_specs=[pl.BlockSpec((_IN_WIDTH, W_PREP_COLS), lambda c: (0, c))],
        out_specs=pl.BlockSpec((_CAT_WIDTH, W_PREP_COLS), lambda c: (0, c)),
        out_shape=jax.ShapeDtypeStruct((_CAT_WIDTH, d), BF16),
        compiler_params=pltpu.CompilerParams(
            dimension_semantics=("parallel",), vmem_limit_bytes=VMEM_LIMIT),
        name="w_prep",
    )(w_t)


def kernel(x_prompt, x_sample, cache_sb_k, cache_sb_v, state_ssm, state_conv, cache_mem_k,
           cache_mem_v, page_table, mem_prompt, norm_w, w_in, sb_q_norm, sb_k_norm, sb_bias,
           conv_w, conv_b, dt_bias, a_log, d_skip, ssd_norm_w, mem_norm_w, w_mem_kv, mem_q_norm,
           mem_k_norm, w_out):
    depth = w_in.shape[0]
    assert depth == 1
    layer = 0
    bp, sp, d = x_prompt.shape
    bs, ts, _ = x_sample.shape
    n_pool = cache_sb_k.shape[1]
    L = SSD_CHUNK

    w_cat = _rearranged_w_in(w_in[layer].T)
    w_o = w_out[layer]
    head_norms = jnp.concatenate(
        [sb_q_norm[layer][None], sb_k_norm[layer][None], mem_q_norm[layer][None],
         mem_k_norm[layer][None], jnp.zeros((SUBLANES - 4, HEAD_DIM), F32)], axis=0)
    ssd_params = (conv_w[layer], conv_b[layer], dt_bias[layer], a_log[layer], d_skip[layer],
                  ssd_norm_w[layer])

    xp = x_prompt.reshape(bp * sp, d)
    mem_plan = (((0, MEM_WIDTH, 3, 0, None), (MEM_WIDTH, MEM_WIDTH, None, 1, None)),)
    (mk, mv), _ = _proj(mem_prompt.reshape(bp * MEM_TOKENS, d), mem_norm_w[layer],
                        w_mem_kv[layer], head_norms, mem_plan,
                        [(MEM_WIDTH, F32), (MEM_WIDTH, F32)], tm=PROJ_TM,
                        w_rows_are_outputs=False)
    plan, outs, nm = _in_proj_plan(BF16)
    xs = x_sample.reshape(bs * ts, d)
    plan_s, outs_s, ns = _in_proj_plan(F32)
    pr, ps = _proj(xp, norm_w[layer], w_cat, head_norms, plan, outs, tm=PROJ_TM,
                   side=(xs, plan_s, outs_s))
    sb, sb_s = _sb_fused(pr[nm['q']], pr[nm['k16']], pr[nm['v16']], pr[nm['g']], bp, sp, ATTN_TQ,
                         ps[ns['q']], ps[ns['k32']], ps[ns['v32']], ps[ns['g']],
                         cache_sb_k, cache_sb_v, layer, page_table, sb_bias[layer], ts)
    ssd, conv_p, ssm_p = _ssd(
        pr[nm['xbc']], pr[nm['zdt']], jnp.zeros((bp, CONV_WIDTH - 1, XBC_WIDTH), F32),
        jnp.zeros((bp, SSD_HEADS, SSD_HEAD_DIM, SSD_STATE), F32), *ssd_params,
        batch=bp, n_chunks=sp // L, length=L, out_dtype=BF16)
    mo = _mem_attn(pr[nm['mq']], pr[nm['mg']], mk.reshape(bp, MEM_TOKENS, MEM_WIDTH),
                   mv.reshape(bp, MEM_TOKENS, MEM_WIDTH), bp, sp, tq=MEM_TQ, out_dtype=BF16)
    yp = _out_proj(xp, sb, ssd, mo, w_o, tm=OUT_TM, tn=d)

    ssd_s, conv_s, ssm_s = _ssd(
        ps[ns['xbc']], ps[ns['zdt']], state_conv[layer], state_ssm[layer],
        *ssd_params, batch=bs, n_chunks=1, length=ts, out_dtype=F32)
    mo_s = _mem_attn_rows(ps[ns['mq']], ps[ns['mg']],
                          cache_mem_k[layer].reshape(bs, MEM_TOKENS * MEM_HEADS, HEAD_DIM),
                          cache_mem_v[layer].reshape(bs, MEM_TOKENS * MEM_HEADS, HEAD_DIM),
                          bs, ts, out_dtype=F32)
    ys = _out_proj(xs, sb_s, ssd_s, mo_s, w_o, tm=bs * ts, tn=d)

    tail = slice(SUBLANES - (CONV_WIDTH - 1), SUBLANES)
    return (
        yp.reshape(bp, sp, d),
        ys.reshape(bs, ts, d),
        pr[nm['k32']].reshape(1, bp, sp, SB_HEADS, HEAD_DIM),
        pr[nm['v32']].reshape(1, bp, sp, SB_HEADS, HEAD_DIM),
        ssm_p.reshape(1, bp, SSD_HEADS, SSD_HEAD_DIM, SSD_STATE),
        conv_p[:, tail][None],
        mk.reshape(1, bp, MEM_TOKENS, MEM_HEADS, HEAD_DIM),
        mv.reshape(1, bp, MEM_TOKENS, MEM_HEADS, HEAD_DIM),
        ps[ns['k32']].reshape(1, bs, ts, SB_HEADS, HEAD_DIM),
        ps[ns['v32']].reshape(1, bs, ts, SB_HEADS, HEAD_DIM),
        ssm_s.reshape(1, bs, SSD_HEADS, SSD_HEAD_DIM, SSD_STATE),
        conv_s[:, tail][None],
    )
```

```python
import functools
import math

import jax
import jax.numpy as jnp
from jax import lax
from jax.experimental import pallas as pl
from jax.experimental.pallas import tpu as pltpu

F32 = jnp.float32
BF16 = jnp.bfloat16

SB_HEADS = 8
HEAD_DIM = 128
SB_WIDTH = SB_HEADS * HEAD_DIM
SSD_HEADS = 8
SSD_HEAD_DIM = 64
SSD_WIDTH = SSD_HEADS * SSD_HEAD_DIM
SSD_GROUPS = 2
SSD_STATE = 128
CONV_WIDTH = 4
XBC_WIDTH = SSD_WIDTH + 2 * SSD_GROUPS * SSD_STATE
MEM_TOKENS = 256
MEM_HEADS = 4
MEM_WIDTH = MEM_HEADS * HEAD_DIM
PAGE_SIZE = 128
EPS = 1e-6
ATTN_SCALE = HEAD_DIM ** -0.5
LOG2E = math.log2(math.e)

SSD_CHUNK = 128
PROJ_TN = 1024
PROJ_CHUNK = 256
ZDT_WIDTH = SSD_WIDTH + PROJ_CHUNK
SAMPLE_BUFS = 3
SSD_SEQS_PER_STEP = 4
SUBLANES = 8
SB_HEAD_BITS = SB_HEADS.bit_length() - 1
NORM_ROW_CHUNK = 128
PROJ_TM = 512
OUT_TM = 512
ATTN_TQ = 256
MEM_TQ = 512
VMEM_LIMIT = 56 * 1024 * 1024
VMEM_LIMIT_FUSED = 62 * 1024 * 1024

_NT = (((1,), (1,)), ((), ()))


def _dot(a, b):
    return jnp.dot(a, b, preferred_element_type=F32)


def _dot_nt(a, b):
    return lax.dot_general(a, b, _NT, preferred_element_type=F32)


def _split2(x):
    hi = x.astype(BF16)
    lo = (x - hi.astype(F32)).astype(BF16)
    return hi, lo


def _split3(x):
    hi = x.astype(BF16)
    r = x - hi.astype(F32)
    mid = r.astype(BF16)
    lo = (r - mid.astype(F32)).astype(BF16)
    return hi, mid, lo


def _dots_exact_lhs(xs, m):
    rows = xs[0].shape[0]
    r = _dot(jnp.concatenate([t for x in xs for t in _split3(x)], axis=0), m)
    part = lambda i: r[rows * i:rows * (i + 1), :]
    return [part(3 * i) + part(3 * i + 1) + part(3 * i + 2) for i in range(len(xs))]


def _dot_exact_rhs(m, x):
    n = x.shape[1]
    r = _dot(m, jnp.concatenate(_split3(x), axis=1))
    return r[:, :n] + r[:, n:2 * n] + r[:, 2 * n:]


def _silu(x):
    return x * (1.0 / (1.0 + jnp.exp(-x)))


def _rmsnorm_rows(x_ref, nw_ref, h_ref):
    rows_total = x_ref.shape[0]
    rc = min(rows_total, NORM_ROW_CHUNK)

    def body(r, carry):
        rows = pl.ds(pl.multiple_of(r * rc, rc), rc)
        xv = x_ref[rows, :]
        ms = jnp.mean(xv * xv, axis=-1, keepdims=True)
        h_ref[rows, :] = (xv * lax.rsqrt(ms + EPS) * nw_ref[...]).astype(BF16)
        return carry

    lax.fori_loop(0, rows_total // rc, body, 0)


def _proj_step(segs, h_ref, w_ref, hn_ref, outs, w_rows_are_outputs):
    for col0, width, hn_row, o32, o16 in segs:
        for c0 in range(0, width, PROJ_CHUNK):
            chunk = slice(col0 + c0, col0 + c0 + PROJ_CHUNK)
            if w_rows_are_outputs:
                y = _dot_nt(h_ref[...], w_ref[chunk, :].astype(BF16))
            else:
                y = _dot(h_ref[...], w_ref[:, chunk].astype(BF16))
            for c in range(0, PROJ_CHUNK, HEAD_DIM):
                yc = y[:, c:c + HEAD_DIM]
                if hn_row is not None:
                    ms = jnp.mean(yc * yc, axis=-1, keepdims=True)
                    yc = yc * lax.rsqrt(ms + EPS) * hn_ref[hn_row:hn_row + 1, :]
                cols = slice(c0 + c, c0 + c + HEAD_DIM)
                if o32 is not None:
                    outs[o32][:, cols] = yc
                if o16 is not None:
                    outs[o16][:, cols] = yc.astype(BF16)


def _proj_kernel(plan, n_out, side_plan, n_side, w_rows_are_outputs, *refs):
    if side_plan is None:
        x_ref, nw_ref, w_ref, hn_ref = refs[:4]
        rest = refs[4:]
    else:
        x_ref, xs_ref, nw_ref, w_ref, hn_ref = refs[:5]
        rest = refs[5:]
    outs, side_outs = rest[:n_out], rest[n_out:n_out + n_side]
    scratch = rest[n_out + n_side:]
    h_ref = scratch[0]
    m = pl.program_id(0)
    n = pl.program_id(1)

    @pl.when(n == 0)
    def _():
        _rmsnorm_rows(x_ref, nw_ref, h_ref)

    if side_plan is not None:
        hs_ref = scratch[1]

        @pl.when((n == 0) & (m == 0))
        def _():
            _rmsnorm_rows(xs_ref, nw_ref, hs_ref)

    for step, segs in enumerate(plan):
        @pl.when(n == step)
        def _(step=step, segs=segs):
            _proj_step(segs, h_ref, w_ref, hn_ref, outs, w_rows_are_outputs)
            if side_plan is not None:
                @pl.when(m == 0)
                def _():
                    _proj_step(side_plan[step], hs_ref, w_ref, hn_ref, side_outs,
                               w_rows_are_outputs)


def _proj(x, norm_w, w, head_norms, plan, out_defs, tm, side=None, w_rows_are_outputs=True):
    t, d = x.shape
    n_steps = len(plan)
    w_shape = (n_steps * PROJ_TN, d) if w_rows_are_outputs else (d, n_steps * PROJ_TN)
    assert w.shape == w_shape and t % tm == 0
    w_spec = (pl.BlockSpec((PROJ_TN, d), lambda m, n: (n, 0)) if w_rows_are_outputs
              else pl.BlockSpec((d, PROJ_TN), lambda m, n: (0, n)))
    row = lambda m, n: (m, 0)
    const = lambda m, n: (0, 0)
    in_specs = [pl.BlockSpec((tm, d), row)]
    operands = [x]
    out_specs = [pl.BlockSpec((tm, w), row) for w, _ in out_defs]
    out_shape = [jax.ShapeDtypeStruct((t, w), dt) for w, dt in out_defs]
    scratch = [pltpu.VMEM((tm, d), BF16)]
    side_plan, n_side = None, 0
    if side is not None:
        x_side, side_plan, side_defs = side
        ts = x_side.shape[0]
        assert len(side_plan) == n_steps
        n_side = len(side_defs)
        in_specs.append(pl.BlockSpec((ts, d), const))
        operands.append(x_side)
        out_specs += [pl.BlockSpec((ts, w), const) for w, _ in side_defs]
        out_shape += [jax.ShapeDtypeStruct((ts, w), dt) for w, dt in side_defs]
        scratch.append(pltpu.VMEM((ts, d), BF16))
    in_specs += [pl.BlockSpec((1, d), const), w_spec,
                 pl.BlockSpec((SUBLANES, HEAD_DIM), const)]
    operands += [norm_w.reshape(1, d), w, head_norms]
    kern = functools.partial(_proj_kernel, plan, len(out_defs), side_plan, n_side,
                             w_rows_are_outputs)
    res = pl.pallas_call(
        kern,
        grid=(t // tm, n_steps),
        in_specs=in_specs,
        out_specs=out_specs,
        out_shape=out_shape,
        scratch_shapes=scratch,
        compiler_params=pltpu.CompilerParams(
            dimension_semantics=("arbitrary", "arbitrary"), vmem_limit_bytes=VMEM_LIMIT),
        name="norm_proj",
    )(*operands)
    return res[:len(out_defs)], res[len(out_defs):]


def _log2_fail(z2):
    nz = -z2
    return jnp.minimum(nz, 0.0) - jnp.log(1.0 + jnp.exp2(jnp.minimum(z2, nz))) * LOG2E


def _sb_fused_kernel(tq, nq, ppc, n_pages, t_new, page_base,
                     pt_ref, bias_ref, qa_ref, qb_ref, k_ref, v_ref, ga_ref, gb_ref, uu_ref,
                     qs_ref, kn_ref, vn_ref, gs_ref, uo_ref, ck_hbm, cv_hbm,
                     op_ref, os_ref, acc_ref, c_ref, accs_ref, cs_ref, kbuf, vbuf, sem):
    step = pl.program_id(0)
    n_steps = pl.num_programs(0)
    pair = lax.rem(step, nq // 2)
    i1 = pair
    i2 = nq - 1 - pair
    chunks_per_seq = n_pages // ppc
    seq_steps = chunks_per_seq // nq
    part = lax.rem(step, seq_steps)
    n_chunks = n_steps * nq

    def chunk_copies(chunk):
        slot = lax.rem(chunk, SAMPLE_BUFS)
        seq = lax.div(chunk, chunks_per_seq)
        first_pos = (n_pages - 1) - lax.rem(chunk, chunks_per_seq) * ppc
        copies = []
        for j in range(ppc):
            page = page_base + pt_ref[seq * n_pages + first_pos - j]
            copies.append(pltpu.make_async_copy(ck_hbm.at[page], kbuf.at[slot, j], sem.at[slot, 0]))
            copies.append(pltpu.make_async_copy(cv_hbm.at[page], vbuf.at[slot, j], sem.at[slot, 1]))
        return copies

    def start_chunk(chunk):
        for cp in chunk_copies(chunk):
            cp.start()

    def wait_chunk(chunk):
        slot = lax.rem(chunk, SAMPLE_BUFS)
        pltpu.make_async_copy(ck_hbm.at[pl.ds(0, ppc)], kbuf.at[slot], sem.at[slot, 0]).wait()
        pltpu.make_async_copy(cv_hbm.at[pl.ds(0, ppc)], vbuf.at[slot], sem.at[slot, 1]).wait()

    def start_next(chunk):
        @pl.when(chunk + SAMPLE_BUFS < n_chunks)
        def _():
            start_chunk(chunk + SAMPLE_BUFS)

    @pl.when(step == 0)
    def _():
        for c in range(SAMPLE_BUFS):
            start_chunk(jnp.int32(c))

    lanes = PAGE_SIZE * SB_HEADS
    lane =lax.broadcasted_iota(jnp.int32, (t_new, lanes), 1)
    lane_head = lane & (SB_HEADS - 1)
    biases = [bias_ref[h] * LOG2E for h in range(SB_HEADS)]
    bias_lanes = jnp.full((t_new, lanes), biases[0], F32)
    for h in range(1, SB_HEADS):
        bias_lanes = jnp.where(lane_head == h, biases[h], bias_lanes)
    qs = qs_ref[...]
    q_all = jnp.concatenate([qs[:, HEAD_DIM * h:HEAD_DIM * (h + 1)] for h in range(SB_HEADS)],
                            axis=0).astype(BF16)

    def page_scores(kpages, n, mask):
        s_cat = _dot_nt(q_all, kpages.astype(BF16))
        width = kpages.shape[0] // n
        return [scores(s_cat[:, width * p:width * (p + 1)], mask) for p in range(n)]

    def scores(s_all, mask):
        width = s_all.shape[1]
        head = lane_head[:, :width]
        sc = s_all[0:t_new, :]
        for h in range(1, SB_HEADS):
            sc = jnp.where(head == h, s_all[t_new * h:t_new * (h + 1), :], sc)
        z2 = sc * (ATTN_SCALE * LOG2E) + bias_lanes[:, :width]
        lf = _log2_fail(z2)
        if mask is not None:
            lf = jnp.where(mask, lf, 0.0)
        blocks = jnp.concatenate([lf[:, HEAD_DIM * j:HEAD_DIM * (j + 1)]
                                  for j in range(width // HEAD_DIM)], axis=0)
        hi, lo = _split2(blocks)
        return z2, jnp.concatenate([hi, lo], axis=1)

    def stacked_sums(operands, matrix):
        rows = operands[0].shape[0]
        res = _dot(jnp.concatenate(operands, axis=0), matrix)
        return [res[rows * i:rows * (i + 1), :] for i in range(len(operands))]

    def weights(z2, res, mask, run):
        width = z2.shape[1]
        head = lane_head[:, :width]
        n_blk = width // HEAD_DIM
        ws = [None] * n_blk
        for j in reversed(range(n_blk)):
            rows = slice(t_new * j, t_new * (j + 1))
            logw = z2[:, HEAD_DIM * j:HEAD_DIM * (j + 1)] + res[rows, :HEAD_DIM]
            if run is not None:
                logw = logw + run
            ws[j] = jnp.exp2(logw)
            tot = res[rows, HEAD_DIM:]
            run = tot if run is None else run + tot
        w = jnp.concatenate(ws, axis=1)
        if mask is not None:
            w = jnp.where(mask, w, 0.0)
        w_all = jnp.concatenate([jnp.where(head == h, w, 0.0) for h in range(SB_HEADS)],
                                axis=0).astype(BF16)
        return w_all, run

    def new_keys():
        width = kn_ref.shape[0]
        mask = ((lane >> SB_HEAD_BITS) < lax.broadcasted_iota(jnp.int32, (t_new, lanes), 0)
                )[:, :width]
        (z2, hilo), = page_scores(kn_ref[...], 1, mask)
        res, = stacked_sums([hilo], uo_ref[...])
        w_all, run = weights(z2, res, mask, None)
        cs_ref[...] = run
        accs_ref[...] = _dot(w_all, vn_ref[...].astype(BF16))

    uu = uu_ref[...]
    cols = [slice(HEAD_DIM * h, HEAD_DIM * (h + 1)) for h in range(SB_HEADS)]

    def section(q_ref, start, diag, first, chunk):
        kb = k_ref[pl.ds(start, tq), :]
        vb = v_ref[pl.ds(start, tq), :]
        n_pages_here = 0 if chunk is None else ppc
        slot = None if chunk is None else lax.rem(chunk, SAMPLE_BUFS)
        if n_pages_here:
            run = cs_ref[...]
            acc = accs_ref[...]
        p_parts, s_parts = {}, {}
        n_idx = max(SB_HEADS, n_pages_here)

        raw = [_dot_nt(q_ref[:, cols[h]], kb[:, cols[h]]) for h in range(SB_HEADS)]
        if n_pages_here:
            s_split = page_scores(kbuf[slot].reshape(ppc * lanes, HEAD_DIM), ppc, None)
        split = []
        for h in range(SB_HEADS):
            z2 = raw[h] * (ATTN_SCALE * LOG2E) + biases[h]
            lf = _log2_fail(z2)
            if diag is not None:
                lf = jnp.where(diag, lf, 0.0)
            hi, lo = _split2(lf)
            split.append((z2, jnp.concatenate([hi, lo], axis=1)))
        incl = stacked_sums([hilo for _, hilo in split], uu)
        for h in range(SB_HEADS):
            p_parts[h] = (split[h][0], incl[h])
        if n_pages_here:
            res = stacked_sums([hilo for _, hilo in s_split], uo_ref[...])
            for idx in range(n_pages_here):
                s_parts[idx] = (s_split[idx][0], res[idx])
        w_pages = []
        for idx in range(n_idx):
            if idx < SB_HEADS:
                z2, incl_h = p_parts[idx]
                logw = z2 + incl_h
                if not first:
                    logw = logw + c_ref[idx]
                w = jnp.exp2(logw)
                if diag is not None:
                    w = jnp.where(diag, w, 0.0)
                pv = _dot(w.astype(BF16), vb[:, cols[idx]])
                total = incl_h[:, 0:1]
                if first:
                    acc_ref[:, cols[idx]] = pv
                    c_ref[idx] = total
                else:
                    acc_ref[:, cols[idx]] += pv
                    c_ref[idx] += total
            if idx < n_pages_here:
                w_all, run = weights(*s_parts[idx], None, run)
                w_pages.append(w_all)
        if n_pages_here:
            cs_ref[...] = run
            accs_ref[...] = acc + _dot(jnp.concatenate(w_pages, axis=1),
                                       vbuf[slot].reshape(ppc * lanes, HEAD_DIM).astype(BF16))

    def finish(i, g_ref):
        rows = pl.ds(pl.multiple_of(i * tq, tq), tq)
        op_ref[rows, :] = (acc_ref[...] * _silu(g_ref[...])).astype(op_ref.dtype)

    row = lax.broadcasted_iota(jnp.int32, (tq, tq), 0)
    col = lax.broadcasted_iota(jnp.int32, (tq, tq), 1)
    diag = col < row
    chunk0 = step * nq

    @pl.when(part == 0)
    def _():
        new_keys()

    section(qa_ref, pl.multiple_of(i1 * tq, tq), diag, True, None)

    def body_a(t, carry):
        chunk = chunk0 + t
        wait_chunk(chunk)
        section(qa_ref, pl.multiple_of((i1 - 1 - t) * tq, tq), None, False, chunk)
        start_next(chunk)
        return carry

    lax.fori_loop(0, i1, body_a, 0)
    finish(i1, ga_ref)

    chunk = chunk0 + i1
    wait_chunk(chunk)
    section(qb_ref, pl.multiple_of(i2 * tq, tq), diag, True, chunk)
    start_next(chunk)

    def body_b(t, carry):
        chunk = chunk0 + i1 + 1 + t
        wait_chunk(chunk)
        section(qb_ref, pl.multiple_of((i2 - 1 - t) * tq, tq), None, False, chunk)
        start_next(chunk)
        return carry

    lax.fori_loop(0, i2, body_b, 0)
    finish(i2, gb_ref)

    @pl.when(part == seq_steps - 1)
    def _():
        g = gs_ref[...]
        for h in range(SB_HEADS):
            os_ref[:, cols[h]] = accs_ref[t_new * h:t_new * (h + 1), :] * _silu(g[:, cols[h]])


def _sb_fused(q, k, v, g, batch, seq, tq, q_s, k_new, v_new, g_s, cache_k, cache_v, layer,
              page_table, sb_bias, t_new):
    n_seq, n_pages = page_table.shape
    n_pool = cache_k.shape[1]
    nq = seq // tq
    n_steps = batch * (nq // 2)
    assert nq % 2 == 0 and (n_seq * n_pages) % (n_steps * nq) == 0
    ppc = n_seq * n_pages // (n_steps * nq)
    assert n_pages % (ppc * nq) == 0
    seq_steps = n_pages // (ppc * nq)
    page_rows = PAGE_SIZE * SB_HEADS

    q3, k3, v3, g3 = (a.reshape(batch, seq, SB_WIDTH) for a in (q, k, v, g))
    u = (jnp.arange(tq)[:, None] >= jnp.arange(tq)[None, :]).astype(BF16)
    uu = jnp.concatenate([u, u], axis=0)

    new_tokens = -(-t_new * SB_HEADS // HEAD_DIM) * HEAD_DIM // SB_HEADS
    new_rows = new_tokens * SB_HEADS

    def as_page(a):
        a = a.reshape(n_seq, t_new, SB_HEADS, HEAD_DIM)
        a = jnp.pad(a, ((0, 0), (0, new_tokens - t_new), (0, 0), (0, 0)))
        return a.reshape(n_seq * new_rows, HEAD_DIM)

    cache_k = cache_k.reshape(-1, page_rows, HEAD_DIM)
    cache_v = cache_v.reshape(-1, page_rows, HEAD_DIM)
    r = jnp.arange(HEAD_DIM)
    same_head = (r[:, None] % SB_HEADS) == (r[None, :] % SB_HEADS)
    not_earlier = (r[:, None] // SB_HEADS) >= (r[None, :] // SB_HEADS)
    uo = jnp.concatenate([same_head & not_earlier, same_head], axis=1).astype(BF16)
    uo = jnp.concatenate([uo, uo], axis=0)

    half = nq // 2
    b_of = lambda s: s // half
    qa_map = lambda s, pt: (b_of(s), s % half, 0)
    qb_map = lambda s, pt: (b_of(s), nq - 1 - s % half, 0)
    seq_map = lambda s, pt: (b_of(s), 0, 0)
    samp_map = lambda s, pt: (s // seq_steps, 0)
    const = lambda s, pt: (0, 0)
    kern = functools.partial(_sb_fused_kernel, tq, nq, ppc, n_pages, t_new, layer * n_pool)
    grid_spec = pltpu.PrefetchScalarGridSpec(
        num_scalar_prefetch=1,
        grid=(n_steps,),
        in_specs=[
            pl.BlockSpec(memory_space=pltpu.SMEM),
            pl.BlockSpec((None, tq, SB_WIDTH), qa_map),
            pl.BlockSpec((None, tq, SB_WIDTH), qb_map),
            pl.BlockSpec((None, seq, SB_WIDTH), seq_map, pipeline_mode=pl.Buffered(1)),
            pl.BlockSpec((None, seq, SB_WIDTH), seq_map, pipeline_mode=pl.Buffered(1)),
            pl.BlockSpec((None, tq, SB_WIDTH), qa_map),
            pl.BlockSpec((None, tq, SB_WIDTH), qb_map),
            pl.BlockSpec((2 * tq, tq), const),
            pl.BlockSpec((t_new, SB_WIDTH), samp_map),
            pl.BlockSpec((new_rows, HEAD_DIM), samp_map),
            pl.BlockSpec((new_rows, HEAD_DIM), samp_map),
            pl.BlockSpec((t_new, SB_WIDTH), samp_map),
            pl.BlockSpec((2 * HEAD_DIM, 2 * HEAD_DIM), const),
            pl.BlockSpec(memory_space=pl.ANY),
            pl.BlockSpec(memory_space=pl.ANY),
        ],
        out_specs=[
            pl.BlockSpec((None, seq, SB_WIDTH), seq_map, pipeline_mode=pl.Buffered(1)),
            pl.BlockSpec((t_new, SB_WIDTH), samp_map),
        ],
        scratch_shapes=[
            pltpu.VMEM((tq, SB_WIDTH), F32),
            pltpu.VMEM((SB_HEADS, tq, 1), F32),
            pltpu.VMEM((SB_HEADS * t_new, HEAD_DIM), F32),
            pltpu.VMEM((t_new, HEAD_DIM), F32),
            pltpu.VMEM((SAMPLE_BUFS, ppc, page_rows, HEAD_DIM), F32),
            pltpu.VMEM((SAMPLE_BUFS, ppc, page_rows, HEAD_DIM), F32),
            pltpu.SemaphoreType.DMA((SAMPLE_BUFS, 2)),
        ],
    )
    out_p, out_s = pl.pallas_call(
        kern,
        grid_spec=grid_spec,
        out_shape=[jax.ShapeDtypeStruct((batch, seq, SB_WIDTH), BF16),
                   jax.ShapeDtypeStruct((n_seq * t_new, SB_WIDTH), F32)],
        compiler_params=pltpu.CompilerParams(
            dimension_semantics=("arbitrary",), vmem_limit_bytes=VMEM_LIMIT_FUSED),
        name="sb_fused",
    )(page_table.reshape(-1), sb_bias, q3, q3, k3, v3, g3, g3, uu,
      q_s, as_page(k_new), as_page(v_new), g_s, uo, cache_k, cache_v)
    return out_p.reshape(batch * seq, SB_WIDTH), out_s


def _ssd_kernel(length, n_par, *refs):
    per_seq_in, shared, per_seq_out = refs[:5], refs[5:13], refs[13:]
    pre_ref, h0_ref = per_seq_in[3:5]
    ext_ref, st_ref = per_seq_out[3:5]

    @pl.when(pl.program_id(1) == 0)
    def _():
        ext_ref[:, 0:SUBLANES, :] = pre_ref[...]
        st_ref[...] = h0_ref[...]

    for s in range(n_par):
        _ssd_chunk(length, *[r.at[s] for r in per_seq_in[:3]], *shared,
                   *[r.at[s] for r in per_seq_out])


def _pad_rows(a, rows):
    if a.shape[0] == rows:
        return a
    return jnp.concatenate([a, jnp.zeros((rows - a.shape[0], a.shape[1]), a.dtype)], axis=0)


def _ssd_chunk(length, xbc_ref, z_ref, dt_ref, cw_ref, cb_ref, dtb_ref,
               alog_ref, dsk_ref, nw_ref, ltri_ref, e_ref, out_ref, cnew_ref, snew_ref,
               ext_ref, st_ref):
    L = SSD_CHUNK
    P = SSD_HEAD_DIM

    ext_ref[SUBLANES:SUBLANES + L, :] = _pad_rows(xbc_ref[...], L)
    cw = cw_ref[...]
    conv = cb_ref[...]
    for j in range(CONV_WIDTH):
        off = SUBLANES - (CONV_WIDTH - 1) + j
        conv = conv + ext_ref[off:off + L, :] * cw[j:j + 1, :]
    act = _silu(conv)
    tail = ext_ref[length:length + SUBLANES, :]
    cnew_ref[...] = tail
    ext_ref[0:SUBLANES, :] = tail

    xs = act[:, :SSD_WIDTH]
    bm = act[:, SSD_WIDTH:SSD_WIDTH + SSD_GROUPS * SSD_STATE]
    cm = act[:, SSD_WIDTH + SSD_GROUPS * SSD_STATE:]

    x_dt = _pad_rows(dt_ref[...], L) + dtb_ref[...]
    dt = jnp.maximum(x_dt, 0.0) + jnp.log1p(jnp.exp(-jnp.abs(x_dt)))
    if length < L:
        valid = lax.broadcasted_iota(jnp.int32, dt.shape, 0) < length
        dt = jnp.where(valid, dt, 0.0)
    da = dt * (-jnp.exp(alog_ref[...]))
    cs = _dot_exact_rhs(ltri_ref[...], da)
    cs_t = cs.T
    e = e_ref[...]
    dt_x, cs_x = _dots_exact_lhs([dt, cs], e)
    xdt = xs * dt_x
    ecs = jnp.exp(cs_x)
    xw_t = (xdt * jnp.exp(cs_x[L - 1:L, :] - cs_x)).T
    xdt16 = xdt.astype(BF16)

    row = lax.broadcasted_iota(jnp.int32, (L, L), 0)
    col = lax.broadcasted_iota(jnp.int32, (L, L), 1)
    causal = col <= row
    heads_per_group = SSD_HEADS // SSD_GROUPS
    gw = heads_per_group * P
    y_diag, y_off = [], []
    for g in range(SSD_GROUPS):
        bg = bm[:, SSD_STATE * g:SSD_STATE * (g + 1)].astype(BF16)
        cg = cm[:, SSD_STATE * g:SSD_STATE * (g + 1)].astype(BF16)
        cb = _dot_nt(cg, bg)
        prev = st_ref[gw * g:gw * (g + 1), :]
        y_off.append(_dot_nt(cg, prev.astype(BF16)))
        new = _dot(xw_t[gw * g:gw * (g + 1), :].astype(BF16), bg)
        for r in range(heads_per_group):
            h = heads_per_group * g + r
            seg = cs[:, h:h + 1] - cs_t[h:h + 1, :]
            decay = jnp.exp(jnp.where(causal, seg, -jnp.inf))
            y_diag.append(_dot((cb * decay).astype(BF16), xdt16[:, P * h:P * (h + 1)]))
            chunk_decay = jnp.exp(cs[L - 1:L, h:h + 1])
            st_ref[P * h:P * (h + 1), :] = (prev[P * r:P * (r + 1), :] * chunk_decay
                                            + new[P * r:P * (r + 1), :])
    snew_ref[...] = st_ref[...]
    y = (jnp.concatenate(y_diag, axis=1) + jnp.concatenate(y_off, axis=1) * ecs
         + xs * dsk_ref[...])
    gated = y[:length] * _silu(z_ref[...])
    ms = jnp.mean(gated * gated, axis=-1, keepdims=True)
    out_ref[...] = (gated * lax.rsqrt(ms + EPS) * nw_ref[...]).astype(out_ref.dtype)


def _ssd(xbc, zdt, prefix, h0, conv_w, conv_b, dt_bias, a_log, d_skip, ssd_norm_w,
         batch, n_chunks, length, out_dtype):
    L = SSD_CHUNK
    pre = jnp.pad(prefix, ((0, 0), (SUBLANES - (CONV_WIDTH - 1), 0), (0, 0)))
    pad_h = lambda a: jnp.pad(a, (0, HEAD_DIM - SSD_HEADS)).reshape(1, HEAD_DIM)
    ltri = (jnp.arange(L)[:, None] >= jnp.arange(L)[None, :]).astype(BF16)
    expand = (jnp.arange(HEAD_DIM)[:, None] == jnp.arange(SSD_WIDTH)[None, :] // SSD_HEAD_DIM
              ).astype(BF16)
    dsk = jnp.repeat(d_skip, SSD_HEAD_DIM).reshape(1, SSD_WIDTH)
    z_blk = SSD_WIDTH // HEAD_DIM
    n_par = SSD_SEQS_PER_STEP
    assert batch % n_par == 0 and length <= L and (length == L or n_chunks == 1)
    rows = n_chunks * length
    xbc3 = xbc.reshape(batch, rows, XBC_WIDTH)
    zdt3 = zdt.reshape(batch, rows, ZDT_WIDTH)
    const = lambda b, c: (0, 0)
    seq = lambda b, c: (b, 0, 0)
    kern = functools.partial(_ssd_kernel, length, n_par)
    out, conv_new, ssm_new = pl.pallas_call(
        kern,
        grid=(batch // n_par, n_chunks),
        in_specs=[
            pl.BlockSpec((n_par, length, XBC_WIDTH), lambda b, c: (b, c, 0)),
            pl.BlockSpec((n_par, length, SSD_WIDTH), lambda b, c: (b, c, 0)),
            pl.BlockSpec((n_par, length, HEAD_DIM), lambda b, c: (b, c, z_blk)),
            pl.BlockSpec((n_par, SUBLANES, XBC_WIDTH), seq),
            pl.BlockSpec((n_par, SSD_WIDTH, SSD_STATE), seq),
            pl.BlockSpec((CONV_WIDTH, XBC_WIDTH), const),
            pl.BlockSpec((1, XBC_WIDTH), const),
            pl.BlockSpec((1, HEAD_DIM), const),
            pl.BlockSpec((1, HEAD_DIM), const),
            pl.BlockSpec((1, SSD_WIDTH), const),
            pl.BlockSpec((1, SSD_WIDTH), const),
            pl.BlockSpec((L, L), const),
            pl.BlockSpec((HEAD_DIM, SSD_WIDTH), const),
        ],
        out_specs=[
            pl.BlockSpec((n_par, length, SSD_WIDTH), lambda b, c: (b, c, 0)),
            pl.BlockSpec((n_par, SUBLANES, XBC_WIDTH), seq),
            pl.BlockSpec((n_par, SSD_WIDTH, SSD_STATE), seq),
        ],
        out_shape=[
            jax.ShapeDtypeStruct((batch, rows, SSD_WIDTH), out_dtype),
            jax.ShapeDtypeStruct((batch, SUBLANES, XBC_WIDTH), F32),
            jax.ShapeDtypeStruct((batch, SSD_WIDTH, SSD_STATE), F32),
        ],
        scratch_shapes=[pltpu.VMEM((n_par, SUBLANES + L, XBC_WIDTH), F32),
                        pltpu.VMEM((n_par, SSD_WIDTH, SSD_STATE), F32)],
        compiler_params=pltpu.CompilerParams(
            dimension_semantics=("parallel", "arbitrary"), vmem_limit_bytes=VMEM_LIMIT),
        name="ssd_scan",
    )(xbc3, zdt3, zdt3, pre, h0.reshape(batch, SSD_WIDTH, SSD_STATE), conv_w,
      conv_b.reshape(1, XBC_WIDTH), pad_h(dt_bias), pad_h(a_log), dsk,
      ssd_norm_w.reshape(1, SSD_WIDTH), ltri, expand)
    return out.reshape(batch * rows, SSD_WIDTH), conv_new, ssm_new


def _mem_attn_kernel(q_ref, g_ref, k_ref, v_ref, o_ref):
    for h in range(MEM_HEADS):
        cols = slice(HEAD_DIM * h, HEAD_DIM * (h + 1))
        s = _dot_nt(q_ref[:, cols].astype(BF16), k_ref[:, cols].astype(BF16)) * ATTN_SCALE
        p = jnp.exp(s - jnp.max(s, axis=-1, keepdims=True))
        den = jnp.sum(p, axis=-1, keepdims=True)
        o = _dot(p.astype(BF16), v_ref[:, cols].astype(BF16)) / den
        o_ref[:, cols] = (o * _silu(g_ref[:, cols])).astype(o_ref.dtype)


def _mem_attn(q, g, mem_k, mem_v, batch, t, tq, out_dtype):
    nq = t // tq
    return pl.pallas_call(
        _mem_attn_kernel,
        grid=(batch, nq),
        in_specs=[
            pl.BlockSpec((tq, MEM_WIDTH), lambda b, i: (b * nq + i, 0)),
            pl.BlockSpec((tq, MEM_WIDTH), lambda b, i: (b * nq + i, 0)),
            pl.BlockSpec((None, MEM_TOKENS, MEM_WIDTH), lambda b, i: (b, 0, 0)),
            pl.BlockSpec((None, MEM_TOKENS, MEM_WIDTH), lambda b, i: (b, 0, 0)),
        ],
        out_specs=pl.BlockSpec((tq, MEM_WIDTH), lambda b, i: (b * nq + i, 0)),
        out_shape=jax.ShapeDtypeStruct((batch * t, MEM_WIDTH), out_dtype),
        compiler_params=pltpu.CompilerParams(
            dimension_semantics=("parallel", "parallel"), vmem_limit_bytes=VMEM_LIMIT),
        name="mem_attn",
    )(q, g, mem_k, mem_v)


def _mem_attn_rows_kernel(t, q_ref, g_ref, k_ref, v_ref, o_ref):
    q = q_ref[...]
    g = g_ref[...]
    cols = [slice(HEAD_DIM * h, HEAD_DIM * (h + 1)) for h in range(MEM_HEADS)]
    q_all = jnp.concatenate([q[:, c] for c in cols], axis=0).astype(BF16)
    s = _dot_nt(q_all, k_ref[...].astype(BF16)) * ATTN_SCALE
    n_rows = s.shape[1]
    row_head = jnp.concatenate([jnp.full((t, n_rows), h, jnp.int32) for h in range(MEM_HEADS)],
                               axis=0)
    lane_head = lax.broadcasted_iota(jnp.int32, s.shape, 1) & (MEM_HEADS - 1)
    s = jnp.where(row_head == lane_head, s, -jnp.inf)
    p = jnp.exp(s - jnp.max(s, axis=-1, keepdims=True))
    den = jnp.sum(p, axis=-1, keepdims=True)
    o = _dot(p.astype(BF16), v_ref[...].astype(BF16)) / den
    for h in range(MEM_HEADS):
        o_ref[:, cols[h]] = (o[t * h:t * (h + 1), :] * _silu(g[:, cols[h]])).astype(o_ref.dtype)


def _mem_attn_rows(q, g, mem_k, mem_v, batch, t, out_dtype):
    rows = MEM_TOKENS * MEM_HEADS
    kern = functools.partial(_mem_attn_rows_kernel, t)
    return pl.pallas_call(
        kern,
        grid=(batch,),
        in_specs=[
            pl.BlockSpec((t, MEM_WIDTH), lambda b: (b, 0)),
            pl.BlockSpec((t, MEM_WIDTH), lambda b: (b, 0)),
            pl.BlockSpec((None, rows, HEAD_DIM), lambda b: (b, 0, 0)),
            pl.BlockSpec((None, rows, HEAD_DIM), lambda b: (b, 0, 0)),
        ],
        out_specs=pl.BlockSpec((t, MEM_WIDTH), lambda b: (b, 0)),
        out_shape=jax.ShapeDtypeStruct((batch * t, MEM_WIDTH), out_dtype),
        compiler_params=pltpu.CompilerParams(
            dimension_semantics=("parallel",), vmem_limit_bytes=VMEM_LIMIT),
        name="mem_attn_rows",
    )(q, g, mem_k, mem_v)


def _out_proj_rows(x_ref, sb_ref, ssd_ref, mo_ref, w_ref, o_ref):
    mix = jnp.concatenate([sb_ref[...].astype(BF16), ssd_ref[...].astype(BF16),
                           mo_ref[...].astype(BF16)], axis=-1)
    for c in range(0, o_ref.shape[1], PROJ_CHUNK):
        cols = slice(c, c + PROJ_CHUNK)
        o_ref[:, cols] = x_ref[:, cols] + _dot(mix, w_ref[:, cols].astype(BF16))


def _out_proj_kernel(x_ref, sb_ref, ssd_ref, mo_ref, xs_ref, sbs_ref, ssds_ref, mos_ref, w_ref,
                     o_ref, os_ref):
    _out_proj_rows(x_ref, sb_ref, ssd_ref, mo_ref, w_ref, o_ref)

    @pl.when(pl.program_id(0) == 0)
    def _():
        _out_proj_rows(xs_ref, sbs_ref, ssds_ref, mos_ref, w_ref, os_ref)


def _out_proj(x, sb, ssd, mo, side, w_out, tm):
    t, d = x.shape
    ts = side[0].shape[0]
    row = lambda width: pl.BlockSpec((tm, width), lambda m: (m, 0))
    fixed = lambda width: pl.BlockSpec((ts, width), lambda m: (0, 0))
    widths = (d, SB_WIDTH, SSD_WIDTH, MEM_WIDTH)
    return pl.pallas_call(
        _out_proj_kernel,
        grid=(t // tm,),
        in_specs=[
            *[row(w) for w in widths],
            *[fixed(w) for w in widths],
            pl.BlockSpec(w_out.shape, lambda m: (0, 0), pipeline_mode=pl.Buffered(1)),
        ],
        out_specs=[row(d), fixed(d)],
        out_shape=[jax.ShapeDtypeStruct((t, d), F32), jax.ShapeDtypeStruct((ts, d), F32)],
        compiler_params=pltpu.CompilerParams(
            dimension_semantics=("arbitrary",), vmem_limit_bytes=VMEM_LIMIT),
        name="out_proj",
    )(x, sb, ssd, mo, *side, w_out)


def _in_proj_plan(act_dtype):
    lowp = act_dtype == BF16
    outs, plan = [], []

    def add(width, dtype):
        outs.append((width, dtype))
        return len(outs) - 1

    q = add(SB_WIDTH, act_dtype)
    plan.append(((0, SB_WIDTH, 0, None if lowp else q, q if lowp else None),))
    k32 = add(SB_WIDTH, F32)
    k16 = add(SB_WIDTH, BF16) if lowp else None
    plan.append(((0, SB_WIDTH, 1, k32, k16),))
    v32 = add(SB_WIDTH, F32)
    v16 = add(SB_WIDTH, BF16) if lowp else None
    plan.append(((0, SB_WIDTH, None, v32, v16),))
    g = add(SB_WIDTH, F32)
    plan.append(((0, SB_WIDTH, None, g, None),))
    xbc = add(XBC_WIDTH, F32)
    plan.append(((0, XBC_WIDTH, None, xbc, None),))
    zdt = add(ZDT_WIDTH, F32)
    plan.append(((0, ZDT_WIDTH, None, zdt, None),))
    mq = add(MEM_WIDTH, act_dtype)
    mg = add(MEM_WIDTH, F32)
    plan.append(((0, MEM_WIDTH, 2, None if lowp else mq, mq if lowp else None),
                 (MEM_WIDTH, MEM_WIDTH, None, mg, None)))
    names = dict(q=q, k32=k32, k16=k16, v32=v32, v16=v16, g=g, xbc=xbc, zdt=zdt, mq=mq, mg=mg)
    return tuple(plan), outs, names


_O_Z = 4 * SB_WIDTH
_O_XBC = _O_Z + SSD_WIDTH
_O_DT = _O_XBC + XBC_WIDTH
_O_MEM = _O_DT + SSD_HEADS
_IN_WIDTH = _O_MEM + 2 * MEM_WIDTH
_CAT_WIDTH = 7 * PROJ_TN
W_PREP_COLS = 256
BF16_ROWS = 16


def _w_prep_kernel(w_ref, o_ref):
    def put(dst, src, rows):
        o_ref[dst:dst + rows, :] = w_ref[src:src + rows, :].astype(BF16)

    cols = w_ref.shape[1]
    put(0, 0, _O_Z)
    put(_O_Z, _O_XBC, XBC_WIDTH)
    put(_O_Z + XBC_WIDTH, _O_Z, SSD_WIDTH)
    dt0 = _O_Z + XBC_WIDTH + SSD_WIDTH
    o_ref[dt0:dt0 + BF16_ROWS, :] = jnp.concatenate(
        [w_ref[_O_DT:_O_MEM, :], jnp.zeros((BF16_ROWS - SSD_HEADS, cols), F32)], axis=0
    ).astype(BF16)
    o_ref[dt0 + BF16_ROWS:6 * PROJ_TN, :] = jnp.zeros((6 * PROJ_TN - dt0 - BF16_ROWS, cols), BF16)
    put(6 * PROJ_TN, _O_MEM, 2 * MEM_WIDTH)


def _rearranged_w_in(w_t):
    d = w_t.shape[1]
    assert w_t.shape[0] == _IN_WIDTH and d % W_PREP_COLS == 0
    return pl.pallas_call(
        _w_prep_kernel,
        grid=(d // W_PREP_COLS,),
        in_specs=[pl.BlockSpec((_IN_WIDTH, W_PREP_COLS), lambda c: (0, c))],
        out_specs=pl.BlockSpec((_CAT_WIDTH, W_PREP_COLS), lambda c: (0, c)),
        out_shape=jax.ShapeDtypeStruct((_CAT_WIDTH, d), BF16),
        compiler_params=pltpu.CompilerParams(
            dimension_semantics=("parallel",), vmem_limit_bytes=VMEM_LIMIT),
        name="w_prep",
    )(w_t)


def kernel(x_prompt, x_sample, cache_sb_k, cache_sb_v, state_ssm, state_conv, cache_mem_k,
           cache_mem_v, page_table, mem_prompt, norm_w, w_in, sb_q_norm, sb_k_norm, sb_bias,
           conv_w, conv_b, dt_bias, a_log, d_skip, ssd_norm_w, mem_norm_w, w_mem_kv, mem_q_norm,
           mem_k_norm, w_out):
    depth = w_in.shape[0]
    assert depth == 1
    layer = 0
    bp, sp, d = x_prompt.shape
    bs, ts, _ = x_sample.shape
    n_pool = cache_sb_k.shape[1]
    L = SSD_CHUNK

    w_cat = _rearranged_w_in(w_in[layer].T)
    w_o = w_out[layer]
    head_norms = jnp.concatenate(
        [sb_q_norm[layer][None], sb_k_norm[layer][None], mem_q_norm[layer][None],
         mem_k_norm[layer][None], jnp.zeros((SUBLANES - 4, HEAD_DIM), F32)], axis=0)
    ssd_params = (conv_w[layer], conv_b[layer], dt_bias[layer], a_log[layer], d_skip[layer],
                  ssd_norm_w[layer])

    xp = x_prompt.reshape(bp * sp, d)
    mem_plan = (((0, MEM_WIDTH, 3, 0, None), (MEM_WIDTH, MEM_WIDTH, None, 1, None)),)
    (mk, mv), _ = _proj(mem_prompt.reshape(bp * MEM_TOKENS, d), mem_norm_w[layer],
                        w_mem_kv[layer], head_norms, mem_plan,
                        [(MEM_WIDTH, F32), (MEM_WIDTH, F32)], tm=PROJ_TM,
                        w_rows_are_outputs=False)
    plan, outs, nm = _in_proj_plan(BF16)
    xs = x_sample.reshape(bs * ts, d)
    plan_s, outs_s, ns = _in_proj_plan(F32)
    pr, ps = _proj(xp, norm_w[layer], w_cat, head_norms, plan, outs, tm=PROJ_TM,
                   side=(xs, plan_s, outs_s))
    sb, sb_s = _sb_fused(pr[nm['q']], pr[nm['k16']], pr[nm['v16']], pr[nm['g']], bp, sp, ATTN_TQ,
                         ps[ns['q']], ps[ns['k32']], ps[ns['v32']], ps[ns['g']],
                         cache_sb_k, cache_sb_v, layer, page_table, sb_bias[layer], ts)
    ssd, conv_p, ssm_p = _ssd(
        pr[nm['xbc']], pr[nm['zdt']], jnp.zeros((bp, CONV_WIDTH - 1, XBC_WIDTH), F32),
        jnp.zeros((bp, SSD_HEADS, SSD_HEAD_DIM, SSD_STATE), F32), *ssd_params,
        batch=bp, n_chunks=sp // L, length=L, out_dtype=BF16)
    mo = _mem_attn(pr[nm['mq']], pr[nm['mg']], mk.reshape(bp, MEM_TOKENS, MEM_WIDTH),
                   mv.reshape(bp, MEM_TOKENS, MEM_WIDTH), bp, sp, tq=MEM_TQ, out_dtype=BF16)
    ssd_s, conv_s, ssm_s = _ssd(
        ps[ns['xbc']], ps[ns['zdt']], state_conv[layer], state_ssm[layer],
        *ssd_params, batch=bs, n_chunks=1, length=ts, out_dtype=F32)
    mo_s = _mem_attn_rows(ps[ns['mq']], ps[ns['mg']],
                          cache_mem_k[layer].reshape(bs, MEM_TOKENS * MEM_HEADS, HEAD_DIM),
                          cache_mem_v[layer].reshape(bs, MEM_TOKENS * MEM_HEADS, HEAD_DIM),
                          bs, ts, out_dtype=F32)
    yp, ys = _out_proj(xp, sb, ssd, mo, (xs, sb_s, ssd_s, mo_s), w_o, tm=OUT_TM)

    tail = slice(SUBLANES - (CONV_WIDTH - 1), SUBLANES)
    return (
        yp.reshape(bp, sp, d),
        ys.reshape(bs, ts, d),
        pr[nm['k32']].reshape(1, bp, sp, SB_HEADS, HEAD_DIM),
        pr[nm['v32']].reshape(1, bp, sp, SB_HEADS, HEAD_DIM),
        ssm_p.reshape(1, bp, SSD_HEADS, SSD_HEAD_DIM, SSD_STATE),
        conv_p[:, tail][None],
        mk.reshape(1, bp, MEM_TOKENS, MEM_HEADS, HEAD_DIM),
        mv.reshape(1, bp, MEM_TOKENS, MEM_HEADS, HEAD_DIM),
        ps[ns['k32']].reshape(1, bs, ts, SB_HEADS, HEAD_DIM),
        ps[ns['v32']].reshape(1, bs, ts, SB_HEADS, HEAD_DIM),
        ssm_s.reshape(1, bs, SSD_HEADS, SSD_HEAD_DIM, SSD_STATE),
        conv_s[:, tail][None],
    )
```

```python
import functools
import math

import jax
import jax.numpy as jnp
from jax import lax
from jax.experimental import pallas as pl
from jax.experimental.pallas import tpu as pltpu

F32 = jnp.float32
BF16 = jnp.bfloat16

SB_HEADS = 8
HEAD_DIM = 128
SB_WIDTH = SB_HEADS * HEAD_DIM
SSD_HEADS = 8
SSD_HEAD_DIM = 64
SSD_WIDTH = SSD_HEADS * SSD_HEAD_DIM
SSD_GROUPS = 2
SSD_STATE = 128
CONV_WIDTH = 4
XBC_WIDTH = SSD_WIDTH + 2 * SSD_GROUPS * SSD_STATE
MEM_TOKENS = 256
MEM_HEADS = 4
MEM_WIDTH = MEM_HEADS * HEAD_DIM
PAGE_SIZE = 128
EPS = 1e-6
ATTN_SCALE = HEAD_DIM ** -0.5
LOG2E = math.log2(math.e)

SSD_CHUNK = 128
PROJ_TN = 1024
PROJ_CHUNK = 256
ZDT_WIDTH = SSD_WIDTH + PROJ_CHUNK
SAMPLE_BUFS = 3
SSD_SEQS_PER_STEP = 4
SUBLANES = 8
SB_HEAD_BITS = SB_HEADS.bit_length() - 1
NORM_ROW_CHUNK = 128
PROJ_TM = 512
OUT_TM = 512
ATTN_TQ = 256
MEM_TQ = 1024
VMEM_LIMIT = 56 * 1024 * 1024
VMEM_LIMIT_FUSED = 62 * 1024 * 1024

_NT = (((1,), (1,)), ((), ()))


def _dot(a, b):
    return jnp.dot(a, b, preferred_element_type=F32)


def _dot_nt(a, b):
    return lax.dot_general(a, b, _NT, preferred_element_type=F32)


def _split2(x):
    hi = x.astype(BF16)
    lo = (x - hi.astype(F32)).astype(BF16)
    return hi, lo


def _split3(x):
    hi = x.astype(BF16)
    r = x - hi.astype(F32)
    mid = r.astype(BF16)
    lo = (r - mid.astype(F32)).astype(BF16)
    return hi, mid, lo


def _dots_exact_lhs(xs, m):
    rows = xs[0].shape[0]
    r = _dot(jnp.concatenate([t for x in xs for t in _split3(x)], axis=0), m)
    part = lambda i: r[rows * i:rows * (i + 1), :]
    return [part(3 * i) + part(3 * i + 1) + part(3 * i + 2) for i in range(len(xs))]


def _dot_exact_rhs(m, x):
    n = x.shape[1]
    r = _dot(m, jnp.concatenate(_split3(x), axis=1))
    return r[:, :n] + r[:, n:2 * n] + r[:, 2 * n:]


def _silu(x):
    return x * (1.0 / (1.0 + jnp.exp(-x)))


def _rmsnorm_rows(x_ref, nw_ref, h_ref):
    rows_total = x_ref.shape[0]
    rc = min(rows_total, NORM_ROW_CHUNK)

    def body(r, carry):
        rows = pl.ds(pl.multiple_of(r * rc, rc), rc)
        xv = x_ref[rows, :]
        ms = jnp.mean(xv * xv, axis=-1, keepdims=True)
        h_ref[rows, :] = (xv * lax.rsqrt(ms + EPS) * nw_ref[...]).astype(BF16)
        return carry

    lax.fori_loop(0, rows_total // rc, body, 0)


def _proj_step(segs, h_ref, w_ref, hn_ref, outs, w_rows_are_outputs):
    for col0, width, hn_row, o32, o16 in segs:
        for c0 in range(0, width, PROJ_CHUNK):
            chunk = slice(col0 + c0, col0 + c0 + PROJ_CHUNK)
            if w_rows_are_outputs:
                y = _dot_nt(h_ref[...], w_ref[chunk, :].astype(BF16))
            else:
                y = _dot(h_ref[...], w_ref[:, chunk].astype(BF16))
            for c in range(0, PROJ_CHUNK, HEAD_DIM):
                yc = y[:, c:c + HEAD_DIM]
                if hn_row is not None:
                    ms = jnp.mean(yc * yc, axis=-1, keepdims=True)
                    yc = yc * lax.rsqrt(ms + EPS) * hn_ref[hn_row:hn_row + 1, :]
                cols = slice(c0 + c, c0 + c + HEAD_DIM)
                if o32 is not None:
                    outs[o32][:, cols] = yc
                if o16 is not None:
                    outs[o16][:, cols] = yc.astype(BF16)


def _proj_kernel(plan, n_out, side_plan, n_side, w_rows_are_outputs, *refs):
    if side_plan is None:
        x_ref, nw_ref, w_ref, hn_ref = refs[:4]
        rest = refs[4:]
    else:
        x_ref, xs_ref, nw_ref, w_ref, hn_ref = refs[:5]
        rest = refs[5:]
    outs, side_outs = rest[:n_out], rest[n_out:n_out + n_side]
    scratch = rest[n_out + n_side:]
    h_ref = scratch[0]
    m = pl.program_id(0)
    n = pl.program_id(1)

    @pl.when(n == 0)
    def _():
        _rmsnorm_rows(x_ref, nw_ref, h_ref)

    if side_plan is not None:
        hs_ref = scratch[1]

        @pl.when((n == 0) & (m == 0))
        def _():
            _rmsnorm_rows(xs_ref, nw_ref, hs_ref)

    for step, segs in enumerate(plan):
        @pl.when(n == step)
        def _(step=step, segs=segs):
            _proj_step(segs, h_ref, w_ref, hn_ref, outs, w_rows_are_outputs)
            if side_plan is not None:
                @pl.when(m == 0)
                def _():
                    _proj_step(side_plan[step], hs_ref, w_ref, hn_ref, side_outs,
                               w_rows_are_outputs)


def _proj(x, norm_w, w, head_norms, plan, out_defs, tm, side=None, w_rows_are_outputs=True):
    t, d = x.shape
    n_steps = len(plan)
    w_shape = (n_steps * PROJ_TN, d) if w_rows_are_outputs else (d, n_steps * PROJ_TN)
    assert w.shape == w_shape and t % tm == 0
    w_spec = (pl.BlockSpec((PROJ_TN, d), lambda m, n: (n, 0)) if w_rows_are_outputs
              else pl.BlockSpec((d, PROJ_TN), lambda m, n: (0, n)))
    row = lambda m, n: (m, 0)
    const = lambda m, n: (0, 0)
    in_specs = [pl.BlockSpec((tm, d), row)]
    operands = [x]
    out_specs = [pl.BlockSpec((tm, w), row) for w, _ in out_defs]
    out_shape = [jax.ShapeDtypeStruct((t, w), dt) for w, dt in out_defs]
    scratch = [pltpu.VMEM((tm, d), BF16)]
    side_plan, n_side = None, 0
    if side is not None:
        x_side, side_plan, side_defs = side
        ts = x_side.shape[0]
        assert len(side_plan) == n_steps
        n_side = len(side_defs)
        in_specs.append(pl.BlockSpec((ts, d), const))
        operands.append(x_side)
        out_specs += [pl.BlockSpec((ts, w), const) for w, _ in side_defs]
        out_shape += [jax.ShapeDtypeStruct((ts, w), dt) for w, dt in side_defs]
        scratch.append(pltpu.VMEM((ts, d), BF16))
    in_specs += [pl.BlockSpec((1, d), const), w_spec,
                 pl.BlockSpec((SUBLANES, HEAD_DIM), const)]
    operands += [norm_w.reshape(1, d), w, head_norms]
    kern = functools.partial(_proj_kernel, plan, len(out_defs), side_plan, n_side,
                             w_rows_are_outputs)
    res = pl.pallas_call(
        kern,
        grid=(t // tm, n_steps),
        in_specs=in_specs,
        out_specs=out_specs,
        out_shape=out_shape,
        scratch_shapes=scratch,
        compiler_params=pltpu.CompilerParams(
            dimension_semantics=("arbitrary", "arbitrary"), vmem_limit_bytes=VMEM_LIMIT),
        name="norm_proj",
    )(*operands)
    return res[:len(out_defs)], res[len(out_defs):]


def _log2_fail(z2):
    nz = -z2
    return jnp.minimum(nz, 0.0) - jnp.log(1.0 + jnp.exp2(jnp.minimum(z2, nz))) * LOG2E


def _sb_fused_kernel(tq, nq, ppc, n_pages, t_new, page_base,
                     pt_ref, bias_ref, qa_ref, qb_ref, k_ref, v_ref, ga_ref, gb_ref, uu_ref,
                     qs_ref, kn_ref, vn_ref, gs_ref, uo_ref, ck_hbm, cv_hbm,
                     op_ref, os_ref, acc_ref, c_ref, accs_ref, cs_ref, kbuf, vbuf, sem):
    step = pl.program_id(0)
    n_steps = pl.num_programs(0)
    pair = lax.rem(step, nq // 2)
    i1 = pair
    i2 = nq - 1 - pair
    chunks_per_seq = n_pages // ppc
    seq_steps = chunks_per_seq // nq
    part = lax.rem(step, seq_steps)
    n_chunks = n_steps * nq

    def chunk_copies(chunk):
        slot = lax.rem(chunk, SAMPLE_BUFS)
        seq = lax.div(chunk, chunks_per_seq)
        first_pos = (n_pages - 1) - lax.rem(chunk, chunks_per_seq) * ppc
        copies = []
        for j in range(ppc):
            page = page_base + pt_ref[seq * n_pages + first_pos - j]
            copies.append(pltpu.make_async_copy(ck_hbm.at[page], kbuf.at[slot, j], sem.at[slot, 0]))
            copies.append(pltpu.make_async_copy(cv_hbm.at[page], vbuf.at[slot, j], sem.at[slot, 1]))
        return copies

    def start_chunk(chunk):
        for cp in chunk_copies(chunk):
            cp.start()

    def wait_chunk(chunk):
        slot = lax.rem(chunk, SAMPLE_BUFS)
        pltpu.make_async_copy(ck_hbm.at[pl.ds(0, ppc)], kbuf.at[slot], sem.at[slot, 0]).wait()
        pltpu.make_async_copy(cv_hbm.at[pl.ds(0, ppc)], vbuf.at[slot], sem.at[slot, 1]).wait()

    def start_next(chunk):
        @pl.when(chunk + SAMPLE_BUFS < n_chunks)
        def _():
            start_chunk(chunk + SAMPLE_BUFS)

    @pl.when(step == 0)
    def _():
        for c in range(SAMPLE_BUFS):
            start_chunk(jnp.int32(c))

    lanes = PAGE_SIZE * SB_HEADS
    lane =lax.broadcasted_iota(jnp.int32, (t_new, lanes), 1)
    lane_head = lane & (SB_HEADS - 1)
    biases = [bias_ref[h] * LOG2E for h in range(SB_HEADS)]
    bias_lanes = jnp.full((t_new, lanes), biases[0], F32)
    for h in range(1, SB_HEADS):
        bias_lanes = jnp.where(lane_head == h, biases[h], bias_lanes)
    qs = qs_ref[...]
    q_all = jnp.concatenate([qs[:, HEAD_DIM * h:HEAD_DIM * (h + 1)] for h in range(SB_HEADS)],
                            axis=0).astype(BF16)

    def page_scores(kpages, n, mask):
        s_cat = _dot_nt(q_all, kpages.astype(BF16))
        width = kpages.shape[0] // n
        return [scores(s_cat[:, width * p:width * (p + 1)], mask) for p in range(n)]

    def scores(s_all, mask):
        width = s_all.shape[1]
        head = lane_head[:, :width]
        sc = s_all[0:t_new, :]
        for h in range(1, SB_HEADS):
            sc = jnp.where(head == h, s_all[t_new * h:t_new * (h + 1), :], sc)
        z2 = sc * (ATTN_SCALE * LOG2E) + bias_lanes[:, :width]
        lf = _log2_fail(z2)
        if mask is not None:
            lf = jnp.where(mask, lf, 0.0)
        blocks = jnp.concatenate([lf[:, HEAD_DIM * j:HEAD_DIM * (j + 1)]
                                  for j in range(width // HEAD_DIM)], axis=0)
        hi, lo = _split2(blocks)
        return z2, jnp.concatenate([hi, lo], axis=1)

    def stacked_sums(operands, matrix):
        rows = operands[0].shape[0]
        res = _dot(jnp.concatenate(operands, axis=0), matrix)
        return [res[rows * i:rows * (i + 1), :] for i in range(len(operands))]

    def weights(z2, res, mask, run):
        width = z2.shape[1]
        head = lane_head[:, :width]
        n_blk = width // HEAD_DIM
        ws = [None] * n_blk
        for j in reversed(range(n_blk)):
            rows = slice(t_new * j, t_new * (j + 1))
            logw = z2[:, HEAD_DIM * j:HEAD_DIM * (j + 1)] + res[rows, :HEAD_DIM]
            if run is not None:
                logw = logw + run
            ws[j] = jnp.exp2(logw)
            tot = res[rows, HEAD_DIM:]
            run = tot if run is None else run + tot
        w = jnp.concatenate(ws, axis=1)
        if mask is not None:
            w = jnp.where(mask, w, 0.0)
        w_all = jnp.concatenate([jnp.where(head == h, w, 0.0) for h in range(SB_HEADS)],
                                axis=0).astype(BF16)
        return w_all, run

    def new_keys():
        width = kn_ref.shape[0]
        mask = ((lane >> SB_HEAD_BITS) < lax.broadcasted_iota(jnp.int32, (t_new, lanes), 0)
                )[:, :width]
        (z2, hilo), = page_scores(kn_ref[...], 1, mask)
        res, = stacked_sums([hilo], uo_ref[...])
        w_all, run = weights(z2, res, mask, None)
        cs_ref[...] = run
        accs_ref[...] = _dot(w_all, vn_ref[...].astype(BF16))

    uu = uu_ref[...]
    cols = [slice(HEAD_DIM * h, HEAD_DIM * (h + 1)) for h in range(SB_HEADS)]

    def section(q_ref, start, diag, first, chunk):
        kb = k_ref[pl.ds(start, tq), :]
        vb = v_ref[pl.ds(start, tq), :]
        n_pages_here = 0 if chunk is None else ppc
        slot = None if chunk is None else lax.rem(chunk, SAMPLE_BUFS)
        if n_pages_here:
            run = cs_ref[...]
            acc = accs_ref[...]
        p_parts, s_parts = {}, {}
        n_idx = max(SB_HEADS, n_pages_here)

        raw = [_dot_nt(q_ref[:, cols[h]], kb[:, cols[h]]) for h in range(SB_HEADS)]
        if n_pages_here:
            s_split = page_scores(kbuf[slot].reshape(ppc * lanes, HEAD_DIM), ppc, None)
        split = []
        for h in range(SB_HEADS):
            z2 = raw[h] * (ATTN_SCALE * LOG2E) + biases[h]
            lf = _log2_fail(z2)
            if diag is not None:
                lf = jnp.where(diag, lf, 0.0)
            hi, lo = _split2(lf)
            split.append((z2, jnp.concatenate([hi, lo], axis=1)))
        incl = stacked_sums([hilo for _, hilo in split], uu)
        for h in range(SB_HEADS):
            p_parts[h] = (split[h][0], incl[h])
        if n_pages_here:
            res = stacked_sums([hilo for _, hilo in s_split], uo_ref[...])
            for idx in range(n_pages_here):
                s_parts[idx] = (s_split[idx][0], res[idx])
        w_pages = []
        for idx in range(n_idx):
            if idx < SB_HEADS:
                z2, incl_h = p_parts[idx]
                logw = z2 + incl_h
                if not first:
                    logw = logw + c_ref[idx]
                w = jnp.exp2(logw)
                if diag is not None:
                    w = jnp.where(diag, w, 0.0)
                pv = _dot(w.astype(BF16), vb[:, cols[idx]])
                total = incl_h[:, 0:1]
                if first:
                    acc_ref[:, cols[idx]] = pv
                    c_ref[idx] = total
                else:
                    acc_ref[:, cols[idx]] += pv
                    c_ref[idx] += total
            if idx < n_pages_here:
                w_all, run = weights(*s_parts[idx], None, run)
                w_pages.append(w_all)
        if n_pages_here:
            cs_ref[...] = run
            accs_ref[...] = acc + _dot(jnp.concatenate(w_pages, axis=1),
                                       vbuf[slot].reshape(ppc * lanes, HEAD_DIM).astype(BF16))

    def finish(i, g_ref):
        rows = pl.ds(pl.multiple_of(i * tq, tq), tq)
        op_ref[rows, :] = (acc_ref[...] * _silu(g_ref[...])).astype(op_ref.dtype)

    row = lax.broadcasted_iota(jnp.int32, (tq, tq), 0)
    col = lax.broadcasted_iota(jnp.int32, (tq, tq), 1)
    diag = col < row
    chunk0 = step * nq

    @pl.when(part == 0)
    def _():
        new_keys()

    section(qa_ref, pl.multiple_of(i1 * tq, tq), diag, True, None)

    def body_a(t, carry):
        chunk = chunk0 + t
        wait_chunk(chunk)
        section(qa_ref, pl.multiple_of((i1 - 1 - t) * tq, tq), None, False, chunk)
        start_next(chunk)
        return carry

    lax.fori_loop(0, i1, body_a, 0)
    finish(i1, ga_ref)

    chunk = chunk0 + i1
    wait_chunk(chunk)
    section(qb_ref, pl.multiple_of(i2 * tq, tq), diag, True, chunk)
    start_next(chunk)

    def body_b(t, carry):
        chunk = chunk0 + i1 + 1 + t
        wait_chunk(chunk)
        section(qb_ref, pl.multiple_of((i2 - 1 - t) * tq, tq), None, False, chunk)
        start_next(chunk)
        return carry

    lax.fori_loop(0, i2, body_b, 0)
    finish(i2, gb_ref)

    @pl.when(part == seq_steps - 1)
    def _():
        g = gs_ref[...]
        for h in range(SB_HEADS):
            os_ref[:, cols[h]] = accs_ref[t_new * h:t_new * (h + 1), :] * _silu(g[:, cols[h]])


def _sb_fused(q, k, v, g, batch, seq, tq, q_s, k_new, v_new, g_s, cache_k, cache_v, layer,
              page_table, sb_bias, t_new):
    n_seq, n_pages = page_table.shape
    n_pool = cache_k.shape[1]
    nq = seq // tq
    n_steps = batch * (nq // 2)
    assert nq % 2 == 0 and (n_seq * n_pages) % (n_steps * nq) == 0
    ppc = n_seq * n_pages // (n_steps * nq)
    assert n_pages % (ppc * nq) == 0
    seq_steps = n_pages // (ppc * nq)
    page_rows = PAGE_SIZE * SB_HEADS

    q3, k3, v3, g3 = (a.reshape(batch, seq, SB_WIDTH) for a in (q, k, v, g))
    u = (jnp.arange(tq)[:, None] >= jnp.arange(tq)[None, :]).astype(BF16)
    uu = jnp.concatenate([u, u], axis=0)

    new_tokens = -(-t_new * SB_HEADS // HEAD_DIM) * HEAD_DIM // SB_HEADS
    new_rows = new_tokens * SB_HEADS

    def as_page(a):
        a = a.reshape(n_seq, t_new, SB_HEADS, HEAD_DIM)
        a = jnp.pad(a, ((0, 0), (0, new_tokens - t_new), (0, 0), (0, 0)))
        return a.reshape(n_seq * new_rows, HEAD_DIM)

    cache_k = cache_k.reshape(-1, page_rows, HEAD_DIM)
    cache_v = cache_v.reshape(-1, page_rows, HEAD_DIM)
    r = jnp.arange(HEAD_DIM)
    same_head = (r[:, None] % SB_HEADS) == (r[None, :] % SB_HEADS)
    not_earlier = (r[:, None] // SB_HEADS) >= (r[None, :] // SB_HEADS)
    uo = jnp.concatenate([same_head & not_earlier, same_head], axis=1).astype(BF16)
    uo = jnp.concatenate([uo, uo], axis=0)

    half = nq // 2
    b_of = lambda s: s // half
    qa_map = lambda s, pt: (b_of(s), s % half, 0)
    qb_map = lambda s, pt: (b_of(s), nq - 1 - s % half, 0)
    seq_map = lambda s, pt: (b_of(s), 0, 0)
    samp_map = lambda s, pt: (s // seq_steps, 0)
    const = lambda s, pt: (0, 0)
    kern = functools.partial(_sb_fused_kernel, tq, nq, ppc, n_pages, t_new, layer * n_pool)
    grid_spec = pltpu.PrefetchScalarGridSpec(
        num_scalar_prefetch=1,
        grid=(n_steps,),
        in_specs=[
            pl.BlockSpec(memory_space=pltpu.SMEM),
            pl.BlockSpec((None, tq, SB_WIDTH), qa_map),
            pl.BlockSpec((None, tq, SB_WIDTH), qb_map),
            pl.BlockSpec((None, seq, SB_WIDTH), seq_map, pipeline_mode=pl.Buffered(1)),
            pl.BlockSpec((None, seq, SB_WIDTH), seq_map, pipeline_mode=pl.Buffered(1)),
            pl.BlockSpec((None, tq, SB_WIDTH), qa_map),
            pl.BlockSpec((None, tq, SB_WIDTH), qb_map),
            pl.BlockSpec((2 * tq, tq), const),
            pl.BlockSpec((t_new, SB_WIDTH), samp_map),
            pl.BlockSpec((new_rows, HEAD_DIM), samp_map),
            pl.BlockSpec((new_rows, HEAD_DIM), samp_map),
            pl.BlockSpec((t_new, SB_WIDTH), samp_map),
            pl.BlockSpec((2 * HEAD_DIM, 2 * HEAD_DIM), const),
            pl.BlockSpec(memory_space=pl.ANY),
            pl.BlockSpec(memory_space=pl.ANY),
        ],
        out_specs=[
            pl.BlockSpec((None, seq, SB_WIDTH), seq_map, pipeline_mode=pl.Buffered(1)),
            pl.BlockSpec((t_new, SB_WIDTH), samp_map),
        ],
        scratch_shapes=[
            pltpu.VMEM((tq, SB_WIDTH), F32),
            pltpu.VMEM((SB_HEADS, tq, 1), F32),
            pltpu.VMEM((SB_HEADS * t_new, HEAD_DIM), F32),
            pltpu.VMEM((t_new, HEAD_DIM), F32),
            pltpu.VMEM((SAMPLE_BUFS, ppc, page_rows, HEAD_DIM), F32),
            pltpu.VMEM((SAMPLE_BUFS, ppc, page_rows, HEAD_DIM), F32),
            pltpu.SemaphoreType.DMA((SAMPLE_BUFS, 2)),
        ],
    )
    out_p, out_s = pl.pallas_call(
        kern,
        grid_spec=grid_spec,
        out_shape=[jax.ShapeDtypeStruct((batch, seq, SB_WIDTH), BF16),
                   jax.ShapeDtypeStruct((n_seq * t_new, SB_WIDTH), F32)],
        compiler_params=pltpu.CompilerParams(
            dimension_semantics=("arbitrary",), vmem_limit_bytes=VMEM_LIMIT_FUSED),
        name="sb_fused",
    )(page_table.reshape(-1), sb_bias, q3, q3, k3, v3, g3, g3, uu,
      q_s, as_page(k_new), as_page(v_new), g_s, uo, cache_k, cache_v)
    return out_p.reshape(batch * seq, SB_WIDTH), out_s


def _ssd_kernel(length, n_par, *refs):
    per_seq_in, shared, per_seq_out = refs[:5], refs[5:13], refs[13:]
    pre_ref, h0_ref = per_seq_in[3:5]
    ext_ref, st_ref = per_seq_out[3:5]

    @pl.when(pl.program_id(1) == 0)
    def _():
        ext_ref[:, 0:SUBLANES, :] = pre_ref[...]
        st_ref[...] = h0_ref[...]

    for s in range(n_par):
        _ssd_chunk(length, *[r.at[s] for r in per_seq_in[:3]], *shared,
                   *[r.at[s] for r in per_seq_out])


def _pad_rows(a, rows):
    if a.shape[0] == rows:
        return a
    return jnp.concatenate([a, jnp.zeros((rows - a.shape[0], a.shape[1]), a.dtype)], axis=0)


def _ssd_chunk(length, xbc_ref, z_ref, dt_ref, cw_ref, cb_ref, dtb_ref,
               alog_ref, dsk_ref, nw_ref, ltri_ref, e_ref, out_ref, cnew_ref, snew_ref,
               ext_ref, st_ref):
    L = SSD_CHUNK
    P = SSD_HEAD_DIM

    ext_ref[SUBLANES:SUBLANES + L, :] = _pad_rows(xbc_ref[...], L)
    cw = cw_ref[...]
    conv = cb_ref[...]
    for j in range(CONV_WIDTH):
        off = SUBLANES - (CONV_WIDTH - 1) + j
        conv = conv + ext_ref[off:off + L, :] * cw[j:j + 1, :]
    act = _silu(conv)
    tail = ext_ref[length:length + SUBLANES, :]
    cnew_ref[...] = tail
    ext_ref[0:SUBLANES, :] = tail

    xs = act[:, :SSD_WIDTH]
    bm = act[:, SSD_WIDTH:SSD_WIDTH + SSD_GROUPS * SSD_STATE]
    cm = act[:, SSD_WIDTH + SSD_GROUPS * SSD_STATE:]

    x_dt = _pad_rows(dt_ref[...], L) + dtb_ref[...]
    dt = jnp.maximum(x_dt, 0.0) + jnp.log1p(jnp.exp(-jnp.abs(x_dt)))
    if length < L:
        valid = lax.broadcasted_iota(jnp.int32, dt.shape, 0) < length
        dt = jnp.where(valid, dt, 0.0)
    da = dt * (-jnp.exp(alog_ref[...]))
    cs = _dot_exact_rhs(ltri_ref[...], da)
    cs_t = cs.T
    e = e_ref[...]
    dt_x, cs_x = _dots_exact_lhs([dt, cs], e)
    xdt = xs * dt_x
    ecs = jnp.exp(cs_x)
    xw_t = (xdt * jnp.exp(cs_x[L - 1:L, :] - cs_x)).T
    xdt16 = xdt.astype(BF16)

    row = lax.broadcasted_iota(jnp.int32, (L, L), 0)
    col = lax.broadcasted_iota(jnp.int32, (L, L), 1)
    causal = col <= row
    heads_per_group = SSD_HEADS // SSD_GROUPS
    gw = heads_per_group * P
    y_diag, y_off = [], []
    for g in range(SSD_GROUPS):
        bg = bm[:, SSD_STATE * g:SSD_STATE * (g + 1)].astype(BF16)
        cg = cm[:, SSD_STATE * g:SSD_STATE * (g + 1)].astype(BF16)
        cb = _dot_nt(cg, bg)
        prev = st_ref[gw * g:gw * (g + 1), :]
        y_off.append(_dot_nt(cg, prev.astype(BF16)))
        new = _dot(xw_t[gw * g:gw * (g + 1), :].astype(BF16), bg)
        for r in range(heads_per_group):
            h = heads_per_group * g + r
            seg = cs[:, h:h + 1] - cs_t[h:h + 1, :]
            decay = jnp.exp(jnp.where(causal, seg, -jnp.inf))
            y_diag.append(_dot((cb * decay).astype(BF16), xdt16[:, P * h:P * (h + 1)]))
            chunk_decay = jnp.exp(cs[L - 1:L, h:h + 1])
            st_ref[P * h:P * (h + 1), :] = (prev[P * r:P * (r + 1), :] * chunk_decay
                                            + new[P * r:P * (r + 1), :])
    snew_ref[...] = st_ref[...]
    y = (jnp.concatenate(y_diag, axis=1) + jnp.concatenate(y_off, axis=1) * ecs
         + xs * dsk_ref[...])
    gated = y[:length] * _silu(z_ref[...])
    ms = jnp.mean(gated * gated, axis=-1, keepdims=True)
    out_ref[...] = (gated * lax.rsqrt(ms + EPS) * nw_ref[...]).astype(out_ref.dtype)


def _ssd(xbc, zdt, prefix, h0, conv_w, conv_b, dt_bias, a_log, d_skip, ssd_norm_w,
         batch, n_chunks, length, out_dtype):
    L = SSD_CHUNK
    pre = jnp.pad(prefix, ((0, 0), (SUBLANES - (CONV_WIDTH - 1), 0), (0, 0)))
    pad_h = lambda a: jnp.pad(a, (0, HEAD_DIM - SSD_HEADS)).reshape(1, HEAD_DIM)
    ltri = (jnp.arange(L)[:, None] >= jnp.arange(L)[None, :]).astype(BF16)
    expand = (jnp.arange(HEAD_DIM)[:, None] == jnp.arange(SSD_WIDTH)[None, :] // SSD_HEAD_DIM
              ).astype(BF16)
    dsk = jnp.repeat(d_skip, SSD_HEAD_DIM).reshape(1, SSD_WIDTH)
    z_blk = SSD_WIDTH // HEAD_DIM
    n_par = SSD_SEQS_PER_STEP
    assert batch % n_par == 0 and length <= L and (length == L or n_chunks == 1)
    rows = n_chunks * length
    xbc3 = xbc.reshape(batch, rows, XBC_WIDTH)
    zdt3 = zdt.reshape(batch, rows, ZDT_WIDTH)
    const = lambda b, c: (0, 0)
    seq = lambda b, c: (b, 0, 0)
    kern = functools.partial(_ssd_kernel, length, n_par)
    out, conv_new, ssm_new = pl.pallas_call(
        kern,
        grid=(batch // n_par, n_chunks),
        in_specs=[
            pl.BlockSpec((n_par, length, XBC_WIDTH), lambda b, c: (b, c, 0)),
            pl.BlockSpec((n_par, length, SSD_WIDTH), lambda b, c: (b, c, 0)),
            pl.BlockSpec((n_par, length, HEAD_DIM), lambda b, c: (b, c, z_blk)),
            pl.BlockSpec((n_par, SUBLANES, XBC_WIDTH), seq),
            pl.BlockSpec((n_par, SSD_WIDTH, SSD_STATE), seq),
            pl.BlockSpec((CONV_WIDTH, XBC_WIDTH), const),
            pl.BlockSpec((1, XBC_WIDTH), const),
            pl.BlockSpec((1, HEAD_DIM), const),
            pl.BlockSpec((1, HEAD_DIM), const),
            pl.BlockSpec((1, SSD_WIDTH), const),
            pl.BlockSpec((1, SSD_WIDTH), const),
            pl.BlockSpec((L, L), const),
            pl.BlockSpec((HEAD_DIM, SSD_WIDTH), const),
        ],
        out_specs=[
            pl.BlockSpec((n_par, length, SSD_WIDTH), lambda b, c: (b, c, 0)),
            pl.BlockSpec((n_par, SUBLANES, XBC_WIDTH), seq),
            pl.BlockSpec((n_par, SSD_WIDTH, SSD_STATE), seq),
        ],
        out_shape=[
            jax.ShapeDtypeStruct((batch, rows, SSD_WIDTH), out_dtype),
            jax.ShapeDtypeStruct((batch, SUBLANES, XBC_WIDTH), F32),
            jax.ShapeDtypeStruct((batch, SSD_WIDTH, SSD_STATE), F32),
        ],
        scratch_shapes=[pltpu.VMEM((n_par, SUBLANES + L, XBC_WIDTH), F32),
                        pltpu.VMEM((n_par, SSD_WIDTH, SSD_STATE), F32)],
        compiler_params=pltpu.CompilerParams(
            dimension_semantics=("parallel", "arbitrary"), vmem_limit_bytes=VMEM_LIMIT),
        name="ssd_scan",
    )(xbc3, zdt3, zdt3, pre, h0.reshape(batch, SSD_WIDTH, SSD_STATE), conv_w,
      conv_b.reshape(1, XBC_WIDTH), pad_h(dt_bias), pad_h(a_log), dsk,
      ssd_norm_w.reshape(1, SSD_WIDTH), ltri, expand)
    return out.reshape(batch * rows, SSD_WIDTH), conv_new, ssm_new


def _mem_attn_kernel(q_ref, g_ref, k_ref, v_ref, o_ref):
    for h in range(MEM_HEADS):
        cols = slice(HEAD_DIM * h, HEAD_DIM * (h + 1))
        s = _dot_nt(q_ref[:, cols].astype(BF16), k_ref[:, cols].astype(BF16)) * ATTN_SCALE
        p = jnp.exp(s - jnp.max(s, axis=-1, keepdims=True))
        den = jnp.sum(p, axis=-1, keepdims=True)
        o = _dot(p.astype(BF16), v_ref[:, cols].astype(BF16)) / den
        o_ref[:, cols] = (o * _silu(g_ref[:, cols])).astype(o_ref.dtype)


def _mem_attn(q, g, mem_k, mem_v, batch, t, tq, out_dtype):
    nq = t // tq
    return pl.pallas_call(
        _mem_attn_kernel,
        grid=(batch, nq),
        in_specs=[
            pl.BlockSpec((tq, MEM_WIDTH), lambda b, i: (b * nq + i, 0)),
            pl.BlockSpec((tq, MEM_WIDTH), lambda b, i: (b * nq + i, 0)),
            pl.BlockSpec((None, MEM_TOKENS, MEM_WIDTH), lambda b, i: (b, 0, 0)),
            pl.BlockSpec((None, MEM_TOKENS, MEM_WIDTH), lambda b, i: (b, 0, 0)),
        ],
        out_specs=pl.BlockSpec((tq, MEM_WIDTH), lambda b, i: (b * nq + i, 0)),
        out_shape=jax.ShapeDtypeStruct((batch * t, MEM_WIDTH), out_dtype),
        compiler_params=pltpu.CompilerParams(
            dimension_semantics=("parallel", "parallel"), vmem_limit_bytes=VMEM_LIMIT),
        name="mem_attn",
    )(q, g, mem_k, mem_v)


def _mem_attn_rows_kernel(t, q_ref, g_ref, k_ref, v_ref, o_ref):
    q = q_ref[...]
    g = g_ref[...]
    cols = [slice(HEAD_DIM * h, HEAD_DIM * (h + 1)) for h in range(MEM_HEADS)]
    q_all = jnp.concatenate([q[:, c] for c in cols], axis=0).astype(BF16)
    s = _dot_nt(q_all, k_ref[...].astype(BF16)) * ATTN_SCALE
    n_rows = s.shape[1]
    row_head = jnp.concatenate([jnp.full((t, n_rows), h, jnp.int32) for h in range(MEM_HEADS)],
                               axis=0)
    lane_head = lax.broadcasted_iota(jnp.int32, s.shape, 1) & (MEM_HEADS - 1)
    s = jnp.where(row_head == lane_head, s, -jnp.inf)
    p = jnp.exp(s - jnp.max(s, axis=-1, keepdims=True))
    den = jnp.sum(p, axis=-1, keepdims=True)
    o = _dot(p.astype(BF16), v_ref[...].astype(BF16)) / den
    for h in range(MEM_HEADS):
        o_ref[:, cols[h]] = (o[t * h:t * (h + 1), :] * _silu(g[:, cols[h]])).astype(o_ref.dtype)


def _mem_attn_rows(q, g, mem_k, mem_v, batch, t, out_dtype):
    rows = MEM_TOKENS * MEM_HEADS
    kern = functools.partial(_mem_attn_rows_kernel, t)
    return pl.pallas_call(
        kern,
        grid=(batch,),
        in_specs=[
            pl.BlockSpec((t, MEM_WIDTH), lambda b: (b, 0)),
            pl.BlockSpec((t, MEM_WIDTH), lambda b: (b, 0)),
            pl.BlockSpec((None, rows, HEAD_DIM), lambda b: (b, 0, 0)),
            pl.BlockSpec((None, rows, HEAD_DIM), lambda b: (b, 0, 0)),
        ],
        out_specs=pl.BlockSpec((t, MEM_WIDTH), lambda b: (b, 0)),
        out_shape=jax.ShapeDtypeStruct((batch * t, MEM_WIDTH), out_dtype),
        compiler_params=pltpu.CompilerParams(
            dimension_semantics=("parallel",), vmem_limit_bytes=VMEM_LIMIT),
        name="mem_attn_rows",
    )(q, g, mem_k, mem_v)


def _out_proj_rows(x_ref, sb_ref, ssd_ref, mo_ref, w_ref, o_ref):
    mix = jnp.concatenate([sb_ref[...].astype(BF16), ssd_ref[...].astype(BF16),
                           mo_ref[...].astype(BF16)], axis=-1)
    for c in range(0, o_ref.shape[1], PROJ_CHUNK):
        cols = slice(c, c + PROJ_CHUNK)
        o_ref[:, cols] = x_ref[:, cols] + _dot(mix, w_ref[:, cols].astype(BF16))


def _out_proj_kernel(x_ref, sb_ref, ssd_ref, mo_ref, xs_ref, sbs_ref, ssds_ref, mos_ref, w_ref,
                     o_ref, os_ref):
    _out_proj_rows(x_ref, sb_ref, ssd_ref, mo_ref, w_ref, o_ref)

    @pl.when(pl.program_id(0) == 0)
    def _():
        _out_proj_rows(xs_ref, sbs_ref, ssds_ref, mos_ref, w_ref, os_ref)


def _out_proj(x, sb, ssd, mo, side, w_out, tm):
    t, d = x.shape
    ts = side[0].shape[0]
    row = lambda width: pl.BlockSpec((tm, width), lambda m: (m, 0))
    fixed = lambda width: pl.BlockSpec((ts, width), lambda m: (0, 0))
    widths = (d, SB_WIDTH, SSD_WIDTH, MEM_WIDTH)
    return pl.pallas_call(
        _out_proj_kernel,
        grid=(t // tm,),
        in_specs=[
            *[row(w) for w in widths],
            *[fixed(w) for w in widths],
            pl.BlockSpec(w_out.shape, lambda m: (0, 0), pipeline_mode=pl.Buffered(1)),
        ],
        out_specs=[row(d), fixed(d)],
        out_shape=[jax.ShapeDtypeStruct((t, d), F32), jax.ShapeDtypeStruct((ts, d), F32)],
        compiler_params=pltpu.CompilerParams(
            dimension_semantics=("arbitrary",), vmem_limit_bytes=VMEM_LIMIT),
        name="out_proj",
    )(x, sb, ssd, mo, *side, w_out)


def _in_proj_plan(act_dtype):
    lowp = act_dtype == BF16
    outs, plan = [], []

    def add(width, dtype):
        outs.append((width, dtype))
        return len(outs) - 1

    q = add(SB_WIDTH, act_dtype)
    plan.append(((0, SB_WIDTH, 0, None if lowp else q, q if lowp else None),))
    k32 = add(SB_WIDTH, F32)
    k16 = add(SB_WIDTH, BF16) if lowp else None
    plan.append(((0, SB_WIDTH, 1, k32, k16),))
    v32 = add(SB_WIDTH, F32)
    v16 = add(SB_WIDTH, BF16) if lowp else None
    plan.append(((0, SB_WIDTH, None, v32, v16),))
    g = add(SB_WIDTH, F32)
    plan.append(((0, SB_WIDTH, None, g, None),))
    xbc = add(XBC_WIDTH, F32)
    plan.append(((0, XBC_WIDTH, None, xbc, None),))
    zdt = add(ZDT_WIDTH, F32)
    plan.append(((0, ZDT_WIDTH, None, zdt, None),))
    mq = add(MEM_WIDTH, act_dtype)
    mg = add(MEM_WIDTH, F32)
    plan.append(((0, MEM_WIDTH, 2, None if lowp else mq, mq if lowp else None),
                 (MEM_WIDTH, MEM_WIDTH, None, mg, None)))
    names = dict(q=q, k32=k32, k16=k16, v32=v32, v16=v16, g=g, xbc=xbc, zdt=zdt, mq=mq, mg=mg)
    return tuple(plan), outs, names


_O_Z = 4 * SB_WIDTH
_O_XBC = _O_Z + SSD_WIDTH
_O_DT = _O_XBC + XBC_WIDTH
_O_MEM = _O_DT + SSD_HEADS
_IN_WIDTH = _O_MEM + 2 * MEM_WIDTH
_CAT_WIDTH = 7 * PROJ_TN
W_PREP_COLS = 512
BF16_ROWS = 16


def _w_prep_kernel(w_ref, o_ref):
    def put(dst, src, rows):
        o_ref[dst:dst + rows, :] = w_ref[src:src + rows, :].astype(BF16)

    cols = w_ref.shape[1]
    put(0, 0, _O_Z)
    put(_O_Z, _O_XBC, XBC_WIDTH)
    put(_O_Z + XBC_WIDTH, _O_Z, SSD_WIDTH)
    dt0 = _O_Z + XBC_WIDTH + SSD_WIDTH
    o_ref[dt0:dt0 + BF16_ROWS, :] = jnp.concatenate(
        [w_ref[_O_DT:_O_MEM, :], jnp.zeros((BF16_ROWS - SSD_HEADS, cols), F32)], axis=0
    ).astype(BF16)
    o_ref[dt0 + BF16_ROWS:6 * PROJ_TN, :] = jnp.zeros((6 * PROJ_TN - dt0 - BF16_ROWS, cols), BF16)
    put(6 * PROJ_TN, _O_MEM, 2 * MEM_WIDTH)


def _rearranged_w_in(w_t):
    d = w_t.shape[1]
    assert w_t.shape[0] == _IN_WIDTH and d % W_PREP_COLS == 0
    return pl.pallas_call(
        _w_prep_kernel,
        grid=(d // W_PREP_COLS,),
        in_specs=[pl.BlockSpec((_IN_WIDTH, W_PREP_COLS), lambda c: (0, c))],
        out_specs=pl.BlockSpec((_CAT_WIDTH, W_PREP_COLS), lambda c: (0, c)),
        out_shape=jax.ShapeDtypeStruct((_CAT_WIDTH, d), BF16),
        compiler_params=pltpu.CompilerParams(
            dimension_semantics=("parallel",), vmem_limit_bytes=VMEM_LIMIT),
        name="w_prep",
    )(w_t)


def kernel(x_prompt, x_sample, cache_sb_k, cache_sb_v, state_ssm, state_conv, cache_mem_k,
           cache_mem_v, page_table, mem_prompt, norm_w, w_in, sb_q_norm, sb_k_norm, sb_bias,
           conv_w, conv_b, dt_bias, a_log, d_skip, ssd_norm_w, mem_norm_w, w_mem_kv, mem_q_norm,
           mem_k_norm, w_out):
    depth = w_in.shape[0]
    assert depth == 1
    layer = 0
    bp, sp, d = x_prompt.shape
    bs, ts, _ = x_sample.shape
    n_pool = cache_sb_k.shape[1]
    L = SSD_CHUNK

    w_cat = _rearranged_w_in(w_in[layer].T)
    w_o = w_out[layer]
    head_norms = jnp.concatenate(
        [sb_q_norm[layer][None], sb_k_norm[layer][None], mem_q_norm[layer][None],
         mem_k_norm[layer][None], jnp.zeros((SUBLANES - 4, HEAD_DIM), F32)], axis=0)
    ssd_params = (conv_w[layer], conv_b[layer], dt_bias[layer], a_log[layer], d_skip[layer],
                  ssd_norm_w[layer])

    xp = x_prompt.reshape(bp * sp, d)
    mem_plan = (((0, MEM_WIDTH, 3, 0, None), (MEM_WIDTH, MEM_WIDTH, None, 1, None)),)
    (mk, mv), _ = _proj(mem_prompt.reshape(bp * MEM_TOKENS, d), mem_norm_w[layer],
                        w_mem_kv[layer], head_norms, mem_plan,
                        [(MEM_WIDTH, F32), (MEM_WIDTH, F32)], tm=PROJ_TM,
                        w_rows_are_outputs=False)
    plan, outs, nm = _in_proj_plan(BF16)
    xs = x_sample.reshape(bs * ts, d)
    plan_s, outs_s, ns = _in_proj_plan(F32)
    pr, ps = _proj(xp, norm_w[layer], w_cat, head_norms, plan, outs, tm=PROJ_TM,
                   side=(xs, plan_s, outs_s))
    sb, sb_s = _sb_fused(pr[nm['q']], pr[nm['k16']], pr[nm['v16']], pr[nm['g']], bp, sp, ATTN_TQ,
                         ps[ns['q']], ps[ns['k32']], ps[ns['v32']], ps[ns['g']],
                         cache_sb_k, cache_sb_v, layer, page_table, sb_bias[layer], ts)
    ssd, conv_p, ssm_p = _ssd(
        pr[nm['xbc']], pr[nm['zdt']], jnp.zeros((bp, CONV_WIDTH - 1, XBC_WIDTH), F32),
        jnp.zeros((bp, SSD_HEADS, SSD_HEAD_DIM, SSD_STATE), F32), *ssd_params,
        batch=bp, n_chunks=sp // L, length=L, out_dtype=BF16)
    mo = _mem_attn(pr[nm['mq']], pr[nm['mg']], mk.reshape(bp, MEM_TOKENS, MEM_WIDTH),
                   mv.reshape(bp, MEM_TOKENS, MEM_WIDTH), bp, sp, tq=MEM_TQ, out_dtype=BF16)
    ssd_s, conv_s, ssm_s = _ssd(
        ps[ns['xbc']], ps[ns['zdt']], state_conv[layer], state_ssm[layer],
        *ssd_params, batch=bs, n_chunks=1, length=ts, out_dtype=F32)
    mo_s = _mem_attn_rows(ps[ns['mq']], ps[ns['mg']],
                          cache_mem_k[layer].reshape(bs, MEM_TOKENS * MEM_HEADS, HEAD_DIM),
                          cache_mem_v[layer].reshape(bs, MEM_TOKENS * MEM_HEADS, HEAD_DIM),
                          bs, ts, out_dtype=F32)
    yp, ys = _out_proj(xp, sb, ssd, mo, (xs, sb_s, ssd_s, mo_s), w_o, tm=OUT_TM)

    tail = slice(SUBLANES - (CONV_WIDTH - 1), SUBLANES)
    return (
        yp.reshape(bp, sp, d),
        ys.reshape(bs, ts, d),
        pr[nm['k32']].reshape(1, bp, sp, SB_HEADS, HEAD_DIM),
        pr[nm['v32']].reshape(1, bp, sp, SB_HEADS, HEAD_DIM),
        ssm_p.reshape(1, bp, SSD_HEADS, SSD_HEAD_DIM, SSD_STATE),
        conv_p[:, tail][None],
        mk.reshape(1, bp, MEM_TOKENS, MEM_HEADS, HEAD_DIM),
        mv.reshape(1, bp, MEM_TOKENS, MEM_HEADS, HEAD_DIM),
        ps[ns['k32']].reshape(1, bs, ts, SB_HEADS, HEAD_DIM),
        ps[ns['v32']].reshape(1, bs, ts, SB_HEADS, HEAD_DIM),
        ssm_s.reshape(1, bs, SSD_HEADS, SSD_HEAD_DIM, SSD_STATE),
        conv_s[:, tail][None],
    )
```

```python
import functools
import math

import jax
import jax.numpy as jnp
from jax import lax
from jax.experimental import pallas as pl
from jax.experimental.pallas import tpu as pltpu

F32 = jnp.float32
BF16 = jnp.bfloat16

SB_HEADS = 8
HEAD_DIM = 128
SB_WIDTH = SB_HEADS * HEAD_DIM
SSD_HEADS = 8
SSD_HEAD_DIM = 64
SSD_WIDTH = SSD_HEADS * SSD_HEAD_DIM
SSD_GROUPS = 2
SSD_STATE = 128
CONV_WIDTH = 4
XBC_WIDTH = SSD_WIDTH + 2 * SSD_GROUPS * SSD_STATE
MEM_TOKENS = 256
MEM_HEADS = 4
MEM_WIDTH = MEM_HEADS * HEAD_DIM
PAGE_SIZE = 128
EPS = 1e-6
ATTN_SCALE = HEAD_DIM ** -0.5
LOG2E = math.log2(math.e)

SSD_CHUNK = 128
PROJ_TN = 1024
PROJ_CHUNK = 256
ZDT_WIDTH = SSD_WIDTH + PROJ_CHUNK
SAMPLE_BUFS = 3
SSD_SEQS_PER_STEP = 4
SUBLANES = 8
SB_HEAD_BITS = SB_HEADS.bit_length() - 1
NORM_ROW_CHUNK = 128
PROJ_TM = 512
OUT_TM = 512
ATTN_TQ = 256
MEM_TQ = 1024
VMEM_LIMIT = 56 * 1024 * 1024
VMEM_LIMIT_FUSED = 62 * 1024 * 1024

_NT = (((1,), (1,)), ((), ()))


def _dot(a, b):
    return jnp.dot(a, b, preferred_element_type=F32)


def _dot_nt(a, b):
    return lax.dot_general(a, b, _NT, preferred_element_type=F32)


def _split2(x):
    hi = x.astype(BF16)
    lo = (x - hi.astype(F32)).astype(BF16)
    return hi, lo


def _split3(x):
    hi = x.astype(BF16)
    r = x - hi.astype(F32)
    mid = r.astype(BF16)
    lo = (r - mid.astype(F32)).astype(BF16)
    return hi, mid, lo


def _dots_exact_lhs(xs, m):
    rows = xs[0].shape[0]
    r = _dot(jnp.concatenate([t for x in xs for t in _split3(x)], axis=0), m)
    part = lambda i: r[rows * i:rows * (i + 1), :]
    return [part(3 * i) + part(3 * i + 1) + part(3 * i + 2) for i in range(len(xs))]


def _dot_exact_rhs(m, x):
    n = x.shape[1]
    r = _dot(m, jnp.concatenate(_split3(x), axis=1))
    return r[:, :n] + r[:, n:2 * n] + r[:, 2 * n:]


def _silu(x):
    return x * (1.0 / (1.0 + jnp.exp(-x)))


def _rmsnorm_rows(x_ref, nw_ref, h_ref):
    rows_total = x_ref.shape[0]
    rc = min(rows_total, NORM_ROW_CHUNK)

    def body(r, carry):
        rows = pl.ds(pl.multiple_of(r * rc, rc), rc)
        xv = x_ref[rows, :]
        ms = jnp.mean(xv * xv, axis=-1, keepdims=True)
        h_ref[rows, :] = (xv * lax.rsqrt(ms + EPS) * nw_ref[...]).astype(BF16)
        return carry

    lax.fori_loop(0, rows_total // rc, body, 0)


def _proj_step(segs, h_ref, w_ref, hn_ref, outs, w_rows_are_outputs):
    for col0, width, hn_row, o32, o16 in segs:
        for c0 in range(0, width, PROJ_CHUNK):
            chunk = slice(col0 + c0, col0 + c0 + PROJ_CHUNK)
            if w_rows_are_outputs:
                y = _dot_nt(h_ref[...], w_ref[chunk, :].astype(BF16))
            else:
                y = _dot(h_ref[...], w_ref[:, chunk].astype(BF16))
            for c in range(0, PROJ_CHUNK, HEAD_DIM):
                yc = y[:, c:c + HEAD_DIM]
                if hn_row is not None:
                    ms = jnp.mean(yc * yc, axis=-1, keepdims=True)
                    yc = yc * lax.rsqrt(ms + EPS) * hn_ref[hn_row:hn_row + 1, :]
                cols = slice(c0 + c, c0 + c + HEAD_DIM)
                if o32 is not None:
                    outs[o32][:, cols] = yc
                if o16 is not None:
                    outs[o16][:, cols] = yc.astype(BF16)


def _proj_kernel(plan, n_out, side_plan, n_side, w_rows_are_outputs, *refs):
    if side_plan is None:
        x_ref, nw_ref, w_ref, hn_ref = refs[:4]
        rest = refs[4:]
    else:
        x_ref, xs_ref, nw_ref, w_ref, hn_ref = refs[:5]
        rest = refs[5:]
    outs, side_outs = rest[:n_out], rest[n_out:n_out + n_side]
    scratch = rest[n_out + n_side:]
    h_ref = scratch[0]
    m = pl.program_id(0)
    n = pl.program_id(1)

    @pl.when(n == 0)
    def _():
        _rmsnorm_rows(x_ref, nw_ref, h_ref)

    if side_plan is not None:
        hs_ref = scratch[1]

        @pl.when((n == 0) & (m == 0))
        def _():
            _rmsnorm_rows(xs_ref, nw_ref, hs_ref)

    for step, segs in enumerate(plan):
        @pl.when(n == step)
        def _(step=step, segs=segs):
            _proj_step(segs, h_ref, w_ref, hn_ref, outs, w_rows_are_outputs)
            if side_plan is not None:
                @pl.when(m == 0)
                def _():
                    _proj_step(side_plan[step], hs_ref, w_ref, hn_ref, side_outs,
                               w_rows_are_outputs)


def _proj(x, norm_w, w, head_norms, plan, out_defs, tm, side=None, w_rows_are_outputs=True):
    t, d = x.shape
    n_steps = len(plan)
    w_shape = (n_steps * PROJ_TN, d) if w_rows_are_outputs else (d, n_steps * PROJ_TN)
    assert w.shape == w_shape and t % tm == 0
    w_spec = (pl.BlockSpec((PROJ_TN, d), lambda m, n: (n, 0)) if w_rows_are_outputs
              else pl.BlockSpec((d, PROJ_TN), lambda m, n: (0, n)))
    row = lambda m, n: (m, 0)
    const = lambda m, n: (0, 0)
    in_specs = [pl.BlockSpec((tm, d), row)]
    operands = [x]
    out_specs = [pl.BlockSpec((tm, w), row) for w, _ in out_defs]
    out_shape = [jax.ShapeDtypeStruct((t, w), dt) for w, dt in out_defs]
    scratch = [pltpu.VMEM((tm, d), BF16)]
    side_plan, n_side = None, 0
    if side is not None:
        x_side, side_plan, side_defs = side
        ts = x_side.shape[0]
        assert len(side_plan) == n_steps
        n_side = len(side_defs)
        in_specs.append(pl.BlockSpec((ts, d), const))
        operands.append(x_side)
        out_specs += [pl.BlockSpec((ts, w), const) for w, _ in side_defs]
        out_shape += [jax.ShapeDtypeStruct((ts, w), dt) for w, dt in side_defs]
        scratch.append(pltpu.VMEM((ts, d), BF16))
    in_specs += [pl.BlockSpec((1, d), const), w_spec,
                 pl.BlockSpec((SUBLANES, HEAD_DIM), const)]
    operands += [norm_w.reshape(1, d), w, head_norms]
    kern = functools.partial(_proj_kernel, plan, len(out_defs), side_plan, n_side,
                             w_rows_are_outputs)
    res = pl.pallas_call(
        kern,
        grid=(t // tm, n_steps),
        in_specs=in_specs,
        out_specs=out_specs,
        out_shape=out_shape,
        scratch_shapes=scratch,
        compiler_params=pltpu.CompilerParams(
            dimension_semantics=("arbitrary", "arbitrary"), vmem_limit_bytes=VMEM_LIMIT),
        name="norm_proj",
    )(*operands)
    return res[:len(out_defs)], res[len(out_defs):]


def _log2_fail(z2):
    nz = -z2
    return jnp.minimum(nz, 0.0) - jnp.log(1.0 + jnp.exp2(jnp.minimum(z2, nz))) * LOG2E


def _sb_fused_kernel(tq, nq, ppc, n_pages, t_new, page_base,
                     pt_ref, bias_ref, qa_ref, qb_ref, k_ref, v_ref, ga_ref, gb_ref, uu_ref,
                     qs_ref, kn_ref, vn_ref, gs_ref, uo_ref, ck_hbm, cv_hbm,
                     op_ref, os_ref, acc_ref, c_ref, accs_ref, cs_ref, kbuf, vbuf, sem):
    step = pl.program_id(0)
    n_steps = pl.num_programs(0)
    pair = lax.rem(step, nq // 2)
    i1 = pair
    i2 = nq - 1 - pair
    chunks_per_seq = n_pages // ppc
    seq_steps = chunks_per_seq // nq
    part = lax.rem(step, seq_steps)
    n_chunks = n_steps * nq

    def chunk_copies(chunk):
        slot = lax.rem(chunk, SAMPLE_BUFS)
        seq = lax.div(chunk, chunks_per_seq)
        first_pos = (n_pages - 1) - lax.rem(chunk, chunks_per_seq) * ppc
        copies = []
        for j in range(ppc):
            page = page_base + pt_ref[seq * n_pages + first_pos - j]
            copies.append(pltpu.make_async_copy(ck_hbm.at[page], kbuf.at[slot, j], sem.at[slot, 0]))
            copies.append(pltpu.make_async_copy(cv_hbm.at[page], vbuf.at[slot, j], sem.at[slot, 1]))
        return copies

    def start_chunk(chunk):
        for i, cp in enumerate(chunk_copies(chunk)):
            cp.start(priority=i % 2)

    def wait_chunk(chunk):
        slot = lax.rem(chunk, SAMPLE_BUFS)
        pltpu.make_async_copy(ck_hbm.at[pl.ds(0, ppc)], kbuf.at[slot], sem.at[slot, 0]).wait()
        pltpu.make_async_copy(cv_hbm.at[pl.ds(0, ppc)], vbuf.at[slot], sem.at[slot, 1]).wait()

    def start_next(chunk):
        @pl.when(chunk + SAMPLE_BUFS < n_chunks)
        def _():
            start_chunk(chunk + SAMPLE_BUFS)

    @pl.when(step == 0)
    def _():
        for c in range(SAMPLE_BUFS):
            start_chunk(jnp.int32(c))

    lanes = PAGE_SIZE * SB_HEADS
    lane =lax.broadcasted_iota(jnp.int32, (t_new, lanes), 1)
    lane_head = lane & (SB_HEADS - 1)
    biases = [bias_ref[h] * LOG2E for h in range(SB_HEADS)]
    bias_lanes = jnp.full((t_new, lanes), biases[0], F32)
    for h in range(1, SB_HEADS):
        bias_lanes = jnp.where(lane_head == h, biases[h], bias_lanes)
    qs = qs_ref[...]
    q_all = jnp.concatenate([qs[:, HEAD_DIM * h:HEAD_DIM * (h + 1)] for h in range(SB_HEADS)],
                            axis=0).astype(BF16)

    def page_scores(kpages, n, mask):
        s_cat = _dot_nt(q_all, kpages.astype(BF16))
        width = kpages.shape[0] // n
        return [scores(s_cat[:, width * p:width * (p + 1)], mask) for p in range(n)]

    def scores(s_all, mask):
        width = s_all.shape[1]
        head = lane_head[:, :width]
        sc = s_all[0:t_new, :]
        for h in range(1, SB_HEADS):
            sc = jnp.where(head == h, s_all[t_new * h:t_new * (h + 1), :], sc)
        z2 = sc * (ATTN_SCALE * LOG2E) + bias_lanes[:, :width]
        lf = _log2_fail(z2)
        if mask is not None:
            lf = jnp.where(mask, lf, 0.0)
        blocks = jnp.concatenate([lf[:, HEAD_DIM * j:HEAD_DIM * (j + 1)]
                                  for j in range(width // HEAD_DIM)], axis=0)
        hi, lo = _split2(blocks)
        return z2, jnp.concatenate([hi, lo], axis=1)

    def stacked_sums(operands, matrix):
        rows = operands[0].shape[0]
        res = _dot(jnp.concatenate(operands, axis=0), matrix)
        return [res[rows * i:rows * (i + 1), :] for i in range(len(operands))]

    def weights(z2, res, mask, run):
        width = z2.shape[1]
        head = lane_head[:, :width]
        n_blk = width // HEAD_DIM
        ws = [None] * n_blk
        for j in reversed(range(n_blk)):
            rows = slice(t_new * j, t_new * (j + 1))
            logw = z2[:, HEAD_DIM * j:HEAD_DIM * (j + 1)] + res[rows, :HEAD_DIM]
            if run is not None:
                logw = logw + run
            ws[j] = jnp.exp2(logw)
            tot = res[rows, HEAD_DIM:]
            run = tot if run is None else run + tot
        w = jnp.concatenate(ws, axis=1)
        if mask is not None:
            w = jnp.where(mask, w, 0.0)
        w_all = jnp.concatenate([jnp.where(head == h, w, 0.0) for h in range(SB_HEADS)],
                                axis=0).astype(BF16)
        return w_all, run

    def new_keys():
        width = kn_ref.shape[0]
        mask = ((lane >> SB_HEAD_BITS) < lax.broadcasted_iota(jnp.int32, (t_new, lanes), 0)
                )[:, :width]
        (z2, hilo), = page_scores(kn_ref[...], 1, mask)
        res, = stacked_sums([hilo], uo_ref[...])
        w_all, run = weights(z2, res, mask, None)
        cs_ref[...] = run
        accs_ref[...] = _dot(w_all, vn_ref[...].astype(BF16))

    uu = uu_ref[...]
    cols = [slice(HEAD_DIM * h, HEAD_DIM * (h + 1)) for h in range(SB_HEADS)]

    def section(q_ref, start, diag, first, chunk):
        kb = k_ref[pl.ds(start, tq), :]
        vb = v_ref[pl.ds(start, tq), :]
        n_pages_here = 0 if chunk is None else ppc
        slot = None if chunk is None else lax.rem(chunk, SAMPLE_BUFS)
        if n_pages_here:
            run = cs_ref[...]
            acc = accs_ref[...]
        p_parts, s_parts = {}, {}
        n_idx = max(SB_HEADS, n_pages_here)

        raw = [_dot_nt(q_ref[:, cols[h]], kb[:, cols[h]]) for h in range(SB_HEADS)]
        if n_pages_here:
            s_split = page_scores(kbuf[slot].reshape(ppc * lanes, HEAD_DIM), ppc, None)
        split = []
        for h in range(SB_HEADS):
            z2 = raw[h] * (ATTN_SCALE * LOG2E) + biases[h]
            lf = _log2_fail(z2)
            if diag is not None:
                lf = jnp.where(diag, lf, 0.0)
            hi, lo = _split2(lf)
            split.append((z2, jnp.concatenate([hi, lo], axis=1)))
        incl = stacked_sums([hilo for _, hilo in split], uu)
        for h in range(SB_HEADS):
            p_parts[h] = (split[h][0], incl[h])
        if n_pages_here:
            res = stacked_sums([hilo for _, hilo in s_split], uo_ref[...])
            for idx in range(n_pages_here):
                s_parts[idx] = (s_split[idx][0], res[idx])
        w_pages = []
        for idx in range(n_idx):
            if idx < SB_HEADS:
                z2, incl_h = p_parts[idx]
                logw = z2 + incl_h
                if not first:
                    logw = logw + c_ref[idx]
                w = jnp.exp2(logw)
                if diag is not None:
                    w = jnp.where(diag, w, 0.0)
                pv = _dot(w.astype(BF16), vb[:, cols[idx]])
                total = incl_h[:, 0:1]
                if first:
                    acc_ref[:, cols[idx]] = pv
                    c_ref[idx] = total
                else:
                    acc_ref[:, cols[idx]] += pv
                    c_ref[idx] += total
            if idx < n_pages_here:
                w_all, run = weights(*s_parts[idx], None, run)
                w_pages.append(w_all)
        if n_pages_here:
            cs_ref[...] = run
            accs_ref[...] = acc + _dot(jnp.concatenate(w_pages, axis=1),
                                       vbuf[slot].reshape(ppc * lanes, HEAD_DIM).astype(BF16))

    def finish(i, g_ref):
        rows = pl.ds(pl.multiple_of(i * tq, tq), tq)
        op_ref[rows, :] = (acc_ref[...] * _silu(g_ref[...])).astype(op_ref.dtype)

    row = lax.broadcasted_iota(jnp.int32, (tq, tq), 0)
    col = lax.broadcasted_iota(jnp.int32, (tq, tq), 1)
    diag = col < row
    chunk0 = step * nq

    @pl.when(part == 0)
    def _():
        new_keys()

    section(qa_ref, pl.multiple_of(i1 * tq, tq), diag, True, None)

    def body_a(t, carry):
        chunk = chunk0 + t
        wait_chunk(chunk)
        section(qa_ref, pl.multiple_of((i1 - 1 - t) * tq, tq), None, False, chunk)
        start_next(chunk)
        return carry

    lax.fori_loop(0, i1, body_a, 0)
    finish(i1, ga_ref)

    chunk = chunk0 + i1
    wait_chunk(chunk)
    section(qb_ref, pl.multiple_of(i2 * tq, tq), diag, True, chunk)
    start_next(chunk)

    def body_b(t, carry):
        chunk = chunk0 + i1 + 1 + t
        wait_chunk(chunk)
        section(qb_ref, pl.multiple_of((i2 - 1 - t) * tq, tq), None, False, chunk)
        start_next(chunk)
        return carry

    lax.fori_loop(0, i2, body_b, 0)
    finish(i2, gb_ref)

    @pl.when(part == seq_steps - 1)
    def _():
        g = gs_ref[...]
        for h in range(SB_HEADS):
            os_ref[:, cols[h]] = accs_ref[t_new * h:t_new * (h + 1), :] * _silu(g[:, cols[h]])


def _sb_fused(q, k, v, g, batch, seq, tq, q_s, k_new, v_new, g_s, cache_k, cache_v, layer,
              page_table, sb_bias, t_new):
    n_seq, n_pages = page_table.shape
    n_pool = cache_k.shape[1]
    nq = seq // tq
    n_steps = batch * (nq // 2)
    assert nq % 2 == 0 and (n_seq * n_pages) % (n_steps * nq) == 0
    ppc = n_seq * n_pages // (n_steps * nq)
    assert n_pages % (ppc * nq) == 0
    seq_steps = n_pages // (ppc * nq)
    page_rows = PAGE_SIZE * SB_HEADS

    q3, k3, v3, g3 = (a.reshape(batch, seq, SB_WIDTH) for a in (q, k, v, g))
    u = (jnp.arange(tq)[:, None] >= jnp.arange(tq)[None, :]).astype(BF16)
    uu = jnp.concatenate([u, u], axis=0)

    new_tokens = -(-t_new * SB_HEADS // HEAD_DIM) * HEAD_DIM // SB_HEADS
    new_rows = new_tokens * SB_HEADS

    def as_page(a):
        a = a.reshape(n_seq, t_new, SB_HEADS, HEAD_DIM)
        a = jnp.pad(a, ((0, 0), (0, new_tokens - t_new), (0, 0), (0, 0)))
        return a.reshape(n_seq * new_rows, HEAD_DIM)

    cache_k = cache_k.reshape(-1, page_rows, HEAD_DIM)
    cache_v = cache_v.reshape(-1, page_rows, HEAD_DIM)
    r = jnp.arange(HEAD_DIM)
    same_head = (r[:, None] % SB_HEADS) == (r[None, :] % SB_HEADS)
    not_earlier = (r[:, None] // SB_HEADS) >= (r[None, :] // SB_HEADS)
    uo = jnp.concatenate([same_head & not_earlier, same_head], axis=1).astype(BF16)
    uo = jnp.concatenate([uo, uo], axis=0)

    half = nq // 2
    b_of = lambda s: s // half
    qa_map = lambda s, pt: (b_of(s), s % half, 0)
    qb_map = lambda s, pt: (b_of(s), nq - 1 - s % half, 0)
    seq_map = lambda s, pt: (b_of(s), 0, 0)
    samp_map = lambda s, pt: (s // seq_steps, 0)
    const = lambda s, pt: (0, 0)
    kern = functools.partial(_sb_fused_kernel, tq, nq, ppc, n_pages, t_new, layer * n_pool)
    grid_spec = pltpu.PrefetchScalarGridSpec(
        num_scalar_prefetch=1,
        grid=(n_steps,),
        in_specs=[
            pl.BlockSpec(memory_space=pltpu.SMEM),
            pl.BlockSpec((None, tq, SB_WIDTH), qa_map),
            pl.BlockSpec((None, tq, SB_WIDTH), qb_map),
            pl.BlockSpec((None, seq, SB_WIDTH), seq_map, pipeline_mode=pl.Buffered(1)),
            pl.BlockSpec((None, seq, SB_WIDTH), seq_map, pipeline_mode=pl.Buffered(1)),
            pl.BlockSpec((None, tq, SB_WIDTH), qa_map),
            pl.BlockSpec((None, tq, SB_WIDTH), qb_map),
            pl.BlockSpec((2 * tq, tq), const),
            pl.BlockSpec((t_new, SB_WIDTH), samp_map),
            pl.BlockSpec((new_rows, HEAD_DIM), samp_map),
            pl.BlockSpec((new_rows, HEAD_DIM), samp_map),
            pl.BlockSpec((t_new, SB_WIDTH), samp_map),
            pl.BlockSpec((2 * HEAD_DIM, 2 * HEAD_DIM), const),
            pl.BlockSpec(memory_space=pl.ANY),
            pl.BlockSpec(memory_space=pl.ANY),
        ],
        out_specs=[
            pl.BlockSpec((None, seq, SB_WIDTH), seq_map, pipeline_mode=pl.Buffered(1)),
            pl.BlockSpec((t_new, SB_WIDTH), samp_map),
        ],
        scratch_shapes=[
            pltpu.VMEM((tq, SB_WIDTH), F32),
            pltpu.VMEM((SB_HEADS, tq, 1), F32),
            pltpu.VMEM((SB_HEADS * t_new, HEAD_DIM), F32),
            pltpu.VMEM((t_new, HEAD_DIM), F32),
            pltpu.VMEM((SAMPLE_BUFS, ppc, page_rows, HEAD_DIM), F32),
            pltpu.VMEM((SAMPLE_BUFS, ppc, page_rows, HEAD_DIM), F32),
            pltpu.SemaphoreType.DMA((SAMPLE_BUFS, 2)),
        ],
    )
    out_p, out_s = pl.pallas_call(
        kern,
        grid_spec=grid_spec,
        out_shape=[jax.ShapeDtypeStruct((batch, seq, SB_WIDTH), BF16),
                   jax.ShapeDtypeStruct((n_seq * t_new, SB_WIDTH), F32)],
        compiler_params=pltpu.CompilerParams(
            dimension_semantics=("arbitrary",), vmem_limit_bytes=VMEM_LIMIT_FUSED),
        name="sb_fused",
    )(page_table.reshape(-1), sb_bias, q3, q3, k3, v3, g3, g3, uu,
      q_s, as_page(k_new), as_page(v_new), g_s, uo, cache_k, cache_v)
    return out_p.reshape(batch * seq, SB_WIDTH), out_s


def _ssd_kernel(length, n_par, *refs):
    per_seq_in, shared, per_seq_out = refs[:5], refs[5:13], refs[13:]
    pre_ref, h0_ref = per_seq_in[3:5]
    ext_ref, st_ref = per_seq_out[3:5]

    @pl.when(pl.program_id(1) == 0)
    def _():
        ext_ref[:, 0:SUBLANES, :] = pre_ref[...]
        st_ref[...] = h0_ref[...]

    for s in range(n_par):
        _ssd_chunk(length, *[r.at[s] for r in per_seq_in[:3]], *shared,
                   *[r.at[s] for r in per_seq_out])


def _pad_rows(a, rows):
    if a.shape[0] == rows:
        return a
    return jnp.concatenate([a, jnp.zeros((rows - a.shape[0], a.shape[1]), a.dtype)], axis=0)


def _ssd_chunk(length, xbc_ref, z_ref, dt_ref, cw_ref, cb_ref, dtb_ref,
               alog_ref, dsk_ref, nw_ref, ltri_ref, e_ref, out_ref, cnew_ref, snew_ref,
               ext_ref, st_ref):
    L = SSD_CHUNK
    P = SSD_HEAD_DIM

    ext_ref[SUBLANES:SUBLANES + L, :] = _pad_rows(xbc_ref[...], L)
    cw = cw_ref[...]
    conv = cb_ref[...]
    for j in range(CONV_WIDTH):
        off = SUBLANES - (CONV_WIDTH - 1) + j
        conv = conv + ext_ref[off:off + L, :] * cw[j:j + 1, :]
    act = _silu(conv)
    tail = ext_ref[length:length + SUBLANES, :]
    cnew_ref[...] = tail
    ext_ref[0:SUBLANES, :] = tail

    xs = act[:, :SSD_WIDTH]
    bm = act[:, SSD_WIDTH:SSD_WIDTH + SSD_GROUPS * SSD_STATE]
    cm = act[:, SSD_WIDTH + SSD_GROUPS * SSD_STATE:]

    x_dt = _pad_rows(dt_ref[...], L) + dtb_ref[...]
    dt = jnp.maximum(x_dt, 0.0) + jnp.log1p(jnp.exp(-jnp.abs(x_dt)))
    if length < L:
        valid = lax.broadcasted_iota(jnp.int32, dt.shape, 0) < length
        dt = jnp.where(valid, dt, 0.0)
    da = dt * (-jnp.exp(alog_ref[...]))
    cs = _dot_exact_rhs(ltri_ref[...], da)
    cs_t = cs.T
    e = e_ref[...]
    dt_x, cs_x = _dots_exact_lhs([dt, cs], e)
    xdt = xs * dt_x
    ecs = jnp.exp(cs_x)
    xw_t = (xdt * jnp.exp(cs_x[L - 1:L, :] - cs_x)).T
    xdt16 = xdt.astype(BF16)

    row = lax.broadcasted_iota(jnp.int32, (L, L), 0)
    col = lax.broadcasted_iota(jnp.int32, (L, L), 1)
    causal = col <= row
    heads_per_group = SSD_HEADS // SSD_GROUPS
    gw = heads_per_group * P
    y_diag, y_off = [], []
    for g in range(SSD_GROUPS):
        bg = bm[:, SSD_STATE * g:SSD_STATE * (g + 1)].astype(BF16)
        cg = cm[:, SSD_STATE * g:SSD_STATE * (g + 1)].astype(BF16)
        cb = _dot_nt(cg, bg)
        prev = st_ref[gw * g:gw * (g + 1), :]
        y_off.append(_dot_nt(cg, prev.astype(BF16)))
        new = _dot(xw_t[gw * g:gw * (g + 1), :].astype(BF16), bg)
        for r in range(heads_per_group):
            h = heads_per_group * g + r
            seg = cs[:, h:h + 1] - cs_t[h:h + 1, :]
            decay = jnp.exp(jnp.where(causal, seg, -jnp.inf))
            y_diag.append(_dot((cb * decay).astype(BF16), xdt16[:, P * h:P * (h + 1)]))
            chunk_decay = jnp.exp(cs[L - 1:L, h:h + 1])
            st_ref[P * h:P * (h + 1), :] = (prev[P * r:P * (r + 1), :] * chunk_decay
                                            + new[P * r:P * (r + 1), :])
    snew_ref[...] = st_ref[...]
    y = (jnp.concatenate(y_diag, axis=1) + jnp.concatenate(y_off, axis=1) * ecs
         + xs * dsk_ref[...])
    gated = y[:length] * _silu(z_ref[...])
    ms = jnp.mean(gated * gated, axis=-1, keepdims=True)
    out_ref[...] = (gated * lax.rsqrt(ms + EPS) * nw_ref[...]).astype(out_ref.dtype)


def _ssd(xbc, zdt, prefix, h0, conv_w, conv_b, dt_bias, a_log, d_skip, ssd_norm_w,
         batch, n_chunks, length, out_dtype):
    L = SSD_CHUNK
    pre = jnp.pad(prefix, ((0, 0), (SUBLANES - (CONV_WIDTH - 1), 0), (0, 0)))
    pad_h = lambda a: jnp.pad(a, (0, HEAD_DIM - SSD_HEADS)).reshape(1, HEAD_DIM)
    ltri = (jnp.arange(L)[:, None] >= jnp.arange(L)[None, :]).astype(BF16)
    expand = (jnp.arange(HEAD_DIM)[:, None] == jnp.arange(SSD_WIDTH)[None, :] // SSD_HEAD_DIM
              ).astype(BF16)
    dsk = jnp.repeat(d_skip, SSD_HEAD_DIM).reshape(1, SSD_WIDTH)
    z_blk = SSD_WIDTH // HEAD_DIM
    n_par = SSD_SEQS_PER_STEP
    assert batch % n_par == 0 and length <= L and (length == L or n_chunks == 1)
    rows = n_chunks * length
    xbc3 = xbc.reshape(batch, rows, XBC_WIDTH)
    zdt3 = zdt.reshape(batch, rows, ZDT_WIDTH)
    const = lambda b, c: (0, 0)
    seq = lambda b, c: (b, 0, 0)
    kern = functools.partial(_ssd_kernel, length, n_par)
    out, conv_new, ssm_new = pl.pallas_call(
        kern,
        grid=(batch // n_par, n_chunks),
        in_specs=[
            pl.BlockSpec((n_par, length, XBC_WIDTH), lambda b, c: (b, c, 0)),
            pl.BlockSpec((n_par, length, SSD_WIDTH), lambda b, c: (b, c, 0)),
            pl.BlockSpec((n_par, length, HEAD_DIM), lambda b, c: (b, c, z_blk)),
            pl.BlockSpec((n_par, SUBLANES, XBC_WIDTH), seq),
            pl.BlockSpec((n_par, SSD_WIDTH, SSD_STATE), seq),
            pl.BlockSpec((CONV_WIDTH, XBC_WIDTH), const),
            pl.BlockSpec((1, XBC_WIDTH), const),
            pl.BlockSpec((1, HEAD_DIM), const),
            pl.BlockSpec((1, HEAD_DIM), const),
            pl.BlockSpec((1, SSD_WIDTH), const),
            pl.BlockSpec((1, SSD_WIDTH), const),
            pl.BlockSpec((L, L), const),
            pl.BlockSpec((HEAD_DIM, SSD_WIDTH), const),
        ],
        out_specs=[
            pl.BlockSpec((n_par, length, SSD_WIDTH), lambda b, c: (b, c, 0)),
            pl.BlockSpec((n_par, SUBLANES, XBC_WIDTH), seq),
            pl.BlockSpec((n_par, SSD_WIDTH, SSD_STATE), seq),
        ],
        out_shape=[
            jax.ShapeDtypeStruct((batch, rows, SSD_WIDTH), out_dtype),
            jax.ShapeDtypeStruct((batch, SUBLANES, XBC_WIDTH), F32),
            jax.ShapeDtypeStruct((batch, SSD_WIDTH, SSD_STATE), F32),
        ],
        scratch_shapes=[pltpu.VMEM((n_par, SUBLANES + L, XBC_WIDTH), F32),
                        pltpu.VMEM((n_par, SSD_WIDTH, SSD_STATE), F32)],
        compiler_params=pltpu.CompilerParams(
            dimension_semantics=("parallel", "arbitrary"), vmem_limit_bytes=VMEM_LIMIT),
        name="ssd_scan",
    )(xbc3, zdt3, zdt3, pre, h0.reshape(batch, SSD_WIDTH, SSD_STATE), conv_w,
      conv_b.reshape(1, XBC_WIDTH), pad_h(dt_bias), pad_h(a_log), dsk,
      ssd_norm_w.reshape(1, SSD_WIDTH), ltri, expand)
    return out.reshape(batch * rows, SSD_WIDTH), conv_new, ssm_new


def _mem_attn_kernel(q_ref, g_ref, k_ref, v_ref, o_ref):
    for h in range(MEM_HEADS):
        cols = slice(HEAD_DIM * h, HEAD_DIM * (h + 1))
        s = _dot_nt(q_ref[:, cols].astype(BF16), k_ref[:, cols].astype(BF16)) * ATTN_SCALE
        p = jnp.exp(s - jnp.max(s, axis=-1, keepdims=True))
        den = jnp.sum(p, axis=-1, keepdims=True)
        o = _dot(p.astype(BF16), v_ref[:, cols].astype(BF16)) / den
        o_ref[:, cols] = (o * _silu(g_ref[:, cols])).astype(o_ref.dtype)


def _mem_attn(q, g, mem_k, mem_v, batch, t, tq, out_dtype):
    nq = t // tq
    return pl.pallas_call(
        _mem_attn_kernel,
        grid=(batch, nq),
        in_specs=[
            pl.BlockSpec((tq, MEM_WIDTH), lambda b, i: (b * nq + i, 0)),
            pl.BlockSpec((tq, MEM_WIDTH), lambda b, i: (b * nq + i, 0)),
            pl.BlockSpec((None, MEM_TOKENS, MEM_WIDTH), lambda b, i: (b, 0, 0)),
            pl.BlockSpec((None, MEM_TOKENS, MEM_WIDTH), lambda b, i: (b, 0, 0)),
        ],
        out_specs=pl.BlockSpec((tq, MEM_WIDTH), lambda b, i: (b * nq + i, 0)),
        out_shape=jax.ShapeDtypeStruct((batch * t, MEM_WIDTH), out_dtype),
        compiler_params=pltpu.CompilerParams(
            dimension_semantics=("parallel", "parallel"), vmem_limit_bytes=VMEM_LIMIT),
        name="mem_attn",
    )(q, g, mem_k, mem_v)


def _mem_attn_rows_kernel(t, q_ref, g_ref, k_ref, v_ref, o_ref):
    q = q_ref[...]
    g = g_ref[...]
    cols = [slice(HEAD_DIM * h, HEAD_DIM * (h + 1)) for h in range(MEM_HEADS)]
    q_all = jnp.concatenate([q[:, c] for c in cols], axis=0).astype(BF16)
    s = _dot_nt(q_all, k_ref[...].astype(BF16)) * ATTN_SCALE
    n_rows = s.shape[1]
    row_head = jnp.concatenate([jnp.full((t, n_rows), h, jnp.int32) for h in range(MEM_HEADS)],
                               axis=0)
    lane_head = lax.broadcasted_iota(jnp.int32, s.shape, 1) & (MEM_HEADS - 1)
    s = jnp.where(row_head == lane_head, s, -jnp.inf)
    p = jnp.exp(s - jnp.max(s, axis=-1, keepdims=True))
    den = jnp.sum(p, axis=-1, keepdims=True)
    o = _dot(p.astype(BF16), v_ref[...].astype(BF16)) / den
    for h in range(MEM_HEADS):
        o_ref[:, cols[h]] = (o[t * h:t * (h + 1), :] * _silu(g[:, cols[h]])).astype(o_ref.dtype)


def _mem_attn_rows(q, g, mem_k, mem_v, batch, t, out_dtype):
    rows = MEM_TOKENS * MEM_HEADS
    kern = functools.partial(_mem_attn_rows_kernel, t)
    return pl.pallas_call(
        kern,
        grid=(batch,),
        in_specs=[
            pl.BlockSpec((t, MEM_WIDTH), lambda b: (b, 0)),
            pl.BlockSpec((t, MEM_WIDTH), lambda b: (b, 0)),
            pl.BlockSpec((None, rows, HEAD_DIM), lambda b: (b, 0, 0)),
            pl.BlockSpec((None, rows, HEAD_DIM), lambda b: (b, 0, 0)),
        ],
        out_specs=pl.BlockSpec((t, MEM_WIDTH), lambda b: (b, 0)),
        out_shape=jax.ShapeDtypeStruct((batch * t, MEM_WIDTH), out_dtype),
        compiler_params=pltpu.CompilerParams(
            dimension_semantics=("parallel",), vmem_limit_bytes=VMEM_LIMIT),
        name="mem_attn_rows",
    )(q, g, mem_k, mem_v)


def _out_proj_rows(x_ref, sb_ref, ssd_ref, mo_ref, w_ref, o_ref):
    mix = jnp.concatenate([sb_ref[...].astype(BF16), ssd_ref[...].astype(BF16),
                           mo_ref[...].astype(BF16)], axis=-1)
    for c in range(0, o_ref.shape[1], PROJ_CHUNK):
        cols = slice(c, c + PROJ_CHUNK)
        o_ref[:, cols] = x_ref[:, cols] + _dot(mix, w_ref[:, cols].astype(BF16))


def _out_proj_kernel(x_ref, sb_ref, ssd_ref, mo_ref, xs_ref, sbs_ref, ssds_ref, mos_ref, w_ref,
                     o_ref, os_ref):
    _out_proj_rows(x_ref, sb_ref, ssd_ref, mo_ref, w_ref, o_ref)

    @pl.when(pl.program_id(0) == 0)
    def _():
        _out_proj_rows(xs_ref, sbs_ref, ssds_ref, mos_ref, w_ref, os_ref)


def _out_proj(x, sb, ssd, mo, side, w_out, tm):
    t, d = x.shape
    ts = side[0].shape[0]
    row = lambda width: pl.BlockSpec((tm, width), lambda m: (m, 0))
    fixed = lambda width: pl.BlockSpec((ts, width), lambda m: (0, 0))
    widths = (d, SB_WIDTH, SSD_WIDTH, MEM_WIDTH)
    return pl.pallas_call(
        _out_proj_kernel,
        grid=(t // tm,),
        in_specs=[
            *[row(w) for w in widths],
            *[fixed(w) for w in widths],
            pl.BlockSpec(w_out.shape, lambda m: (0, 0), pipeline_mode=pl.Buffered(1)),
        ],
        out_specs=[row(d), fixed(d)],
        out_shape=[jax.ShapeDtypeStruct((t, d), F32), jax.ShapeDtypeStruct((ts, d), F32)],
        compiler_params=pltpu.CompilerParams(
            dimension_semantics=("arbitrary",), vmem_limit_bytes=VMEM_LIMIT),
        name="out_proj",
    )(x, sb, ssd, mo, *side, w_out)


def _in_proj_plan(act_dtype):
    lowp = act_dtype == BF16
    outs, plan = [], []

    def add(width, dtype):
        outs.append((width, dtype))
        return len(outs) - 1

    q = add(SB_WIDTH, act_dtype)
    plan.append(((0, SB_WIDTH, 0, None if lowp else q, q if lowp else None),))
    k32 = add(SB_WIDTH, F32)
    k16 = add(SB_WIDTH, BF16) if lowp else None
    plan.append(((0, SB_WIDTH, 1, k32, k16),))
    v32 = add(SB_WIDTH, F32)
    v16 = add(SB_WIDTH, BF16) if lowp else None
    plan.append(((0, SB_WIDTH, None, v32, v16),))
    g = add(SB_WIDTH, F32)
    plan.append(((0, SB_WIDTH, None, g, None),))
    xbc = add(XBC_WIDTH, F32)
    plan.append(((0, XBC_WIDTH, None, xbc, None),))
    zdt = add(ZDT_WIDTH, F32)
    plan.append(((0, ZDT_WIDTH, None, zdt, None),))
    mq = add(MEM_WIDTH, act_dtype)
    mg = add(MEM_WIDTH, F32)
    plan.append(((0, MEM_WIDTH, 2, None if lowp else mq, mq if lowp else None),
                 (MEM_WIDTH, MEM_WIDTH, None, mg, None)))
    names = dict(q=q, k32=k32, k16=k16, v32=v32, v16=v16, g=g, xbc=xbc, zdt=zdt, mq=mq, mg=mg)
    return tuple(plan), outs, names


_O_Z = 4 * SB_WIDTH
_O_XBC = _O_Z + SSD_WIDTH
_O_DT = _O_XBC + XBC_WIDTH
_O_MEM = _O_DT + SSD_HEADS
_IN_WIDTH = _O_MEM + 2 * MEM_WIDTH
_CAT_WIDTH = 7 * PROJ_TN
W_PREP_COLS = 512
BF16_ROWS = 16


def _w_prep_kernel(w_ref, o_ref):
    def put(dst, src, rows):
        o_ref[dst:dst + rows, :] = w_ref[src:src + rows, :].astype(BF16)

    cols = w_ref.shape[1]
    put(0, 0, _O_Z)
    put(_O_Z, _O_XBC, XBC_WIDTH)
    put(_O_Z + XBC_WIDTH, _O_Z, SSD_WIDTH)
    dt0 = _O_Z + XBC_WIDTH + SSD_WIDTH
    o_ref[dt0:dt0 + BF16_ROWS, :] = jnp.concatenate(
        [w_ref[_O_DT:_O_MEM, :], jnp.zeros((BF16_ROWS - SSD_HEADS, cols), F32)], axis=0
    ).astype(BF16)
    o_ref[dt0 + BF16_ROWS:6 * PROJ_TN, :] = jnp.zeros((6 * PROJ_TN - dt0 - BF16_ROWS, cols), BF16)
    put(6 * PROJ_TN, _O_MEM, 2 * MEM_WIDTH)


def _rearranged_w_in(w_t):
    d = w_t.shape[1]
    assert w_t.shape[0] == _IN_WIDTH and d % W_PREP_COLS == 0
    return pl.pallas_call(
        _w_prep_kernel,
        grid=(d // W_PREP_COLS,),
        in_specs=[pl.BlockSpec((_IN_WIDTH, W_PREP_COLS), lambda c: (0, c))],
        out_specs=pl.BlockSpec((_CAT_WIDTH, W_PREP_COLS), lambda c: (0, c)),
        out_shape=jax.ShapeDtypeStruct((_CAT_WIDTH, d), BF16),
        compiler_params=pltpu.CompilerParams(
            dimension_semantics=("parallel",), vmem_limit_bytes=VMEM_LIMIT),
        name="w_prep",
    )(w_t)


def kernel(x_prompt, x_sample, cache_sb_k, cache_sb_v, state_ssm, state_conv, cache_mem_k,
           cache_mem_v, page_table, mem_prompt, norm_w, w_in, sb_q_norm, sb_k_norm, sb_bias,
           conv_w, conv_b, dt_bias, a_log, d_skip, ssd_norm_w, mem_norm_w, w_mem_kv, mem_q_norm,
           mem_k_norm, w_out):
    depth = w_in.shape[0]
    assert depth == 1
    layer = 0
    bp, sp, d = x_prompt.shape
    bs, ts, _ = x_sample.shape
    n_pool = cache_sb_k.shape[1]
    L = SSD_CHUNK

    w_cat = _rearranged_w_in(w_in[layer].T)
    w_o = w_out[layer]
    head_norms = jnp.concatenate(
        [sb_q_norm[layer][None], sb_k_norm[layer][None], mem_q_norm[layer][None],
         mem_k_norm[layer][None], jnp.zeros((SUBLANES - 4, HEAD_DIM), F32)], axis=0)
    ssd_params = (conv_w[layer], conv_b[layer], dt_bias[layer], a_log[layer], d_skip[layer],
                  ssd_norm_w[layer])

    xp = x_prompt.reshape(bp * sp, d)
    mem_plan = (((0, MEM_WIDTH, 3, 0, None), (MEM_WIDTH, MEM_WIDTH, None, 1, None)),)
    (mk, mv), _ = _proj(mem_prompt.reshape(bp * MEM_TOKENS, d), mem_norm_w[layer],
                        w_mem_kv[layer], head_norms, mem_plan,
                        [(MEM_WIDTH, F32), (MEM_WIDTH, F32)], tm=PROJ_TM,
                        w_rows_are_outputs=False)
    plan, outs, nm = _in_proj_plan(BF16)
    xs = x_sample.reshape(bs * ts, d)
    plan_s, outs_s, ns = _in_proj_plan(F32)
    pr, ps = _proj(xp, norm_w[layer], w_cat, head_norms, plan, outs, tm=PROJ_TM,
                   side=(xs, plan_s, outs_s))
    sb, sb_s = _sb_fused(pr[nm['q']], pr[nm['k16']], pr[nm['v16']], pr[nm['g']], bp, sp, ATTN_TQ,
                         ps[ns['q']], ps[ns['k32']], ps[ns['v32']], ps[ns['g']],
                         cache_sb_k, cache_sb_v, layer, page_table, sb_bias[layer], ts)
    ssd, conv_p, ssm_p = _ssd(
        pr[nm['xbc']], pr[nm['zdt']], jnp.zeros((bp, CONV_WIDTH - 1, XBC_WIDTH), F32),
        jnp.zeros((bp, SSD_HEADS, SSD_HEAD_DIM, SSD_STATE), F32), *ssd_params,
        batch=bp, n_chunks=sp // L, length=L, out_dtype=BF16)
    mo = _mem_attn(pr[nm['mq']], pr[nm['mg']], mk.reshape(bp, MEM_TOKENS, MEM_WIDTH),
                   mv.reshape(bp, MEM_TOKENS, MEM_WIDTH), bp, sp, tq=MEM_TQ, out_dtype=BF16)
    ssd_s, conv_s, ssm_s = _ssd(
        ps[ns['xbc']], ps[ns['zdt']], state_conv[layer], state_ssm[layer],
        *ssd_params, batch=bs, n_chunks=1, length=ts, out_dtype=F32)
    mo_s = _mem_attn_rows(ps[ns['mq']], ps[ns['mg']],
                          cache_mem_k[layer].reshape(bs, MEM_TOKENS * MEM_HEADS, HEAD_DIM),
                          cache_mem_v[layer].reshape(bs, MEM_TOKENS * MEM_HEADS, HEAD_DIM),
                          bs, ts, out_dtype=F32)
    yp, ys = _out_proj(xp, sb, ssd, mo, (xs, sb_s, ssd_s, mo_s), w_o, tm=OUT_TM)

    tail = slice(SUBLANES - (CONV_WIDTH - 1), SUBLANES)
    return (
        yp.reshape(bp, sp, d),
        ys.reshape(bs, ts, d),
        pr[nm['k32']].reshape(1, bp, sp, SB_HEADS, HEAD_DIM),
        pr[nm['v32']].reshape(1, bp, sp, SB_HEADS, HEAD_DIM),
        ssm_p.reshape(1, bp, SSD_HEADS, SSD_HEAD_DIM, SSD_STATE),
        conv_p[:, tail][None],
        mk.reshape(1, bp, MEM_TOKENS, MEM_HEADS, HEAD_DIM),
        mv.reshape(1, bp, MEM_TOKENS, MEM_HEADS, HEAD_DIM),
        ps[ns['k32']].reshape(1, bs, ts, SB_HEADS, HEAD_DIM),
        ps[ns['v32']].reshape(1, bs, ts, SB_HEADS, HEAD_DIM),
        ssm_s.reshape(1, bs, SSD_HEADS, SSD_HEAD_DIM, SSD_STATE),
        conv_s[:, tail][None],
    )
```
